```python
import jax, jax.numpy as jnp
from jax import lax
import numpy as np

D_MODEL = 2048
BATCH = 8
SEQ = 4096
DEPTH = 4

MIX_WIDTH = D_MODEL
GROUP_WIDTH = MIX_WIDTH // 4
HEAD_DIM = 128
N_HEADS_PER_MIXER = GROUP_WIDTH // HEAD_DIM
CHUNK = 128
SHORT_CONV = 3
CONFORMER_CONV = 31
POOL_WINDOWS = (2, 4, 8, 16)
POOL_GROUP = GROUP_WIDTH // len(POOL_WINDOWS)
D_FF = ((8 * D_MODEL // 3 + 255) // 256) * 256
PLE_DIM = 256
EPS = 1e-6
A_COLS = 2 * GROUP_WIDTH
B_COLS = 3 * GROUP_WIDTH
C_COLS = 2 * GROUP_WIDTH
D_COLS = GROUP_WIDTH
IN_COLS = A_COLS + B_COLS + C_COLS + D_COLS

kernel_name = "hybrid_sgu_conv_conformer_pool_trunk"


def _rms(x, g):
    xf = x.astype(jnp.float32)
    y = xf * lax.rsqrt(jnp.mean(xf * xf, axis=-1, keepdims=True) + EPS)
    return (y * g.astype(jnp.float32)).astype(x.dtype)


def _ln(x, g, b):
    xf = x.astype(jnp.float32)
    mu = jnp.mean(xf, axis=-1, keepdims=True)
    xc = xf - mu
    var = jnp.mean(xc * xc, axis=-1, keepdims=True)
    y = xc * lax.rsqrt(var + EPS) * g.astype(jnp.float32) + b.astype(jnp.float32)
    return y.astype(x.dtype)


def _causal_dwconv(x, w):
    k, c = w.shape
    return lax.conv_general_dilated(
        x, w[:, None, :].astype(x.dtype), window_strides=(1,), padding=[(k - 1, 0)],
        dimension_numbers=("NWC", "WIO", "NWC"), feature_group_count=c)


def _mixer_sgu(z, ln_g, ln_b, w_s, b_s):
    bsz, s, _ = z.shape
    n = s // CHUNK
    z = jax.nn.gelu(z)
    u, v = jnp.split(z, 2, axis=-1)
    v = _ln(v.reshape(bsz, s, N_HEADS_PER_MIXER, HEAD_DIM), ln_g, ln_b)
    v = v.reshape(bsz, n, CHUNK, N_HEADS_PER_MIXER, HEAD_DIM)
    mask = jnp.tril(jnp.ones((CHUNK, CHUNK), dtype=bool))
    wm = jnp.where(mask[None], w_s, jnp.zeros((), w_s.dtype))
    sp = jnp.einsum("hts,bnshd->bnthd", wm, v) + b_s.T[:, :, None]
    out = u.reshape(bsz, n, CHUNK, N_HEADS_PER_MIXER, HEAD_DIM) * sp
    return out.reshape(bsz, s, GROUP_WIDTH)


def _mixer_shortconv(z, conv_w):
    h, bg, cg = jnp.split(z, 3, axis=-1)
    return bg * _causal_dwconv(cg * h, conv_w)


def _mixer_conformer(z, conv_w, conv_b, ln_g, ln_b):
    a, g = jnp.split(z, 2, axis=-1)
    h = a * jax.nn.sigmoid(g)
    h = _causal_dwconv(h, conv_w) + conv_b
    h = _ln(h, ln_g, ln_b)
    return jax.nn.silu(h)


def _mixer_pool(z, pool_w, pool_scale):
    bsz, s, _ = z.shape
    zg = z.reshape(bsz, s, len(POOL_WINDOWS), POOL_GROUP)
    cs = jnp.cumsum(zg.astype(jnp.float32), axis=1)
    pos = jnp.arange(1, s + 1, dtype=jnp.int32)
    pooled = []
    for gi, w in enumerate(POOL_WINDOWS):
        c = cs[:, :, gi]
        lagged = jnp.pad(c, ((0, 0), (w, 0), (0, 0)))[:, :s]
        count = jnp.minimum(pos, w).astype(jnp.float32)
        pooled.append((c - lagged) / count[None, :, None])
    pooled = jnp.stack(pooled, axis=2).astype(z.dtype) - zg
    out = jnp.einsum("bsgc,gcd->bsgd", pooled, pool_w).reshape(bsz, s, GROUP_WIDTH)
    return out * pool_scale


def _fwd_setup_inputs(seed: int = 0) -> dict:
    key = jax.random.key(seed)
    ks = jax.random.split(key, 24)
    f = jnp.float32
    L, D, G, H, hd = DEPTH, D_MODEL, GROUP_WIDTH, N_HEADS_PER_MIXER, HEAD_DIM
    nrm = lambda k, shape, scale: jax.random.normal(k, shape, f) * scale
    gain = lambda k, shape: 1.0 + 0.05 * jax.random.normal(k, shape, f)
    return {
        "x": jax.random.normal(ks[0], (BATCH, SEQ, D), f),
        "p": jax.random.normal(ks[1], (DEPTH, BATCH, SEQ, PLE_DIM), f),
        "norm_mix_g": gain(ks[2], (L, D)),
        "w_in": nrm(ks[3], (L, D, IN_COLS), D ** -0.5),
        "sgu_ln_g": gain(ks[4], (L, H, hd)),
        "sgu_ln_b": nrm(ks[5], (L, H, hd), 0.02),
        "sgu_w": nrm(ks[6], (L, H, CHUNK, CHUNK), CHUNK ** -0.5),
        "sgu_b": gain(ks[7], (L, H, CHUNK)),
        "sc_conv_w": nrm(ks[8], (L, SHORT_CONV, G), SHORT_CONV ** -0.5),
        "cf_conv_w": nrm(ks[9], (L, CONFORMER_CONV, G), CONFORMER_CONV ** -0.5),
        "cf_conv_b": nrm(ks[10], (L, G), 0.02),
        "cf_ln_g": gain(ks[11], (L, G)),
        "cf_ln_b": nrm(ks[12], (L, G), 0.02),
        "pool_w": nrm(ks[13], (L, len(POOL_WINDOWS), POOL_GROUP, POOL_GROUP), POOL_GROUP ** -0.5),
        "pool_scale": 0.5 + 0.1 * jax.random.normal(ks[14], (L, G), f),
        "w_out": nrm(ks[15], (L, MIX_WIDTH, D), MIX_WIDTH ** -0.5),
        "norm_ffn_g": gain(ks[16], (L, D)),
        "w_gate": nrm(ks[17], (L, D, D_FF), D ** -0.5),
        "w_up": nrm(ks[18], (L, D, D_FF), D ** -0.5),
        "w_down": nrm(ks[19], (L, D_FF, D), D_FF ** -0.5),
        "norm_ple_g": gain(ks[20], (L, D)),
        "w_ple_gate": nrm(ks[21], (L, D, D), D ** -0.5),
        "w_ple_proj": nrm(ks[22], (L, PLE_DIM, D), PLE_DIM ** -0.5),
        "final_norm_g": gain(ks[23], (D,)),
    }


def _fwd_reference(x, p, norm_mix_g, w_in, sgu_ln_g, sgu_ln_b, sgu_w, sgu_b, sc_conv_w,
              cf_conv_w, cf_conv_b, cf_ln_g, cf_ln_b, pool_w, pool_scale, w_out,
              norm_ffn_g, w_gate, w_up, w_down, norm_ple_g, w_ple_gate, w_ple_proj,
              final_norm_g):
    h = x
    split_at = [A_COLS, A_COLS + B_COLS, A_COLS + B_COLS + C_COLS]
    for i in range(DEPTH):
        y = _rms(h, norm_mix_g[i])
        z = y @ w_in[i]
        za, zb, zc, zd = jnp.split(z, split_at, axis=-1)
        oa = _mixer_sgu(za, sgu_ln_g[i], sgu_ln_b[i], sgu_w[i], sgu_b[i])
        ob = _mixer_shortconv(zb, sc_conv_w[i])
        oc = _mixer_conformer(zc, cf_conv_w[i], cf_conv_b[i], cf_ln_g[i], cf_ln_b[i])
        od = _mixer_pool(zd, pool_w[i], pool_scale[i])
        h = h + jnp.concatenate([oa, ob, oc, od], axis=-1) @ w_out[i]
        y = _rms(h, norm_ffn_g[i])
        h = h + (jax.nn.silu(y @ w_gate[i]) * (y @ w_up[i])) @ w_down[i]
        y = _rms(h, norm_ple_g[i])
        h = h + jax.nn.sigmoid(y @ w_ple_gate[i]) * (p[i] @ w_ple_proj[i])
    return _rms(h, final_norm_g)


import jax as _jax
import jax.numpy as _jnp

TWIN_FORMAT = 'train_step'
FWD_PARAMS = ['x', 'p', 'norm_mix_g', 'w_in', 'sgu_ln_g', 'sgu_ln_b', 'sgu_w', 'sgu_b', 'sc_conv_w', 'cf_conv_w', 'cf_conv_b', 'cf_ln_g', 'cf_ln_b', 'pool_w', 'pool_scale', 'w_out', 'norm_ffn_g', 'w_gate', 'w_up', 'w_down', 'norm_ple_g', 'w_ple_gate', 'w_ple_proj', 'final_norm_g']
TWIN_WEIGHTS = ['norm_mix_g', 'w_in', 'sgu_ln_g', 'sgu_ln_b', 'sgu_w', 'sgu_b', 'sc_conv_w', 'cf_conv_w', 'cf_conv_b', 'cf_ln_g', 'cf_ln_b', 'pool_w', 'pool_scale', 'w_out', 'norm_ffn_g', 'w_gate', 'w_up', 'w_down', 'norm_ple_g', 'w_ple_gate', 'w_ple_proj', 'final_norm_g']
TWIN_DIFF_INPUT = 'x'
TWIN_INPUTS = ['x', 'p', 'norm_mix_g', 'w_in', 'sgu_ln_g', 'sgu_ln_b', 'sgu_w', 'sgu_b', 'sc_conv_w', 'cf_conv_w', 'cf_conv_b', 'cf_ln_g', 'cf_ln_b', 'pool_w', 'pool_scale', 'w_out', 'norm_ffn_g', 'w_gate', 'w_up', 'w_down', 'norm_ple_g', 'w_ple_gate', 'w_ple_proj', 'final_norm_g', 'loss_target', 'm_norm_mix_g', 'm_w_in', 'm_sgu_ln_g', 'm_sgu_ln_b', 'm_sgu_w', 'm_sgu_b', 'm_sc_conv_w', 'm_cf_conv_w', 'm_cf_conv_b', 'm_cf_ln_g', 'm_cf_ln_b', 'm_pool_w', 'm_pool_scale', 'm_w_out', 'm_norm_ffn_g', 'm_w_gate', 'm_w_up', 'm_w_down', 'm_norm_ple_g', 'm_w_ple_gate', 'm_w_ple_proj', 'm_final_norm_g', 'v_norm_mix_g', 'v_w_in', 'v_sgu_ln_g', 'v_sgu_ln_b', 'v_sgu_w', 'v_sgu_b', 'v_sc_conv_w', 'v_cf_conv_w', 'v_cf_conv_b', 'v_cf_ln_g', 'v_cf_ln_b', 'v_pool_w', 'v_pool_scale', 'v_w_out', 'v_norm_ffn_g', 'v_w_gate', 'v_w_up', 'v_w_down', 'v_norm_ple_g', 'v_w_ple_gate', 'v_w_ple_proj', 'v_final_norm_g']
TWIN_OUTPUTS = ['loss', 'grad_x', 'grad_norm_mix_g', 'grad_w_in', 'grad_sgu_ln_g', 'grad_sgu_ln_b', 'grad_sgu_w', 'grad_sgu_b', 'grad_sc_conv_w', 'grad_cf_conv_w', 'grad_cf_conv_b', 'grad_cf_ln_g', 'grad_cf_ln_b', 'grad_pool_w', 'grad_pool_scale', 'grad_w_out', 'grad_norm_ffn_g', 'grad_w_gate', 'grad_w_up', 'grad_w_down', 'grad_norm_ple_g', 'grad_w_ple_gate', 'grad_w_ple_proj', 'grad_final_norm_g', 'delta_norm_mix_g', 'delta_w_in', 'delta_sgu_ln_g', 'delta_sgu_ln_b', 'delta_sgu_w', 'delta_sgu_b', 'delta_sc_conv_w', 'delta_cf_conv_w', 'delta_cf_conv_b', 'delta_cf_ln_g', 'delta_cf_ln_b', 'delta_pool_w', 'delta_pool_scale', 'delta_w_out', 'delta_norm_ffn_g', 'delta_w_gate', 'delta_w_up', 'delta_w_down', 'delta_norm_ple_g', 'delta_w_ple_gate', 'delta_w_ple_proj', 'delta_final_norm_g', 'new_m_norm_mix_g', 'new_m_w_in', 'new_m_sgu_ln_g', 'new_m_sgu_ln_b', 'new_m_sgu_w', 'new_m_sgu_b', 'new_m_sc_conv_w', 'new_m_cf_conv_w', 'new_m_cf_conv_b', 'new_m_cf_ln_g', 'new_m_cf_ln_b', 'new_m_pool_w', 'new_m_pool_scale', 'new_m_w_out', 'new_m_norm_ffn_g', 'new_m_w_gate', 'new_m_w_up', 'new_m_w_down', 'new_m_norm_ple_g', 'new_m_w_ple_gate', 'new_m_w_ple_proj', 'new_m_final_norm_g', 'new_v_norm_mix_g', 'new_v_w_in', 'new_v_sgu_ln_g', 'new_v_sgu_ln_b', 'new_v_sgu_w', 'new_v_sgu_b', 'new_v_sc_conv_w', 'new_v_cf_conv_w', 'new_v_cf_conv_b', 'new_v_cf_ln_g', 'new_v_cf_ln_b', 'new_v_pool_w', 'new_v_pool_scale', 'new_v_w_out', 'new_v_norm_ffn_g', 'new_v_w_gate', 'new_v_w_up', 'new_v_w_down', 'new_v_norm_ple_g', 'new_v_w_ple_gate', 'new_v_w_ple_proj', 'new_v_final_norm_g']
TWIN_LEAF_KINDS = {'loss': 'loss', 'grad_x': 'grad_x', 'grad_norm_mix_g': 'grad_w', 'grad_w_in': 'grad_w', 'grad_sgu_ln_g': 'grad_w', 'grad_sgu_ln_b': 'grad_w', 'grad_sgu_w': 'grad_w', 'grad_sgu_b': 'grad_w', 'grad_sc_conv_w': 'grad_w', 'grad_cf_conv_w': 'grad_w', 'grad_cf_conv_b': 'grad_w', 'grad_cf_ln_g': 'grad_w', 'grad_cf_ln_b': 'grad_w', 'grad_pool_w': 'grad_w', 'grad_pool_scale': 'grad_w', 'grad_w_out': 'grad_w', 'grad_norm_ffn_g': 'grad_w', 'grad_w_gate': 'grad_w', 'grad_w_up': 'grad_w', 'grad_w_down': 'grad_w', 'grad_norm_ple_g': 'grad_w', 'grad_w_ple_gate': 'grad_w', 'grad_w_ple_proj': 'grad_w', 'grad_final_norm_g': 'grad_w', 'delta_norm_mix_g': 'delta_w', 'delta_w_in': 'delta_w', 'delta_sgu_ln_g': 'delta_w', 'delta_sgu_ln_b': 'delta_w', 'delta_sgu_w': 'delta_w', 'delta_sgu_b': 'delta_w', 'delta_sc_conv_w': 'delta_w', 'delta_cf_conv_w': 'delta_w', 'delta_cf_conv_b': 'delta_w', 'delta_cf_ln_g': 'delta_w', 'delta_cf_ln_b': 'delta_w', 'delta_pool_w': 'delta_w', 'delta_pool_scale': 'delta_w', 'delta_w_out': 'delta_w', 'delta_norm_ffn_g': 'delta_w', 'delta_w_gate': 'delta_w', 'delta_w_up': 'delta_w', 'delta_w_down': 'delta_w', 'delta_norm_ple_g': 'delta_w', 'delta_w_ple_gate': 'delta_w', 'delta_w_ple_proj': 'delta_w', 'delta_final_norm_g': 'delta_w', 'new_m_norm_mix_g': 'new_m', 'new_m_w_in': 'new_m', 'new_m_sgu_ln_g': 'new_m', 'new_m_sgu_ln_b': 'new_m', 'new_m_sgu_w': 'new_m', 'new_m_sgu_b': 'new_m', 'new_m_sc_conv_w': 'new_m', 'new_m_cf_conv_w': 'new_m', 'new_m_cf_conv_b': 'new_m', 'new_m_cf_ln_g': 'new_m', 'new_m_cf_ln_b': 'new_m', 'new_m_pool_w': 'new_m', 'new_m_pool_scale': 'new_m', 'new_m_w_out': 'new_m', 'new_m_norm_ffn_g': 'new_m', 'new_m_w_gate': 'new_m', 'new_m_w_up': 'new_m', 'new_m_w_down': 'new_m', 'new_m_norm_ple_g': 'new_m', 'new_m_w_ple_gate': 'new_m', 'new_m_w_ple_proj': 'new_m', 'new_m_final_norm_g': 'new_m', 'new_v_norm_mix_g': 'new_v', 'new_v_w_in': 'new_v', 'new_v_sgu_ln_g': 'new_v', 'new_v_sgu_ln_b': 'new_v', 'new_v_sgu_w': 'new_v', 'new_v_sgu_b': 'new_v', 'new_v_sc_conv_w': 'new_v', 'new_v_cf_conv_w': 'new_v', 'new_v_cf_conv_b': 'new_v', 'new_v_cf_ln_g': 'new_v', 'new_v_cf_ln_b': 'new_v', 'new_v_pool_w': 'new_v', 'new_v_pool_scale': 'new_v', 'new_v_w_out': 'new_v', 'new_v_norm_ffn_g': 'new_v', 'new_v_w_gate': 'new_v', 'new_v_w_up': 'new_v', 'new_v_w_down': 'new_v', 'new_v_norm_ple_g': 'new_v', 'new_v_w_ple_gate': 'new_v', 'new_v_w_ple_proj': 'new_v', 'new_v_final_norm_g': 'new_v'}


def _forward(args):
    return _fwd_reference(*[args[k] for k in FWD_PARAMS])


def _output_shape():
    out = _jax.eval_shape(lambda: _forward(_fwd_setup_inputs(0)))
    return out.shape, out.dtype

N_MICROBATCH = 1
ADAM_LR = 0.001
ADAM_B1 = 0.9
ADAM_B2 = 0.999
ADAM_EPS = 1e-08
ADAM_WD = 0.01
ADAM_STEP = 10
PER_EXAMPLE_BATCH_AXIS = {'x': 0, 'p': 1, 'loss_target': 0}
SHARED_INPUTS = []
_WEIGHT_DTYPES = {'norm_mix_g': _jnp.float32, 'w_in': _jnp.float32, 'sgu_ln_g': _jnp.float32, 'sgu_ln_b': _jnp.float32, 'sgu_w': _jnp.float32, 'sgu_b': _jnp.float32, 'sc_conv_w': _jnp.float32, 'cf_conv_w': _jnp.float32, 'cf_conv_b': _jnp.float32, 'cf_ln_g': _jnp.float32, 'cf_ln_b': _jnp.float32, 'pool_w': _jnp.float32, 'pool_scale': _jnp.float32, 'w_out': _jnp.float32, 'norm_ffn_g': _jnp.float32, 'w_gate': _jnp.float32, 'w_up': _jnp.float32, 'w_down': _jnp.float32, 'norm_ple_g': _jnp.float32, 'w_ple_gate': _jnp.float32, 'w_ple_proj': _jnp.float32, 'final_norm_g': _jnp.float32}
MOMENT_SCALE = {'norm_mix_g': 6.926171e-02, 'w_in': 4.975742e-02, 'sgu_ln_g': 3.097251e-02, 'sgu_ln_b': 3.087856e-02, 'sgu_w': 3.021272e-02, 'sgu_b': 4.309424e-02, 'sc_conv_w': 6.910950e-02, 'cf_conv_w': 4.118000e-02, 'cf_conv_b': 9.046810e-02, 'cf_ln_g': 4.874866e-02, 'cf_ln_b': 5.129451e-02, 'pool_w': 2.941819e-02, 'pool_scale': 5.837356e-02, 'w_out': 5.014014e-02, 'norm_ffn_g': 5.081266e-02, 'w_gate': 2.116407e-02, 'w_up': 2.059320e-02, 'w_down': 3.411259e-02, 'norm_ple_g': 1.202490e-02, 'w_ple_gate': 1.215547e-02, 'w_ple_proj': 3.059013e-02, 'final_norm_g': 1.600739e+01}


def _to_microbatches(a, axis):
    t = _jnp.moveaxis(a, axis, 0)
    t = t.reshape((N_MICROBATCH, t.shape[0] // N_MICROBATCH) + t.shape[1:])
    return _jnp.moveaxis(t, 1, axis + 1)


def setup_inputs(seed: int = 0) -> dict:
    inp = _fwd_setup_inputs(seed)
    key = _jax.random.fold_in(_jax.random.key(seed), 7919)
    shape, _ = _output_shape()
    out = dict(inp)
    out["loss_target"] = _jax.random.normal(_jax.random.fold_in(key, 0), shape, _jnp.float32)
    for i, name in enumerate(TWIN_WEIGHTS):
        w = inp[name].astype(_jnp.float32)
        if MOMENT_SCALE is None:
            s = _jnp.sqrt(_jnp.mean(_jnp.square(w)) + 1e-30)
        else:
            s = MOMENT_SCALE[name]
        km, kv = _jax.random.split(_jax.random.fold_in(key, i + 1))
        out[name] = w
        out["m_" + name] = s * _jax.random.normal(km, w.shape, _jnp.float32)
        out["v_" + name] = (s * s) * _jax.random.uniform(kv, w.shape, _jnp.float32, 0.5, 1.5)
    if N_MICROBATCH > 1:
        for name, axis in PER_EXAMPLE_BATCH_AXIS.items():
            out[name] = _to_microbatches(out[name], axis)
    return {'x': out['x'], 'p': out['p'], 'norm_mix_g': out['norm_mix_g'], 'w_in': out['w_in'], 'sgu_ln_g': out['sgu_ln_g'], 'sgu_ln_b': out['sgu_ln_b'], 'sgu_w': out['sgu_w'], 'sgu_b': out['sgu_b'], 'sc_conv_w': out['sc_conv_w'], 'cf_conv_w': out['cf_conv_w'], 'cf_conv_b': out['cf_conv_b'], 'cf_ln_g': out['cf_ln_g'], 'cf_ln_b': out['cf_ln_b'], 'pool_w': out['pool_w'], 'pool_scale': out['pool_scale'], 'w_out': out['w_out'], 'norm_ffn_g': out['norm_ffn_g'], 'w_gate': out['w_gate'], 'w_up': out['w_up'], 'w_down': out['w_down'], 'norm_ple_g': out['norm_ple_g'], 'w_ple_gate': out['w_ple_gate'], 'w_ple_proj': out['w_ple_proj'], 'final_norm_g': out['final_norm_g'], 'loss_target': out['loss_target'], 'm_norm_mix_g': out['m_norm_mix_g'], 'm_w_in': out['m_w_in'], 'm_sgu_ln_g': out['m_sgu_ln_g'], 'm_sgu_ln_b': out['m_sgu_ln_b'], 'm_sgu_w': out['m_sgu_w'], 'm_sgu_b': out['m_sgu_b'], 'm_sc_conv_w': out['m_sc_conv_w'], 'm_cf_conv_w': out['m_cf_conv_w'], 'm_cf_conv_b': out['m_cf_conv_b'], 'm_cf_ln_g': out['m_cf_ln_g'], 'm_cf_ln_b': out['m_cf_ln_b'], 'm_pool_w': out['m_pool_w'], 'm_pool_scale': out['m_pool_scale'], 'm_w_out': out['m_w_out'], 'm_norm_ffn_g': out['m_norm_ffn_g'], 'm_w_gate': out['m_w_gate'], 'm_w_up': out['m_w_up'], 'm_w_down': out['m_w_down'], 'm_norm_ple_g': out['m_norm_ple_g'], 'm_w_ple_gate': out['m_w_ple_gate'], 'm_w_ple_proj': out['m_w_ple_proj'], 'm_final_norm_g': out['m_final_norm_g'], 'v_norm_mix_g': out['v_norm_mix_g'], 'v_w_in': out['v_w_in'], 'v_sgu_ln_g': out['v_sgu_ln_g'], 'v_sgu_ln_b': out['v_sgu_ln_b'], 'v_sgu_w': out['v_sgu_w'], 'v_sgu_b': out['v_sgu_b'], 'v_sc_conv_w': out['v_sc_conv_w'], 'v_cf_conv_w': out['v_cf_conv_w'], 'v_cf_conv_b': out['v_cf_conv_b'], 'v_cf_ln_g': out['v_cf_ln_g'], 'v_cf_ln_b': out['v_cf_ln_b'], 'v_pool_w': out['v_pool_w'], 'v_pool_scale': out['v_pool_scale'], 'v_w_out': out['v_w_out'], 'v_norm_ffn_g': out['v_norm_ffn_g'], 'v_w_gate': out['v_w_gate'], 'v_w_up': out['v_w_up'], 'v_w_down': out['v_w_down'], 'v_norm_ple_g': out['v_norm_ple_g'], 'v_w_ple_gate': out['v_w_ple_gate'], 'v_w_ple_proj': out['v_w_ple_proj'], 'v_final_norm_g': out['v_final_norm_g']}


def _loss(weights, diff, rest, loss_target):
    with _jax.named_scope("forward"):
        args = {**rest, TWIN_DIFF_INPUT: diff, **{k: w.astype(_WEIGHT_DTYPES[k]) for k, w in weights.items()}}
        y = _forward(args)
    with _jax.named_scope("loss_head"):
        err = _jnp.square(y.astype(_jnp.float32) - loss_target)
        return 0.5 * _jnp.sum(_jnp.mean(err, axis=-1)) if err.ndim else 0.5 * err


def _adamw(w, g, m, v):
    m = ADAM_B1 * m + (1.0 - ADAM_B1) * g
    v = ADAM_B2 * v + (1.0 - ADAM_B2) * _jnp.square(g)
    m_hat = m / (1.0 - ADAM_B1 ** ADAM_STEP)
    v_hat = v / (1.0 - ADAM_B2 ** ADAM_STEP)
    delta = -ADAM_LR * (m_hat / (_jnp.sqrt(v_hat) + ADAM_EPS) + ADAM_WD * w)
    return delta, m, v


def reference(x, p, norm_mix_g, w_in, sgu_ln_g, sgu_ln_b, sgu_w, sgu_b, sc_conv_w, cf_conv_w, cf_conv_b, cf_ln_g, cf_ln_b, pool_w, pool_scale, w_out, norm_ffn_g, w_gate, w_up, w_down, norm_ple_g, w_ple_gate, w_ple_proj, final_norm_g, loss_target, m_norm_mix_g, m_w_in, m_sgu_ln_g, m_sgu_ln_b, m_sgu_w, m_sgu_b, m_sc_conv_w, m_cf_conv_w, m_cf_conv_b, m_cf_ln_g, m_cf_ln_b, m_pool_w, m_pool_scale, m_w_out, m_norm_ffn_g, m_w_gate, m_w_up, m_w_down, m_norm_ple_g, m_w_ple_gate, m_w_ple_proj, m_final_norm_g, v_norm_mix_g, v_w_in, v_sgu_ln_g, v_sgu_ln_b, v_sgu_w, v_sgu_b, v_sc_conv_w, v_cf_conv_w, v_cf_conv_b, v_cf_ln_g, v_cf_ln_b, v_pool_w, v_pool_scale, v_w_out, v_norm_ffn_g, v_w_gate, v_w_up, v_w_down, v_norm_ple_g, v_w_ple_gate, v_w_ple_proj, v_final_norm_g):
    given = dict(x=x, p=p, norm_mix_g=norm_mix_g, w_in=w_in, sgu_ln_g=sgu_ln_g, sgu_ln_b=sgu_ln_b, sgu_w=sgu_w, sgu_b=sgu_b, sc_conv_w=sc_conv_w, cf_conv_w=cf_conv_w, cf_conv_b=cf_conv_b, cf_ln_g=cf_ln_g, cf_ln_b=cf_ln_b, pool_w=pool_w, pool_scale=pool_scale, w_out=w_out, norm_ffn_g=norm_ffn_g, w_gate=w_gate, w_up=w_up, w_down=w_down, norm_ple_g=norm_ple_g, w_ple_gate=w_ple_gate, w_ple_proj=w_ple_proj, final_norm_g=final_norm_g, loss_target=loss_target, m_norm_mix_g=m_norm_mix_g, m_w_in=m_w_in, m_sgu_ln_g=m_sgu_ln_g, m_sgu_ln_b=m_sgu_ln_b, m_sgu_w=m_sgu_w, m_sgu_b=m_sgu_b, m_sc_conv_w=m_sc_conv_w, m_cf_conv_w=m_cf_conv_w, m_cf_conv_b=m_cf_conv_b, m_cf_ln_g=m_cf_ln_g, m_cf_ln_b=m_cf_ln_b, m_pool_w=m_pool_w, m_pool_scale=m_pool_scale, m_w_out=m_w_out, m_norm_ffn_g=m_norm_ffn_g, m_w_gate=m_w_gate, m_w_up=m_w_up, m_w_down=m_w_down, m_norm_ple_g=m_norm_ple_g, m_w_ple_gate=m_w_ple_gate, m_w_ple_proj=m_w_ple_proj, m_final_norm_g=m_final_norm_g, v_norm_mix_g=v_norm_mix_g, v_w_in=v_w_in, v_sgu_ln_g=v_sgu_ln_g, v_sgu_ln_b=v_sgu_ln_b, v_sgu_w=v_sgu_w, v_sgu_b=v_sgu_b, v_sc_conv_w=v_sc_conv_w, v_cf_conv_w=v_cf_conv_w, v_cf_conv_b=v_cf_conv_b, v_cf_ln_g=v_cf_ln_g, v_cf_ln_b=v_cf_ln_b, v_pool_w=v_pool_w, v_pool_scale=v_pool_scale, v_w_out=v_w_out, v_norm_ffn_g=v_norm_ffn_g, v_w_gate=v_w_gate, v_w_up=v_w_up, v_w_down=v_w_down, v_norm_ple_g=v_norm_ple_g, v_w_ple_gate=v_w_ple_gate, v_w_ple_proj=v_w_ple_proj, v_final_norm_g=v_final_norm_g)
    weights = {n: given[n] for n in TWIN_WEIGHTS}
    shared = {n: given[n] for n in SHARED_INPUTS}
    per_example = {n: given[n] for n in ['x', 'p']}
    grad_fn = _jax.value_and_grad(_loss, argnums=(0, 1))

    def one_microbatch(ex, loss_target):
        ex = dict(ex)
        diff = ex.pop(TWIN_DIFF_INPUT)
        return grad_fn(weights, diff, {**shared, **ex}, loss_target)

    if N_MICROBATCH == 1:
        loss, (grad_w, grad_x) = one_microbatch(per_example, given["loss_target"])
    else:
        def body(carry, xs):
            loss_sum, grad_sum = carry
            l_k, (gw_k, gx_k) = one_microbatch(xs[0], xs[1])
            with _jax.named_scope("update"):
                return (loss_sum + l_k, _jax.tree.map(_jnp.add, grad_sum, gw_k)), gx_k

        init = (_jnp.zeros((), _jnp.float32), _jax.tree.map(_jnp.zeros_like, weights))
        (loss, grad_w), grad_x = _jax.lax.scan(body, init, (per_example, given["loss_target"]))
    with _jax.named_scope("update"):
        delta_w, new_m, new_v = {}, {}, {}
        for n in TWIN_WEIGHTS:
            delta_w[n], new_m[n], new_v[n] = _adamw(weights[n], grad_w[n], given["m_" + n], given["v_" + n])
    return (loss, grad_x, *[grad_w[n] for n in TWIN_WEIGHTS], *[delta_w[n] for n in TWIN_WEIGHTS],
            *[new_m[n] for n in TWIN_WEIGHTS], *[new_v[n] for n in TWIN_WEIGHTS])
```

```python
import functools
import math

import jax
import jax.numpy as jnp
from jax import lax
from jax.experimental import pallas as pl
from jax.experimental.pallas import tpu as pltpu

F32 = jnp.float32
BF16 = jnp.bfloat16
EPS = 1e-6
HEAD = 128
GROUP = 4 * HEAD
HALO = 32
SHORT_K = 3
CONF_K = 31
POOL_WINDOWS = (2, 4, 8, 16)
N_DEV = 8
MESH = pl.DeviceIdType.MESH
VMEM_LIMIT_BYTES = 56 * 1024 * 1024

ADAM_LR = 0.001
ADAM_B1 = 0.9
ADAM_B2 = 0.999
ADAM_EPS = 1e-08
ADAM_WD = 0.01
ADAM_STEP = 10

ANY = pl.BlockSpec(memory_space=pl.ANY)


def _params(*sem):
    return pltpu.CompilerParams(dimension_semantics=sem, vmem_limit_bytes=VMEM_LIMIT_BYTES)


def _rms(x, g):
    return x * lax.rsqrt(jnp.mean(x * x, axis=-1, keepdims=True) + EPS) * g


def _ln(x, g, b):
    mu = jnp.mean(x, axis=-1, keepdims=True)
    xc = x - mu
    var = jnp.mean(xc * xc, axis=-1, keepdims=True)
    return xc * lax.rsqrt(var + EPS) * g + b


def _bdot(a, b, dims=(((1,), (0,)), ((), ()))):
    return lax.dot_general(a.astype(BF16), b.astype(BF16), dims, preferred_element_type=F32)


def _sgu_piece(zu, zv, lg, lb, w, b):
    u = jax.nn.gelu(zu)
    v = _ln(jax.nn.gelu(zv), lg, lb)
    row = lax.broadcasted_iota(jnp.int32, w.shape, 0)
    col = lax.broadcasted_iota(jnp.int32, w.shape, 1)
    wm = jnp.where(row >= col, w, 0.0)
    return u * (_bdot(wm, v) + b)


def _conf_post(c, g, b):
    return jax.nn.silu(_ln(c, g, b))


def _pool_count(first_pos, rows, w):
    pos = first_pos + lax.broadcasted_iota(jnp.int32, (rows, 1), 0) + 1
    return jnp.minimum(pos, w).astype(F32)


_DIMS = {
    "nn": (((1,), (0,)), ((), ())),
    "nt": (((1,), (1,)), ((), ())),
    "tn": (((0,), (0,)), ((), ())),
}


def _matmul(name, kind, grid, a, a_spec, b, b_spec, outs, out_specs, acc_shape,
            extras=(), extra_specs=(), epilogue=None):
    nk = grid[2]
    ne, no = len(extras), len(outs)
    dims = _DIMS[kind]

    def body(a_ref, b_ref, *rest):
        ex, out_refs = rest[:ne], rest[ne:ne + no]

        def finish(acc):
            vals = epilogue(acc, *[e[...] for e in ex]) if epilogue is not None else (acc,)
            for r, v in zip(out_refs, vals):
                r[...] = v.astype(r.dtype)

        prod = _bdot(a_ref[...], b_ref[...], dims)
        if nk == 1:
            finish(prod)
        else:
            acc_ref = rest[ne + no]
            k = pl.program_id(2)

            @pl.when(k == 0)
            def _():
                acc_ref[...] = prod

            @pl.when(k > 0)
            def _():
                acc_ref[...] += prod

            @pl.when(k == nk - 1)
            def _():
                finish(acc_ref[...])

    res = pl.pallas_call(
        body, name=name, grid=grid,
        in_specs=[a_spec, b_spec, *extra_specs],
        out_specs=list(out_specs),
        out_shape=list(outs),
        scratch_shapes=[pltpu.VMEM(acc_shape, F32)] if nk > 1 else [],
        compiler_params=_params("parallel", "parallel", "arbitrary"),
    )(a, b, *extras)
    return res


def _sds(shape, dtype):
    return jax.ShapeDtypeStruct(tuple(shape), dtype)


def _tile(n, want):
    t = min(n, want)
    assert n % t == 0, (n, want)
    return t


def _rms_fwd(name, h, g):
    s, d = h.shape
    tm = _tile(s, 512)

    def body(h_ref, g_ref, y_ref):
        y_ref[...] = _rms(h_ref[...], g_ref[...]).astype(BF16)

    return pl.pallas_call(
        body, name=name, grid=(s // tm,),
        in_specs=[pl.BlockSpec((tm, d), lambda i: (i, 0)), pl.BlockSpec((1, d), lambda i: (0, 0))],
        out_specs=pl.BlockSpec((tm, d), lambda i: (i, 0)),
        out_shape=_sds((s, d), BF16),
        compiler_params=_params("parallel"),
    )(h, g)


def _rms_bwd(name, h, g, dy, dh_in):
    s, d = h.shape
    tm = _tile(s, 256)

    def body(h_ref, g_ref, dy_ref, dhin_ref, dh_ref, dg_ref):
        _, vjp = jax.vjp(_rms, h_ref[...], g_ref[...])
        dh, dg = vjp(dy_ref[...].astype(F32))
        dh_ref[...] = dhin_ref[...] + dh

        @pl.when(pl.program_id(0) == 0)
        def _():
            dg_ref[...] = jnp.zeros_like(dg_ref)

        dg_ref[...] += dg

    tok = pl.BlockSpec((tm, d), lambda i: (i, 0))
    vec = pl.BlockSpec((1, d), lambda i: (0, 0))
    return pl.pallas_call(
        body, name=name, grid=(s // tm,),
        in_specs=[tok, vec, tok, tok],
        out_specs=[tok, vec],
        out_shape=[_sds((s, d), F32), _sds((1, d), F32)],
        compiler_params=_params("arbitrary"),
    )(h, g, dy, dh_in)


def _loss_head(h, g, target):
    s, d = h.shape
    tm = _tile(s, 256)

    def body(h_ref, g_ref, t_ref, loss_ref, dh_ref, dg_ref):
        y, vjp = jax.vjp(_rms, h_ref[...], g_ref[...])
        err = y - t_ref[...]
        dh, dg = vjp(err * (1.0 / d))
        dh_ref[...] = dh
        per_token = jnp.mean(err * err, axis=-1, keepdims=True)
        part = 0.5 * jnp.sum(per_token, axis=0, keepdims=True)

        @pl.when(pl.program_id(0) == 0)
        def _():
            dg_ref[...] = jnp.zeros_like(dg_ref)
            loss_ref[...] = jnp.zeros_like(loss_ref)

        dg_ref[...] += dg
        loss_ref[...] += part

    tok = pl.BlockSpec((tm, d), lambda i: (i, 0))
    vec = pl.BlockSpec((1, d), lambda i: (0, 0))
    return pl.pallas_call(
        body, name="loss_head", grid=(s // tm,),
        in_specs=[tok, vec, tok],
        out_specs=[pl.BlockSpec((1, 1), lambda i: (0, 0)), tok, vec],
        out_shape=[_sds((1, 1), F32), _sds((s, d), F32), _sds((1, d), F32)],
        compiler_params=_params("arbitrary"),
    )(h, g, target)


def _ple_bwd(name, dh, pg_pre, pp):
    s, d = dh.shape
    tm = _tile(s, 512)

    def body(dh_ref, pg_ref, pp_ref, dpg_ref, dpp_ref):
        sg = jax.nn.sigmoid(pg_ref[...])
        dhv = dh_ref[...]
        dpg_ref[...] = (dhv * pp_ref[...] * sg * (1.0 - sg)).astype(BF16)
        dpp_ref[...] = (dhv * sg).astype(BF16)

    tok = pl.BlockSpec((tm, d), lambda i: (i, 0))
    return pl.pallas_call(
        body, name=name, grid=(s // tm,),
        in_specs=[tok, tok, tok], out_specs=[tok, tok],
        out_shape=[_sds((s, d), BF16), _sds((s, d), BF16)],
        compiler_params=_params("parallel"),
    )(dh, pg_pre, pp)


_U, _V, _HB, _BG, _CG, _CA, _CGT, _PD = (GROUP * i for i in range(8))


def _cols(c0, w=GROUP):
    return slice(c0, c0 + w)


def _mixer_param_specs():
    full = lambda *shape: pl.BlockSpec(shape, lambda i: (0,) * len(shape))
    return [
        full(4, 1, HEAD), full(4, 1, HEAD), full(4, HEAD, HEAD), full(4, HEAD, 1),
        full(8, GROUP), full(32, GROUP), full(1, GROUP), full(1, GROUP), full(1, GROUP),
        full(4, HEAD, HEAD), full(1, GROUP),
    ]


def _mixer_fwd(name, z, prm):
    s = z.shape[0]
    t = _tile(s, 256)
    hb = t // HALO

    def body(zp_ref, zm_ref, lg_ref, lb_ref, sw_ref, sb_ref, scw_ref, cfw_ref, cfb_ref, cg_ref, cb_ref,
             pw_ref, ps_ref, o_ref, ext_ref):
        i = pl.program_id(0)
        keep = (i > 0).astype(F32)
        main = pl.ds(HALO, t)

        for n in range(t // HEAD):
            rows = slice(n * HEAD, (n + 1) * HEAD)
            for hh in range(4):
                cu = _cols(_U + hh * HEAD, HEAD)
                cv = _cols(_V + hh * HEAD, HEAD)
                o_ref[rows, _cols(hh * HEAD, HEAD)] = _sgu_piece(
                    zm_ref[rows, cu], zm_ref[rows, cv], lg_ref[hh], lb_ref[hh], sw_ref[hh], sb_ref[hh]
                ).astype(BF16)

        ext_ref[0:HALO, :] = zp_ref[:, _cols(_CG)] * zp_ref[:, _cols(_HB)] * keep
        ext_ref[main, :] = zm_ref[:, _cols(_CG)] * zm_ref[:, _cols(_HB)]
        y = jnp.zeros((t, GROUP), F32)
        for k in range(SHORT_K):
            y = y + scw_ref[k:k + 1, :] * ext_ref[pl.ds(HALO - (SHORT_K - 1) + k, t), :]
        o_ref[:, _cols(GROUP)] = (zm_ref[:, _cols(_BG)] * y).astype(BF16)

        ext_ref[0:HALO, :] = zp_ref[:, _cols(_CA)] * jax.nn.sigmoid(zp_ref[:, _cols(_CGT)]) * keep
        ext_ref[main, :] = zm_ref[:, _cols(_CA)] * jax.nn.sigmoid(zm_ref[:, _cols(_CGT)])
        c = jnp.zeros((t, GROUP), F32) + cfb_ref[...]
        for k in range(CONF_K):
            c = c + cfw_ref[k:k + 1, :] * ext_ref[pl.ds(HALO - (CONF_K - 1) + k, t), :]
        o_ref[:, _cols(2 * GROUP)] = _conf_post(c, cg_ref[...], cb_ref[...]).astype(BF16)

        ext_ref[0:HALO, :] = zp_ref[:, _cols(_PD)] * keep
        ext_ref[main, :] = zm_ref[:, _cols(_PD)]
        for gi, w in enumerate(POOL_WINDOWS):
            cc = _cols(gi * HEAD, HEAD)
            acc = ext_ref[main, cc]
            for j in range(1, w):
                acc = acc + ext_ref[pl.ds(HALO - j, t), cc]
            q = acc / _pool_count(i * t, t, w) - ext_ref[main, cc]
            o_ref[:, _cols(3 * GROUP + gi * HEAD, HEAD)] = (_bdot(q, pw_ref[gi]) * ps_ref[:, cc]).astype(BF16)

    return pl.pallas_call(
        body, name=name, grid=(s // t,),
        in_specs=[pl.BlockSpec((HALO, 8 * GROUP), lambda i: (jnp.maximum(i * hb - 1, 0), 0)),
                  pl.BlockSpec((t, 8 * GROUP), lambda i: (i, 0)),
                  *_mixer_param_specs()],
        out_specs=pl.BlockSpec((t, 4 * GROUP), lambda i: (i, 0)),
        out_shape=_sds((s, 4 * GROUP), BF16),
        scratch_shapes=[pltpu.VMEM((HALO + t, GROUP), F32)],
        compiler_params=_params("parallel"),
    )(z, z, *prm)


def _mixer_bwd(name, z, dmo, prm):
    s = z.shape[0]
    t = _tile(s, 256)
    hb = t // HALO
    nt = s // t
    last_halo = s // HALO - 1

    def body(zp_ref, zm_ref, zn_ref, dm_ref, dn_ref,
             lg_ref, lb_ref, sw_ref, sb_ref, scw_ref, cfw_ref, cfb_ref, cg_ref, cb_ref, pw_ref, ps_ref,
             dz_ref, dlg_ref, dlb_ref, dsw_ref, dsb_ref, dscw_ref, dcfw_ref, dcfb_ref, dcg_ref, dcb_ref,
             dpw_ref, dps_ref, extp_ref, extn_ref, extc_ref):
        i = pl.program_id(0)
        keep_prev = (i > 0).astype(F32)
        keep_next = (i < nt - 1).astype(F32)
        main = pl.ds(HALO, t)

        @pl.when(i == 0)
        def _():
            for r in (dlg_ref, dlb_ref, dsw_ref, dsb_ref, dscw_ref, dcfw_ref, dcfb_ref, dcg_ref, dcb_ref,
                      dpw_ref, dps_ref):
                r[...] = jnp.zeros_like(r)

        def rowsum(v):
            return jnp.sum(v, axis=0, keepdims=True)

        for n in range(t // HEAD):
            rows = slice(n * HEAD, (n + 1) * HEAD)
            for hh in range(4):
                cu = _cols(_U + hh * HEAD, HEAD)
                cv = _cols(_V + hh * HEAD, HEAD)
                _, vjp = jax.vjp(_sgu_piece, zm_ref[rows, cu], zm_ref[rows, cv],
                                 lg_ref[hh], lb_ref[hh], sw_ref[hh], sb_ref[hh])
                dzu, dzv, dlg, dlb, dsw, dsb = vjp(dm_ref[rows, _cols(hh * HEAD, HEAD)].astype(F32))
                dz_ref[rows, cu] = dzu.astype(BF16)
                dz_ref[rows, cv] = dzv.astype(BF16)
                dlg_ref[hh] += dlg
                dlb_ref[hh] += dlb
                dsw_ref[hh] += dsw
                dsb_ref[hh] += dsb

        extp_ref[0:HALO, :] = zp_ref[:, _cols(_CG)] * zp_ref[:, _cols(_HB)] * keep_prev
        extp_ref[main, :] = zm_ref[:, _cols(_CG)] * zm_ref[:, _cols(_HB)]
        dob = dm_ref[:, _cols(GROUP)].astype(F32)
        dy = dob * zm_ref[:, _cols(_BG)]
        extn_ref[0:t, :] = dy
        extn_ref[t:t + HALO, :] = dn_ref[:, _cols(GROUP)].astype(F32) * zn_ref[:, _cols(_BG)] * keep_next
        y = jnp.zeros((t, GROUP), F32)
        dx = jnp.zeros((t, GROUP), F32)
        for k in range(SHORT_K):
            xs = extp_ref[pl.ds(HALO - (SHORT_K - 1) + k, t), :]
            y = y + scw_ref[k:k + 1, :] * xs
            dscw_ref[k:k + 1, :] += rowsum(dy * xs)
            dx = dx + scw_ref[k:k + 1, :] * extn_ref[pl.ds((SHORT_K - 1) - k, t), :]
        dz_ref[:, _cols(_BG)] = (dob * y).astype(BF16)
        dz_ref[:, _cols(_CG)] = (dx * zm_ref[:, _cols(_HB)]).astype(BF16)
        dz_ref[:, _cols(_HB)] = (dx * zm_ref[:, _cols(_CG)]).astype(BF16)

        extc_ref[0:HALO, :] = zp_ref[:, _cols(_CA)] * jax.nn.sigmoid(zp_ref[:, _cols(_CGT)]) * keep_prev
        sg = jax.nn.sigmoid(zm_ref[:, _cols(_CGT)])
        extc_ref[main, :] = zm_ref[:, _cols(_CA)] * sg
        extc_ref[HALO + t:HALO + t + HALO, :] = zn_ref[:, _cols(_CA)] * jax.nn.sigmoid(zn_ref[:, _cols(_CGT)])
        c_main = jnp.zeros((t, GROUP), F32) + cfb_ref[...]
        c_next = jnp.zeros((HALO, GROUP), F32) + cfb_ref[...]
        for k in range(CONF_K):
            wk = cfw_ref[k:k + 1, :]
            c_main = c_main + wk * extc_ref[pl.ds(HALO - (CONF_K - 1) + k, t), :]
            c_next = c_next + wk * extc_ref[pl.ds(HALO + t - (CONF_K - 1) + k, HALO), :]
        _, vjp = jax.vjp(_conf_post, c_main, cg_ref[...], cb_ref[...])
        dc, dcg, dcb = vjp(dm_ref[:, _cols(2 * GROUP)].astype(F32))
        dcg_ref[...] += dcg
        dcb_ref[...] += dcb
        dcfb_ref[...] += rowsum(dc)
        _, vjp_next = jax.vjp(lambda cv: _conf_post(cv, cg_ref[...], cb_ref[...]), c_next)
        (dc_next,) = vjp_next(dn_ref[:, _cols(2 * GROUP)].astype(F32) * keep_next)
        extn_ref[0:t, :] = dc
        extn_ref[t:t + HALO, :] = dc_next
        dhc = jnp.zeros((t, GROUP), F32)
        for k in range(CONF_K):
            dhc = dhc + cfw_ref[k:k + 1, :] * extn_ref[pl.ds((CONF_K - 1) - k, t), :]
            dcfw_ref[k:k + 1, :] += rowsum(dc * extc_ref[pl.ds(HALO - (CONF_K - 1) + k, t), :])
        dz_ref[:, _cols(_CA)] = (dhc * sg).astype(BF16)
        dz_ref[:, _cols(_CGT)] = (dhc * zm_ref[:, _cols(_CA)] * sg * (1.0 - sg)).astype(BF16)

        extp_ref[0:HALO, :] = zp_ref[:, _cols(_PD)] * keep_prev
        extp_ref[main, :] = zm_ref[:, _cols(_PD)]
        for gi, w in enumerate(POOL_WINDOWS):
            cc = _cols(gi * HEAD, HEAD)
            oc = _cols(3 * GROUP + gi * HEAD, HEAD)
            acc = extp_ref[main, cc]
            for j in range(1, w):
                acc = acc + extp_ref[pl.ds(HALO - j, t), cc]
            count = _pool_count(i * t, t, w)
            q = acc / count - extp_ref[main, cc]
            dod = dm_ref[:, oc].astype(F32)
            dps_ref[:, cc] += rowsum(dod * _bdot(q, pw_ref[gi]))
            ds = dod * ps_ref[:, cc]
            dpw_ref[gi] += _bdot(q, ds, _DIMS["tn"])
            dq = _bdot(ds, pw_ref[gi], _DIMS["nt"])
            ds_next = dn_ref[:, oc].astype(F32) * ps_ref[:, cc] * keep_next
            dq_next = _bdot(ds_next, pw_ref[gi], _DIMS["nt"])
            extn_ref[0:t, cc] = dq / count
            extn_ref[t:t + HALO, cc] = dq_next * (1.0 / w)
            back = extn_ref[0:t, cc]
            for j in range(1, w):
                back = back + extn_ref[pl.ds(j, t), cc]
            dz_ref[:, _cols(_PD + gi * HEAD, HEAD)] = (back - dq).astype(BF16)

    full = lambda *shape: pl.BlockSpec(shape, lambda i: (0,) * len(shape))
    grad_specs = [full(4, 1, HEAD), full(4, 1, HEAD), full(4, HEAD, HEAD), full(4, HEAD, 1),
                  full(8, GROUP), full(32, GROUP), full(1, GROUP), full(1, GROUP), full(1, GROUP),
                  full(4, HEAD, HEAD), full(1, GROUP)]
    grad_shapes = [_sds(sp.block_shape, F32) for sp in grad_specs]
    nxt = lambda i: (jnp.minimum((i + 1) * hb, last_halo), 0)
    return pl.pallas_call(
        body, name=name, grid=(nt,),
        in_specs=[pl.BlockSpec((HALO, 8 * GROUP), lambda i: (jnp.maximum(i * hb - 1, 0), 0)),
                  pl.BlockSpec((t, 8 * GROUP), lambda i: (i, 0)),
                  pl.BlockSpec((HALO, 8 * GROUP), nxt),
                  pl.BlockSpec((t, 4 * GROUP), lambda i: (i, 0)),
                  pl.BlockSpec((HALO, 4 * GROUP), nxt),
                  *_mixer_param_specs()],
        out_specs=[pl.BlockSpec((t, 8 * GROUP), lambda i: (i, 0)), *grad_specs],
        out_shape=[_sds((s, 8 * GROUP), BF16), *grad_shapes],
        scratch_shapes=[pltpu.VMEM((HALO + t, GROUP), F32), pltpu.VMEM((t + HALO, GROUP), F32),
                        pltpu.VMEM((HALO + t + HALO, GROUP), F32)],
        compiler_params=_params("arbitrary"),
    )(z, z, z, dmo, dmo, *prm)


def _layer_fwd(li, h, p_all, gw, sm):
    s, d = h.shape
    f8 = gw["wg"].shape[2]
    pd = p_all.shape[2]
    nb = N_DEV
    tm = _tile(s, 1024)
    tn = _tile(d, 512)
    sv = {"h": h}

    y1 = _rms_fwd("rms_mix", h, sm["norm_mix_g"])
    (z,) = _matmul(
        "w_in_fwd", "nn", (s // tm, nb, 1),
        y1, pl.BlockSpec((tm, d), lambda i, j, k: (i, 0)),
        gw["win"], pl.BlockSpec((None, d, GROUP), lambda i, j, k: (j, 0, 0)),
        [_sds((s, nb * GROUP), F32)], [pl.BlockSpec((tm, GROUP), lambda i, j, k: (i, j))], None)
    mo = _mixer_fwd("mixer_fwd", z, sm["mixer"])
    (h2,) = _matmul(
        "w_out_fwd", "nn", (s // tm, d // tn, 1),
        mo, pl.BlockSpec((tm, d), lambda i, j, k: (i, 0)),
        gw["wout"], pl.BlockSpec((d, tn), lambda i, j, k: (0, j)),
        [_sds((s, d), F32)], [pl.BlockSpec((tm, tn), lambda i, j, k: (i, j))], None,
        extras=[h], extra_specs=[pl.BlockSpec((tm, tn), lambda i, j, k: (i, j))],
        epilogue=lambda acc, hv: (hv + acc,))
    sv.update(y1=y1, z=z, mo=mo, h2=h2)

    y2 = _rms_fwd("rms_ffn", h2, sm["norm_ffn_g"])
    gate_pre, up_pre, hmid = _ffn_up("ffn_up_fwd", y2, gw["wg"], gw["wu"])
    tnd = _tile(d, 1024)
    (h3,) = _matmul(
        "w_down_fwd", "nn", (s // tm, d // tnd, nb),
        hmid, pl.BlockSpec((None, tm, f8), lambda i, j, k: (k, i, 0)),
        gw["wd"], pl.BlockSpec((None, f8, tnd), lambda i, j, k: (k, 0, j)),
        [_sds((s, d), F32)], [pl.BlockSpec((tm, tnd), lambda i, j, k: (i, j))], (tm, tnd),
        extras=[h2], extra_specs=[pl.BlockSpec((tm, tnd), lambda i, j, k: (i, j))],
        epilogue=lambda acc, hv: (hv + acc,))
    sv.update(y2=y2, gate_pre=gate_pre, up_pre=up_pre, hmid=hmid, h3=h3)

    y3 = _rms_fwd("rms_ple", h3, sm["norm_ple_g"])
    (pp,) = _matmul(
        "w_ple_proj_fwd", "nn", (s // tm, nb, 1),
        p_all, pl.BlockSpec((None, tm, pd), lambda i, j, k: (li, i, 0)),
        gw["wpp"], pl.BlockSpec((None, pd, pd), lambda i, j, k: (j, 0, 0)),
        [_sds((s, d), F32)], [pl.BlockSpec((tm, pd), lambda i, j, k: (i, j))], None)
    h4, pg_pre = _matmul(
        "w_ple_gate_fwd", "nn", (s // tm, d // tn, 1),
        y3, pl.BlockSpec((tm, d), lambda i, j, k: (i, 0)),
        gw["wpg"], pl.BlockSpec((d, tn), lambda i, j, k: (0, j)),
        [_sds((s, d), F32), _sds((s, d), F32)],
        [pl.BlockSpec((tm, tn), lambda i, j, k: (i, j))] * 2, None,
        extras=[h3, pp], extra_specs=[pl.BlockSpec((tm, tn), lambda i, j, k: (i, j))] * 2,
        epilogue=lambda acc, hv, ppv: (hv + jax.nn.sigmoid(acc) * ppv, acc))
    sv.update(y3=y3, pp=pp, pg_pre=pg_pre)
    return h4, sv


def _ffn_up(name, y2, wg, wu):
    s, d = y2.shape
    nb, _, f8 = wg.shape
    tm = _tile(s, 512)

    def body(y_ref, wg_ref, wu_ref, g_ref, u_ref, m_ref):
        yv = y_ref[...]
        g = _bdot(yv, wg_ref[...])
        u = _bdot(yv, wu_ref[...])
        g_ref[...] = g
        u_ref[...] = u
        m_ref[...] = (jax.nn.silu(g) * u).astype(BF16)

    wspec = pl.BlockSpec((None, d, f8), lambda i, j: (j, 0, 0))
    ospec = pl.BlockSpec((None, tm, f8), lambda i, j: (j, i, 0))
    return pl.pallas_call(
        body, name=name, grid=(s // tm, nb),
        in_specs=[pl.BlockSpec((tm, d), lambda i, j: (i, 0)), wspec, wspec],
        out_specs=[ospec, ospec, ospec],
        out_shape=[_sds((nb, s, f8), F32), _sds((nb, s, f8), F32), _sds((nb, s, f8), BF16)],
        compiler_params=_params("parallel", "parallel"),
    )(y2, wg, wu)


def _swiglu_bwd(acc, g, u):
    sg = jax.nn.sigmoid(g)
    silu = g * sg
    return acc * u * (sg + silu * (1.0 - sg)), acc * silu


def _layer_bwd(li, dh, p_all, gw, sm, sv):
    s, d = dh.shape
    f8 = gw["wg"].shape[2]
    pd = p_all.shape[2]
    nb = N_DEV
    tm = _tile(s, 1024)
    tn = _tile(d, 512)
    tt = _tile(s, 512)
    tk1 = _tile(d, 1024)
    gr, sg = {}, {}
    tile_ij = lambda r, c: pl.BlockSpec((r, c), lambda i, j, k: (i, j))

    dpg, dpp = _ple_bwd("ple_bwd", dh, sv["pg_pre"], sv["pp"])
    (gr["wpp"],) = _matmul(
        "w_ple_proj_dw", "tn", (1, nb, s // tt),
        p_all, pl.BlockSpec((None, tt, pd), lambda i, j, k: (li, k, 0)),
        dpp, pl.BlockSpec((tt, pd), lambda i, j, k: (k, j)),
        [_sds((nb, pd, pd), F32)], [pl.BlockSpec((None, pd, pd), lambda i, j, k: (j, 0, 0))], (pd, pd))
    (gr["wpg"],) = _matmul(
        "w_ple_gate_dw", "tn", (d // tk1, d // tk1, s // tt),
        sv["y3"], pl.BlockSpec((tt, tk1), lambda i, j, k: (k, i)),
        dpg, pl.BlockSpec((tt, tk1), lambda i, j, k: (k, j)),
        [_sds((d, d), F32)], [tile_ij(tk1, tk1)], (tk1, tk1))
    (dy3,) = _matmul(
        "w_ple_gate_dx", "nt", (s // tm, d // tn, 1),
        dpg, pl.BlockSpec((tm, d), lambda i, j, k: (i, 0)),
        gw["wpg"], pl.BlockSpec((tn, d), lambda i, j, k: (j, 0)),
        [_sds((s, d), F32)], [tile_ij(tm, tn)], None)
    dh3, sg["norm_ple_g"] = _rms_bwd("rms_ple_bwd", sv["h3"], sm["norm_ple_g"], dy3, dh)

    tms = _tile(s, 512)
    dgate, dup = _matmul(
        "w_down_dx", "nt", (s // tms, nb, 1),
        dh3, pl.BlockSpec((tms, d), lambda i, j, k: (i, 0)),
        gw["wd"], pl.BlockSpec((None, f8, d), lambda i, j, k: (j, 0, 0)),
        [_sds((nb, s, f8), BF16)] * 2, [pl.BlockSpec((None, tms, f8), lambda i, j, k: (j, i, 0))] * 2, None,
        extras=[sv["gate_pre"], sv["up_pre"]],
        extra_specs=[pl.BlockSpec((None, tms, f8), lambda i, j, k: (j, i, 0))] * 2,
        epilogue=_swiglu_bwd)
    (gr["wd"],) = _matmul(
        "w_down_dw", "tn", (nb, d // tk1, s // tt),
        sv["hmid"], pl.BlockSpec((None, tt, f8), lambda i, j, k: (i, k, 0)),
        dh3, pl.BlockSpec((tt, tk1), lambda i, j, k: (k, j)),
        [_sds((nb, f8, d), F32)], [pl.BlockSpec((None, f8, tk1), lambda i, j, k: (i, 0, j))], (f8, tk1))
    for nm, dact in (("wg", dgate), ("wu", dup)):
        (gr[nm],) = _matmul(
            "w_" + nm[1:] + "_dw", "tn", (d // tk1, nb, s // tt),
            sv["y2"], pl.BlockSpec((tt, tk1), lambda i, j, k: (k, i)),
            dact, pl.BlockSpec((None, tt, f8), lambda i, j, k: (j, k, 0)),
            [_sds((nb, d, f8), F32)], [pl.BlockSpec((None, tk1, f8), lambda i, j, k: (j, i, 0))], (tk1, f8))
    tnd = _tile(d, 1024)
    (dy2,) = _matmul(
        "w_gate_dx", "nt", (s // tm, d // tnd, nb),
        dgate, pl.BlockSpec((None, tm, f8), lambda i, j, k: (k, i, 0)),
        gw["wg"], pl.BlockSpec((None, tnd, f8), lambda i, j, k: (k, j, 0)),
        [_sds((s, d), F32)], [tile_ij(tm, tnd)], (tm, tnd))
    (dy2,) = _matmul(
        "w_up_dx", "nt", (s // tm, d // tnd, nb),
        dup, pl.BlockSpec((None, tm, f8), lambda i, j, k: (k, i, 0)),
        gw["wu"], pl.BlockSpec((None, tnd, f8), lambda i, j, k: (k, j, 0)),
        [_sds((s, d), F32)], [tile_ij(tm, tnd)], (tm, tnd),
        extras=[dy2], extra_specs=[tile_ij(tm, tnd)], epilogue=lambda acc, prev: (prev + acc,))
    dh2, sg["norm_ffn_g"] = _rms_bwd("rms_ffn_bwd", sv["h2"], sm["norm_ffn_g"], dy2, dh3)

    (dmo,) = _matmul(
        "w_out_dx", "nt", (s // tm, d // tn, 1),
        dh2, pl.BlockSpec((tm, d), lambda i, j, k: (i, 0)),
        gw["wout"], pl.BlockSpec((tn, d), lambda i, j, k: (j, 0)),
        [_sds((s, d), BF16)], [tile_ij(tm, tn)], None)
    (gr["wout"],) = _matmul(
        "w_out_dw", "tn", (d // tk1, d // tk1, s // tt),
        sv["mo"], pl.BlockSpec((tt, tk1), lambda i, j, k: (k, i)),
        dh2, pl.BlockSpec((tt, tk1), lambda i, j, k: (k, j)),
        [_sds((d, d), F32)], [tile_ij(tk1, tk1)], (tk1, tk1))
    dz, *mix_grads = _mixer_bwd("mixer_bwd", sv["z"], dmo, sm["mixer"])
    sg["mixer"] = mix_grads
    (gr["win"],) = _matmul(
        "w_in_dw", "tn", (d // tk1, nb, s // tt),
        sv["y1"], pl.BlockSpec((tt, tk1), lambda i, j, k: (k, i)),
        dz, pl.BlockSpec((tt, GROUP), lambda i, j, k: (k, j)),
        [_sds((nb, d, GROUP), F32)], [pl.BlockSpec((None, tk1, GROUP), lambda i, j, k: (j, i, 0))], (tk1, GROUP))
    (dy1,) = _matmul(
        "w_in_dx", "nt", (s // tm, d // tnd, nb),
        dz, pl.BlockSpec((tm, GROUP), lambda i, j, k: (i, k)),
        gw["win"], pl.BlockSpec((None, tnd, GROUP), lambda i, j, k: (k, j, 0)),
        [_sds((s, d), F32)], [tile_ij(tm, tnd)], (tm, tnd))
    dh_in, sg["norm_mix_g"] = _rms_bwd("rms_mix_bwd", sv["h"], sm["norm_mix_g"], dy1, dh2)
    return dh_in, gr, sg


def _place():
    return lax.axis_index("x"), lax.axis_index("y"), lax.axis_index("c")


def _gather_layer(li, shards):
    nw = len(shards)

    def body(*refs):
        ins, outs = refs[:nw], refs[nw:2 * nw]
        send_sems, recv_sems, local_sems = refs[2 * nw:]
        x, y, c = _place()
        me, sib = (x, y, c), (x, y, 1 - c)
        chips = [(1 - x, y), (x, 1 - y), (1 - x, 1 - y)]

        def copy(w, k, block, to, src=None):
            px, py, pc = block
            dst = outs[w].at[4 * px + 2 * py + pc]
            return pltpu.make_async_remote_copy(
                src_ref=dst if src is None else src, dst_ref=dst,
                send_sem=send_sems.at[w, k], recv_sem=recv_sems.at[w, k], device_id=to, device_id_type=MESH)

        mine = [pltpu.make_async_copy(ins[w].at[li], outs[w].at[4 * x + 2 * y + c], local_sems.at[w]) for w in range(nw)]
        for cp in mine:
            cp.start()
        sent = []
        for w in range(nw):
            sent.append(copy(w, 0, me, sib, src=ins[w].at[li]))
            sent += [copy(w, 1 + j, me, (*chip, c), src=ins[w].at[li]) for j, chip in enumerate(chips)]
        for cp in sent:
            cp.start()
        for j, chip in enumerate(chips):
            for w in range(nw):
                copy(w, 1 + j, (*chip, c), me).wait_recv()
                fwd = copy(w, 4 + j, (*chip, c), sib)
                fwd.start()
                sent.append(fwd)
        for w in range(nw):
            copy(w, 0, sib, me).wait_recv()
            for j, chip in enumerate(chips):
                copy(w, 4 + j, (*chip, 1 - c), me).wait_recv()
        for cp in sent:
            cp.wait_send()
        for cp in mine:
            cp.wait()

    return pl.pallas_call(
        body, name="gather_weights",
        in_specs=[ANY] * nw, out_specs=[ANY] * nw,
        out_shape=[_sds((N_DEV,) + a.shape[1:], a.dtype) for a in shards],
        scratch_shapes=[pltpu.SemaphoreType.DMA((nw, 7)), pltpu.SemaphoreType.DMA((nw, 7)),
                        pltpu.SemaphoreType.DMA((nw,))],
    )(*shards)


def _exchange_siblings(grads):
    nw = len(grads)

    def body(*refs):
        ins, outs = refs[:nw], refs[nw:2 * nw]
        send_sems, recv_sems = refs[2 * nw:]
        x, y, c = _place()
        copies = []
        for w in range(nw):
            for q in range(4):
                copies.append(pltpu.make_async_remote_copy(
                    src_ref=ins[w].at[2 * q + (1 - c)], dst_ref=outs[w].at[q],
                    send_sem=send_sems.at[w, q], recv_sem=recv_sems.at[w, q],
                    device_id=(x, y, 1 - c), device_id_type=MESH))
        for cp in copies:
            cp.start()
        for cp in copies:
            cp.wait()

    return pl.pallas_call(
        body, name="rs_siblings",
        in_specs=[ANY] * nw, out_specs=[ANY] * nw,
        out_shape=[_sds((4,) + g.shape[1:], g.dtype) for g in grads],
        scratch_shapes=[pltpu.SemaphoreType.DMA((nw, 4)), pltpu.SemaphoreType.DMA((nw, 4))],
    )(*grads)


def _exchange_chips(partials):
    nw = len(partials)
    flips = [(1, 0), (0, 1), (1, 1)]

    def body(*refs):
        ins, outs = refs[:nw], refs[nw:2 * nw]
        send_sems, recv_sems = refs[2 * nw:]
        x, y, c = _place()
        copies = []
        for w in range(nw):
            for r, (fx, fy) in enumerate(flips):
                px = 1 - x if fx else x
                py = 1 - y if fy else y
                copies.append(pltpu.make_async_remote_copy(
                    src_ref=ins[w].at[2 * px + py], dst_ref=outs[w].at[r],
                    send_sem=send_sems.at[w, r], recv_sem=recv_sems.at[w, r],
                    device_id=(px, py, c), device_id_type=MESH))
        for cp in copies:
            cp.start()
        for cp in copies:
            cp.wait()

    return pl.pallas_call(
        body, name="rs_chips",
        in_specs=[ANY] * nw, out_specs=[ANY] * nw,
        out_shape=[_sds((3,) + a.shape[1:], a.dtype) for a in partials],
        scratch_shapes=[pltpu.SemaphoreType.DMA((nw, 3)), pltpu.SemaphoreType.DMA((nw, 3))],
    )(*partials)


def _row_tile(n, want):
    best = None
    for t in range(16, min(n, want) + 1, 16):
        if n % t == 0:
            best = t
    assert best is not None, n
    return best


def _chip_partials(name, where, grad, from_sibling):
    _, r, cdim = grad.shape
    tr = _row_tile(r, 256)

    def body(where_ref, g_ref, s_ref, pb_ref, own_ref):
        tot = g_ref[...] + s_ref[...]
        pb_ref[...] = tot.astype(BF16)

        @pl.when(pl.program_id(1) == where_ref[1])
        def _():
            own_ref[...] = tot

    grid_spec = pltpu.PrefetchScalarGridSpec(
        num_scalar_prefetch=1, grid=(r // tr, 4),
        in_specs=[pl.BlockSpec((None, tr, cdim), lambda i, q, wh: (2 * q + wh[0], i, 0)),
                  pl.BlockSpec((None, tr, cdim), lambda i, q, wh: (q, i, 0))],
        out_specs=[pl.BlockSpec((None, tr, cdim), lambda i, q, wh: (q, i, 0)),
                   pl.BlockSpec((tr, cdim), lambda i, q, wh: (i, 0))])
    return pl.pallas_call(
        body, name=name, grid_spec=grid_spec,
        out_shape=[_sds((4, r, cdim), BF16), _sds((r, cdim), F32)],
        compiler_params=_params("parallel", "arbitrary"),
    )(where, grad, from_sibling)


def _adamw(w, g, m, v):
    m = ADAM_B1 * m + (1.0 - ADAM_B1) * g
    v = ADAM_B2 * v + (1.0 - ADAM_B2) * (g * g)
    m_hat = m / (1.0 - ADAM_B1 ** ADAM_STEP)
    v_hat = v / (1.0 - ADAM_B2 ** ADAM_STEP)
    delta = -ADAM_LR * (m_hat / (jnp.sqrt(v_hat) + ADAM_EPS) + ADAM_WD * w)
    return delta, m, v


def _finish_weight(name, li, own, from_chips, w, m, v):
    r, cdim = own.shape
    tr = _row_tile(r, 256)

    def body(own_ref, fc_ref, w_ref, m_ref, v_ref, g_out, d_out, m_out, v_out):
        g = own_ref[...] + fc_ref[0].astype(F32) + fc_ref[1].astype(F32) + fc_ref[2].astype(F32)
        delta, mn, vn = _adamw(w_ref[...], g, m_ref[...], v_ref[...])
        g_out[...] = g
        d_out[...] = delta
        m_out[...] = mn
        v_out[...] = vn

    tile = pl.BlockSpec((tr, cdim), lambda i: (i, 0))
    lay = pl.BlockSpec((None, tr, cdim), lambda i: (li, i, 0))
    return pl.pallas_call(
        body, name=name, grid=(r // tr,),
        in_specs=[tile, pl.BlockSpec((3, tr, cdim), lambda i: (0, i, 0)), lay, lay, lay],
        out_specs=[tile] * 4, out_shape=[_sds((r, cdim), F32)] * 4,
        compiler_params=_params("parallel"),
    )(own, from_chips, w, m, v)


def _allgather_small(name, v, reduce):
    r = v.shape[0]

    def body(x_ref, out_ref, *rest):
        if reduce:
            sum_ref, send_sems, recv_sems, local_sem = rest
        else:
            send_sems, recv_sems, local_sem = rest
        x, y, c = _place()
        me, sib = (x, y, c), (x, y, 1 - c)
        chips = [(1 - x, y), (x, 1 - y), (1 - x, 1 - y)]

        def rows(px, py, pc):
            return out_ref.at[pl.ds(pl.multiple_of((4 * px + 2 * py + pc) * r, 8), r), :]

        def copy(k, block, to, src=None):
            return pltpu.make_async_remote_copy(
                src_ref=rows(*block) if src is None else src, dst_ref=rows(*block),
                send_sem=send_sems.at[k], recv_sem=recv_sems.at[k], device_id=to, device_id_type=MESH)

        mine = pltpu.make_async_copy(x_ref, rows(*me), local_sem)
        mine.start()
        first = [copy(0, me, sib, src=x_ref)]
        first += [copy(1 + j, me, (*chip, c), src=x_ref) for j, chip in enumerate(chips)]
        for cp in first:
            cp.start()
        passed = [copy(4 + j, (*chip, c), sib) for j, chip in enumerate(chips)]
        for j, chip in enumerate(chips):
            copy(1 + j, (*chip, c), me).wait_recv()
            passed[j].start()
        copy(0, sib, me).wait_recv()
        for j, chip in enumerate(chips):
            copy(4 + j, (*chip, 1 - c), me).wait_recv()
        for cp in first + passed:
            cp.wait_send()
        mine.wait()
        if reduce:
            tot = out_ref[0:r, :]
            for d in range(1, N_DEV):
                tot = tot + out_ref[d * r:(d + 1) * r, :]
            sum_ref[...] = tot

    vm = pl.BlockSpec(memory_space=pltpu.VMEM)
    outs = [_sds((N_DEV * r, 128), F32)] + ([_sds((r, 128), F32)] if reduce else [])
    res = pl.pallas_call(
        body, name=name,
        in_specs=[vm], out_specs=[vm] * len(outs), out_shape=outs,
        scratch_shapes=[pltpu.SemaphoreType.DMA((7,)), pltpu.SemaphoreType.DMA((7,)), pltpu.SemaphoreType.DMA],
        compiler_params=pltpu.CompilerParams(vmem_limit_bytes=VMEM_LIMIT_BYTES),
    )(v)
    return res


def _adamw_small(name, w, g, m, v):
    def body(w_ref, g_ref, m_ref, v_ref, d_out, m_out, v_out):
        delta, mn, vn = _adamw(w_ref[...], g_ref[...], m_ref[...], v_ref[...])
        d_out[...] = delta
        m_out[...] = mn
        v_out[...] = vn

    vm = pl.BlockSpec(memory_space=pltpu.VMEM)
    return pl.pallas_call(
        body, name=name, in_specs=[vm] * 4, out_specs=[vm] * 3, out_shape=[_sds(w.shape, F32)] * 3,
        compiler_params=pltpu.CompilerParams(vmem_limit_bytes=VMEM_LIMIT_BYTES),
    )(w, g, m, v)


def _pack(arrays):
    flat, layout, off = [], [], 0
    for a in arrays:
        flat.append(a.reshape(-1).astype(F32))
        layout.append((off, a.shape))
        off += a.size
    total = -(-off // 1024) * 1024
    if total > off:
        flat.append(jnp.zeros((total - off,), F32))
    return jnp.concatenate(flat).reshape(total // 128, 128), layout


def _unpack(packed, layout):
    flat = packed.reshape(-1)
    return [flat[off:off + math.prod(shape)].reshape(shape) for off, shape in layout]


_BIG = ("win", "wout", "wg", "wu", "wd", "wpg", "wpp")
_BIG_FULL = {"win": "w_in", "wout": "w_out", "wg": "w_gate", "wu": "w_up", "wd": "w_down",
             "wpg": "w_ple_gate", "wpp": "w_ple_proj"}
_SMALL_REPLICATED = ("norm_mix_g", "sgu_ln_g", "sgu_ln_b", "sgu_w", "sgu_b", "cf_conv_b", "cf_ln_g", "cf_ln_b",
                     "pool_w", "pool_scale", "norm_ffn_g", "norm_ple_g", "final_norm_g")
_SMALL_SHARDED = ("sc_conv_w", "cf_conv_w")
_WEIGHTS = ("norm_mix_g", "w_in", "sgu_ln_g", "sgu_ln_b", "sgu_w", "sgu_b", "sc_conv_w", "cf_conv_w", "cf_conv_b",
            "cf_ln_g", "cf_ln_b", "pool_w", "pool_scale", "w_out", "norm_ffn_g", "w_gate", "w_up", "w_down",
            "norm_ple_g", "w_ple_gate", "w_ple_proj", "final_norm_g")


def _pad_rows(a, rows):
    return jnp.concatenate([a, jnp.zeros((rows - a.shape[0],) + a.shape[1:], a.dtype)], axis=0)


def _mixer_params(li, W, sc_full, cf_full):
    return [W["sgu_ln_g"][li][:, None, :], W["sgu_ln_b"][li][:, None, :], W["sgu_w"][li], W["sgu_b"][li][:, :, None],
            _pad_rows(sc_full[li], 8), _pad_rows(cf_full[li], 32),
            W["cf_conv_b"][li][None, :], W["cf_ln_g"][li][None, :], W["cf_ln_b"][li][None, :],
            W["pool_w"][li], W["pool_scale"][li][None, :]]


def _step(W, M, V, x, p, loss_target):
    n_layers = W["w_in"].shape[0]
    h = x[0]
    target = loss_target[0]
    p_all = p[:, 0]
    xi, yi, ci = _place()
    blk = 4 * xi + 2 * yi + ci
    where = jnp.stack([ci, 2 * xi + yi]).astype(jnp.int32)
    csh = W["sc_conv_w"].shape[2]

    packed, lay = _pack([W[n] for n in _SMALL_SHARDED])
    (taps,) = _allgather_small("gather_conv_taps", packed, reduce=False)
    per_dev = [_unpack(taps[d * packed.shape[0]:(d + 1) * packed.shape[0]], lay) for d in range(N_DEV)]
    sc_full = jnp.concatenate([pd[0] for pd in per_dev], axis=-1)
    cf_full = jnp.concatenate([pd[1] for pd in per_dev], axis=-1)

    big16 = [W[_BIG_FULL[n]].astype(BF16) for n in _BIG]
    d = h.shape[1]

    saved, gathered, smalls = [], [], []
    for li in range(n_layers):
        g = dict(zip(_BIG, _gather_layer(li, big16)))
        g["wout"] = g["wout"].reshape(d, d)
        g["wpg"] = g["wpg"].reshape(d, d)
        sm = {"norm_mix_g": W["norm_mix_g"][li][None, :], "norm_ffn_g": W["norm_ffn_g"][li][None, :],
              "norm_ple_g": W["norm_ple_g"][li][None, :], "mixer": _mixer_params(li, W, sc_full, cf_full)}
        h, sv = _layer_fwd(li, h, p_all, g, sm)
        saved.append(sv)
        gathered.append(g)
        smalls.append(sm)
    loss, dh, d_final_g = _loss_head(h, W["final_norm_g"][None, :], target)

    big_out = {n: [] for n in _BIG}
    small_grads = []
    for li in reversed(range(n_layers)):
        dh, gr, sg = _layer_bwd(li, dh, p_all, gathered[li], smalls[li], saved[li])
        small_grads.append(sg)
        gr["wout"] = gr["wout"].reshape(N_DEV, d // N_DEV, d)
        gr["wpg"] = gr["wpg"].reshape(N_DEV, d // N_DEV, d)
        grads = [gr[n] for n in _BIG]
        from_sib = _exchange_siblings(grads)
        parts = [_chip_partials("rs_sum_" + n, where, g_, s_) for n, g_, s_ in zip(_BIG, grads, from_sib)]
        from_chips = _exchange_chips([pb for pb, _ in parts])
        for n, (_, own), fc in zip(_BIG, parts, from_chips):
            full = _BIG_FULL[n]
            big_out[n].append(_finish_weight("adamw_" + n, li, own, fc, W[full], M[full], V[full]))
    small_grads.reverse()
    for n in _BIG:
        big_out[n].reverse()

    def stacked(fn):
        return jnp.stack([fn(sg) for sg in small_grads])

    mix = lambda i: (lambda sg: sg["mixer"][i])
    grads_small = {
        "norm_mix_g": stacked(lambda sg: sg["norm_mix_g"][0]),
        "sgu_ln_g": stacked(mix(0))[:, :, 0, :], "sgu_ln_b": stacked(mix(1))[:, :, 0, :],
        "sgu_w": stacked(mix(2)), "sgu_b": stacked(mix(3))[:, :, :, 0],
        "sc_conv_w": stacked(mix(4))[:, :SHORT_K], "cf_conv_w": stacked(mix(5))[:, :CONF_K],
        "cf_conv_b": stacked(mix(6))[:, 0], "cf_ln_g": stacked(mix(7))[:, 0], "cf_ln_b": stacked(mix(8))[:, 0],
        "pool_w": stacked(mix(9)), "pool_scale": stacked(mix(10))[:, 0],
        "norm_ffn_g": stacked(lambda sg: sg["norm_ffn_g"][0]), "norm_ple_g": stacked(lambda sg: sg["norm_ple_g"][0]),
        "final_norm_g": d_final_g[0],
    }
    order = _SMALL_REPLICATED + _SMALL_SHARDED
    packed, lay = _pack([grads_small[n] for n in order] + [loss])
    _, summed = _allgather_small("allreduce_small", packed, reduce=True)
    total = dict(zip(order + ("loss",), _unpack(summed, lay)))
    loss_all = total["loss"][0, 0]

    out_g, out_d, out_m, out_v = {}, {}, {}, {}
    pw, lay_r = _pack([W[n] for n in _SMALL_REPLICATED])
    pg, _ = _pack([total[n] for n in _SMALL_REPLICATED])
    pm, _ = _pack([M[n] for n in _SMALL_REPLICATED])
    pv, _ = _pack([V[n] for n in _SMALL_REPLICATED])
    dd, mm, vv = _adamw_small("adamw_small", pw, pg, pm, pv)
    for n, a, b, c_ in zip(_SMALL_REPLICATED, _unpack(dd, lay_r), _unpack(mm, lay_r), _unpack(vv, lay_r)):
        out_g[n], out_d[n], out_m[n], out_v[n] = total[n], a, b, c_
    mine = {n: lax.dynamic_slice_in_dim(total[n], blk * csh, csh, axis=2) for n in _SMALL_SHARDED}
    pw, lay_s = _pack([W[n] for n in _SMALL_SHARDED])
    pg, _ = _pack([mine[n] for n in _SMALL_SHARDED])
    pm, _ = _pack([M[n] for n in _SMALL_SHARDED])
    pv, _ = _pack([V[n] for n in _SMALL_SHARDED])
    dd, mm, vv = _adamw_small("adamw_conv_taps", pw, pg, pm, pv)
    for n, a, b, c_ in zip(_SMALL_SHARDED, _unpack(dd, lay_s), _unpack(mm, lay_s), _unpack(vv, lay_s)):
        out_g[n], out_d[n], out_m[n], out_v[n] = mine[n], a, b, c_
    for n in _BIG:
        for k, dst in enumerate((out_g, out_d, out_m, out_v)):
            dst[_BIG_FULL[n]] = jnp.stack([per_layer[k] for per_layer in big_out[n]])

    return (loss_all, dh[None], *[out_g[n] for n in _WEIGHTS], *[out_d[n] for n in _WEIGHTS],
            *[out_m[n] for n in _WEIGHTS], *[out_v[n] for n in _WEIGHTS])


def kernel(x, p, norm_mix_g, w_in, sgu_ln_g, sgu_ln_b, sgu_w, sgu_b, sc_conv_w, cf_conv_w, cf_conv_b, cf_ln_g, cf_ln_b, pool_w, pool_scale, w_out, norm_ffn_g, w_gate, w_up, w_down, norm_ple_g, w_ple_gate, w_ple_proj, final_norm_g, loss_target, m_norm_mix_g, m_w_in, m_sgu_ln_g, m_sgu_ln_b, m_sgu_w, m_sgu_b, m_sc_conv_w, m_cf_conv_w, m_cf_conv_b, m_cf_ln_g, m_cf_ln_b, m_pool_w, m_pool_scale, m_w_out, m_norm_ffn_g, m_w_gate, m_w_up, m_w_down, m_norm_ple_g, m_w_ple_gate, m_w_ple_proj, m_final_norm_g, v_norm_mix_g, v_w_in, v_sgu_ln_g, v_sgu_ln_b, v_sgu_w, v_sgu_b, v_sc_conv_w, v_cf_conv_w, v_cf_conv_b, v_cf_ln_g, v_cf_ln_b, v_pool_w, v_pool_scale, v_w_out, v_norm_ffn_g, v_w_gate, v_w_up, v_w_down, v_norm_ple_g, v_w_ple_gate, v_w_ple_proj, v_final_norm_g):
    given = dict(locals())
    W = {n: given[n] for n in _WEIGHTS}
    M = {n: given["m_" + n] for n in _WEIGHTS}
    V = {n: given["v_" + n] for n in _WEIGHTS}
    return _step(W, M, V, x, p, loss_target)
```

```python
import functools
import math

import jax
import jax.numpy as jnp
from jax import lax
from jax.experimental import pallas as pl
from jax.experimental.pallas import tpu as pltpu

F32 = jnp.float32
BF16 = jnp.bfloat16
EPS = 1e-6
HEAD = 128
GROUP = 4 * HEAD
HALO = 32
SHORT_K = 3
CONF_K = 31
POOL_WINDOWS = (2, 4, 8, 16)
N_DEV = 8
MESH = pl.DeviceIdType.MESH
VMEM_LIMIT_BYTES = 56 * 1024 * 1024

ADAM_LR = 0.001
ADAM_B1 = 0.9
ADAM_B2 = 0.999
ADAM_EPS = 1e-08
ADAM_WD = 0.01
ADAM_STEP = 10

ANY = pl.BlockSpec(memory_space=pl.ANY)


def _params(*sem):
    return pltpu.CompilerParams(dimension_semantics=sem, vmem_limit_bytes=VMEM_LIMIT_BYTES)


def _rms(x, g):
    return x * lax.rsqrt(jnp.mean(x * x, axis=-1, keepdims=True) + EPS) * g


def _ln(x, g, b):
    mu = jnp.mean(x, axis=-1, keepdims=True)
    xc = x - mu
    var = jnp.mean(xc * xc, axis=-1, keepdims=True)
    return xc * lax.rsqrt(var + EPS) * g + b


def _bdot(a, b, dims=(((1,), (0,)), ((), ()))):
    return lax.dot_general(a.astype(BF16), b.astype(BF16), dims, preferred_element_type=F32)


def _sgu_piece(zu, zv, lg, lb, w, b):
    u = jax.nn.gelu(zu)
    v = _ln(jax.nn.gelu(zv), lg, lb)
    row = lax.broadcasted_iota(jnp.int32, w.shape, 0)
    col = lax.broadcasted_iota(jnp.int32, w.shape, 1)
    wm = jnp.where(row >= col, w, 0.0)
    return u * (_bdot(wm, v) + b)


def _conf_post(c, g, b):
    return jax.nn.silu(_ln(c, g, b))


def _pool_count(first_pos, rows, w):
    pos = first_pos + lax.broadcasted_iota(jnp.int32, (rows, 1), 0) + 1
    return jnp.minimum(pos, w).astype(F32)


_DIMS = {
    "nn": (((1,), (0,)), ((), ())),
    "nt": (((1,), (1,)), ((), ())),
    "tn": (((0,), (0,)), ((), ())),
}


def _tiles(name, grid, ins, in_specs, outs, out_specs, compute, summed=()):
    ni = len(ins)

    def body(*refs):
        vals = compute(*refs[:ni])
        first = functools.reduce(jnp.logical_and, [pl.program_id(a) == 0 for a in range(len(grid))])
        for idx, (r, v) in enumerate(zip(refs[ni:], vals)):
            if idx in summed:
                @pl.when(first)
                def _(r=r):
                    r[...] = jnp.zeros_like(r)

                r[...] += v
            else:
                r[...] = v.astype(r.dtype)

    sem = ("arbitrary" if summed else "parallel",) * len(grid)
    return pl.pallas_call(
        body, name=name, grid=grid, in_specs=list(in_specs), out_specs=list(out_specs), out_shape=list(outs),
        compiler_params=_params(*sem),
    )(*ins)


def _resident(shape):
    return pl.BlockSpec(shape, lambda *_: (0,) * len(shape), pipeline_mode=pl.Buffered(1))


def _sds(shape, dtype):
    return jax.ShapeDtypeStruct(tuple(shape), dtype)


def _tile(n, want):
    t = min(n, want)
    assert n % t == 0, (n, want)
    return t


def _rms_fwd(name, h, g):
    s, d = h.shape
    tm = _tile(s, 512)

    def body(h_ref, g_ref, y_ref):
        y_ref[...] = _rms(h_ref[...], g_ref[...]).astype(BF16)

    return pl.pallas_call(
        body, name=name, grid=(s // tm,),
        in_specs=[pl.BlockSpec((tm, d), lambda i: (i, 0)), pl.BlockSpec((1, d), lambda i: (0, 0))],
        out_specs=pl.BlockSpec((tm, d), lambda i: (i, 0)),
        out_shape=_sds((s, d), BF16),
        compiler_params=_params("parallel"),
    )(h, g)


def _rms_back(h, g, dy, dh_in):
    _, vjp = jax.vjp(_rms, h, g)
    dh, dg = vjp(dy)
    return dh_in + dh, dg


def _rms_bwd(name, h, g, dy, dh_in):
    s, d = h.shape
    tm = _tile(s, 256)

    def body(h_ref, g_ref, dy_ref, dhin_ref, dh_ref, dh16_ref, dg_ref):
        dh, dg = _rms_back(h_ref[...], g_ref[...], dy_ref[...].astype(F32), dhin_ref[...])
        dh_ref[...] = dh
        dh16_ref[...] = dh.astype(BF16)

        @pl.when(pl.program_id(0) == 0)
        def _():
            dg_ref[...] = jnp.zeros_like(dg_ref)

        dg_ref[...] += dg

    tok = pl.BlockSpec((tm, d), lambda i: (i, 0))
    vec = pl.BlockSpec((1, d), lambda i: (0, 0))
    return pl.pallas_call(
        body, name=name, grid=(s // tm,),
        in_specs=[tok, vec, tok, tok],
        out_specs=[tok, tok, vec],
        out_shape=[_sds((s, d), F32), _sds((s, d), BF16), _sds((1, d), F32)],
        compiler_params=_params("arbitrary"),
    )(h, g, dy, dh_in)


def _loss_head(h, g, target):
    s, d = h.shape
    tm = _tile(s, 256)

    def body(h_ref, g_ref, t_ref, loss_ref, dh_ref, dg_ref):
        y, vjp = jax.vjp(_rms, h_ref[...], g_ref[...])
        err = y - t_ref[...]
        dh, dg = vjp(err * (1.0 / d))
        dh_ref[...] = dh
        per_token = jnp.mean(err * err, axis=-1, keepdims=True)
        part = 0.5 * jnp.sum(per_token, axis=0, keepdims=True)

        @pl.when(pl.program_id(0) == 0)
        def _():
            dg_ref[...] = jnp.zeros_like(dg_ref)
            loss_ref[...] = jnp.zeros_like(loss_ref)

        dg_ref[...] += dg
        loss_ref[...] += part

    tok = pl.BlockSpec((tm, d), lambda i: (i, 0))
    vec = pl.BlockSpec((1, d), lambda i: (0, 0))
    return pl.pallas_call(
        body, name="loss_head", grid=(s // tm,),
        in_specs=[tok, vec, tok],
        out_specs=[pl.BlockSpec((1, 1), lambda i: (0, 0)), tok, vec],
        out_shape=[_sds((1, 1), F32), _sds((s, d), F32), _sds((1, d), F32)],
        compiler_params=_params("arbitrary"),
    )(h, g, target)


def _ple_bwd(name, dh, pg_pre, pp):
    s, d = dh.shape
    tm = _tile(s, 512)

    def body(dh_ref, pg_ref, pp_ref, dpg_ref, dpp_ref):
        sg = jax.nn.sigmoid(pg_ref[...].astype(F32))
        dhv = dh_ref[...]
        dpg_ref[...] = (dhv * pp_ref[...].astype(F32) * sg * (1.0 - sg)).astype(BF16)
        dpp_ref[...] = (dhv * sg).astype(BF16)

    tok = pl.BlockSpec((tm, d), lambda i: (i, 0))
    return pl.pallas_call(
        body, name=name, grid=(s // tm,),
        in_specs=[tok, tok, tok], out_specs=[tok, tok],
        out_shape=[_sds((s, d), BF16), _sds((s, d), BF16)],
        compiler_params=_params("parallel"),
    )(dh, pg_pre, pp)


_U, _V, _HB, _BG, _CG, _CA, _CGT, _PD = (GROUP * i for i in range(8))


def _cols(c0, w=GROUP):
    return slice(c0, c0 + w)


def _mixer_param_specs():
    full = lambda *shape: pl.BlockSpec(shape, lambda i: (0,) * len(shape))
    return [
        full(4, 1, HEAD), full(4, 1, HEAD), full(4, HEAD, HEAD), full(4, HEAD, 1),
        full(8, GROUP), full(32, GROUP), full(1, GROUP), full(1, GROUP), full(1, GROUP),
        full(4, HEAD, HEAD), full(1, GROUP),
    ]


def _mixer_fwd(name, z, prm):
    s = z.shape[0]
    t = _tile(s, 256)
    hb = t // HALO

    def body(zp_ref, zm_ref, lg_ref, lb_ref, sw_ref, sb_ref, scw_ref, cfw_ref, cfb_ref, cg_ref, cb_ref,
             pw_ref, ps_ref, o_ref, ext_ref):
        i = pl.program_id(0)
        keep = (i > 0).astype(F32)
        main = pl.ds(HALO, t)

        for n in range(t // HEAD):
            rows = slice(n * HEAD, (n + 1) * HEAD)
            for hh in range(4):
                cu = _cols(_U + hh * HEAD, HEAD)
                cv = _cols(_V + hh * HEAD, HEAD)
                o_ref[rows, _cols(hh * HEAD, HEAD)] = _sgu_piece(
                    zm_ref[rows, cu], zm_ref[rows, cv], lg_ref[hh], lb_ref[hh], sw_ref[hh], sb_ref[hh]
                ).astype(BF16)

        ext_ref[0:HALO, :] = zp_ref[:, _cols(_CG)] * zp_ref[:, _cols(_HB)] * keep
        ext_ref[main, :] = zm_ref[:, _cols(_CG)] * zm_ref[:, _cols(_HB)]
        y = jnp.zeros((t, GROUP), F32)
        for k in range(SHORT_K):
            y = y + scw_ref[k:k + 1, :] * ext_ref[pl.ds(HALO - (SHORT_K - 1) + k, t), :]
        o_ref[:, _cols(GROUP)] = (zm_ref[:, _cols(_BG)] * y).astype(BF16)

        ext_ref[0:HALO, :] = zp_ref[:, _cols(_CA)] * jax.nn.sigmoid(zp_ref[:, _cols(_CGT)]) * keep
        ext_ref[main, :] = zm_ref[:, _cols(_CA)] * jax.nn.sigmoid(zm_ref[:, _cols(_CGT)])
        c = jnp.zeros((t, GROUP), F32) + cfb_ref[...]
        for k in range(CONF_K):
            c = c + cfw_ref[k:k + 1, :] * ext_ref[pl.ds(HALO - (CONF_K - 1) + k, t), :]
        o_ref[:, _cols(2 * GROUP)] = _conf_post(c, cg_ref[...], cb_ref[...]).astype(BF16)

        ext_ref[0:HALO, :] = zp_ref[:, _cols(_PD)] * keep
        ext_ref[main, :] = zm_ref[:, _cols(_PD)]
        for gi, w in enumerate(POOL_WINDOWS):
            cc = _cols(gi * HEAD, HEAD)
            acc = ext_ref[main, cc]
            for j in range(1, w):
                acc = acc + ext_ref[pl.ds(HALO - j, t), cc]
            q = acc / _pool_count(i * t, t, w) - ext_ref[main, cc]
            o_ref[:, _cols(3 * GROUP + gi * HEAD, HEAD)] = (_bdot(q, pw_ref[gi]) * ps_ref[:, cc]).astype(BF16)

    return pl.pallas_call(
        body, name=name, grid=(s // t,),
        in_specs=[pl.BlockSpec((HALO, 8 * GROUP), lambda i: (jnp.maximum(i * hb - 1, 0), 0)),
                  pl.BlockSpec((t, 8 * GROUP), lambda i: (i, 0)),
                  *_mixer_param_specs()],
        out_specs=pl.BlockSpec((t, 4 * GROUP), lambda i: (i, 0)),
        out_shape=_sds((s, 4 * GROUP), BF16),
        scratch_shapes=[pltpu.VMEM((HALO + t, GROUP), F32)],
        compiler_params=_params("parallel"),
    )(z, z, *prm)


def _mixer_bwd(name, z, dmo, prm):
    s = z.shape[0]
    t = _tile(s, 256)
    hb = t // HALO
    nt = s // t
    last_halo = s // HALO - 1

    def body(zp_ref, zm_ref, zn_ref, dm_ref, dn_ref,
             lg_ref, lb_ref, sw_ref, sb_ref, scw_ref, cfw_ref, cfb_ref, cg_ref, cb_ref, pw_ref, ps_ref,
             dz_ref, dlg_ref, dlb_ref, dsw_ref, dsb_ref, dscw_ref, dcfw_ref, dcfb_ref, dcg_ref, dcb_ref,
             dpw_ref, dps_ref, extp_ref, extn_ref, extc_ref):
        i = pl.program_id(0)
        keep_prev = (i > 0).astype(F32)
        keep_next = (i < nt - 1).astype(F32)
        main = pl.ds(HALO, t)

        @pl.when(i == 0)
        def _():
            for r in (dlg_ref, dlb_ref, dsw_ref, dsb_ref, dscw_ref, dcfw_ref, dcfb_ref, dcg_ref, dcb_ref,
                      dpw_ref, dps_ref):
                r[...] = jnp.zeros_like(r)

        def rowsum(v):
            return jnp.sum(v, axis=0, keepdims=True)

        for n in range(t // HEAD):
            rows = slice(n * HEAD, (n + 1) * HEAD)
            for hh in range(4):
                cu = _cols(_U + hh * HEAD, HEAD)
                cv = _cols(_V + hh * HEAD, HEAD)
                _, vjp = jax.vjp(_sgu_piece, zm_ref[rows, cu], zm_ref[rows, cv],
                                 lg_ref[hh], lb_ref[hh], sw_ref[hh], sb_ref[hh])
                dzu, dzv, dlg, dlb, dsw, dsb = vjp(dm_ref[rows, _cols(hh * HEAD, HEAD)].astype(F32))
                dz_ref[rows, cu] = dzu.astype(BF16)
                dz_ref[rows, cv] = dzv.astype(BF16)
                dlg_ref[hh] += dlg
                dlb_ref[hh] += dlb
                dsw_ref[hh] += dsw
                dsb_ref[hh] += dsb

        extp_ref[0:HALO, :] = zp_ref[:, _cols(_CG)] * zp_ref[:, _cols(_HB)] * keep_prev
        extp_ref[main, :] = zm_ref[:, _cols(_CG)] * zm_ref[:, _cols(_HB)]
        dob = dm_ref[:, _cols(GROUP)].astype(F32)
        dy = dob * zm_ref[:, _cols(_BG)]
        extn_ref[0:t, :] = dy
        extn_ref[t:t + HALO, :] = dn_ref[:, _cols(GROUP)].astype(F32) * zn_ref[:, _cols(_BG)] * keep_next
        y = jnp.zeros((t, GROUP), F32)
        dx = jnp.zeros((t, GROUP), F32)
        for k in range(SHORT_K):
            xs = extp_ref[pl.ds(HALO - (SHORT_K - 1) + k, t), :]
            y = y + scw_ref[k:k + 1, :] * xs
            dscw_ref[k:k + 1, :] += rowsum(dy * xs)
            dx = dx + scw_ref[k:k + 1, :] * extn_ref[pl.ds((SHORT_K - 1) - k, t), :]
        dz_ref[:, _cols(_BG)] = (dob * y).astype(BF16)
        dz_ref[:, _cols(_CG)] = (dx * zm_ref[:, _cols(_HB)]).astype(BF16)
        dz_ref[:, _cols(_HB)] = (dx * zm_ref[:, _cols(_CG)]).astype(BF16)

        extc_ref[0:HALO, :] = zp_ref[:, _cols(_CA)] * jax.nn.sigmoid(zp_ref[:, _cols(_CGT)]) * keep_prev
        sg = jax.nn.sigmoid(zm_ref[:, _cols(_CGT)])
        extc_ref[main, :] = zm_ref[:, _cols(_CA)] * sg
        extc_ref[HALO + t:HALO + t + HALO, :] = zn_ref[:, _cols(_CA)] * jax.nn.sigmoid(zn_ref[:, _cols(_CGT)])
        c_main = jnp.zeros((t, GROUP), F32) + cfb_ref[...]
        c_next = jnp.zeros((HALO, GROUP), F32) + cfb_ref[...]
        for k in range(CONF_K):
            wk = cfw_ref[k:k + 1, :]
            c_main = c_main + wk * extc_ref[pl.ds(HALO - (CONF_K - 1) + k, t), :]
            c_next = c_next + wk * extc_ref[pl.ds(HALO + t - (CONF_K - 1) + k, HALO), :]
        _, vjp = jax.vjp(_conf_post, c_main, cg_ref[...], cb_ref[...])
        dc, dcg, dcb = vjp(dm_ref[:, _cols(2 * GROUP)].astype(F32))
        dcg_ref[...] += dcg
        dcb_ref[...] += dcb
        dcfb_ref[...] += rowsum(dc)
        _, vjp_next = jax.vjp(lambda cv: _conf_post(cv, cg_ref[...], cb_ref[...]), c_next)
        (dc_next,) = vjp_next(dn_ref[:, _cols(2 * GROUP)].astype(F32) * keep_next)
        extn_ref[0:t, :] = dc
        extn_ref[t:t + HALO, :] = dc_next
        dhc = jnp.zeros((t, GROUP), F32)
        for k in range(CONF_K):
            dhc = dhc + cfw_ref[k:k + 1, :] * extn_ref[pl.ds((CONF_K - 1) - k, t), :]
            dcfw_ref[k:k + 1, :] += rowsum(dc * extc_ref[pl.ds(HALO - (CONF_K - 1) + k, t), :])
        dz_ref[:, _cols(_CA)] = (dhc * sg).astype(BF16)
        dz_ref[:, _cols(_CGT)] = (dhc * zm_ref[:, _cols(_CA)] * sg * (1.0 - sg)).astype(BF16)

        extp_ref[0:HALO, :] = zp_ref[:, _cols(_PD)] * keep_prev
        extp_ref[main, :] = zm_ref[:, _cols(_PD)]
        for gi, w in enumerate(POOL_WINDOWS):
            cc = _cols(gi * HEAD, HEAD)
            oc = _cols(3 * GROUP + gi * HEAD, HEAD)
            acc = extp_ref[main, cc]
            for j in range(1, w):
                acc = acc + extp_ref[pl.ds(HALO - j, t), cc]
            count = _pool_count(i * t, t, w)
            q = acc / count - extp_ref[main, cc]
            dod = dm_ref[:, oc].astype(F32)
            dps_ref[:, cc] += rowsum(dod * _bdot(q, pw_ref[gi]))
            ds = dod * ps_ref[:, cc]
            dpw_ref[gi] += _bdot(q, ds, _DIMS["tn"])
            dq = _bdot(ds, pw_ref[gi], _DIMS["nt"])
            ds_next = dn_ref[:, oc].astype(F32) * ps_ref[:, cc] * keep_next
            dq_next = _bdot(ds_next, pw_ref[gi], _DIMS["nt"])
            extn_ref[0:t, cc] = dq / count
            extn_ref[t:t + HALO, cc] = dq_next * (1.0 / w)
            back = extn_ref[0:t, cc]
            for j in range(1, w):
                back = back + extn_ref[pl.ds(j, t), cc]
            dz_ref[:, _cols(_PD + gi * HEAD, HEAD)] = (back - dq).astype(BF16)

    full = lambda *shape: pl.BlockSpec(shape, lambda i: (0,) * len(shape))
    grad_specs = [full(4, 1, HEAD), full(4, 1, HEAD), full(4, HEAD, HEAD), full(4, HEAD, 1),
                  full(8, GROUP), full(32, GROUP), full(1, GROUP), full(1, GROUP), full(1, GROUP),
                  full(4, HEAD, HEAD), full(1, GROUP)]
    grad_shapes = [_sds(sp.block_shape, F32) for sp in grad_specs]
    nxt = lambda i: (jnp.minimum((i + 1) * hb, last_halo), 0)
    return pl.pallas_call(
        body, name=name, grid=(nt,),
        in_specs=[pl.BlockSpec((HALO, 8 * GROUP), lambda i: (jnp.maximum(i * hb - 1, 0), 0)),
                  pl.BlockSpec((t, 8 * GROUP), lambda i: (i, 0)),
                  pl.BlockSpec((HALO, 8 * GROUP), nxt),
                  pl.BlockSpec((t, 4 * GROUP), lambda i: (i, 0)),
                  pl.BlockSpec((HALO, 4 * GROUP), nxt),
                  *_mixer_param_specs()],
        out_specs=[pl.BlockSpec((t, 8 * GROUP), lambda i: (i, 0)), *grad_specs],
        out_shape=[_sds((s, 8 * GROUP), BF16), *grad_shapes],
        scratch_shapes=[pltpu.VMEM((HALO + t, GROUP), F32), pltpu.VMEM((t + HALO, GROUP), F32),
                        pltpu.VMEM((HALO + t + HALO, GROUP), F32)],
        compiler_params=_params("arbitrary"),
    )(z, z, z, dmo, dmo, *prm)


def _dot(a, b, kind="nn"):
    return _bdot(a, b, _DIMS[kind])


def _chain(terms, kind):
    acc = None
    for a, b in terms:
        p = _dot(a, b, kind)
        acc = p if acc is None else acc + p
    return acc


def _layer_fwd(li, h, p_all, gw, sm):
    s, d = h.shape
    nb, _, f8 = gw["wg"].shape
    pd = p_all.shape[2]
    sv = {"h": h}
    rows = lambda t, w: pl.BlockSpec((t, w), lambda i: (i, 0))

    y1 = _rms_fwd("rms_mix", h, sm["norm_mix_g"])
    tm = _tile(s, 1024)
    (z,) = _tiles(
        "w_in_fwd", (s // tm, nb),
        [y1, gw["win"]],
        [pl.BlockSpec((tm, d), lambda i, j: (i, 0)), pl.BlockSpec((None, d, GROUP), lambda i, j: (j, 0, 0))],
        [_sds((s, nb * GROUP), F32)], [pl.BlockSpec((tm, GROUP), lambda i, j: (i, j))],
        lambda a, w: (_dot(a[...], w[...]),))
    mo = _mixer_fwd("mixer_fwd", z, sm["mixer"])
    tm = _tile(s, 512)
    (h2,) = _tiles(
        "w_out_fwd", (s // tm,),
        [mo, gw["wout"], h], [rows(tm, d), _resident((d, d)), rows(tm, d)],
        [_sds((s, d), F32)], [rows(tm, d)],
        lambda a, w, hv: (hv[...] + _dot(a[...], w[...]),))
    sv.update(y1=y1, z=z, mo=mo, h2=h2)

    y2 = _rms_fwd("rms_ffn", h2, sm["norm_ffn_g"])
    gate_pre, up_pre, hmid = _ffn_up("ffn_up_fwd", y2, gw["wg"], gw["wu"])
    tm, tn = _tile(s, 512), _tile(d, 1024)
    (h3,) = _tiles(
        "w_down_fwd", (d // tn, s // tm),
        [hmid, gw["wd"], h2],
        [pl.BlockSpec((nb, tm, f8), lambda j, i: (0, i, 0)), pl.BlockSpec((nb, f8, tn), lambda j, i: (0, 0, j)),
         pl.BlockSpec((tm, tn), lambda j, i: (i, j))],
        [_sds((s, d), F32)], [pl.BlockSpec((tm, tn), lambda j, i: (i, j))],
        lambda a, w, hv: (hv[...] + _chain([(a[k], w[k]) for k in range(nb)], "nn"),))
    sv.update(y2=y2, gate_pre=gate_pre, up_pre=up_pre, hmid=hmid, h3=h3)

    y3 = _rms_fwd("rms_ple", h3, sm["norm_ple_g"])
    tm = _tile(s, 1024)
    (pp,) = _tiles(
        "w_ple_proj_fwd", (s // tm, nb),
        [p_all, gw["wpp"]],
        [pl.BlockSpec((None, tm, pd), lambda i, j: (li, i, 0)), pl.BlockSpec((None, pd, pd), lambda i, j: (j, 0, 0))],
        [_sds((s, d), BF16)], [pl.BlockSpec((tm, pd), lambda i, j: (i, j))],
        lambda a, w: (_dot(a[...], w[...]),))
    tm = _tile(s, 256)

    def ple(a, w, hv, ppv):
        pg = _dot(a[...], w[...])
        return hv[...] + jax.nn.sigmoid(pg) * ppv[...].astype(F32), pg

    h4, pg_pre = _tiles(
        "w_ple_gate_fwd", (s // tm,),
        [y3, gw["wpg"], h3, pp], [rows(tm, d), _resident((d, d)), rows(tm, d), rows(tm, d)],
        [_sds((s, d), F32), _sds((s, d), BF16)], [rows(tm, d), rows(tm, d)], ple)
    sv.update(y3=y3, pp=pp, pg_pre=pg_pre)
    return h4, sv


def _ffn_up(name, y2, wg, wu):
    s, d = y2.shape
    nb, _, f8 = wg.shape
    tm = _tile(s, 1024)

    def compute(y_ref, wg_ref, wu_ref):
        yv = y_ref[...]
        g = _dot(yv, wg_ref[...])
        u = _dot(yv, wu_ref[...])
        return g, u, jax.nn.silu(g) * u

    wspec = pl.BlockSpec((None, d, f8), lambda i, j: (j, 0, 0))
    ospec = pl.BlockSpec((None, tm, f8), lambda i, j: (j, i, 0))
    return _tiles(name, (s // tm, nb), [y2, wg, wu], [pl.BlockSpec((tm, d), lambda i, j: (i, 0)), wspec, wspec],
                  [_sds((nb, s, f8), BF16)] * 3, [ospec] * 3, compute)


def _swiglu_bwd(dm, g, u):
    sg = jax.nn.sigmoid(g)
    silu = g * sg
    return dm * u * (sg + silu * (1.0 - sg)), dm * silu


def _layer_bwd(li, dh, p_all, gw, sm, sv):
    s, d = dh.shape
    nb, _, f8 = gw["wg"].shape
    pd = p_all.shape[2]
    gr, sg = {}, {}
    rows = lambda t, w: pl.BlockSpec((t, w), lambda i: (i, 0))
    vec = pl.BlockSpec((1, d), lambda i: (0, 0))
    tw = _tile(d, 1024)
    whole = lambda w: pl.BlockSpec((s, w), lambda i, j: (0, i))
    whole_j = lambda w: pl.BlockSpec((s, w), lambda i, j: (0, j))
    tn_dot = lambda a, b: (_dot(a[...], b[...], "tn"),)

    dpg, dpp = _ple_bwd("ple_bwd", dh, sv["pg_pre"], sv["pp"])
    (gr["wpp"],) = _tiles(
        "w_ple_proj_dw", (nb,),
        [p_all, dpp], [pl.BlockSpec((None, s, pd), lambda j: (li, 0, 0)), pl.BlockSpec((s, pd), lambda j: (0, j))],
        [_sds((nb, pd, pd), F32)], [pl.BlockSpec((None, pd, pd), lambda j: (j, 0, 0))], tn_dot)
    (gr["wpg"],) = _tiles(
        "w_ple_gate_dw", (d // tw, d // tw), [sv["y3"], dpg], [whole(tw), whole_j(tw)],
        [_sds((d, d), F32)], [pl.BlockSpec((tw, tw), lambda i, j: (i, j))], tn_dot)
    tm = _tile(s, 256)

    def ple_dx(a, w, hv, gv, dv):
        dh3, dg = _rms_back(hv[...], gv[...], _dot(a[...], w[...], "nt"), dv[...])
        return dh3, dh3, dg

    dh3, dh3_16, sg["norm_ple_g"] = _tiles(
        "w_ple_gate_dx", (s // tm,),
        [dpg, gw["wpg"], sv["h3"], sm["norm_ple_g"], dh], [rows(tm, d), _resident((d, d)), rows(tm, d), vec, rows(tm, d)],
        [_sds((s, d), F32), _sds((s, d), BF16), _sds((1, d), F32)], [rows(tm, d), rows(tm, d), vec],
        ple_dx, summed=(2,))

    tm = _tile(s, 1024)
    blk_rows = pl.BlockSpec((None, tm, f8), lambda i, j: (j, i, 0))
    dgate, dup = _tiles(
        "w_down_dx", (s // tm, nb),
        [dh3_16, gw["wd"], sv["gate_pre"], sv["up_pre"]],
        [pl.BlockSpec((tm, d), lambda i, j: (i, 0)), pl.BlockSpec((None, f8, d), lambda i, j: (j, 0, 0)), blk_rows, blk_rows],
        [_sds((nb, s, f8), BF16)] * 2, [blk_rows] * 2,
        lambda a, w, g, u: _swiglu_bwd(_dot(a[...], w[...], "nt"), g[...].astype(F32), u[...].astype(F32)))
    (gr["wd"],) = _tiles(
        "w_down_dw", (nb, d // tw),
        [sv["hmid"], dh3_16], [pl.BlockSpec((None, s, f8), lambda i, j: (i, 0, 0)), whole_j(tw)],
        [_sds((nb, f8, d), F32)], [pl.BlockSpec((None, f8, tw), lambda i, j: (i, 0, j))], tn_dot)
    for nm, dact in (("wg", dgate), ("wu", dup)):
        (gr[nm],) = _tiles(
            "w_" + {"wg": "gate", "wu": "up"}[nm] + "_dw", (d // tw, nb),
            [sv["y2"], dact], [whole(tw), pl.BlockSpec((None, s, f8), lambda i, j: (j, 0, 0))],
            [_sds((nb, d, f8), F32)], [pl.BlockSpec((None, tw, f8), lambda i, j: (j, i, 0))], tn_dot)
    tm, tn = _tile(s, 512), _tile(d, 1024)
    act_spec = pl.BlockSpec((nb, tm, f8), lambda j, i: (0, i, 0))
    wt_spec = pl.BlockSpec((nb, tn, f8), lambda j, i: (0, j, 0))
    out_spec = pl.BlockSpec((tm, tn), lambda j, i: (i, j))
    (dy2,) = _tiles(
        "w_gate_dx", (d // tn, s // tm), [dgate, gw["wg"]], [act_spec, wt_spec],
        [_sds((s, d), F32)], [out_spec],
        lambda a, w: (_chain([(a[k], w[k]) for k in range(nb)], "nt"),))
    (dy2,) = _tiles(
        "w_up_dx", (d // tn, s // tm), [dup, gw["wu"], dy2], [act_spec, wt_spec, out_spec],
        [_sds((s, d), F32)], [out_spec],
        lambda a, w, prev: (prev[...] + _chain([(a[k], w[k]) for k in range(nb)], "nt"),))
    dh2, dh2_16, sg["norm_ffn_g"] = _rms_bwd("rms_ffn_bwd", sv["h2"], sm["norm_ffn_g"], dy2, dh3)

    tm = _tile(s, 512)
    (dmo,) = _tiles(
        "w_out_dx", (s // tm,), [dh2_16, gw["wout"]], [rows(tm, d), _resident((d, d))],
        [_sds((s, d), BF16)], [rows(tm, d)], lambda a, w: (_dot(a[...], w[...], "nt"),))
    (gr["wout"],) = _tiles(
        "w_out_dw", (d // tw, d // tw), [sv["mo"], dh2_16], [whole(tw), whole_j(tw)],
        [_sds((d, d), F32)], [pl.BlockSpec((tw, tw), lambda i, j: (i, j))], tn_dot)
    dz, *mix_grads = _mixer_bwd("mixer_bwd", sv["z"], dmo, sm["mixer"])
    sg["mixer"] = mix_grads
    (gr["win"],) = _tiles(
        "w_in_dw", (d // tw, nb), [sv["y1"], dz], [whole(tw), pl.BlockSpec((s, GROUP), lambda i, j: (0, j))],
        [_sds((nb, d, GROUP), F32)], [pl.BlockSpec((None, tw, GROUP), lambda i, j: (j, i, 0))], tn_dot)
    tm = _tile(s, 256)

    def in_dx(a, w, hv, gv, dv):
        dy1 = _chain([(a[:, k * GROUP:(k + 1) * GROUP], w[k]) for k in range(nb)], "nt")
        return _rms_back(hv[...], gv[...], dy1, dv[...])

    dh_in, sg["norm_mix_g"] = _tiles(
        "w_in_dx", (s // tm,),
        [dz, gw["win"], sv["h"], sm["norm_mix_g"], dh2],
        [rows(tm, nb * GROUP), _resident((nb, d, GROUP)), rows(tm, d), vec, rows(tm, d)],
        [_sds((s, d), F32), _sds((1, d), F32)], [rows(tm, d), vec], in_dx, summed=(1,))
    return dh_in, gr, sg


def _place():
    return lax.axis_index("x"), lax.axis_index("y"), lax.axis_index("c")


def _gather_layer(li, shards):
    nw = len(shards)

    def body(*refs):
        ins, outs = refs[:nw], refs[nw:2 * nw]
        send_sems, recv_sems, local_sems = refs[2 * nw:]
        x, y, c = _place()
        me, sib = (x, y, c), (x, y, 1 - c)
        chips = [(1 - x, y), (x, 1 - y), (1 - x, 1 - y)]

        def copy(w, k, block, to, src=None):
            px, py, pc = block
            dst = outs[w].at[4 * px + 2 * py + pc]
            return pltpu.make_async_remote_copy(
                src_ref=dst if src is None else src, dst_ref=dst,
                send_sem=send_sems.at[w, k], recv_sem=recv_sems.at[w, k], device_id=to, device_id_type=MESH)

        mine = [pltpu.make_async_copy(ins[w].at[li], outs[w].at[4 * x + 2 * y + c], local_sems.at[w]) for w in range(nw)]
        for cp in mine:
            cp.start()
        sent = []
        for w in range(nw):
            sent.append(copy(w, 0, me, sib, src=ins[w].at[li]))
            sent += [copy(w, 1 + j, me, (*chip, c), src=ins[w].at[li]) for j, chip in enumerate(chips)]
        for cp in sent:
            cp.start()
        for j, chip in enumerate(chips):
            for w in range(nw):
                copy(w, 1 + j, (*chip, c), me).wait_recv()
                fwd = copy(w, 4 + j, (*chip, c), sib)
                fwd.start()
                sent.append(fwd)
        for w in range(nw):
            copy(w, 0, sib, me).wait_recv()
            for j, chip in enumerate(chips):
                copy(w, 4 + j, (*chip, 1 - c), me).wait_recv()
        for cp in sent:
            cp.wait_send()
        for cp in mine:
            cp.wait()

    return pl.pallas_call(
        body, name="gather_weights",
        in_specs=[ANY] * nw, out_specs=[ANY] * nw,
        out_shape=[_sds((N_DEV,) + a.shape[1:], a.dtype) for a in shards],
        scratch_shapes=[pltpu.SemaphoreType.DMA((nw, 7)), pltpu.SemaphoreType.DMA((nw, 7)),
                        pltpu.SemaphoreType.DMA((nw,))],
    )(*shards)


def _exchange_siblings(grads):
    nw = len(grads)

    def body(*refs):
        ins, outs = refs[:nw], refs[nw:2 * nw]
        send_sems, recv_sems = refs[2 * nw:]
        x, y, c = _place()
        copies = []
        for w in range(nw):
            for q in range(4):
                copies.append(pltpu.make_async_remote_copy(
                    src_ref=ins[w].at[2 * q + (1 - c)], dst_ref=outs[w].at[q],
                    send_sem=send_sems.at[w, q], recv_sem=recv_sems.at[w, q],
                    device_id=(x, y, 1 - c), device_id_type=MESH))
        for cp in copies:
            cp.start()
        for cp in copies:
            cp.wait()

    return pl.pallas_call(
        body, name="rs_siblings",
        in_specs=[ANY] * nw, out_specs=[ANY] * nw,
        out_shape=[_sds((4,) + g.shape[1:], g.dtype) for g in grads],
        scratch_shapes=[pltpu.SemaphoreType.DMA((nw, 4)), pltpu.SemaphoreType.DMA((nw, 4))],
    )(*grads)


def _exchange_chips(partials):
    nw = len(partials)
    flips = [(1, 0), (0, 1), (1, 1)]

    def body(*refs):
        ins, outs = refs[:nw], refs[nw:2 * nw]
        send_sems, recv_sems = refs[2 * nw:]
        x, y, c = _place()
        copies = []
        for w in range(nw):
            for r, (fx, fy) in enumerate(flips):
                px = 1 - x if fx else x
                py = 1 - y if fy else y
                copies.append(pltpu.make_async_remote_copy(
                    src_ref=ins[w].at[2 * px + py], dst_ref=outs[w].at[r],
                    send_sem=send_sems.at[w, r], recv_sem=recv_sems.at[w, r],
                    device_id=(px, py, c), device_id_type=MESH))
        for cp in copies:
            cp.start()
        for cp in copies:
            cp.wait()

    return pl.pallas_call(
        body, name="rs_chips",
        in_specs=[ANY] * nw, out_specs=[ANY] * nw,
        out_shape=[_sds((3,) + a.shape[1:], a.dtype) for a in partials],
        scratch_shapes=[pltpu.SemaphoreType.DMA((nw, 3)), pltpu.SemaphoreType.DMA((nw, 3))],
    )(*partials)


def _row_tile(n, want):
    best = None
    for t in range(16, min(n, want) + 1, 16):
        if n % t == 0:
            best = t
    assert best is not None, n
    return best


def _chip_partials(name, where, grad, from_sibling):
    _, r, cdim = grad.shape
    tr = _row_tile(r, 256)

    def body(where_ref, g_ref, s_ref, pb_ref, own_ref):
        tot = g_ref[...] + s_ref[...]
        pb_ref[...] = tot.astype(BF16)

        @pl.when(pl.program_id(1) == where_ref[1])
        def _():
            own_ref[...] = tot

    grid_spec = pltpu.PrefetchScalarGridSpec(
        num_scalar_prefetch=1, grid=(r // tr, 4),
        in_specs=[pl.BlockSpec((None, tr, cdim), lambda i, q, wh: (2 * q + wh[0], i, 0)),
                  pl.BlockSpec((None, tr, cdim), lambda i, q, wh: (q, i, 0))],
        out_specs=[pl.BlockSpec((None, tr, cdim), lambda i, q, wh: (q, i, 0)),
                   pl.BlockSpec((tr, cdim), lambda i, q, wh: (i, 0))])
    return pl.pallas_call(
        body, name=name, grid_spec=grid_spec,
        out_shape=[_sds((4, r, cdim), BF16), _sds((r, cdim), F32)],
        compiler_params=_params("parallel", "arbitrary"),
    )(where, grad, from_sibling)


def _adamw(w, g, m, v):
    m = ADAM_B1 * m + (1.0 - ADAM_B1) * g
    v = ADAM_B2 * v + (1.0 - ADAM_B2) * (g * g)
    m_hat = m / (1.0 - ADAM_B1 ** ADAM_STEP)
    v_hat = v / (1.0 - ADAM_B2 ** ADAM_STEP)
    delta = -ADAM_LR * (m_hat / (jnp.sqrt(v_hat) + ADAM_EPS) + ADAM_WD * w)
    return delta, m, v


def _finish_weight(name, li, own, from_chips, w, m, v, stacked):
    r, cdim = own.shape
    tr = _row_tile(r, 256)
    if stacked is None:
        stacked = [lax.empty(w.shape, F32) for _ in range(4)]

    def body(own_ref, fc_ref, w_ref, m_ref, v_ref, *rest):
        g_out, d_out, m_out, v_out = rest[4:]
        g = own_ref[...] + fc_ref[0].astype(F32) + fc_ref[1].astype(F32) + fc_ref[2].astype(F32)
        delta, mn, vn = _adamw(w_ref[...], g, m_ref[...], v_ref[...])
        g_out[...] = g
        d_out[...] = delta
        m_out[...] = mn
        v_out[...] = vn

    tile = pl.BlockSpec((tr, cdim), lambda i: (i, 0))
    lay = pl.BlockSpec((None, tr, cdim), lambda i: (li, i, 0))
    return pl.pallas_call(
        body, name=name, grid=(r // tr,),
        in_specs=[tile, pl.BlockSpec((3, tr, cdim), lambda i: (0, i, 0)), lay, lay, lay] + [ANY] * 4,
        out_specs=[lay] * 4, out_shape=[_sds(w.shape, F32)] * 4,
        input_output_aliases={5: 0, 6: 1, 7: 2, 8: 3},
        compiler_params=_params("parallel"),
    )(own, from_chips, w, m, v, *stacked)


def _allgather_small(name, v, reduce):
    r = v.shape[0]

    def body(x_ref, out_ref, *rest):
        if reduce:
            sum_ref, send_sems, recv_sems, local_sem = rest
        else:
            send_sems, recv_sems, local_sem = rest
        x, y, c = _place()
        me, sib = (x, y, c), (x, y, 1 - c)
        chips = [(1 - x, y), (x, 1 - y), (1 - x, 1 - y)]

        def rows(px, py, pc):
            return out_ref.at[pl.ds(pl.multiple_of((4 * px + 2 * py + pc) * r, 8), r), :]

        def copy(k, block, to, src=None):
            return pltpu.make_async_remote_copy(
                src_ref=rows(*block) if src is None else src, dst_ref=rows(*block),
                send_sem=send_sems.at[k], recv_sem=recv_sems.at[k], device_id=to, device_id_type=MESH)

        mine = pltpu.make_async_copy(x_ref, rows(*me), local_sem)
        mine.start()
        first = [copy(0, me, sib, src=x_ref)]
        first += [copy(1 + j, me, (*chip, c), src=x_ref) for j, chip in enumerate(chips)]
        for cp in first:
            cp.start()
        passed = [copy(4 + j, (*chip, c), sib) for j, chip in enumerate(chips)]
        for j, chip in enumerate(chips):
            copy(1 + j, (*chip, c), me).wait_recv()
            passed[j].start()
        copy(0, sib, me).wait_recv()
        for j, chip in enumerate(chips):
            copy(4 + j, (*chip, 1 - c), me).wait_recv()
        for cp in first + passed:
            cp.wait_send()
        mine.wait()
        if reduce:
            tot = out_ref[0:r, :]
            for d in range(1, N_DEV):
                tot = tot + out_ref[d * r:(d + 1) * r, :]
            sum_ref[...] = tot

    vm = pl.BlockSpec(memory_space=pltpu.VMEM)
    outs = [_sds((N_DEV * r, 128), F32)] + ([_sds((r, 128), F32)] if reduce else [])
    res = pl.pallas_call(
        body, name=name,
        in_specs=[vm], out_specs=[vm] * len(outs), out_shape=outs,
        scratch_shapes=[pltpu.SemaphoreType.DMA((7,)), pltpu.SemaphoreType.DMA((7,)), pltpu.SemaphoreType.DMA],
        compiler_params=pltpu.CompilerParams(vmem_limit_bytes=VMEM_LIMIT_BYTES),
    )(v)
    return res


def _adamw_small(name, w, g, m, v):
    def body(w_ref, g_ref, m_ref, v_ref, d_out, m_out, v_out):
        delta, mn, vn = _adamw(w_ref[...], g_ref[...], m_ref[...], v_ref[...])
        d_out[...] = delta
        m_out[...] = mn
        v_out[...] = vn

    vm = pl.BlockSpec(memory_space=pltpu.VMEM)
    return pl.pallas_call(
        body, name=name, in_specs=[vm] * 4, out_specs=[vm] * 3, out_shape=[_sds(w.shape, F32)] * 3,
        compiler_params=pltpu.CompilerParams(vmem_limit_bytes=VMEM_LIMIT_BYTES),
    )(w, g, m, v)


def _pack(arrays):
    flat, layout, off = [], [], 0
    for a in arrays:
        flat.append(a.reshape(-1).astype(F32))
        layout.append((off, a.shape))
        off += a.size
    total = -(-off // 1024) * 1024
    if total > off:
        flat.append(jnp.zeros((total - off,), F32))
    return jnp.concatenate(flat).reshape(total // 128, 128), layout


def _unpack(packed, layout):
    flat = packed.reshape(-1)
    return [flat[off:off + math.prod(shape)].reshape(shape) for off, shape in layout]


_BIG = ("win", "wout", "wg", "wu", "wd", "wpg", "wpp")
_BIG_FULL = {"win": "w_in", "wout": "w_out", "wg": "w_gate", "wu": "w_up", "wd": "w_down",
             "wpg": "w_ple_gate", "wpp": "w_ple_proj"}
_SMALL_REPLICATED = ("norm_mix_g", "sgu_ln_g", "sgu_ln_b", "sgu_w", "sgu_b", "cf_conv_b", "cf_ln_g", "cf_ln_b",
                     "pool_w", "pool_scale", "norm_ffn_g", "norm_ple_g", "final_norm_g")
_SMALL_SHARDED = ("sc_conv_w", "cf_conv_w")
_WEIGHTS = ("norm_mix_g", "w_in", "sgu_ln_g", "sgu_ln_b", "sgu_w", "sgu_b", "sc_conv_w", "cf_conv_w", "cf_conv_b",
            "cf_ln_g", "cf_ln_b", "pool_w", "pool_scale", "w_out", "norm_ffn_g", "w_gate", "w_up", "w_down",
            "norm_ple_g", "w_ple_gate", "w_ple_proj", "final_norm_g")


def _pad_rows(a, rows):
    return jnp.concatenate([a, jnp.zeros((rows - a.shape[0],) + a.shape[1:], a.dtype)], axis=0)


def _mixer_params(li, W, sc_full, cf_full):
    return [W["sgu_ln_g"][li][:, None, :], W["sgu_ln_b"][li][:, None, :], W["sgu_w"][li], W["sgu_b"][li][:, :, None],
            _pad_rows(sc_full[li], 8), _pad_rows(cf_full[li], 32),
            W["cf_conv_b"][li][None, :], W["cf_ln_g"][li][None, :], W["cf_ln_b"][li][None, :],
            W["pool_w"][li], W["pool_scale"][li][None, :]]


def _step(W, M, V, x, p, loss_target):
    n_layers = W["w_in"].shape[0]
    h = x[0]
    target = loss_target[0]
    p_all = p[:, 0]
    xi, yi, ci = _place()
    blk = 4 * xi + 2 * yi + ci
    where = jnp.stack([ci, 2 * xi + yi]).astype(jnp.int32)
    csh = W["sc_conv_w"].shape[2]

    packed, lay = _pack([W[n] for n in _SMALL_SHARDED])
    (taps,) = _allgather_small("gather_conv_taps", packed, reduce=False)
    per_dev = [_unpack(taps[d * packed.shape[0]:(d + 1) * packed.shape[0]], lay) for d in range(N_DEV)]
    sc_full = jnp.concatenate([pd[0] for pd in per_dev], axis=-1)
    cf_full = jnp.concatenate([pd[1] for pd in per_dev], axis=-1)

    big16 = [W[_BIG_FULL[n]].astype(BF16) for n in _BIG]
    d = h.shape[1]

    saved, gathered, smalls = [], [], []
    for li in range(n_layers):
        g = dict(zip(_BIG, _gather_layer(li, big16)))
        g["wout"] = g["wout"].reshape(d, d)
        g["wpg"] = g["wpg"].reshape(d, d)
        sm = {"norm_mix_g": W["norm_mix_g"][li][None, :], "norm_ffn_g": W["norm_ffn_g"][li][None, :],
              "norm_ple_g": W["norm_ple_g"][li][None, :], "mixer": _mixer_params(li, W, sc_full, cf_full)}
        h, sv = _layer_fwd(li, h, p_all, g, sm)
        saved.append(sv)
        gathered.append(g)
        smalls.append(sm)
    loss, dh, d_final_g = _loss_head(h, W["final_norm_g"][None, :], target)

    big_out = {n: None for n in _BIG}
    small_grads = []
    for li in reversed(range(n_layers)):
        dh, gr, sg = _layer_bwd(li, dh, p_all, gathered[li], smalls[li], saved[li])
        small_grads.append(sg)
        gr["wout"] = gr["wout"].reshape(N_DEV, d // N_DEV, d)
        gr["wpg"] = gr["wpg"].reshape(N_DEV, d // N_DEV, d)
        grads = [gr[n] for n in _BIG]
        from_sib = _exchange_siblings(grads)
        parts = [_chip_partials("rs_sum_" + n, where, g_, s_) for n, g_, s_ in zip(_BIG, grads, from_sib)]
        from_chips = _exchange_chips([pb for pb, _ in parts])
        for n, (_, own), fc in zip(_BIG, parts, from_chips):
            full = _BIG_FULL[n]
            big_out[n] = _finish_weight("adamw_" + n, li, own, fc, W[full], M[full], V[full], big_out[n])
    small_grads.reverse()

    def stacked(fn):
        return jnp.stack([fn(sg) for sg in small_grads])

    mix = lambda i: (lambda sg: sg["mixer"][i])
    grads_small = {
        "norm_mix_g": stacked(lambda sg: sg["norm_mix_g"][0]),
        "sgu_ln_g": stacked(mix(0))[:, :, 0, :], "sgu_ln_b": stacked(mix(1))[:, :, 0, :],
        "sgu_w": stacked(mix(2)), "sgu_b": stacked(mix(3))[:, :, :, 0],
        "sc_conv_w": stacked(mix(4))[:, :SHORT_K], "cf_conv_w": stacked(mix(5))[:, :CONF_K],
        "cf_conv_b": stacked(mix(6))[:, 0], "cf_ln_g": stacked(mix(7))[:, 0], "cf_ln_b": stacked(mix(8))[:, 0],
        "pool_w": stacked(mix(9)), "pool_scale": stacked(mix(10))[:, 0],
        "norm_ffn_g": stacked(lambda sg: sg["norm_ffn_g"][0]), "norm_ple_g": stacked(lambda sg: sg["norm_ple_g"][0]),
        "final_norm_g": d_final_g[0],
    }
    order = _SMALL_REPLICATED + _SMALL_SHARDED
    packed, lay = _pack([grads_small[n] for n in order] + [loss])
    _, summed = _allgather_small("allreduce_small", packed, reduce=True)
    total = dict(zip(order + ("loss",), _unpack(summed, lay)))
    loss_all = total["loss"][0, 0]

    out_g, out_d, out_m, out_v = {}, {}, {}, {}
    pw, lay_r = _pack([W[n] for n in _SMALL_REPLICATED])
    pg, _ = _pack([total[n] for n in _SMALL_REPLICATED])
    pm, _ = _pack([M[n] for n in _SMALL_REPLICATED])
    pv, _ = _pack([V[n] for n in _SMALL_REPLICATED])
    dd, mm, vv = _adamw_small("adamw_small", pw, pg, pm, pv)
    for n, a, b, c_ in zip(_SMALL_REPLICATED, _unpack(dd, lay_r), _unpack(mm, lay_r), _unpack(vv, lay_r)):
        out_g[n], out_d[n], out_m[n], out_v[n] = total[n], a, b, c_
    mine = {n: lax.dynamic_slice_in_dim(total[n], blk * csh, csh, axis=2) for n in _SMALL_SHARDED}
    pw, lay_s = _pack([W[n] for n in _SMALL_SHARDED])
    pg, _ = _pack([mine[n] for n in _SMALL_SHARDED])
    pm, _ = _pack([M[n] for n in _SMALL_SHARDED])
    pv, _ = _pack([V[n] for n in _SMALL_SHARDED])
    dd, mm, vv = _adamw_small("adamw_conv_taps", pw, pg, pm, pv)
    for n, a, b, c_ in zip(_SMALL_SHARDED, _unpack(dd, lay_s), _unpack(mm, lay_s), _unpack(vv, lay_s)):
        out_g[n], out_d[n], out_m[n], out_v[n] = mine[n], a, b, c_
    for n in _BIG:
        for k, dst in enumerate((out_g, out_d, out_m, out_v)):
            dst[_BIG_FULL[n]] = big_out[n][k]

    return (loss_all, dh[None], *[out_g[n] for n in _WEIGHTS], *[out_d[n] for n in _WEIGHTS],
            *[out_m[n] for n in _WEIGHTS], *[out_v[n] for n in _WEIGHTS])


def kernel(x, p, norm_mix_g, w_in, sgu_ln_g, sgu_ln_b, sgu_w, sgu_b, sc_conv_w, cf_conv_w, cf_conv_b, cf_ln_g, cf_ln_b, pool_w, pool_scale, w_out, norm_ffn_g, w_gate, w_up, w_down, norm_ple_g, w_ple_gate, w_ple_proj, final_norm_g, loss_target, m_norm_mix_g, m_w_in, m_sgu_ln_g, m_sgu_ln_b, m_sgu_w, m_sgu_b, m_sc_conv_w, m_cf_conv_w, m_cf_conv_b, m_cf_ln_g, m_cf_ln_b, m_pool_w, m_pool_scale, m_w_out, m_norm_ffn_g, m_w_gate, m_w_up, m_w_down, m_norm_ple_g, m_w_ple_gate, m_w_ple_proj, m_final_norm_g, v_norm_mix_g, v_w_in, v_sgu_ln_g, v_sgu_ln_b, v_sgu_w, v_sgu_b, v_sc_conv_w, v_cf_conv_w, v_cf_conv_b, v_cf_ln_g, v_cf_ln_b, v_pool_w, v_pool_scale, v_w_out, v_norm_ffn_g, v_w_gate, v_w_up, v_w_down, v_norm_ple_g, v_w_ple_gate, v_w_ple_proj, v_final_norm_g):
    given = dict(locals())
    W = {n: given[n] for n in _WEIGHTS}
    M = {n: given["m_" + n] for n in _WEIGHTS}
    V = {n: given["v_" + n] for n in _WEIGHTS}
    return _step(W, M, V, x, p, loss_target)
```

```python
import functools
import math

import jax
import jax.numpy as jnp
from jax import lax
from jax.experimental import pallas as pl
from jax.experimental.pallas import tpu as pltpu

F32 = jnp.float32
BF16 = jnp.bfloat16
EPS = 1e-6
HEAD = 128
GROUP = 4 * HEAD
HALO = 32
SHORT_K = 3
CONF_K = 31
POOL_WINDOWS = (2, 4, 8, 16)
N_DEV = 8
MESH = pl.DeviceIdType.MESH
VMEM_LIMIT_BYTES = 56 * 1024 * 1024

ADAM_LR = 0.001
ADAM_B1 = 0.9
ADAM_B2 = 0.999
ADAM_EPS = 1e-08
ADAM_WD = 0.01
ADAM_STEP = 10

ANY = pl.BlockSpec(memory_space=pl.ANY)


def _params(*sem):
    return pltpu.CompilerParams(dimension_semantics=sem, vmem_limit_bytes=VMEM_LIMIT_BYTES)


def _rms(x, g):
    return x * lax.rsqrt(jnp.mean(x * x, axis=-1, keepdims=True) + EPS) * g


def _ln(x, g, b):
    mu = jnp.mean(x, axis=-1, keepdims=True)
    xc = x - mu
    var = jnp.mean(xc * xc, axis=-1, keepdims=True)
    return xc * lax.rsqrt(var + EPS) * g + b


def _bdot(a, b, dims=(((1,), (0,)), ((), ()))):
    return lax.dot_general(a.astype(BF16), b.astype(BF16), dims, preferred_element_type=F32)


def _sgu_piece(zu, zv, lg, lb, w, b):
    u = jax.nn.gelu(zu)
    v = _ln(jax.nn.gelu(zv), lg, lb)
    row = lax.broadcasted_iota(jnp.int32, w.shape, 0)
    col = lax.broadcasted_iota(jnp.int32, w.shape, 1)
    wm = jnp.where(row >= col, w, 0.0)
    return u * (_bdot(wm, v) + b)


def _conf_post(c, g, b):
    return jax.nn.silu(_ln(c, g, b))


def _pool_count(first_pos, rows, w):
    pos = first_pos + lax.broadcasted_iota(jnp.int32, (rows, 1), 0) + 1
    return jnp.minimum(pos, w).astype(F32)


_DIMS = {
    "nn": (((1,), (0,)), ((), ())),
    "nt": (((1,), (1,)), ((), ())),
    "tn": (((0,), (0,)), ((), ())),
}


def _tiles(name, grid, ins, in_specs, outs, out_specs, compute, summed=()):
    ni = len(ins)

    def body(*refs):
        vals = compute(*refs[:ni])
        first = functools.reduce(jnp.logical_and, [pl.program_id(a) == 0 for a in range(len(grid))])
        for idx, (r, v) in enumerate(zip(refs[ni:], vals)):
            if idx in summed:
                @pl.when(first)
                def _(r=r):
                    r[...] = jnp.zeros_like(r)

                r[...] += v
            else:
                r[...] = v.astype(r.dtype)

    sem = ("arbitrary" if summed else "parallel",) * len(grid)
    return pl.pallas_call(
        body, name=name, grid=grid, in_specs=list(in_specs), out_specs=list(out_specs), out_shape=list(outs),
        compiler_params=_params(*sem),
    )(*ins)


def _resident(shape):
    return pl.BlockSpec(shape, lambda *_: (0,) * len(shape), pipeline_mode=pl.Buffered(1))


def _sds(shape, dtype):
    return jax.ShapeDtypeStruct(tuple(shape), dtype)


def _tile(n, want):
    t = min(n, want)
    assert n % t == 0, (n, want)
    return t


def _rms_fwd(name, h, g):
    s, d = h.shape
    tm = _tile(s, 512)

    def body(h_ref, g_ref, y_ref):
        y_ref[...] = _rms(h_ref[...], g_ref[...]).astype(BF16)

    return pl.pallas_call(
        body, name=name, grid=(s // tm,),
        in_specs=[pl.BlockSpec((tm, d), lambda i: (i, 0)), pl.BlockSpec((1, d), lambda i: (0, 0))],
        out_specs=pl.BlockSpec((tm, d), lambda i: (i, 0)),
        out_shape=_sds((s, d), BF16),
        compiler_params=_params("parallel"),
    )(h, g)


def _rms_back(h, g, dy, dh_in):
    _, vjp = jax.vjp(_rms, h, g)
    dh, dg = vjp(dy)
    return dh_in + dh, dg


def _rms_bwd(name, h, g, dy, dh_in):
    s, d = h.shape
    tm = _tile(s, 256)

    def body(h_ref, g_ref, dy_ref, dhin_ref, dh_ref, dh16_ref, dg_ref):
        dh, dg = _rms_back(h_ref[...], g_ref[...], dy_ref[...].astype(F32), dhin_ref[...])
        dh_ref[...] = dh
        dh16_ref[...] = dh.astype(BF16)

        @pl.when(pl.program_id(0) == 0)
        def _():
            dg_ref[...] = jnp.zeros_like(dg_ref)

        dg_ref[...] += dg

    tok = pl.BlockSpec((tm, d), lambda i: (i, 0))
    vec = pl.BlockSpec((1, d), lambda i: (0, 0))
    return pl.pallas_call(
        body, name=name, grid=(s // tm,),
        in_specs=[tok, vec, tok, tok],
        out_specs=[tok, tok, vec],
        out_shape=[_sds((s, d), F32), _sds((s, d), BF16), _sds((1, d), F32)],
        compiler_params=_params("arbitrary"),
    )(h, g, dy, dh_in)


def _loss_head(h, g, target):
    s, d = h.shape
    tm = _tile(s, 256)

    def body(h_ref, g_ref, t_ref, loss_ref, dh_ref, dg_ref):
        y, vjp = jax.vjp(_rms, h_ref[...], g_ref[...])
        err = y - t_ref[...]
        dh, dg = vjp(err * (1.0 / d))
        dh_ref[...] = dh
        per_token = jnp.mean(err * err, axis=-1, keepdims=True)
        part = 0.5 * jnp.sum(per_token, axis=0, keepdims=True)

        @pl.when(pl.program_id(0) == 0)
        def _():
            dg_ref[...] = jnp.zeros_like(dg_ref)
            loss_ref[...] = jnp.zeros_like(loss_ref)

        dg_ref[...] += dg
        loss_ref[...] += part

    tok = pl.BlockSpec((tm, d), lambda i: (i, 0))
    vec = pl.BlockSpec((1, d), lambda i: (0, 0))
    return pl.pallas_call(
        body, name="loss_head", grid=(s // tm,),
        in_specs=[tok, vec, tok],
        out_specs=[pl.BlockSpec((1, 1), lambda i: (0, 0)), tok, vec],
        out_shape=[_sds((1, 1), F32), _sds((s, d), F32), _sds((1, d), F32)],
        compiler_params=_params("arbitrary"),
    )(h, g, target)


def _ple_bwd(name, dh, pg_pre, pp):
    s, d = dh.shape
    tm = _tile(s, 512)

    def body(dh_ref, pg_ref, pp_ref, dpg_ref, dpp_ref):
        sg = jax.nn.sigmoid(pg_ref[...].astype(F32))
        dhv = dh_ref[...]
        dpg_ref[...] = (dhv * pp_ref[...].astype(F32) * sg * (1.0 - sg)).astype(BF16)
        dpp_ref[...] = (dhv * sg).astype(BF16)

    tok = pl.BlockSpec((tm, d), lambda i: (i, 0))
    return pl.pallas_call(
        body, name=name, grid=(s // tm,),
        in_specs=[tok, tok, tok], out_specs=[tok, tok],
        out_shape=[_sds((s, d), BF16), _sds((s, d), BF16)],
        compiler_params=_params("parallel"),
    )(dh, pg_pre, pp)


_U, _V, _HB, _BG, _CG, _CA, _CGT, _PD = (GROUP * i for i in range(8))


def _cols(c0, w=GROUP):
    return slice(c0, c0 + w)


def _mixer_param_specs():
    full = lambda *shape: pl.BlockSpec(shape, lambda i: (0,) * len(shape))
    return [
        full(4, 1, HEAD), full(4, 1, HEAD), full(4, HEAD, HEAD), full(4, HEAD, 1),
        full(8, GROUP), full(32, GROUP), full(1, GROUP), full(1, GROUP), full(1, GROUP),
        full(4, HEAD, HEAD), full(1, GROUP),
    ]


def _mixer_fwd(name, z, prm):
    s = z.shape[0]
    t = _tile(s, 256)
    hb = t // HALO

    def body(zp_ref, zm_ref, lg_ref, lb_ref, sw_ref, sb_ref, scw_ref, cfw_ref, cfb_ref, cg_ref, cb_ref,
             pw_ref, ps_ref, o_ref, ext_ref):
        i = pl.program_id(0)
        keep = (i > 0).astype(F32)
        main = pl.ds(HALO, t)

        for n in range(t // HEAD):
            rows = slice(n * HEAD, (n + 1) * HEAD)
            for hh in range(4):
                cu = _cols(_U + hh * HEAD, HEAD)
                cv = _cols(_V + hh * HEAD, HEAD)
                o_ref[rows, _cols(hh * HEAD, HEAD)] = _sgu_piece(
                    zm_ref[rows, cu], zm_ref[rows, cv], lg_ref[hh], lb_ref[hh], sw_ref[hh], sb_ref[hh]
                ).astype(BF16)

        ext_ref[0:HALO, :] = zp_ref[:, _cols(_CG)] * zp_ref[:, _cols(_HB)] * keep
        ext_ref[main, :] = zm_ref[:, _cols(_CG)] * zm_ref[:, _cols(_HB)]
        y = jnp.zeros((t, GROUP), F32)
        for k in range(SHORT_K):
            y = y + scw_ref[k:k + 1, :] * ext_ref[pl.ds(HALO - (SHORT_K - 1) + k, t), :]
        o_ref[:, _cols(GROUP)] = (zm_ref[:, _cols(_BG)] * y).astype(BF16)

        ext_ref[0:HALO, :] = zp_ref[:, _cols(_CA)] * jax.nn.sigmoid(zp_ref[:, _cols(_CGT)]) * keep
        ext_ref[main, :] = zm_ref[:, _cols(_CA)] * jax.nn.sigmoid(zm_ref[:, _cols(_CGT)])
        c = jnp.zeros((t, GROUP), F32) + cfb_ref[...]
        for k in range(CONF_K):
            c = c + cfw_ref[k:k + 1, :] * ext_ref[pl.ds(HALO - (CONF_K - 1) + k, t), :]
        o_ref[:, _cols(2 * GROUP)] = _conf_post(c, cg_ref[...], cb_ref[...]).astype(BF16)

        ext_ref[0:HALO, :] = zp_ref[:, _cols(_PD)] * keep
        ext_ref[main, :] = zm_ref[:, _cols(_PD)]
        for gi, w in enumerate(POOL_WINDOWS):
            cc = _cols(gi * HEAD, HEAD)
            acc = ext_ref[main, cc]
            for j in range(1, w):
                acc = acc + ext_ref[pl.ds(HALO - j, t), cc]
            q = acc / _pool_count(i * t, t, w) - ext_ref[main, cc]
            o_ref[:, _cols(3 * GROUP + gi * HEAD, HEAD)] = (_bdot(q, pw_ref[gi]) * ps_ref[:, cc]).astype(BF16)

    return pl.pallas_call(
        body, name=name, grid=(s // t,),
        in_specs=[pl.BlockSpec((HALO, 8 * GROUP), lambda i: (jnp.maximum(i * hb - 1, 0), 0)),
                  pl.BlockSpec((t, 8 * GROUP), lambda i: (i, 0)),
                  *_mixer_param_specs()],
        out_specs=pl.BlockSpec((t, 4 * GROUP), lambda i: (i, 0)),
        out_shape=_sds((s, 4 * GROUP), BF16),
        scratch_shapes=[pltpu.VMEM((HALO + t, GROUP), F32)],
        compiler_params=_params("parallel"),
    )(z, z, *prm)


def _mixer_bwd(name, z, dmo, prm):
    s = z.shape[0]
    t = _tile(s, 256)
    hb = t // HALO
    nt = s // t
    last_halo = s // HALO - 1

    def body(zp_ref, zm_ref, zn_ref, dm_ref, dn_ref,
             lg_ref, lb_ref, sw_ref, sb_ref, scw_ref, cfw_ref, cfb_ref, cg_ref, cb_ref, pw_ref, ps_ref,
             dz_ref, dlg_ref, dlb_ref, dsw_ref, dsb_ref, dscw_ref, dcfw_ref, dcfb_ref, dcg_ref, dcb_ref,
             dpw_ref, dps_ref, extp_ref, extn_ref, extc_ref):
        i = pl.program_id(0)
        keep_prev = (i > 0).astype(F32)
        keep_next = (i < nt - 1).astype(F32)
        main = pl.ds(HALO, t)

        @pl.when(i == 0)
        def _():
            for r in (dlg_ref, dlb_ref, dsw_ref, dsb_ref, dscw_ref, dcfw_ref, dcfb_ref, dcg_ref, dcb_ref,
                      dpw_ref, dps_ref):
                r[...] = jnp.zeros_like(r)

        def rowsum(v):
            return jnp.sum(v, axis=0, keepdims=True)

        for n in range(t // HEAD):
            rows = slice(n * HEAD, (n + 1) * HEAD)
            for hh in range(4):
                cu = _cols(_U + hh * HEAD, HEAD)
                cv = _cols(_V + hh * HEAD, HEAD)
                _, vjp = jax.vjp(_sgu_piece, zm_ref[rows, cu], zm_ref[rows, cv],
                                 lg_ref[hh], lb_ref[hh], sw_ref[hh], sb_ref[hh])
                dzu, dzv, dlg, dlb, dsw, dsb = vjp(dm_ref[rows, _cols(hh * HEAD, HEAD)].astype(F32))
                dz_ref[rows, cu] = dzu.astype(BF16)
                dz_ref[rows, cv] = dzv.astype(BF16)
                dlg_ref[hh] += dlg
                dlb_ref[hh] += dlb
                dsw_ref[hh] += dsw
                dsb_ref[hh] += dsb

        extp_ref[0:HALO, :] = zp_ref[:, _cols(_CG)] * zp_ref[:, _cols(_HB)] * keep_prev
        extp_ref[main, :] = zm_ref[:, _cols(_CG)] * zm_ref[:, _cols(_HB)]
        dob = dm_ref[:, _cols(GROUP)].astype(F32)
        dy = dob * zm_ref[:, _cols(_BG)]
        extn_ref[0:t, :] = dy
        extn_ref[t:t + HALO, :] = dn_ref[:, _cols(GROUP)].astype(F32) * zn_ref[:, _cols(_BG)] * keep_next
        y = jnp.zeros((t, GROUP), F32)
        dx = jnp.zeros((t, GROUP), F32)
        for k in range(SHORT_K):
            xs = extp_ref[pl.ds(HALO - (SHORT_K - 1) + k, t), :]
            y = y + scw_ref[k:k + 1, :] * xs
            dscw_ref[k:k + 1, :] += rowsum(dy * xs)
            dx = dx + scw_ref[k:k + 1, :] * extn_ref[pl.ds((SHORT_K - 1) - k, t), :]
        dz_ref[:, _cols(_BG)] = (dob * y).astype(BF16)
        dz_ref[:, _cols(_CG)] = (dx * zm_ref[:, _cols(_HB)]).astype(BF16)
        dz_ref[:, _cols(_HB)] = (dx * zm_ref[:, _cols(_CG)]).astype(BF16)

        extc_ref[0:HALO, :] = zp_ref[:, _cols(_CA)] * jax.nn.sigmoid(zp_ref[:, _cols(_CGT)]) * keep_prev
        sg = jax.nn.sigmoid(zm_ref[:, _cols(_CGT)])
        extc_ref[main, :] = zm_ref[:, _cols(_CA)] * sg
        extc_ref[HALO + t:HALO + t + HALO, :] = zn_ref[:, _cols(_CA)] * jax.nn.sigmoid(zn_ref[:, _cols(_CGT)])
        c_main = jnp.zeros((t, GROUP), F32) + cfb_ref[...]
        c_next = jnp.zeros((HALO, GROUP), F32) + cfb_ref[...]
        for k in range(CONF_K):
            wk = cfw_ref[k:k + 1, :]
            c_main = c_main + wk * extc_ref[pl.ds(HALO - (CONF_K - 1) + k, t), :]
            c_next = c_next + wk * extc_ref[pl.ds(HALO + t - (CONF_K - 1) + k, HALO), :]
        _, vjp = jax.vjp(_conf_post, c_main, cg_ref[...], cb_ref[...])
        dc, dcg, dcb = vjp(dm_ref[:, _cols(2 * GROUP)].astype(F32))
        dcg_ref[...] += dcg
        dcb_ref[...] += dcb
        dcfb_ref[...] += rowsum(dc)
        _, vjp_next = jax.vjp(lambda cv: _conf_post(cv, cg_ref[...], cb_ref[...]), c_next)
        (dc_next,) = vjp_next(dn_ref[:, _cols(2 * GROUP)].astype(F32) * keep_next)
        extn_ref[0:t, :] = dc
        extn_ref[t:t + HALO, :] = dc_next
        dhc = jnp.zeros((t, GROUP), F32)
        for k in range(CONF_K):
            dhc = dhc + cfw_ref[k:k + 1, :] * extn_ref[pl.ds((CONF_K - 1) - k, t), :]
            dcfw_ref[k:k + 1, :] += rowsum(dc * extc_ref[pl.ds(HALO - (CONF_K - 1) + k, t), :])
        dz_ref[:, _cols(_CA)] = (dhc * sg).astype(BF16)
        dz_ref[:, _cols(_CGT)] = (dhc * zm_ref[:, _cols(_CA)] * sg * (1.0 - sg)).astype(BF16)

        extp_ref[0:HALO, :] = zp_ref[:, _cols(_PD)] * keep_prev
        extp_ref[main, :] = zm_ref[:, _cols(_PD)]
        for gi, w in enumerate(POOL_WINDOWS):
            cc = _cols(gi * HEAD, HEAD)
            oc = _cols(3 * GROUP + gi * HEAD, HEAD)
            acc = extp_ref[main, cc]
            for j in range(1, w):
                acc = acc + extp_ref[pl.ds(HALO - j, t), cc]
            count = _pool_count(i * t, t, w)
            q = acc / count - extp_ref[main, cc]
            dod = dm_ref[:, oc].astype(F32)
            dps_ref[:, cc] += rowsum(dod * _bdot(q, pw_ref[gi]))
            ds = dod * ps_ref[:, cc]
            dpw_ref[gi] += _bdot(q, ds, _DIMS["tn"])
            dq = _bdot(ds, pw_ref[gi], _DIMS["nt"])
            ds_next = dn_ref[:, oc].astype(F32) * ps_ref[:, cc] * keep_next
            dq_next = _bdot(ds_next, pw_ref[gi], _DIMS["nt"])
            extn_ref[0:t, cc] = dq / count
            extn_ref[t:t + HALO, cc] = dq_next * (1.0 / w)
            back = extn_ref[0:t, cc]
            for j in range(1, w):
                back = back + extn_ref[pl.ds(j, t), cc]
            dz_ref[:, _cols(_PD + gi * HEAD, HEAD)] = (back - dq).astype(BF16)

    full = lambda *shape: pl.BlockSpec(shape, lambda i: (0,) * len(shape))
    grad_specs = [full(4, 1, HEAD), full(4, 1, HEAD), full(4, HEAD, HEAD), full(4, HEAD, 1),
                  full(8, GROUP), full(32, GROUP), full(1, GROUP), full(1, GROUP), full(1, GROUP),
                  full(4, HEAD, HEAD), full(1, GROUP)]
    grad_shapes = [_sds(sp.block_shape, F32) for sp in grad_specs]
    nxt = lambda i: (jnp.minimum((i + 1) * hb, last_halo), 0)
    return pl.pallas_call(
        body, name=name, grid=(nt,),
        in_specs=[pl.BlockSpec((HALO, 8 * GROUP), lambda i: (jnp.maximum(i * hb - 1, 0), 0)),
                  pl.BlockSpec((t, 8 * GROUP), lambda i: (i, 0)),
                  pl.BlockSpec((HALO, 8 * GROUP), nxt),
                  pl.BlockSpec((t, 4 * GROUP), lambda i: (i, 0)),
                  pl.BlockSpec((HALO, 4 * GROUP), nxt),
                  *_mixer_param_specs()],
        out_specs=[pl.BlockSpec((t, 8 * GROUP), lambda i: (i, 0)), *grad_specs],
        out_shape=[_sds((s, 8 * GROUP), BF16), *grad_shapes],
        scratch_shapes=[pltpu.VMEM((HALO + t, GROUP), F32), pltpu.VMEM((t + HALO, GROUP), F32),
                        pltpu.VMEM((HALO + t + HALO, GROUP), F32)],
        compiler_params=_params("arbitrary"),
    )(z, z, z, dmo, dmo, *prm)


def _dot(a, b, kind="nn"):
    return _bdot(a, b, _DIMS[kind])


def _chain(terms, kind):
    acc = None
    for a, b in terms:
        p = _dot(a, b, kind)
        acc = p if acc is None else acc + p
    return acc


def _after(g, token):
    return g if token is None else g + token[0:1, 0:1]


def _layer_fwd(li, h, p_all, gw, sm, start_token=None, between=None):
    s, d = h.shape
    nb, f8, _ = gw["wg"].shape
    pd = p_all.shape[2]
    sv = {"h": h}
    rows = lambda t, w: pl.BlockSpec((t, w), lambda i: (i, 0))

    y1 = _rms_fwd("rms_mix", h, _after(sm["norm_mix_g"], start_token))
    tm = _tile(s, 1024)
    (z,) = _tiles(
        "w_in_fwd", (s // tm, nb),
        [y1, gw["win"]],
        [pl.BlockSpec((tm, d), lambda i, j: (i, 0)), pl.BlockSpec((None, d, GROUP), lambda i, j: (j, 0, 0))],
        [_sds((s, nb * GROUP), F32)], [pl.BlockSpec((tm, GROUP), lambda i, j: (i, j))],
        lambda a, w: (_dot(a[...], w[...]),))
    mo = _mixer_fwd("mixer_fwd", z, sm["mixer"])
    tm = _tile(s, 512)
    (h2,) = _tiles(
        "w_out_fwd", (s // tm,),
        [mo, gw["wout"], h], [rows(tm, d), _resident((d, d)), rows(tm, d)],
        [_sds((s, d), F32)], [rows(tm, d)],
        lambda a, w, hv: (hv[...] + _dot(a[...], w[...]),))
    sv.update(y1=y1, z=z, mo=mo, h2=h2)

    y2 = _rms_fwd("rms_ffn", h2, _after(sm["norm_ffn_g"], None if between is None else between(h2)))
    gate_pre, up_pre, hmid = _ffn_up("ffn_up_fwd", y2, gw["wg"], gw["wu"])
    tm, tn = _tile(s, 512), _tile(d, 1024)
    (h3,) = _tiles(
        "w_down_fwd", (d // tn, s // tm),
        [hmid, gw["wd"], h2],
        [pl.BlockSpec((nb, tm, f8), lambda j, i: (0, i, 0)), pl.BlockSpec((nb, f8, tn), lambda j, i: (0, 0, j)),
         pl.BlockSpec((tm, tn), lambda j, i: (i, j))],
        [_sds((s, d), F32)], [pl.BlockSpec((tm, tn), lambda j, i: (i, j))],
        lambda a, w, hv: (hv[...] + _chain([(a[k], w[k]) for k in range(nb)], "nn"),))
    sv.update(y2=y2, gate_pre=gate_pre, up_pre=up_pre, hmid=hmid, h3=h3)

    y3 = _rms_fwd("rms_ple", h3, sm["norm_ple_g"])
    tm = _tile(s, 1024)
    (pp,) = _tiles(
        "w_ple_proj_fwd", (s // tm, nb),
        [p_all, gw["wpp"]],
        [pl.BlockSpec((None, tm, pd), lambda i, j: (li, i, 0)), pl.BlockSpec((None, pd, pd), lambda i, j: (j, 0, 0))],
        [_sds((s, d), BF16)], [pl.BlockSpec((tm, pd), lambda i, j: (i, j))],
        lambda a, w: (_dot(a[...], w[...]),))
    tm = _tile(s, 256)

    def ple(a, w, hv, ppv):
        pg = _dot(a[...], w[...])
        return hv[...] + jax.nn.sigmoid(pg) * ppv[...].astype(F32), pg

    h4, pg_pre = _tiles(
        "w_ple_gate_fwd", (s // tm,),
        [y3, gw["wpg"], h3, pp], [rows(tm, d), _resident((d, d)), rows(tm, d), rows(tm, d)],
        [_sds((s, d), F32), _sds((s, d), BF16)], [rows(tm, d), rows(tm, d)], ple)
    sv.update(y3=y3, pp=pp, pg_pre=pg_pre)
    return h4, sv


def _ffn_up(name, y2, wg, wu):
    s, d = y2.shape
    nb, f8, _ = wg.shape
    tm = _tile(s, 1024)

    def compute(y_ref, wg_ref, wu_ref):
        yv = y_ref[...]
        g = _dot(yv, wg_ref[...], "nt")
        u = _dot(yv, wu_ref[...], "nt")
        return g, u, jax.nn.silu(g) * u

    wspec = pl.BlockSpec((None, f8, d), lambda i, j: (j, 0, 0))
    ospec = pl.BlockSpec((None, tm, f8), lambda i, j: (j, i, 0))
    return _tiles(name, (s // tm, nb), [y2, wg, wu], [pl.BlockSpec((tm, d), lambda i, j: (i, 0)), wspec, wspec],
                  [_sds((nb, s, f8), BF16)] * 3, [ospec] * 3, compute)


def _swiglu_bwd(dm, g, u):
    sg = jax.nn.sigmoid(g)
    silu = g * sg
    return dm * u * (sg + silu * (1.0 - sg)), dm * silu


def _layer_bwd(li, dh, p_all, gw, sm, sv, start_token=None, between=None):
    s, d = dh.shape
    nb, f8, _ = gw["wg"].shape
    pd = p_all.shape[2]
    gr, sg = {}, {}
    rows = lambda t, w: pl.BlockSpec((t, w), lambda i: (i, 0))
    vec = pl.BlockSpec((1, d), lambda i: (0, 0))
    tw = _tile(d, 1024)
    whole = lambda w: pl.BlockSpec((s, w), lambda i, j: (0, i))
    whole_j = lambda w: pl.BlockSpec((s, w), lambda i, j: (0, j))
    tn_dot = lambda a, b: (_dot(a[...], b[...], "tn"),)

    dpg, dpp = _ple_bwd("ple_bwd", dh, sv["pg_pre"], sv["pp"])
    (gr["wpp"],) = _tiles(
        "w_ple_proj_dw", (nb,),
        [p_all, dpp], [pl.BlockSpec((None, s, pd), lambda j: (li, 0, 0)), pl.BlockSpec((s, pd), lambda j: (0, j))],
        [_sds((nb, pd, pd), F32)], [pl.BlockSpec((None, pd, pd), lambda j: (j, 0, 0))], tn_dot)
    (gr["wpg"],) = _tiles(
        "w_ple_gate_dw", (d // tw, d // tw), [sv["y3"], dpg], [whole(tw), whole_j(tw)],
        [_sds((d, d), F32)], [pl.BlockSpec((tw, tw), lambda i, j: (i, j))], tn_dot)
    tm = _tile(s, 256)

    def ple_dx(a, w, hv, gv, dv):
        dh3, dg = _rms_back(hv[...], gv[...], _dot(a[...], w[...], "nt"), dv[...])
        return dh3, dh3, dg

    dh3, dh3_16, sg["norm_ple_g"] = _tiles(
        "w_ple_gate_dx", (s // tm,),
        [dpg, gw["wpg"], sv["h3"], _after(sm["norm_ple_g"], start_token), dh],
        [rows(tm, d), _resident((d, d)), rows(tm, d), vec, rows(tm, d)],
        [_sds((s, d), F32), _sds((s, d), BF16), _sds((1, d), F32)], [rows(tm, d), rows(tm, d), vec],
        ple_dx, summed=(2,))

    tm = _tile(s, 1024)
    blk_rows = pl.BlockSpec((None, tm, f8), lambda i, j: (j, i, 0))
    dgate, dup = _tiles(
        "w_down_dx", (s // tm, nb),
        [dh3_16, gw["wd"], sv["gate_pre"], sv["up_pre"]],
        [pl.BlockSpec((tm, d), lambda i, j: (i, 0)), pl.BlockSpec((None, f8, d), lambda i, j: (j, 0, 0)), blk_rows, blk_rows],
        [_sds((nb, s, f8), BF16)] * 2, [blk_rows] * 2,
        lambda a, w, g, u: _swiglu_bwd(_dot(a[...], w[...], "nt"), g[...].astype(F32), u[...].astype(F32)))
    (gr["wd"],) = _tiles(
        "w_down_dw", (nb, d // tw),
        [sv["hmid"], dh3_16], [pl.BlockSpec((None, s, f8), lambda i, j: (i, 0, 0)), whole_j(tw)],
        [_sds((nb, f8, d), F32)], [pl.BlockSpec((None, f8, tw), lambda i, j: (i, 0, j))], tn_dot)
    mid_token = None if between is None else between(dgate)
    for nm, dact in (("wg", dgate), ("wu", dup)):
        (gr[nm],) = _tiles(
            "w_" + {"wg": "gate", "wu": "up"}[nm] + "_dw", (d // tw, nb),
            [dact, sv["y2"]], [pl.BlockSpec((None, s, f8), lambda i, j: (j, 0, 0)), whole(tw)],
            [_sds((nb, f8, d), F32)], [pl.BlockSpec((None, f8, tw), lambda i, j: (j, 0, i))], tn_dot)
    tm, tn = _tile(s, 512), _tile(d, 1024)
    act_spec = pl.BlockSpec((nb, tm, f8), lambda j, i: (0, i, 0))
    wt_spec = pl.BlockSpec((nb, f8, tn), lambda j, i: (0, 0, j))
    out_spec = pl.BlockSpec((tm, tn), lambda j, i: (i, j))
    (dy2,) = _tiles(
        "w_gate_dx", (d // tn, s // tm), [dgate, gw["wg"]], [act_spec, wt_spec],
        [_sds((s, d), F32)], [out_spec],
        lambda a, w: (_chain([(a[k], w[k]) for k in range(nb)], "nn"),))
    (dy2,) = _tiles(
        "w_up_dx", (d // tn, s // tm), [dup, gw["wu"], dy2], [act_spec, wt_spec, out_spec],
        [_sds((s, d), F32)], [out_spec],
        lambda a, w, prev: (prev[...] + _chain([(a[k], w[k]) for k in range(nb)], "nn"),))
    dh2, dh2_16, sg["norm_ffn_g"] = _rms_bwd("rms_ffn_bwd", sv["h2"], _after(sm["norm_ffn_g"], mid_token), dy2, dh3)

    tm = _tile(s, 512)
    (dmo,) = _tiles(
        "w_out_dx", (s // tm,), [dh2_16, gw["wout"]], [rows(tm, d), _resident((d, d))],
        [_sds((s, d), BF16)], [rows(tm, d)], lambda a, w: (_dot(a[...], w[...], "nt"),))
    (gr["wout"],) = _tiles(
        "w_out_dw", (d // tw, d // tw), [sv["mo"], dh2_16], [whole(tw), whole_j(tw)],
        [_sds((d, d), F32)], [pl.BlockSpec((tw, tw), lambda i, j: (i, j))], tn_dot)
    dz, *mix_grads = _mixer_bwd("mixer_bwd", sv["z"], dmo, sm["mixer"])
    sg["mixer"] = mix_grads
    (gr["win"],) = _tiles(
        "w_in_dw", (d // tw, nb), [sv["y1"], dz], [whole(tw), pl.BlockSpec((s, GROUP), lambda i, j: (0, j))],
        [_sds((nb, d, GROUP), F32)], [pl.BlockSpec((None, tw, GROUP), lambda i, j: (j, i, 0))], tn_dot)
    tm = _tile(s, 256)

    def in_dx(a, w, hv, gv, dv):
        dy1 = _chain([(a[:, k * GROUP:(k + 1) * GROUP], w[k]) for k in range(nb)], "nt")
        return _rms_back(hv[...], gv[...], dy1, dv[...])

    dh_in, sg["norm_mix_g"] = _tiles(
        "w_in_dx", (s // tm,),
        [dz, gw["win"], sv["h"], sm["norm_mix_g"], dh2],
        [rows(tm, nb * GROUP), _resident((nb, d, GROUP)), rows(tm, d), vec, rows(tm, d)],
        [_sds((s, d), F32), _sds((1, d), F32)], [rows(tm, d), vec], in_dx, summed=(1,))
    return dh_in, gr, sg


def _place():
    return lax.axis_index("x"), lax.axis_index("y"), lax.axis_index("c")


_HBM = pl.BlockSpec(memory_space=pltpu.HBM)
_SEM = pl.BlockSpec(memory_space=pltpu.SEMAPHORE)


def _split_call(name, bufs, old_sems, n_new, after, body, want_token):
    nb, no = len(bufs), len(old_sems)
    extra = [] if after is None else [after]

    def kbody(*refs):
        new = refs[nb + no + len(extra):nb + no + len(extra) + n_new]
        body(refs[:nb], refs[nb:nb + no], new)
        if want_token:
            refs[-1][...] = jnp.zeros_like(refs[-1])

    outs = pl.pallas_call(
        kbody, name=name,
        out_shape=tuple([pltpu.SemaphoreType.DMA(())] * n_new + [pltpu.HBM(b.shape, b.dtype) for b in bufs]
                        + ([_sds((8, 128), F32)] if want_token else [])),
        in_specs=[_HBM] * nb + [_SEM] * no + [ANY] * len(extra),
        out_specs=tuple([_SEM] * n_new + [_HBM] * nb
                        + ([pl.BlockSpec(memory_space=pltpu.VMEM)] if want_token else [])),
        input_output_aliases={i: n_new + i for i in range(nb)},
        compiler_params=pltpu.CompilerParams(has_side_effects=pltpu.SideEffectType.DATAFLOW_SIDE_EFFECTING),
    )(*[pltpu.with_memory_space_constraint(b, pltpu.HBM) for b in bufs], *old_sems, *extra)
    return list(outs[:n_new]), list(outs[n_new:n_new + nb]), (outs[-1] if want_token else None)


def _remote(ref_src, ref_dst, send_sem, recv_sem, to):
    return pltpu.make_async_remote_copy(src_ref=ref_src, dst_ref=ref_dst, send_sem=send_sem, recv_sem=recv_sem,
                                        device_id=to, device_id_type=MESH)


def _to_bf16(name, w):
    n_layers, r, cdim = w.shape
    tr = _row_tile(r, 256)

    def body(w_ref, o_ref):
        o_ref[...] = w_ref[...].astype(BF16)

    spec = pl.BlockSpec((n_layers, tr, cdim), lambda i: (0, i, 0))
    return pl.pallas_call(body, name=name, grid=(r // tr,), in_specs=[spec], out_specs=spec,
                          out_shape=_sds(w.shape, BF16), compiler_params=_params("parallel"))(w)


def _place_shards(shards):
    nw = len(shards)
    n_layers = shards[0].shape[0]

    def body(*refs):
        srcs, outs, sem = refs[:nw], refs[nw:-1], refs[-1]
        x, y, c = _place()
        copies = [pltpu.make_async_copy(srcs[w].at[l], outs[l * nw + w].at[4 * x + 2 * y + c], sem.at[l * nw + w])
                  for l in range(n_layers) for w in range(nw)]
        for cp in copies:
            cp.start()
        for cp in copies:
            cp.wait()

    outs = pl.pallas_call(
        body, name="place_shards", in_specs=[ANY] * nw, out_specs=[ANY] * (nw * n_layers),
        out_shape=[_sds((N_DEV,) + shards[w].shape[1:], BF16) for _ in range(n_layers) for w in range(nw)],
        scratch_shapes=[pltpu.SemaphoreType.DMA((nw * n_layers,))],
    )(*shards)
    return [list(outs[l * nw:(l + 1) * nw]) for l in range(n_layers)]


def _gather_start(li, lands, after):
    nw = len(lands)

    def body(bufs, _, new):
        x, y, c = _place()
        chips = [(1 - x, y), (x, 1 - y), (1 - x, 1 - y)]
        for w in range(nw):
            mine = bufs[w].at[4 * x + 2 * y + c]
            _remote(mine, mine, new[8 * w], new[8 * w + 4], (x, y, 1 - c)).start()
            for j, (px, py) in enumerate(chips):
                _remote(mine, mine, new[8 * w + 1 + j], new[8 * w + 5 + j], (px, py, c)).start()

    return _split_call("gather_start_l%d" % li, lands, [], 8 * nw, after, body, True)


def _gather_forward(li, lands, sems, after):
    nw = len(lands)
    arrivals = [sems[8 * w + 5 + j] for w in range(nw) for j in range(3)]

    def body(bufs, old, new):
        x, y, c = _place()
        chips = [(1 - x, y), (x, 1 - y), (1 - x, 1 - y)]
        for j, (px, py) in enumerate(chips):
            for w in range(nw):
                got = bufs[w].at[4 * px + 2 * py + c]
                _remote(got, got, new[6 * w + j], old[3 * w + j], (px, py, c)).wait_recv()
                _remote(got, got, new[6 * w + j], new[6 * w + 3 + j], (x, y, 1 - c)).start()

    return _split_call("gather_forward_l%d" % li, lands, arrivals, 6 * nw, after, body, True)


def _gather_wait(li, lands, sems, fwd_sems, after):
    nw = len(lands)
    first = [sems[8 * w + k] for w in range(nw) for k in range(5)]

    def body(bufs, old, _):
        x, y, c = _place()
        sib = (x, y, 1 - c)
        chips = [(1 - x, y), (x, 1 - y), (1 - x, 1 - y)]
        n1 = 5 * nw
        for w in range(nw):
            mine = bufs[w].at[4 * x + 2 * y + c]
            theirs = bufs[w].at[4 * x + 2 * y + 1 - c]
            _remote(theirs, theirs, old[5 * w], old[5 * w + 4], sib).wait_recv()
            for k in range(4):
                _remote(mine, mine, old[5 * w + k], old[5 * w + 4], sib).wait_send()
            for j, (px, py) in enumerate(chips):
                sent = bufs[w].at[4 * px + 2 * py + c]
                got = bufs[w].at[4 * px + 2 * py + 1 - c]
                _remote(sent, sent, old[n1 + 6 * w + j], old[n1 + 6 * w + 3 + j], sib).wait_send()
                _remote(got, got, old[n1 + 6 * w + j], old[n1 + 6 * w + 3 + j], sib).wait_recv()

    _, lands, _ = _split_call("gather_wait_l%d" % li, lands, first + list(fwd_sems), 0, after, body, False)
    return lands


def _siblings_start(li, grads, after):
    nw = len(grads)
    lands = [lax.empty((4,) + g.shape[1:], g.dtype) for g in grads]

    def body(bufs, _, new):
        x, y, c = _place()
        for w in range(nw):
            for q in range(4):
                _remote(bufs[w].at[2 * q + (1 - c)], bufs[nw + w].at[q], new[8 * w + q], new[8 * w + 4 + q],
                        (x, y, 1 - c)).start()

    return _split_call("rs_siblings_start_l%d" % li, list(grads) + lands, [], 8 * nw, after, body, True)


def _siblings_wait(li, bufs, sems, after):
    nw = len(bufs) // 2

    def body(refs, old, _):
        x, y, c = _place()
        for w in range(nw):
            for q in range(4):
                _remote(refs[w].at[2 * q + (1 - c)], refs[nw + w].at[q], old[8 * w + q], old[8 * w + 4 + q],
                        (x, y, 1 - c)).wait()

    _, bufs, _ = _split_call("rs_siblings_wait_l%d" % li, bufs, sems, 0, after, body, False)
    return bufs[:nw], bufs[nw:]


_FLIPS = ((1, 0), (0, 1), (1, 1))


def _chips_copies(refs, nw, sems):
    x, y, c = _place()
    for w in range(nw):
        for r, (fx, fy) in enumerate(_FLIPS):
            px = 1 - x if fx else x
            py = 1 - y if fy else y
            yield _remote(refs[w].at[2 * px + py], refs[nw + w].at[r], sems[6 * w + r], sems[6 * w + 3 + r], (px, py, c))


def _chips_start(li, partials, after):
    nw = len(partials)
    lands = [lax.empty((3,) + a.shape[1:], a.dtype) for a in partials]

    def body(bufs, _, new):
        for cp in _chips_copies(bufs, nw, new):
            cp.start()

    return _split_call("rs_chips_start_l%d" % li, list(partials) + lands, [], 6 * nw, after, body, True)


def _chips_wait(li, bufs, sems, after):
    nw = len(bufs) // 2

    def body(refs, old, _):
        for cp in _chips_copies(refs, nw, old):
            cp.wait()

    _, bufs, _ = _split_call("rs_chips_wait_l%d" % li, bufs, sems, 0, after, body, False)
    return bufs[nw:]


def _row_tile(n, want):
    best = None
    for t in range(16, min(n, want) + 1, 16):
        if n % t == 0:
            best = t
    assert best is not None, n
    return best


def _chip_partials(name, grad, from_sibling):
    _, r, cdim = grad.shape
    tr = _row_tile(r, 256)
    nt = r // tr
    steps = 4 * nt

    def body(g_hbm, s_ref, pb_ref, own_ref, buf, sem):
        i, q = pl.program_id(0), pl.program_id(1)
        x, y, c = _place()
        n = 4 * i + q
        slot = n % 2

        def fetch(step, into):
            rows = pl.ds(pl.multiple_of((step // 4) * tr, 16), tr)
            return pltpu.make_async_copy(g_hbm.at[2 * (step % 4) + c, rows, :], buf.at[into], sem.at[into])

        @pl.when(n == 0)
        def _():
            fetch(0, 0).start()

        @pl.when(n + 1 < steps)
        def _():
            fetch(n + 1, 1 - slot).start()

        fetch(n, slot).wait()
        tot = buf[slot] + s_ref[...]
        pb_ref[...] = tot.astype(BF16)

        @pl.when(q == 2 * x + y)
        def _():
            own_ref[...] = tot

    return pl.pallas_call(
        body, name=name, grid=(nt, 4),
        in_specs=[ANY, pl.BlockSpec((None, tr, cdim), lambda i, q: (q, i, 0))],
        out_specs=[pl.BlockSpec((None, tr, cdim), lambda i, q: (q, i, 0)), pl.BlockSpec((tr, cdim), lambda i, q: (i, 0))],
        out_shape=[_sds((4, r, cdim), BF16), _sds((r, cdim), F32)],
        scratch_shapes=[pltpu.VMEM((2, tr, cdim), F32), pltpu.SemaphoreType.DMA((2,))],
        compiler_params=_params("arbitrary", "arbitrary"),
    )(grad, from_sibling)


def _adamw(w, g, m, v):
    m = ADAM_B1 * m + (1.0 - ADAM_B1) * g
    v = ADAM_B2 * v + (1.0 - ADAM_B2) * (g * g)
    m_hat = m / (1.0 - ADAM_B1 ** ADAM_STEP)
    v_hat = v / (1.0 - ADAM_B2 ** ADAM_STEP)
    delta = -ADAM_LR * (m_hat / (jnp.sqrt(v_hat) + ADAM_EPS) + ADAM_WD * w)
    return delta, m, v


def _finish_weight(name, li, own, from_chips, w, m, v, stacked):
    r, cdim = own.shape
    tr = _row_tile(r, 256)
    if stacked is None:
        stacked = [lax.empty(w.shape, F32) for _ in range(4)]

    def body(own_ref, fc_ref, w_ref, m_ref, v_ref, *rest):
        g_out, d_out, m_out, v_out = rest[4:]
        g = own_ref[...] + fc_ref[0].astype(F32) + fc_ref[1].astype(F32) + fc_ref[2].astype(F32)
        delta, mn, vn = _adamw(w_ref[...], g, m_ref[...], v_ref[...])
        g_out[...] = g
        d_out[...] = delta
        m_out[...] = mn
        v_out[...] = vn

    tile = pl.BlockSpec((tr, cdim), lambda i: (i, 0))
    lay = pl.BlockSpec((None, tr, cdim), lambda i: (li, i, 0))
    return pl.pallas_call(
        body, name=name, grid=(r // tr,),
        in_specs=[tile, pl.BlockSpec((3, tr, cdim), lambda i: (0, i, 0)), lay, lay, lay] + [ANY] * 4,
        out_specs=[lay] * 4, out_shape=[_sds(w.shape, F32)] * 4,
        input_output_aliases={5: 0, 6: 1, 7: 2, 8: 3},
        compiler_params=_params("parallel"),
    )(own, from_chips, w, m, v, *stacked)


def _allgather_small(name, v, reduce):
    r = v.shape[0]

    def body(x_ref, out_ref, *rest):
        if reduce:
            sum_ref, send_sems, recv_sems, local_sem = rest
        else:
            send_sems, recv_sems, local_sem = rest
        x, y, c = _place()
        me, sib = (x, y, c), (x, y, 1 - c)
        chips = [(1 - x, y), (x, 1 - y), (1 - x, 1 - y)]

        def rows(px, py, pc):
            return out_ref.at[pl.ds(pl.multiple_of((4 * px + 2 * py + pc) * r, 8), r), :]

        def copy(k, block, to, src=None):
            return pltpu.make_async_remote_copy(
                src_ref=rows(*block) if src is None else src, dst_ref=rows(*block),
                send_sem=send_sems.at[k], recv_sem=recv_sems.at[k], device_id=to, device_id_type=MESH)

        mine = pltpu.make_async_copy(x_ref, rows(*me), local_sem)
        mine.start()
        first = [copy(0, me, sib, src=x_ref)]
        first += [copy(1 + j, me, (*chip, c), src=x_ref) for j, chip in enumerate(chips)]
        for cp in first:
            cp.start()
        passed = [copy(4 + j, (*chip, c), sib) for j, chip in enumerate(chips)]
        for j, chip in enumerate(chips):
            copy(1 + j, (*chip, c), me).wait_recv()
            passed[j].start()
        copy(0, sib, me).wait_recv()
        for j, chip in enumerate(chips):
            copy(4 + j, (*chip, 1 - c), me).wait_recv()
        for cp in first + passed:
            cp.wait_send()
        mine.wait()
        if reduce:
            tot = out_ref[0:r, :]
            for d in range(1, N_DEV):
                tot = tot + out_ref[d * r:(d + 1) * r, :]
            sum_ref[...] = tot

    vm = pl.BlockSpec(memory_space=pltpu.VMEM)
    outs = [_sds((N_DEV * r, 128), F32)] + ([_sds((r, 128), F32)] if reduce else [])
    res = pl.pallas_call(
        body, name=name,
        in_specs=[vm], out_specs=[vm] * len(outs), out_shape=outs,
        scratch_shapes=[pltpu.SemaphoreType.DMA((7,)), pltpu.SemaphoreType.DMA((7,)), pltpu.SemaphoreType.DMA],
        compiler_params=pltpu.CompilerParams(vmem_limit_bytes=VMEM_LIMIT_BYTES),
    )(v)
    return res


def _adamw_small(name, w, g, m, v):
    def body(w_ref, g_ref, m_ref, v_ref, d_out, m_out, v_out):
        delta, mn, vn = _adamw(w_ref[...], g_ref[...], m_ref[...], v_ref[...])
        d_out[...] = delta
        m_out[...] = mn
        v_out[...] = vn

    vm = pl.BlockSpec(memory_space=pltpu.VMEM)
    return pl.pallas_call(
        body, name=name, in_specs=[vm] * 4, out_specs=[vm] * 3, out_shape=[_sds(w.shape, F32)] * 3,
        compiler_params=pltpu.CompilerParams(vmem_limit_bytes=VMEM_LIMIT_BYTES),
    )(w, g, m, v)


def _pack(arrays):
    flat, layout, off = [], [], 0
    for a in arrays:
        flat.append(a.reshape(-1).astype(F32))
        layout.append((off, a.shape))
        off += a.size
    total = -(-off // 1024) * 1024
    if total > off:
        flat.append(jnp.zeros((total - off,), F32))
    return jnp.concatenate(flat).reshape(total // 128, 128), layout


def _unpack(packed, layout):
    flat = packed.reshape(-1)
    return [flat[off:off + math.prod(shape)].reshape(shape) for off, shape in layout]


_BIG = ("win", "wout", "wg", "wu", "wd", "wpg", "wpp")
_BIG_FULL = {"win": "w_in", "wout": "w_out", "wg": "w_gate", "wu": "w_up", "wd": "w_down",
             "wpg": "w_ple_gate", "wpp": "w_ple_proj"}
_SMALL_REPLICATED = ("norm_mix_g", "sgu_ln_g", "sgu_ln_b", "sgu_w", "sgu_b", "cf_conv_b", "cf_ln_g", "cf_ln_b",
                     "pool_w", "pool_scale", "norm_ffn_g", "norm_ple_g", "final_norm_g")
_SMALL_SHARDED = ("sc_conv_w", "cf_conv_w")
_WEIGHTS = ("norm_mix_g", "w_in", "sgu_ln_g", "sgu_ln_b", "sgu_w", "sgu_b", "sc_conv_w", "cf_conv_w", "cf_conv_b",
            "cf_ln_g", "cf_ln_b", "pool_w", "pool_scale", "w_out", "norm_ffn_g", "w_gate", "w_up", "w_down",
            "norm_ple_g", "w_ple_gate", "w_ple_proj", "final_norm_g")


def _pad_rows(a, rows):
    return jnp.concatenate([a, jnp.zeros((rows - a.shape[0],) + a.shape[1:], a.dtype)], axis=0)


def _mixer_params(li, W, sc_full, cf_full):
    return [W["sgu_ln_g"][li][:, None, :], W["sgu_ln_b"][li][:, None, :], W["sgu_w"][li], W["sgu_b"][li][:, :, None],
            _pad_rows(sc_full[li], 8), _pad_rows(cf_full[li], 32),
            W["cf_conv_b"][li][None, :], W["cf_ln_g"][li][None, :], W["cf_ln_b"][li][None, :],
            W["pool_w"][li], W["pool_scale"][li][None, :]]


def _step(W, M, V, x, p, loss_target):
    n_layers = W["w_in"].shape[0]
    h = x[0]
    target = loss_target[0]
    p_all = p[:, 0]
    xi, yi, ci = _place()
    blk = 4 * xi + 2 * yi + ci
    csh = W["sc_conv_w"].shape[2]
    turned = ("w_gate", "w_up")
    W, M, V = ({n: (jnp.swapaxes(a, 1, 2) if n in turned else a) for n, a in t.items()} for t in (W, M, V))

    packed, lay = _pack([W[n] for n in _SMALL_SHARDED])
    (taps,) = _allgather_small("gather_conv_taps", packed, reduce=False)
    per_dev = [_unpack(taps[d * packed.shape[0]:(d + 1) * packed.shape[0]], lay) for d in range(N_DEV)]
    sc_full = jnp.concatenate([pd[0] for pd in per_dev], axis=-1)
    cf_full = jnp.concatenate([pd[1] for pd in per_dev], axis=-1)

    d = h.shape[1]
    lands = _place_shards([_to_bf16("to_bf16_" + n, W[_BIG_FULL[n]]) for n in _BIG])

    def gathered_weights(bufs):
        g = dict(zip(_BIG, bufs))
        g["wout"] = g["wout"].reshape(d, d)
        g["wpg"] = g["wpg"].reshape(d, d)
        return g

    sems, bufs, tok = _gather_start(0, lands[0], None)
    fwd_sems, bufs, tok = _gather_forward(0, bufs, sems, tok)
    arrived = _gather_wait(0, bufs, sems, fwd_sems, tok)
    saved, gathered, smalls = [], [], []
    for li in range(n_layers):
        g = gathered_weights(arrived)
        sm = {"norm_mix_g": W["norm_mix_g"][li][None, :], "norm_ffn_g": W["norm_ffn_g"][li][None, :],
              "norm_ple_g": W["norm_ple_g"][li][None, :], "mixer": _mixer_params(li, W, sc_full, cf_full)}
        if li + 1 < n_layers:
            sems, bufs, tok = _gather_start(li + 1, lands[li + 1], arrived[0])
            moving = {}

            def forward_next(h2, li=li, sems=sems, bufs=bufs, moving=moving):
                moving["fwd_sems"], moving["bufs"], tok2 = _gather_forward(li + 1, bufs, sems, h2)
                return tok2

            h, sv = _layer_fwd(li, h, p_all, g, sm, tok, forward_next)
            arrived = _gather_wait(li + 1, moving["bufs"], sems, moving["fwd_sems"], h)
        else:
            h, sv = _layer_fwd(li, h, p_all, g, sm)
        saved.append(sv)
        gathered.append(g)
        smalls.append(sm)
    loss, dh, d_final_g = _loss_head(h, W["final_norm_g"][None, :], target)

    big_out = {n: None for n in _BIG}
    small_grads = []

    def finish(li, st):
        from_chips = _chips_wait(li, st["bufs"], st["sems"], st["after"])
        for n, own, fc in zip(_BIG, st["own"], from_chips):
            full = _BIG_FULL[n]
            big_out[n] = _finish_weight("adamw_" + n, li, own, fc, W[full], M[full], V[full], big_out[n])
        return big_out[_BIG[0]][0]

    def sum_and_send(li, st, after):
        grads, from_sib = _siblings_wait(li, st["bufs"], st["sems"], after)
        parts = [_chip_partials("rs_sum_" + n, g_, s_) for n, g_, s_ in zip(_BIG, grads, from_sib)]
        st["own"] = [own for _, own in parts]
        st["sems"], st["bufs"], tok2 = _chips_start(li, [pb for pb, _ in parts], None)
        return tok2

    pending, tok = None, None
    for li in reversed(range(n_layers)):
        between = None if pending is None else functools.partial(sum_and_send, li + 1, pending)
        dh, gr, sg = _layer_bwd(li, dh, p_all, gathered[li], smalls[li], saved[li], tok, between)
        small_grads.append(sg)
        gr["wout"] = gr["wout"].reshape(N_DEV, d // N_DEV, d)
        gr["wpg"] = gr["wpg"].reshape(N_DEV, d // N_DEV, d)
        done = None
        if pending is not None:
            pending["after"] = dh
            done = finish(li + 1, pending)
        pending = {}
        pending["sems"], pending["bufs"], tok = _siblings_start(li, [gr[n] for n in _BIG], done)
    sum_and_send(0, pending, None)
    pending["after"] = None
    finish(0, pending)
    small_grads.reverse()

    def stacked(fn):
        return jnp.stack([fn(sg) for sg in small_grads])

    mix = lambda i: (lambda sg: sg["mixer"][i])
    grads_small = {
        "norm_mix_g": stacked(lambda sg: sg["norm_mix_g"][0]),
        "sgu_ln_g": stacked(mix(0))[:, :, 0, :], "sgu_ln_b": stacked(mix(1))[:, :, 0, :],
        "sgu_w": stacked(mix(2)), "sgu_b": stacked(mix(3))[:, :, :, 0],
        "sc_conv_w": stacked(mix(4))[:, :SHORT_K], "cf_conv_w": stacked(mix(5))[:, :CONF_K],
        "cf_conv_b": stacked(mix(6))[:, 0], "cf_ln_g": stacked(mix(7))[:, 0], "cf_ln_b": stacked(mix(8))[:, 0],
        "pool_w": stacked(mix(9)), "pool_scale": stacked(mix(10))[:, 0],
        "norm_ffn_g": stacked(lambda sg: sg["norm_ffn_g"][0]), "norm_ple_g": stacked(lambda sg: sg["norm_ple_g"][0]),
        "final_norm_g": d_final_g[0],
    }
    order = _SMALL_REPLICATED + _SMALL_SHARDED
    packed, lay = _pack([grads_small[n] for n in order] + [loss])
    _, summed = _allgather_small("allreduce_small", packed, reduce=True)
    total = dict(zip(order + ("loss",), _unpack(summed, lay)))
    loss_all = total["loss"][0, 0]

    out_g, out_d, out_m, out_v = {}, {}, {}, {}
    pw, lay_r = _pack([W[n] for n in _SMALL_REPLICATED])
    pg, _ = _pack([total[n] for n in _SMALL_REPLICATED])
    pm, _ = _pack([M[n] for n in _SMALL_REPLICATED])
    pv, _ = _pack([V[n] for n in _SMALL_REPLICATED])
    dd, mm, vv = _adamw_small("adamw_small", pw, pg, pm, pv)
    for n, a, b, c_ in zip(_SMALL_REPLICATED, _unpack(dd, lay_r), _unpack(mm, lay_r), _unpack(vv, lay_r)):
        out_g[n], out_d[n], out_m[n], out_v[n] = total[n], a, b, c_
    pick = (jnp.arange(N_DEV) == blk).astype(F32)[None, None, :, None]
    mine = {n: jnp.sum(total[n].reshape(total[n].shape[:2] + (N_DEV, csh)) * pick, axis=2) for n in _SMALL_SHARDED}
    pw, lay_s = _pack([W[n] for n in _SMALL_SHARDED])
    pg, _ = _pack([mine[n] for n in _SMALL_SHARDED])
    pm, _ = _pack([M[n] for n in _SMALL_SHARDED])
    pv, _ = _pack([V[n] for n in _SMALL_SHARDED])
    dd, mm, vv = _adamw_small("adamw_conv_taps", pw, pg, pm, pv)
    for n, a, b, c_ in zip(_SMALL_SHARDED, _unpack(dd, lay_s), _unpack(mm, lay_s), _unpack(vv, lay_s)):
        out_g[n], out_d[n], out_m[n], out_v[n] = mine[n], a, b, c_
    for n in _BIG:
        for k, dst in enumerate((out_g, out_d, out_m, out_v)):
            full = _BIG_FULL[n]
            dst[full] = jnp.swapaxes(big_out[n][k], 1, 2) if full in turned else big_out[n][k]

    return (loss_all, dh[None], *[out_g[n] for n in _WEIGHTS], *[out_d[n] for n in _WEIGHTS],
            *[out_m[n] for n in _WEIGHTS], *[out_v[n] for n in _WEIGHTS])


def kernel(x, p, norm_mix_g, w_in, sgu_ln_g, sgu_ln_b, sgu_w, sgu_b, sc_conv_w, cf_conv_w, cf_conv_b, cf_ln_g, cf_ln_b, pool_w, pool_scale, w_out, norm_ffn_g, w_gate, w_up, w_down, norm_ple_g, w_ple_gate, w_ple_proj, final_norm_g, loss_target, m_norm_mix_g, m_w_in, m_sgu_ln_g, m_sgu_ln_b, m_sgu_w, m_sgu_b, m_sc_conv_w, m_cf_conv_w, m_cf_conv_b, m_cf_ln_g, m_cf_ln_b, m_pool_w, m_pool_scale, m_w_out, m_norm_ffn_g, m_w_gate, m_w_up, m_w_down, m_norm_ple_g, m_w_ple_gate, m_w_ple_proj, m_final_norm_g, v_norm_mix_g, v_w_in, v_sgu_ln_g, v_sgu_ln_b, v_sgu_w, v_sgu_b, v_sc_conv_w, v_cf_conv_w, v_cf_conv_b, v_cf_ln_g, v_cf_ln_b, v_pool_w, v_pool_scale, v_w_out, v_norm_ffn_g, v_w_gate, v_w_up, v_w_down, v_norm_ple_g, v_w_ple_gate, v_w_ple_proj, v_final_norm_g):
    given = dict(locals())
    W = {n: given[n] for n in _WEIGHTS}
    M = {n: given["m_" + n] for n in _WEIGHTS}
    V = {n: given["v_" + n] for n in _WEIGHTS}
    return _step(W, M, V, x, p, loss_target)
```

```python
import functools
import math

import jax
import jax.numpy as jnp
from jax import lax
from jax.experimental import pallas as pl
from jax.experimental.pallas import tpu as pltpu

F32 = jnp.float32
BF16 = jnp.bfloat16
EPS = 1e-6
HEAD = 128
GROUP = 4 * HEAD
HALO = 32
SHORT_K = 3
CONF_K = 31
POOL_WINDOWS = (2, 4, 8, 16)
N_DEV = 8
MESH = pl.DeviceIdType.MESH
VMEM_LIMIT_BYTES = 56 * 1024 * 1024

ADAM_LR = 0.001
ADAM_B1 = 0.9
ADAM_B2 = 0.999
ADAM_EPS = 1e-08
ADAM_WD = 0.01
ADAM_STEP = 10

ANY = pl.BlockSpec(memory_space=pl.ANY)


def _params(*sem):
    return pltpu.CompilerParams(dimension_semantics=sem, vmem_limit_bytes=VMEM_LIMIT_BYTES)


def _rms(x, g):
    return x * lax.rsqrt(jnp.mean(x * x, axis=-1, keepdims=True) + EPS) * g


def _ln(x, g, b):
    mu = jnp.mean(x, axis=-1, keepdims=True)
    xc = x - mu
    var = jnp.mean(xc * xc, axis=-1, keepdims=True)
    return xc * lax.rsqrt(var + EPS) * g + b


def _bdot(a, b, dims=(((1,), (0,)), ((), ()))):
    return lax.dot_general(a.astype(BF16), b.astype(BF16), dims, preferred_element_type=F32)


def _sgu_piece(zu, zv, lg, lb, w, b):
    u = jax.nn.gelu(zu)
    v = _ln(jax.nn.gelu(zv), lg, lb)
    row = lax.broadcasted_iota(jnp.int32, w.shape, 0)
    col = lax.broadcasted_iota(jnp.int32, w.shape, 1)
    wm = jnp.where(row >= col, w, 0.0)
    return u * (_bdot(wm, v) + b)


def _conf_post(c, g, b):
    return jax.nn.silu(_ln(c, g, b))


def _pool_count(first_pos, rows, w):
    pos = first_pos + lax.broadcasted_iota(jnp.int32, (rows, 1), 0) + 1
    return jnp.minimum(pos, w).astype(F32)


_DIMS = {
    "nn": (((1,), (0,)), ((), ())),
    "nt": (((1,), (1,)), ((), ())),
    "tn": (((0,), (0,)), ((), ())),
}


def _tiles(name, grid, ins, in_specs, outs, out_specs, compute, summed=(), deps=()):
    ni = len(ins)
    nd = len(deps)

    def body(*refs):
        vals = compute(*refs[:ni])
        first = functools.reduce(jnp.logical_and, [pl.program_id(a) == 0 for a in range(len(grid))])
        for idx, (r, v) in enumerate(zip(refs[ni + nd:], vals)):
            if idx in summed:
                @pl.when(first)
                def _(r=r):
                    r[...] = jnp.zeros_like(r)

                r[...] += v
            else:
                r[...] = v.astype(r.dtype)

    sem = ("arbitrary" if summed else "parallel",) * len(grid)
    return pl.pallas_call(
        body, name=name, grid=grid, in_specs=list(in_specs) + [ANY] * nd, out_specs=list(out_specs),
        out_shape=list(outs), compiler_params=_params(*sem),
    )(*ins, *deps)


def _resident(shape):
    return pl.BlockSpec(shape, lambda *_: (0,) * len(shape), pipeline_mode=pl.Buffered(1))


def _sds(shape, dtype):
    return jax.ShapeDtypeStruct(tuple(shape), dtype)


def _tile(n, want):
    t = min(n, want)
    assert n % t == 0, (n, want)
    return t


def _rms_fwd(name, h, g):
    s, d = h.shape
    tm = _tile(s, 512)

    def body(h_ref, g_ref, y_ref):
        y_ref[...] = _rms(h_ref[...], g_ref[...]).astype(BF16)

    return pl.pallas_call(
        body, name=name, grid=(s // tm,),
        in_specs=[pl.BlockSpec((tm, d), lambda i: (i, 0)), pl.BlockSpec((1, d), lambda i: (0, 0))],
        out_specs=pl.BlockSpec((tm, d), lambda i: (i, 0)),
        out_shape=_sds((s, d), BF16),
        compiler_params=_params("parallel"),
    )(h, g)


def _rms_back(h, g, dy, dh_in):
    _, vjp = jax.vjp(_rms, h, g)
    dh, dg = vjp(dy)
    return dh_in + dh, dg


def _rms_bwd(name, h, g, dy, dh_in):
    s, d = h.shape
    tm = _tile(s, 256)

    def body(h_ref, g_ref, dy_ref, dhin_ref, dh_ref, dh16_ref, dg_ref):
        dh, dg = _rms_back(h_ref[...], g_ref[...], dy_ref[...].astype(F32), dhin_ref[...])
        dh_ref[...] = dh
        dh16_ref[...] = dh.astype(BF16)

        @pl.when(pl.program_id(0) == 0)
        def _():
            dg_ref[...] = jnp.zeros_like(dg_ref)

        dg_ref[...] += dg

    tok = pl.BlockSpec((tm, d), lambda i: (i, 0))
    vec = pl.BlockSpec((1, d), lambda i: (0, 0))
    return pl.pallas_call(
        body, name=name, grid=(s // tm,),
        in_specs=[tok, vec, tok, tok],
        out_specs=[tok, tok, vec],
        out_shape=[_sds((s, d), F32), _sds((s, d), BF16), _sds((1, d), F32)],
        compiler_params=_params("arbitrary"),
    )(h, g, dy, dh_in)


def _loss_head(h, g, target):
    s, d = h.shape
    tm = _tile(s, 256)

    def body(h_ref, g_ref, t_ref, loss_ref, dh_ref, dg_ref):
        y, vjp = jax.vjp(_rms, h_ref[...], g_ref[...])
        err = y - t_ref[...]
        dh, dg = vjp(err * (1.0 / d))
        dh_ref[...] = dh
        per_token = jnp.mean(err * err, axis=-1, keepdims=True)
        part = 0.5 * jnp.sum(per_token, axis=0, keepdims=True)

        @pl.when(pl.program_id(0) == 0)
        def _():
            dg_ref[...] = jnp.zeros_like(dg_ref)
            loss_ref[...] = jnp.zeros_like(loss_ref)

        dg_ref[...] += dg
        loss_ref[...] += part

    tok = pl.BlockSpec((tm, d), lambda i: (i, 0))
    vec = pl.BlockSpec((1, d), lambda i: (0, 0))
    return pl.pallas_call(
        body, name="loss_head", grid=(s // tm,),
        in_specs=[tok, vec, tok],
        out_specs=[pl.BlockSpec((1, 1), lambda i: (0, 0)), tok, vec],
        out_shape=[_sds((1, 1), F32), _sds((s, d), F32), _sds((1, d), F32)],
        compiler_params=_params("arbitrary"),
    )(h, g, target)


def _ple_bwd(name, dh, pg_pre, pp):
    s, d = dh.shape
    tm = _tile(s, 512)

    def body(dh_ref, pg_ref, pp_ref, dpg_ref, dpp_ref):
        sg = jax.nn.sigmoid(pg_ref[...].astype(F32))
        dhv = dh_ref[...]
        dpg_ref[...] = (dhv * pp_ref[...].astype(F32) * sg * (1.0 - sg)).astype(BF16)
        dpp_ref[...] = (dhv * sg).astype(BF16)

    tok = pl.BlockSpec((tm, d), lambda i: (i, 0))
    return pl.pallas_call(
        body, name=name, grid=(s // tm,),
        in_specs=[tok, tok, tok], out_specs=[tok, tok],
        out_shape=[_sds((s, d), BF16), _sds((s, d), BF16)],
        compiler_params=_params("parallel"),
    )(dh, pg_pre, pp)


_U, _V, _HB, _BG, _CG, _CA, _CGT, _PD = (GROUP * i for i in range(8))


def _cols(c0, w=GROUP):
    return slice(c0, c0 + w)


def _mixer_param_specs():
    full = lambda *shape: pl.BlockSpec(shape, lambda i: (0,) * len(shape))
    return [
        full(4, 1, HEAD), full(4, 1, HEAD), full(4, HEAD, HEAD), full(4, HEAD, 1),
        full(8, GROUP), full(32, GROUP), full(1, GROUP), full(1, GROUP), full(1, GROUP),
        full(4, HEAD, HEAD), full(1, GROUP),
    ]


def _mixer_fwd(name, z, prm):
    s = z.shape[0]
    t = _tile(s, 256)
    hb = t // HALO

    def body(zp_ref, zm_ref, lg_ref, lb_ref, sw_ref, sb_ref, scw_ref, cfw_ref, cfb_ref, cg_ref, cb_ref,
             pw_ref, ps_ref, o_ref, ext_ref):
        i = pl.program_id(0)
        keep = (i > 0).astype(F32)
        main = pl.ds(HALO, t)

        for n in range(t // HEAD):
            rows = slice(n * HEAD, (n + 1) * HEAD)
            for hh in range(4):
                cu = _cols(_U + hh * HEAD, HEAD)
                cv = _cols(_V + hh * HEAD, HEAD)
                o_ref[rows, _cols(hh * HEAD, HEAD)] = _sgu_piece(
                    zm_ref[rows, cu], zm_ref[rows, cv], lg_ref[hh], lb_ref[hh], sw_ref[hh], sb_ref[hh]
                ).astype(BF16)

        ext_ref[0:HALO, :] = zp_ref[:, _cols(_CG)] * zp_ref[:, _cols(_HB)] * keep
        ext_ref[main, :] = zm_ref[:, _cols(_CG)] * zm_ref[:, _cols(_HB)]
        y = jnp.zeros((t, GROUP), F32)
        for k in range(SHORT_K):
            y = y + scw_ref[k:k + 1, :] * ext_ref[pl.ds(HALO - (SHORT_K - 1) + k, t), :]
        o_ref[:, _cols(GROUP)] = (zm_ref[:, _cols(_BG)] * y).astype(BF16)

        ext_ref[0:HALO, :] = zp_ref[:, _cols(_CA)] * jax.nn.sigmoid(zp_ref[:, _cols(_CGT)]) * keep
        ext_ref[main, :] = zm_ref[:, _cols(_CA)] * jax.nn.sigmoid(zm_ref[:, _cols(_CGT)])
        c = jnp.zeros((t, GROUP), F32) + cfb_ref[...]
        for k in range(CONF_K):
            c = c + cfw_ref[k:k + 1, :] * ext_ref[pl.ds(HALO - (CONF_K - 1) + k, t), :]
        o_ref[:, _cols(2 * GROUP)] = _conf_post(c, cg_ref[...], cb_ref[...]).astype(BF16)

        ext_ref[0:HALO, :] = zp_ref[:, _cols(_PD)] * keep
        ext_ref[main, :] = zm_ref[:, _cols(_PD)]
        for gi, w in enumerate(POOL_WINDOWS):
            cc = _cols(gi * HEAD, HEAD)
            acc = ext_ref[main, cc]
            for j in range(1, w):
                acc = acc + ext_ref[pl.ds(HALO - j, t), cc]
            q = acc / _pool_count(i * t, t, w) - ext_ref[main, cc]
            o_ref[:, _cols(3 * GROUP + gi * HEAD, HEAD)] = (_bdot(q, pw_ref[gi]) * ps_ref[:, cc]).astype(BF16)

    return pl.pallas_call(
        body, name=name, grid=(s // t,),
        in_specs=[pl.BlockSpec((HALO, 8 * GROUP), lambda i: (jnp.maximum(i * hb - 1, 0), 0)),
                  pl.BlockSpec((t, 8 * GROUP), lambda i: (i, 0)),
                  *_mixer_param_specs()],
        out_specs=pl.BlockSpec((t, 4 * GROUP), lambda i: (i, 0)),
        out_shape=_sds((s, 4 * GROUP), BF16),
        scratch_shapes=[pltpu.VMEM((HALO + t, GROUP), F32)],
        compiler_params=_params("parallel"),
    )(z, z, *prm)


def _mixer_bwd(name, z, dmo, prm):
    s = z.shape[0]
    t = _tile(s, 256)
    hb = t // HALO
    nt = s // t
    last_halo = s // HALO - 1

    def body(zp_ref, zm_ref, zn_ref, dm_ref, dn_ref,
             lg_ref, lb_ref, sw_ref, sb_ref, scw_ref, cfw_ref, cfb_ref, cg_ref, cb_ref, pw_ref, ps_ref,
             dz_ref, dlg_ref, dlb_ref, dsw_ref, dsb_ref, dscw_ref, dcfw_ref, dcfb_ref, dcg_ref, dcb_ref,
             dpw_ref, dps_ref, extp_ref, extn_ref, extc_ref):
        i = pl.program_id(0)
        keep_prev = (i > 0).astype(F32)
        keep_next = (i < nt - 1).astype(F32)
        main = pl.ds(HALO, t)

        @pl.when(i == 0)
        def _():
            for r in (dlg_ref, dlb_ref, dsw_ref, dsb_ref, dscw_ref, dcfw_ref, dcfb_ref, dcg_ref, dcb_ref,
                      dpw_ref, dps_ref):
                r[...] = jnp.zeros_like(r)

        def rowsum(v):
            return jnp.sum(v, axis=0, keepdims=True)

        for n in range(t // HEAD):
            rows = slice(n * HEAD, (n + 1) * HEAD)
            for hh in range(4):
                cu = _cols(_U + hh * HEAD, HEAD)
                cv = _cols(_V + hh * HEAD, HEAD)
                _, vjp = jax.vjp(_sgu_piece, zm_ref[rows, cu], zm_ref[rows, cv],
                                 lg_ref[hh], lb_ref[hh], sw_ref[hh], sb_ref[hh])
                dzu, dzv, dlg, dlb, dsw, dsb = vjp(dm_ref[rows, _cols(hh * HEAD, HEAD)].astype(F32))
                dz_ref[rows, cu] = dzu.astype(BF16)
                dz_ref[rows, cv] = dzv.astype(BF16)
                dlg_ref[hh] += dlg
                dlb_ref[hh] += dlb
                dsw_ref[hh] += dsw
                dsb_ref[hh] += dsb

        extp_ref[0:HALO, :] = zp_ref[:, _cols(_CG)] * zp_ref[:, _cols(_HB)] * keep_prev
        extp_ref[main, :] = zm_ref[:, _cols(_CG)] * zm_ref[:, _cols(_HB)]
        dob = dm_ref[:, _cols(GROUP)].astype(F32)
        dy = dob * zm_ref[:, _cols(_BG)]
        extn_ref[0:t, :] = dy
        extn_ref[t:t + HALO, :] = dn_ref[:, _cols(GROUP)].astype(F32) * zn_ref[:, _cols(_BG)] * keep_next
        y = jnp.zeros((t, GROUP), F32)
        dx = jnp.zeros((t, GROUP), F32)
        for k in range(SHORT_K):
            xs = extp_ref[pl.ds(HALO - (SHORT_K - 1) + k, t), :]
            y = y + scw_ref[k:k + 1, :] * xs
            dscw_ref[k:k + 1, :] += rowsum(dy * xs)
            dx = dx + scw_ref[k:k + 1, :] * extn_ref[pl.ds((SHORT_K - 1) - k, t), :]
        dz_ref[:, _cols(_BG)] = (dob * y).astype(BF16)
        dz_ref[:, _cols(_CG)] = (dx * zm_ref[:, _cols(_HB)]).astype(BF16)
        dz_ref[:, _cols(_HB)] = (dx * zm_ref[:, _cols(_CG)]).astype(BF16)

        extc_ref[0:HALO, :] = zp_ref[:, _cols(_CA)] * jax.nn.sigmoid(zp_ref[:, _cols(_CGT)]) * keep_prev
        sg = jax.nn.sigmoid(zm_ref[:, _cols(_CGT)])
        extc_ref[main, :] = zm_ref[:, _cols(_CA)] * sg
        extc_ref[HALO + t:HALO + t + HALO, :] = zn_ref[:, _cols(_CA)] * jax.nn.sigmoid(zn_ref[:, _cols(_CGT)])
        c_main = jnp.zeros((t, GROUP), F32) + cfb_ref[...]
        c_next = jnp.zeros((HALO, GROUP), F32) + cfb_ref[...]
        for k in range(CONF_K):
            wk = cfw_ref[k:k + 1, :]
            c_main = c_main + wk * extc_ref[pl.ds(HALO - (CONF_K - 1) + k, t), :]
            c_next = c_next + wk * extc_ref[pl.ds(HALO + t - (CONF_K - 1) + k, HALO), :]
        _, vjp = jax.vjp(_conf_post, c_main, cg_ref[...], cb_ref[...])
        dc, dcg, dcb = vjp(dm_ref[:, _cols(2 * GROUP)].astype(F32))
        dcg_ref[...] += dcg
        dcb_ref[...] += dcb
        dcfb_ref[...] += rowsum(dc)
        _, vjp_next = jax.vjp(lambda cv: _conf_post(cv, cg_ref[...], cb_ref[...]), c_next)
        (dc_next,) = vjp_next(dn_ref[:, _cols(2 * GROUP)].astype(F32) * keep_next)
        extn_ref[0:t, :] = dc
        extn_ref[t:t + HALO, :] = dc_next
        dhc = jnp.zeros((t, GROUP), F32)
        for k in range(CONF_K):
            dhc = dhc + cfw_ref[k:k + 1, :] * extn_ref[pl.ds((CONF_K - 1) - k, t), :]
            dcfw_ref[k:k + 1, :] += rowsum(dc * extc_ref[pl.ds(HALO - (CONF_K - 1) + k, t), :])
        dz_ref[:, _cols(_CA)] = (dhc * sg).astype(BF16)
        dz_ref[:, _cols(_CGT)] = (dhc * zm_ref[:, _cols(_CA)] * sg * (1.0 - sg)).astype(BF16)

        extp_ref[0:HALO, :] = zp_ref[:, _cols(_PD)] * keep_prev
        extp_ref[main, :] = zm_ref[:, _cols(_PD)]
        for gi, w in enumerate(POOL_WINDOWS):
            cc = _cols(gi * HEAD, HEAD)
            oc = _cols(3 * GROUP + gi * HEAD, HEAD)
            acc = extp_ref[main, cc]
            for j in range(1, w):
                acc = acc + extp_ref[pl.ds(HALO - j, t), cc]
            count = _pool_count(i * t, t, w)
            q = acc / count - extp_ref[main, cc]
            dod = dm_ref[:, oc].astype(F32)
            dps_ref[:, cc] += rowsum(dod * _bdot(q, pw_ref[gi]))
            ds = dod * ps_ref[:, cc]
            dpw_ref[gi] += _bdot(q, ds, _DIMS["tn"])
            dq = _bdot(ds, pw_ref[gi], _DIMS["nt"])
            ds_next = dn_ref[:, oc].astype(F32) * ps_ref[:, cc] * keep_next
            dq_next = _bdot(ds_next, pw_ref[gi], _DIMS["nt"])
            extn_ref[0:t, cc] = dq / count
            extn_ref[t:t + HALO, cc] = dq_next * (1.0 / w)
            back = extn_ref[0:t, cc]
            for j in range(1, w):
                back = back + extn_ref[pl.ds(j, t), cc]
            dz_ref[:, _cols(_PD + gi * HEAD, HEAD)] = (back - dq).astype(BF16)

    full = lambda *shape: pl.BlockSpec(shape, lambda i: (0,) * len(shape))
    grad_specs = [full(4, 1, HEAD), full(4, 1, HEAD), full(4, HEAD, HEAD), full(4, HEAD, 1),
                  full(8, GROUP), full(32, GROUP), full(1, GROUP), full(1, GROUP), full(1, GROUP),
                  full(4, HEAD, HEAD), full(1, GROUP)]
    grad_shapes = [_sds(sp.block_shape, F32) for sp in grad_specs]
    nxt = lambda i: (jnp.minimum((i + 1) * hb, last_halo), 0)
    return pl.pallas_call(
        body, name=name, grid=(nt,),
        in_specs=[pl.BlockSpec((HALO, 8 * GROUP), lambda i: (jnp.maximum(i * hb - 1, 0), 0)),
                  pl.BlockSpec((t, 8 * GROUP), lambda i: (i, 0)),
                  pl.BlockSpec((HALO, 8 * GROUP), nxt),
                  pl.BlockSpec((t, 4 * GROUP), lambda i: (i, 0)),
                  pl.BlockSpec((HALO, 4 * GROUP), nxt),
                  *_mixer_param_specs()],
        out_specs=[pl.BlockSpec((t, 8 * GROUP), lambda i: (i, 0)), *grad_specs],
        out_shape=[_sds((s, 8 * GROUP), BF16), *grad_shapes],
        scratch_shapes=[pltpu.VMEM((HALO + t, GROUP), F32), pltpu.VMEM((t + HALO, GROUP), F32),
                        pltpu.VMEM((HALO + t + HALO, GROUP), F32)],
        compiler_params=_params("arbitrary"),
    )(z, z, z, dmo, dmo, *prm)


def _dot(a, b, kind="nn"):
    return _bdot(a, b, _DIMS[kind])


def _chain(terms, kind):
    acc = None
    for a, b in terms:
        p = _dot(a, b, kind)
        acc = p if acc is None else acc + p
    return acc


def _after(g, token):
    return g if token is None else g + token[0:1, 0:1]


def _layer_fwd(li, h, p_all, fetch, sm, after_down=None):
    s, d = h.shape
    nb = N_DEV
    pd = p_all.shape[2]
    sv = {"h": h}
    rows = lambda t, w: pl.BlockSpec((t, w), lambda i: (i, 0))

    y1 = _rms_fwd("rms_mix", h, sm["norm_mix_g"])
    tm = _tile(s, 1024)
    (z,) = _tiles(
        "w_in_fwd", (s // tm, nb),
        [y1, fetch("win", y1)],
        [pl.BlockSpec((tm, d), lambda i, j: (i, 0)), pl.BlockSpec((None, d, GROUP), lambda i, j: (j, 0, 0))],
        [_sds((s, nb * GROUP), F32)], [pl.BlockSpec((tm, GROUP), lambda i, j: (i, j))],
        lambda a, w: (_dot(a[...], w[...]),))
    mo = _mixer_fwd("mixer_fwd", z, sm["mixer"])
    tm = _tile(s, 512)
    (h2,) = _tiles(
        "w_out_fwd", (s // tm,),
        [mo, fetch("wout", z), h], [rows(tm, d), _resident((d, d)), rows(tm, d)],
        [_sds((s, d), F32)], [rows(tm, d)],
        lambda a, w, hv: (hv[...] + _dot(a[...], w[...]),))
    sv.update(y1=y1, z=z, mo=mo, h2=h2)

    y2 = _rms_fwd("rms_ffn", h2, sm["norm_ffn_g"])
    gate_pre, up_pre, hmid = _ffn_up("ffn_up_fwd", y2, fetch("wg", z), fetch("wu", z))
    wd = fetch("wd", hmid)
    f8 = wd.shape[1]
    tm, tn = _tile(s, 512), _tile(d, 1024)
    (h3,) = _tiles(
        "w_down_fwd", (d // tn, s // tm),
        [hmid, wd, h2],
        [pl.BlockSpec((nb, tm, f8), lambda j, i: (0, i, 0)), pl.BlockSpec((nb, f8, tn), lambda j, i: (0, 0, j)),
         pl.BlockSpec((tm, tn), lambda j, i: (i, j))],
        [_sds((s, d), F32)], [pl.BlockSpec((tm, tn), lambda j, i: (i, j))],
        lambda a, w, hv: (hv[...] + _chain([(a[k], w[k]) for k in range(nb)], "nn"),))
    sv.update(y2=y2, gate_pre=gate_pre, up_pre=up_pre, hmid=hmid, h3=h3)

    y3 = _rms_fwd("rms_ple", h3, _after(sm["norm_ple_g"], None if after_down is None else after_down(h3)))
    tm = _tile(s, 1024)
    (pp,) = _tiles(
        "w_ple_proj_fwd", (s // tm, nb),
        [p_all, fetch("wpp", hmid)],
        [pl.BlockSpec((None, tm, pd), lambda i, j: (li, i, 0)), pl.BlockSpec((None, pd, pd), lambda i, j: (j, 0, 0))],
        [_sds((s, d), BF16)], [pl.BlockSpec((tm, pd), lambda i, j: (i, j))],
        lambda a, w: (_dot(a[...], w[...]),))
    tm = _tile(s, 256)

    def ple(a, w, hv, ppv):
        pg = _dot(a[...], w[...])
        return hv[...] + jax.nn.sigmoid(pg) * ppv[...].astype(F32), pg

    h4, pg_pre = _tiles(
        "w_ple_gate_fwd", (s // tm,),
        [y3, fetch("wpg", hmid), h3, pp], [rows(tm, d), _resident((d, d)), rows(tm, d), rows(tm, d)],
        [_sds((s, d), F32), _sds((s, d), BF16)], [rows(tm, d), rows(tm, d)], ple)
    sv.update(y3=y3, pp=pp, pg_pre=pg_pre)
    return h4, sv


def _ffn_up(name, y2, wg, wu):
    s, d = y2.shape
    nb, f8, _ = wg.shape
    tm = _tile(s, 1024)

    def compute(y_ref, wg_ref, wu_ref):
        yv = y_ref[...]
        g = _dot(yv, wg_ref[...], "nt")
        u = _dot(yv, wu_ref[...], "nt")
        return g, u, jax.nn.silu(g) * u

    wspec = pl.BlockSpec((None, f8, d), lambda i, j: (j, 0, 0))
    ospec = pl.BlockSpec((None, tm, f8), lambda i, j: (j, i, 0))
    return _tiles(name, (s // tm, nb), [y2, wg, wu], [pl.BlockSpec((tm, d), lambda i, j: (i, 0)), wspec, wspec],
                  [_sds((nb, s, f8), BF16)] * 3, [ospec] * 3, compute)


def _swiglu_bwd(dm, g, u):
    sg = jax.nn.sigmoid(g)
    silu = g * sg
    return dm * u * (sg + silu * (1.0 - sg)), dm * silu


def _layer_bwd(li, dh, p_all, gw, sm, sv, start_token=None, hooks=None):
    hooks = hooks or {}
    run = lambda name, *a: hooks[name](*a) if name in hooks else None
    s, d = dh.shape
    nb, f8, _ = gw["wg"].shape
    pd = p_all.shape[2]
    gr, sg = {}, {}
    rows = lambda t, w: pl.BlockSpec((t, w), lambda i: (i, 0))
    vec = pl.BlockSpec((1, d), lambda i: (0, 0))
    tw = _tile(d, 1024)
    whole = lambda w: pl.BlockSpec((s, w), lambda i, j: (0, i))
    whole_j = lambda w: pl.BlockSpec((s, w), lambda i, j: (0, j))
    tn_dot = lambda a, b: (_dot(a[...], b[...], "tn"),)

    dpg, dpp = _ple_bwd("ple_bwd", dh, sv["pg_pre"], sv["pp"])
    (gr["wpp"],) = _tiles(
        "w_ple_proj_dw", (nb,),
        [p_all, dpp], [pl.BlockSpec((None, s, pd), lambda j: (li, 0, 0)), pl.BlockSpec((s, pd), lambda j: (0, j))],
        [_sds((nb, pd, pd), F32)], [pl.BlockSpec((None, pd, pd), lambda j: (j, 0, 0))], tn_dot)
    (gr["wpg"],) = _tiles(
        "w_ple_gate_dw", (d // tw, d // tw), [sv["y3"], dpg], [whole(tw), whole_j(tw)],
        [_sds((d, d), F32)], [pl.BlockSpec((tw, tw), lambda i, j: (i, j))], tn_dot)
    gr["wpg"] = gr["wpg"].reshape(nb, d // nb, d)
    tm = _tile(s, 256)

    def ple_dx(a, w, hv, gv, dv):
        dh3, dg = _rms_back(hv[...], gv[...], _dot(a[...], w[...], "nt"), dv[...])
        return dh3, dh3, dg

    dh3, dh3_16, sg["norm_ple_g"] = _tiles(
        "w_ple_gate_dx", (s // tm,),
        [dpg, gw["wpg"], sv["h3"], _after(sm["norm_ple_g"], start_token), dh],
        [rows(tm, d), _resident((d, d)), rows(tm, d), vec, rows(tm, d)],
        [_sds((s, d), F32), _sds((s, d), BF16), _sds((1, d), F32)], [rows(tm, d), rows(tm, d), vec],
        ple_dx, summed=(2,))

    tm = _tile(s, 1024)
    blk_rows = pl.BlockSpec((None, tm, f8), lambda i, j: (j, i, 0))
    dgate, dup = _tiles(
        "w_down_dx", (s // tm, nb),
        [dh3_16, gw["wd"], sv["gate_pre"], sv["up_pre"]],
        [pl.BlockSpec((tm, d), lambda i, j: (i, 0)), pl.BlockSpec((None, f8, d), lambda i, j: (j, 0, 0)), blk_rows, blk_rows],
        [_sds((nb, s, f8), BF16)] * 2, [blk_rows] * 2,
        lambda a, w, g, u: _swiglu_bwd(_dot(a[...], w[...], "nt"), g[...].astype(F32), u[...].astype(F32)))
    (gr["wd"],) = _tiles(
        "w_down_dw", (nb, d // tw),
        [sv["hmid"], dh3_16], [pl.BlockSpec((None, s, f8), lambda i, j: (i, 0, 0)), whole_j(tw)],
        [_sds((nb, f8, d), F32)], [pl.BlockSpec((None, f8, tw), lambda i, j: (i, 0, j))], tn_dot)
    mid_token = run("after_down_dx", dgate)
    for nm, dact in (("wg", dgate), ("wu", dup)):
        (gr[nm],) = _tiles(
            "w_" + {"wg": "gate", "wu": "up"}[nm] + "_dw", (d // tw, nb),
            [dact, sv["y2"]], [pl.BlockSpec((None, s, f8), lambda i, j: (j, 0, 0)), whole(tw)],
            [_sds((nb, f8, d), F32)], [pl.BlockSpec((None, f8, tw), lambda i, j: (j, 0, i))], tn_dot)
    ffn_token = run("after_ffn_grads", gr, gr["wu"])
    tm, tn = _tile(s, 512), _tile(d, 1024)
    act_spec = pl.BlockSpec((nb, tm, f8), lambda j, i: (0, i, 0))
    wt_spec = pl.BlockSpec((nb, f8, tn), lambda j, i: (0, 0, j))
    out_spec = pl.BlockSpec((tm, tn), lambda j, i: (i, j))
    (dy2,) = _tiles(
        "w_gate_dx", (d // tn, s // tm), [dgate, gw["wg"]], [act_spec, wt_spec],
        [_sds((s, d), F32)], [out_spec],
        lambda a, w: (_chain([(a[k], w[k]) for k in range(nb)], "nn"),),
        deps=[] if ffn_token is None else [ffn_token])
    (dy2,) = _tiles(
        "w_up_dx", (d // tn, s // tm), [dup, gw["wu"], dy2], [act_spec, wt_spec, out_spec],
        [_sds((s, d), F32)], [out_spec],
        lambda a, w, prev: (prev[...] + _chain([(a[k], w[k]) for k in range(nb)], "nn"),))
    g_ffn = _after(_after(sm["norm_ffn_g"], mid_token), run("before_ffn_norm", dy2))
    dh2, dh2_16, sg["norm_ffn_g"] = _rms_bwd("rms_ffn_bwd", sv["h2"], g_ffn, dy2, dh3)

    tm = _tile(s, 512)
    (dmo,) = _tiles(
        "w_out_dx", (s // tm,), [dh2_16, gw["wout"]], [rows(tm, d), _resident((d, d))],
        [_sds((s, d), BF16)], [rows(tm, d)], lambda a, w: (_dot(a[...], w[...], "nt"),))
    (gr["wout"],) = _tiles(
        "w_out_dw", (d // tw, d // tw), [sv["mo"], dh2_16], [whole(tw), whole_j(tw)],
        [_sds((d, d), F32)], [pl.BlockSpec((tw, tw), lambda i, j: (i, j))], tn_dot)
    gr["wout"] = gr["wout"].reshape(nb, d // nb, d)
    dz, *mix_grads = _mixer_bwd("mixer_bwd", sv["z"], dmo, sm["mixer"])
    sg["mixer"] = mix_grads
    (gr["win"],) = _tiles(
        "w_in_dw", (d // tw, nb), [sv["y1"], dz], [whole(tw), pl.BlockSpec((s, GROUP), lambda i, j: (0, j))],
        [_sds((nb, d, GROUP), F32)], [pl.BlockSpec((None, tw, GROUP), lambda i, j: (j, i, 0))], tn_dot)
    tm = _tile(s, 256)

    def in_dx(a, w, hv, gv, dv):
        dy1 = _chain([(a[:, k * GROUP:(k + 1) * GROUP], w[k]) for k in range(nb)], "nt")
        return _rms_back(hv[...], gv[...], dy1, dv[...])

    dh_in, sg["norm_mix_g"] = _tiles(
        "w_in_dx", (s // tm,),
        [dz, gw["win"], sv["h"], sm["norm_mix_g"], dh2],
        [rows(tm, nb * GROUP), _resident((nb, d, GROUP)), rows(tm, d), vec, rows(tm, d)],
        [_sds((s, d), F32), _sds((1, d), F32)], [rows(tm, d), vec], in_dx, summed=(1,))
    return dh_in, gr, sg


def _place():
    return lax.axis_index("x"), lax.axis_index("y"), lax.axis_index("c")


_HBM = pl.BlockSpec(memory_space=pltpu.HBM)
_SEM = pl.BlockSpec(memory_space=pltpu.SEMAPHORE)


def _split_call(name, bufs, old_sems, n_new, after, body, want_token):
    nb, no = len(bufs), len(old_sems)
    extra = [] if after is None else [after]

    def kbody(*refs):
        new = refs[nb + no + len(extra):nb + no + len(extra) + n_new]
        body(refs[:nb], refs[nb:nb + no], new)
        if want_token:
            refs[-1][...] = jnp.zeros_like(refs[-1])

    outs = pl.pallas_call(
        kbody, name=name,
        out_shape=tuple([pltpu.SemaphoreType.DMA(())] * n_new + [pltpu.HBM(b.shape, b.dtype) for b in bufs]
                        + ([_sds((8, 128), F32)] if want_token else [])),
        in_specs=[_HBM] * nb + [_SEM] * no + [ANY] * len(extra),
        out_specs=tuple([_SEM] * n_new + [_HBM] * nb
                        + ([pl.BlockSpec(memory_space=pltpu.VMEM)] if want_token else [])),
        input_output_aliases={i: n_new + i for i in range(nb)},
        compiler_params=pltpu.CompilerParams(has_side_effects=pltpu.SideEffectType.DATAFLOW_SIDE_EFFECTING),
    )(*[pltpu.with_memory_space_constraint(b, pltpu.HBM) for b in bufs], *old_sems, *extra)
    return list(outs[:n_new]), list(outs[n_new:n_new + nb]), (outs[-1] if want_token else None)


def _remote(ref_src, ref_dst, send_sem, recv_sem, to):
    return pltpu.make_async_remote_copy(src_ref=ref_src, dst_ref=ref_dst, send_sem=send_sem, recv_sem=recv_sem,
                                        device_id=to, device_id_type=MESH)


def _place_shard(name, w):
    n_layers, r, cdim = w.shape
    tr = _row_tile(r, 256)
    nt = r // tr

    def body(w_ref, *rest):
        outs, buf, sem = rest[:n_layers], rest[n_layers], rest[n_layers + 1]
        i = pl.program_id(0)
        x, y, c = _place()
        slot = i % 2

        def writes(step, sl):
            rows = pl.ds(pl.multiple_of(step * tr, 16), tr)
            return [pltpu.make_async_copy(buf.at[sl, l], outs[l].at[4 * x + 2 * y + c, rows, :], sem.at[sl, l])
                    for l in range(n_layers)]

        @pl.when(i >= 2)
        def _():
            for cp in writes(i - 2, slot):
                cp.wait()

        buf[slot] = w_ref[...].astype(BF16)
        for cp in writes(i, slot):
            cp.start()

        @pl.when(i == nt - 1)
        def _():
            for cp in writes(i, slot):
                cp.wait()
            if nt >= 2:
                for cp in writes(i - 1, 1 - slot):
                    cp.wait()

    return pl.pallas_call(
        body, name=name, grid=(nt,),
        in_specs=[pl.BlockSpec((n_layers, tr, cdim), lambda i: (0, i, 0))], out_specs=[ANY] * n_layers,
        out_shape=[_sds((N_DEV, r, cdim), BF16)] * n_layers,
        scratch_shapes=[pltpu.VMEM((2, n_layers, tr, cdim), BF16), pltpu.SemaphoreType.DMA((2, n_layers))],
        compiler_params=_params("arbitrary"),
    )(w)


def _gather_start(li, lands, after):
    nw = len(lands)

    def body(bufs, _, new):
        x, y, c = _place()
        chips = [(1 - x, y), (x, 1 - y), (1 - x, 1 - y)]
        for w in range(nw):
            mine = bufs[w].at[4 * x + 2 * y + c]
            _remote(mine, mine, new[8 * w], new[8 * w + 4], (x, y, 1 - c)).start()
            for j, (px, py) in enumerate(chips):
                _remote(mine, mine, new[8 * w + 1 + j], new[8 * w + 5 + j], (px, py, c)).start()

    return _split_call("gather_start_l%s" % li, lands, [], 8 * nw, after, body, True)


def _gather_forward(li, lands, sems, after):
    nw = len(lands)
    arrivals = [sems[8 * w + 5 + j] for w in range(nw) for j in range(3)]

    def body(bufs, old, new):
        x, y, c = _place()
        chips = [(1 - x, y), (x, 1 - y), (1 - x, 1 - y)]
        for j, (px, py) in enumerate(chips):
            for w in range(nw):
                got = bufs[w].at[4 * px + 2 * py + c]
                _remote(got, got, new[6 * w + j], old[3 * w + j], (px, py, c)).wait_recv()
                _remote(got, got, new[6 * w + j], new[6 * w + 3 + j], (x, y, 1 - c)).start()

    return _split_call("gather_forward_l%s" % li, lands, arrivals, 6 * nw, after, body, True)


def _gather_wait(li, lands, sems, fwd_sems, after):
    nw = len(lands)
    first = [sems[8 * w + k] for w in range(nw) for k in range(5)]

    def body(bufs, old, _):
        x, y, c = _place()
        sib = (x, y, 1 - c)
        chips = [(1 - x, y), (x, 1 - y), (1 - x, 1 - y)]
        n1 = 5 * nw
        for w in range(nw):
            mine = bufs[w].at[4 * x + 2 * y + c]
            theirs = bufs[w].at[4 * x + 2 * y + 1 - c]
            _remote(theirs, theirs, old[5 * w], old[5 * w + 4], sib).wait_recv()
            for k in range(4):
                _remote(mine, mine, old[5 * w + k], old[5 * w + 4], sib).wait_send()
            for j, (px, py) in enumerate(chips):
                sent = bufs[w].at[4 * px + 2 * py + c]
                got = bufs[w].at[4 * px + 2 * py + 1 - c]
                _remote(sent, sent, old[n1 + 6 * w + j], old[n1 + 6 * w + 3 + j], sib).wait_send()
                _remote(got, got, old[n1 + 6 * w + j], old[n1 + 6 * w + 3 + j], sib).wait_recv()

    _, lands, _ = _split_call("gather_wait_l%s" % li, lands, first + list(fwd_sems), 0, after, body, False)
    return lands


def _siblings_start(li, grads, after):
    nw = len(grads)
    lands = [lax.empty((4,) + g.shape[1:], g.dtype) for g in grads]

    def body(bufs, _, new):
        x, y, c = _place()
        for w in range(nw):
            for q in range(4):
                _remote(bufs[w].at[2 * q + (1 - c)], bufs[nw + w].at[q], new[8 * w + q], new[8 * w + 4 + q],
                        (x, y, 1 - c)).start()

    return _split_call("rs_siblings_start_l%s" % li, list(grads) + lands, [], 8 * nw, after, body, True)


def _siblings_wait(li, bufs, sems, after):
    nw = len(bufs) // 2

    def body(refs, old, _):
        x, y, c = _place()
        for w in range(nw):
            for q in range(4):
                _remote(refs[w].at[2 * q + (1 - c)], refs[nw + w].at[q], old[8 * w + q], old[8 * w + 4 + q],
                        (x, y, 1 - c)).wait()

    _, bufs, _ = _split_call("rs_siblings_wait_l%s" % li, bufs, sems, 0, after, body, False)
    return bufs[:nw], bufs[nw:]


_FLIPS = ((1, 0), (0, 1), (1, 1))


def _chips_copies(refs, nw, sems):
    x, y, c = _place()
    for w in range(nw):
        for r, (fx, fy) in enumerate(_FLIPS):
            px = 1 - x if fx else x
            py = 1 - y if fy else y
            yield _remote(refs[w].at[2 * px + py], refs[nw + w].at[r], sems[6 * w + r], sems[6 * w + 3 + r], (px, py, c))


def _chips_start(li, partials, after):
    nw = len(partials)
    lands = [lax.empty((3,) + a.shape[1:], a.dtype) for a in partials]

    def body(bufs, _, new):
        for cp in _chips_copies(bufs, nw, new):
            cp.start()

    return _split_call("rs_chips_start_l%s" % li, list(partials) + lands, [], 6 * nw, after, body, True)


def _chips_wait(li, bufs, sems, after):
    nw = len(bufs) // 2

    def body(refs, old, _):
        for cp in _chips_copies(refs, nw, old):
            cp.wait()

    _, bufs, _ = _split_call("rs_chips_wait_l%s" % li, bufs, sems, 0, after, body, False)
    return bufs[nw:]


def _row_tile(n, want):
    best = None
    for t in range(16, min(n, want) + 1, 16):
        if n % t == 0:
            best = t
    assert best is not None, n
    return best


def _chip_partials(name, grad, from_sibling):
    _, r, cdim = grad.shape
    tr = _row_tile(r, 256)
    nt = r // tr
    steps = 4 * nt

    def body(g_hbm, s_ref, pb_ref, own_ref, buf, sem):
        i, q = pl.program_id(0), pl.program_id(1)
        x, y, c = _place()
        n = 4 * i + q
        slot = n % 2

        def fetch(step, into):
            rows = pl.ds(pl.multiple_of((step // 4) * tr, 16), tr)
            return pltpu.make_async_copy(g_hbm.at[2 * (step % 4) + c, rows, :], buf.at[into], sem.at[into])

        @pl.when(n == 0)
        def _():
            fetch(0, 0).start()

        @pl.when(n + 1 < steps)
        def _():
            fetch(n + 1, 1 - slot).start()

        fetch(n, slot).wait()
        tot = buf[slot] + s_ref[...]
        pb_ref[...] = tot.astype(BF16)

        @pl.when(q == 2 * x + y)
        def _():
            own_ref[...] = tot

    return pl.pallas_call(
        body, name=name, grid=(nt, 4),
        in_specs=[ANY, pl.BlockSpec((None, tr, cdim), lambda i, q: (q, i, 0))],
        out_specs=[pl.BlockSpec((None, tr, cdim), lambda i, q: (q, i, 0)), pl.BlockSpec((tr, cdim), lambda i, q: (i, 0))],
        out_shape=[_sds((4, r, cdim), BF16), _sds((r, cdim), F32)],
        scratch_shapes=[pltpu.VMEM((2, tr, cdim), F32), pltpu.SemaphoreType.DMA((2,))],
        compiler_params=_params("arbitrary", "arbitrary"),
    )(grad, from_sibling)


def _adamw(w, g, m, v):
    m = ADAM_B1 * m + (1.0 - ADAM_B1) * g
    v = ADAM_B2 * v + (1.0 - ADAM_B2) * (g * g)
    m_hat = m / (1.0 - ADAM_B1 ** ADAM_STEP)
    v_hat = v / (1.0 - ADAM_B2 ** ADAM_STEP)
    delta = -ADAM_LR * (m_hat / (jnp.sqrt(v_hat) + ADAM_EPS) + ADAM_WD * w)
    return delta, m, v


def _finish_weight(name, li, own, from_chips, w, m, v, stacked):
    r, cdim = own.shape
    tr = _row_tile(r, 256)
    if stacked is None:
        stacked = [lax.empty(w.shape, F32) for _ in range(4)]

    def body(own_ref, fc_ref, w_ref, m_ref, v_ref, *rest):
        g_out, d_out, m_out, v_out = rest[4:]
        g = own_ref[...] + fc_ref[0].astype(F32) + fc_ref[1].astype(F32) + fc_ref[2].astype(F32)
        delta, mn, vn = _adamw(w_ref[...], g, m_ref[...], v_ref[...])
        g_out[...] = g
        d_out[...] = delta
        m_out[...] = mn
        v_out[...] = vn

    tile = pl.BlockSpec((tr, cdim), lambda i: (i, 0))
    lay = pl.BlockSpec((None, tr, cdim), lambda i: (li, i, 0))
    return pl.pallas_call(
        body, name=name, grid=(r // tr,),
        in_specs=[tile, pl.BlockSpec((3, tr, cdim), lambda i: (0, i, 0)), lay, lay, lay] + [ANY] * 4,
        out_specs=[lay] * 4, out_shape=[_sds(w.shape, F32)] * 4,
        input_output_aliases={5: 0, 6: 1, 7: 2, 8: 3},
        compiler_params=_params("parallel"),
    )(own, from_chips, w, m, v, *stacked)


def _allgather_small(name, v, reduce):
    r = v.shape[0]

    def body(x_ref, out_ref, *rest):
        if reduce:
            sum_ref, send_sems, recv_sems, local_sem = rest
        else:
            send_sems, recv_sems, local_sem = rest
        x, y, c = _place()
        me, sib = (x, y, c), (x, y, 1 - c)
        chips = [(1 - x, y), (x, 1 - y), (1 - x, 1 - y)]

        def rows(px, py, pc):
            return out_ref.at[pl.ds(pl.multiple_of((4 * px + 2 * py + pc) * r, 8), r), :]

        def copy(k, block, to, src=None):
            return pltpu.make_async_remote_copy(
                src_ref=rows(*block) if src is None else src, dst_ref=rows(*block),
                send_sem=send_sems.at[k], recv_sem=recv_sems.at[k], device_id=to, device_id_type=MESH)

        mine = pltpu.make_async_copy(x_ref, rows(*me), local_sem)
        mine.start()
        first = [copy(0, me, sib, src=x_ref)]
        first += [copy(1 + j, me, (*chip, c), src=x_ref) for j, chip in enumerate(chips)]
        for cp in first:
            cp.start()
        passed = [copy(4 + j, (*chip, c), sib) for j, chip in enumerate(chips)]
        for j, chip in enumerate(chips):
            copy(1 + j, (*chip, c), me).wait_recv()
            passed[j].start()
        copy(0, sib, me).wait_recv()
        for j, chip in enumerate(chips):
            copy(4 + j, (*chip, 1 - c), me).wait_recv()
        for cp in first + passed:
            cp.wait_send()
        mine.wait()
        if reduce:
            tot = out_ref[0:r, :]
            for d in range(1, N_DEV):
                tot = tot + out_ref[d * r:(d + 1) * r, :]
            sum_ref[...] = tot

    vm = pl.BlockSpec(memory_space=pltpu.VMEM)
    outs = [_sds((N_DEV * r, 128), F32)] + ([_sds((r, 128), F32)] if reduce else [])
    res = pl.pallas_call(
        body, name=name,
        in_specs=[vm], out_specs=[vm] * len(outs), out_shape=outs,
        scratch_shapes=[pltpu.SemaphoreType.DMA((7,)), pltpu.SemaphoreType.DMA((7,)), pltpu.SemaphoreType.DMA],
        compiler_params=pltpu.CompilerParams(vmem_limit_bytes=VMEM_LIMIT_BYTES),
    )(v)
    return res


def _adamw_small(name, w, g, m, v):
    def body(w_ref, g_ref, m_ref, v_ref, d_out, m_out, v_out):
        delta, mn, vn = _adamw(w_ref[...], g_ref[...], m_ref[...], v_ref[...])
        d_out[...] = delta
        m_out[...] = mn
        v_out[...] = vn

    vm = pl.BlockSpec(memory_space=pltpu.VMEM)
    return pl.pallas_call(
        body, name=name, in_specs=[vm] * 4, out_specs=[vm] * 3, out_shape=[_sds(w.shape, F32)] * 3,
        compiler_params=pltpu.CompilerParams(vmem_limit_bytes=VMEM_LIMIT_BYTES),
    )(w, g, m, v)


def _pack(arrays):
    flat, layout, off = [], [], 0
    for a in arrays:
        flat.append(a.reshape(-1).astype(F32))
        layout.append((off, a.shape))
        off += a.size
    total = -(-off // 1024) * 1024
    if total > off:
        flat.append(jnp.zeros((total - off,), F32))
    return jnp.concatenate(flat).reshape(total // 128, 128), layout


def _unpack(packed, layout):
    flat = packed.reshape(-1)
    return [flat[off:off + math.prod(shape)].reshape(shape) for off, shape in layout]


_BIG = ("win", "wout", "wg", "wu", "wd", "wpg", "wpp")
_BIG_FULL = {"win": "w_in", "wout": "w_out", "wg": "w_gate", "wu": "w_up", "wd": "w_down",
             "wpg": "w_ple_gate", "wpp": "w_ple_proj"}
_SMALL_REPLICATED = ("norm_mix_g", "sgu_ln_g", "sgu_ln_b", "sgu_w", "sgu_b", "cf_conv_b", "cf_ln_g", "cf_ln_b",
                     "pool_w", "pool_scale", "norm_ffn_g", "norm_ple_g", "final_norm_g")
_SMALL_SHARDED = ("sc_conv_w", "cf_conv_w")
_WEIGHTS = ("norm_mix_g", "w_in", "sgu_ln_g", "sgu_ln_b", "sgu_w", "sgu_b", "sc_conv_w", "cf_conv_w", "cf_conv_b",
            "cf_ln_g", "cf_ln_b", "pool_w", "pool_scale", "w_out", "norm_ffn_g", "w_gate", "w_up", "w_down",
            "norm_ple_g", "w_ple_gate", "w_ple_proj", "final_norm_g")


def _pad_rows(a, rows):
    return jnp.concatenate([a, jnp.zeros((rows - a.shape[0],) + a.shape[1:], a.dtype)], axis=0)


def _mixer_params(li, W, sc_full, cf_full):
    return [W["sgu_ln_g"][li][:, None, :], W["sgu_ln_b"][li][:, None, :], W["sgu_w"][li], W["sgu_b"][li][:, :, None],
            _pad_rows(sc_full[li], 8), _pad_rows(cf_full[li], 32),
            W["cf_conv_b"][li][None, :], W["cf_ln_g"][li][None, :], W["cf_ln_b"][li][None, :],
            W["pool_w"][li], W["pool_scale"][li][None, :]]


def _step(W, M, V, x, p, loss_target):
    n_layers = W["w_in"].shape[0]
    h = x[0]
    target = loss_target[0]
    p_all = p[:, 0]
    xi, yi, ci = _place()
    blk = 4 * xi + 2 * yi + ci
    csh = W["sc_conv_w"].shape[2]
    turned = ("w_gate", "w_up")
    W, M, V = ({n: (jnp.swapaxes(a, 1, 2) if n in turned else a) for n, a in t.items()} for t in (W, M, V))

    packed, lay = _pack([W[n] for n in _SMALL_SHARDED])
    (taps,) = _allgather_small("gather_conv_taps", packed, reduce=False)
    per_dev = [_unpack(taps[d * packed.shape[0]:(d + 1) * packed.shape[0]], lay) for d in range(N_DEV)]
    sc_full = jnp.concatenate([pd[0] for pd in per_dev], axis=-1)
    cf_full = jnp.concatenate([pd[1] for pd in per_dev], axis=-1)

    d = h.shape[1]
    placed = [_place_shard("place_" + n, W[_BIG_FULL[n]]) for n in _BIG]
    lands = [[placed[w][li] for w in range(len(_BIG))] for li in range(n_layers)]

    groups = (("win",), ("wout", "wg", "wu"), ("wd", "wpg", "wpp"))

    def gather(li, after):
        sems, bufs, tok = _gather_start(li, lands[li], after)
        return {"sems": sems, "bufs": bufs, "tok": tok, "got": {}}

    def fetcher(li, st):
        def fetch(name, after):
            if name not in st["got"]:
                grp = next(g_ for g_ in groups if name in g_)
                idx = [_BIG.index(n) for n in grp]
                sems = [s_ for i in idx for s_ in st["sems"][8 * i:8 * i + 8]]
                tag = "%d_%s" % (li, grp[0])
                fwd_sems, bufs, tok = _gather_forward(tag, [st["bufs"][i] for i in idx], sems, after)
                for n, a in zip(grp, _gather_wait(tag, bufs, sems, fwd_sems, tok)):
                    st["got"][n] = a.reshape(d, d) if n in ("wout", "wpg") else a
            return st["got"][name]
        return fetch

    saved, gathered, smalls = [], [], []
    moving = gather(0, None)
    for li in range(n_layers):
        sm = {"norm_mix_g": _after(W["norm_mix_g"][li][None, :], moving["tok"] if li == 0 else None),
              "norm_ffn_g": W["norm_ffn_g"][li][None, :],
              "norm_ple_g": W["norm_ple_g"][li][None, :], "mixer": _mixer_params(li, W, sc_full, cf_full)}
        following = {}

        def start_next(h3, li=li, following=following):
            following.update(gather(li + 1, h3))
            return following["tok"]

        h, sv = _layer_fwd(li, h, p_all, fetcher(li, moving), sm, start_next if li + 1 < n_layers else None)
        saved.append(sv)
        gathered.append(moving["got"])
        smalls.append(sm)
        moving = following
    loss, dh, d_final_g = _loss_head(h, W["final_norm_g"][None, :], target)

    big_out = {n: None for n in _BIG}
    small_grads = []

    def scatter_begin(li, tag, names, gr, after):
        sems, bufs, tok = _siblings_start(tag, [gr[n] for n in names], after)
        return {"li": li, "tag": tag, "names": names, "sems": sems, "bufs": bufs, "tok": tok}

    def scatter_middle(st, after):
        grads, from_sib = _siblings_wait(st["tag"], st["bufs"], st["sems"], after)
        parts = [_chip_partials("rs_sum_" + n, g_, s_) for n, g_, s_ in zip(st["names"], grads, from_sib)]
        st["own"] = [own for _, own in parts]
        st["sems"], st["bufs"], tok = _chips_start(st["tag"], [pb for pb, _ in parts], None)
        return tok

    def scatter_end(st, after):
        from_chips = _chips_wait(st["tag"], st["bufs"], st["sems"], after)
        for n, own, fc in zip(st["names"], st["own"], from_chips):
            full = _BIG_FULL[n]
            big_out[n] = _finish_weight("adamw_" + n, st["li"], own, fc, W[full], M[full], V[full], big_out[n])
        return big_out[st["names"][0]][0]

    early, late = ("wpp", "wpg", "wd", "wg", "wu"), ("wout", "win")
    pending, tok = None, None
    for li in reversed(range(n_layers)):
        hooks, first = {}, {}
        if pending is not None:
            hooks["after_down_dx"] = functools.partial(scatter_middle, pending)
        if li == 0:
            def begin_early(gr, after, first=first):
                first.update(scatter_begin(0, "0_early", early, gr, after))
                return first["tok"]

            hooks["after_ffn_grads"] = begin_early
            hooks["before_ffn_norm"] = lambda after, first=first: scatter_middle(first, after)
        dh, gr, sg = _layer_bwd(li, dh, p_all, gathered[li], smalls[li], saved[li], tok, hooks)
        small_grads.append(sg)
        done = None if pending is None else scatter_end(pending, dh)
        if li > 0:
            pending = scatter_begin(li, str(li), _BIG, gr, done)
            tok = pending["tok"]
        else:
            last = scatter_begin(0, "0_late", late, gr, done)
            scatter_middle(last, None)
            scatter_end(first, last["own"][0])
            scatter_end(last, None)
    small_grads.reverse()

    def stacked(fn):
        return jnp.stack([fn(sg) for sg in small_grads])

    mix = lambda i: (lambda sg: sg["mixer"][i])
    grads_small = {
        "norm_mix_g": stacked(lambda sg: sg["norm_mix_g"][0]),
        "sgu_ln_g": stacked(mix(0))[:, :, 0, :], "sgu_ln_b": stacked(mix(1))[:, :, 0, :],
        "sgu_w": stacked(mix(2)), "sgu_b": stacked(mix(3))[:, :, :, 0],
        "sc_conv_w": stacked(mix(4))[:, :SHORT_K], "cf_conv_w": stacked(mix(5))[:, :CONF_K],
        "cf_conv_b": stacked(mix(6))[:, 0], "cf_ln_g": stacked(mix(7))[:, 0], "cf_ln_b": stacked(mix(8))[:, 0],
        "pool_w": stacked(mix(9)), "pool_scale": stacked(mix(10))[:, 0],
        "norm_ffn_g": stacked(lambda sg: sg["norm_ffn_g"][0]), "norm_ple_g": stacked(lambda sg: sg["norm_ple_g"][0]),
        "final_norm_g": d_final_g[0],
    }
    order = _SMALL_REPLICATED + _SMALL_SHARDED
    packed, lay = _pack([grads_small[n] for n in order] + [loss])
    _, summed = _allgather_small("allreduce_small", packed, reduce=True)
    total = dict(zip(order + ("loss",), _unpack(summed, lay)))
    loss_all = total["loss"][0, 0]

    out_g, out_d, out_m, out_v = {}, {}, {}, {}
    pw, lay_r = _pack([W[n] for n in _SMALL_REPLICATED])
    pg, _ = _pack([total[n] for n in _SMALL_REPLICATED])
    pm, _ = _pack([M[n] for n in _SMALL_REPLICATED])
    pv, _ = _pack([V[n] for n in _SMALL_REPLICATED])
    dd, mm, vv = _adamw_small("adamw_small", pw, pg, pm, pv)
    for n, a, b, c_ in zip(_SMALL_REPLICATED, _unpack(dd, lay_r), _unpack(mm, lay_r), _unpack(vv, lay_r)):
        out_g[n], out_d[n], out_m[n], out_v[n] = total[n], a, b, c_
    pick = (jnp.arange(N_DEV) == blk).astype(F32)[None, None, :, None]
    mine = {n: jnp.sum(total[n].reshape(total[n].shape[:2] + (N_DEV, csh)) * pick, axis=2) for n in _SMALL_SHARDED}
    pw, lay_s = _pack([W[n] for n in _SMALL_SHARDED])
    pg, _ = _pack([mine[n] for n in _SMALL_SHARDED])
    pm, _ = _pack([M[n] for n in _SMALL_SHARDED])
    pv, _ = _pack([V[n] for n in _SMALL_SHARDED])
    dd, mm, vv = _adamw_small("adamw_conv_taps", pw, pg, pm, pv)
    for n, a, b, c_ in zip(_SMALL_SHARDED, _unpack(dd, lay_s), _unpack(mm, lay_s), _unpack(vv, lay_s)):
        out_g[n], out_d[n], out_m[n], out_v[n] = mine[n], a, b, c_
    for n in _BIG:
        for k, dst in enumerate((out_g, out_d, out_m, out_v)):
            full = _BIG_FULL[n]
            dst[full] = jnp.swapaxes(big_out[n][k], 1, 2) if full in turned else big_out[n][k]

    return (loss_all, dh[None], *[out_g[n] for n in _WEIGHTS], *[out_d[n] for n in _WEIGHTS],
            *[out_m[n] for n in _WEIGHTS], *[out_v[n] for n in _WEIGHTS])


def kernel(x, p, norm_mix_g, w_in, sgu_ln_g, sgu_ln_b, sgu_w, sgu_b, sc_conv_w, cf_conv_w, cf_conv_b, cf_ln_g, cf_ln_b, pool_w, pool_scale, w_out, norm_ffn_g, w_gate, w_up, w_down, norm_ple_g, w_ple_gate, w_ple_proj, final_norm_g, loss_target, m_norm_mix_g, m_w_in, m_sgu_ln_g, m_sgu_ln_b, m_sgu_w, m_sgu_b, m_sc_conv_w, m_cf_conv_w, m_cf_conv_b, m_cf_ln_g, m_cf_ln_b, m_pool_w, m_pool_scale, m_w_out, m_norm_ffn_g, m_w_gate, m_w_up, m_w_down, m_norm_ple_g, m_w_ple_gate, m_w_ple_proj, m_final_norm_g, v_norm_mix_g, v_w_in, v_sgu_ln_g, v_sgu_ln_b, v_sgu_w, v_sgu_b, v_sc_conv_w, v_cf_conv_w, v_cf_conv_b, v_cf_ln_g, v_cf_ln_b, v_pool_w, v_pool_scale, v_w_out, v_norm_ffn_g, v_w_gate, v_w_up, v_w_down, v_norm_ple_g, v_w_ple_gate, v_w_ple_proj, v_final_norm_g):
    given = dict(locals())
    W = {n: given[n] for n in _WEIGHTS}
    M = {n: given["m_" + n] for n in _WEIGHTS}
    V = {n: given["v_" + n] for n in _WEIGHTS}
    return _step(W, M, V, x, p, loss_target)
```

```python
import functools
import math

import jax
import jax.numpy as jnp
from jax import lax
from jax.experimental import pallas as pl
from jax.experimental.pallas import tpu as pltpu

F32 = jnp.float32
BF16 = jnp.bfloat16
EPS = 1e-6
HEAD = 128
GROUP = 4 * HEAD
HALO = 32
SHORT_K = 3
CONF_K = 31
POOL_WINDOWS = (2, 4, 8, 16)
N_DEV = 8
MESH = pl.DeviceIdType.MESH
VMEM_LIMIT_BYTES = 56 * 1024 * 1024

ADAM_LR = 0.001
ADAM_B1 = 0.9
ADAM_B2 = 0.999
ADAM_EPS = 1e-08
ADAM_WD = 0.01
ADAM_STEP = 10

ANY = pl.BlockSpec(memory_space=pl.ANY)


def _params(*sem):
    return pltpu.CompilerParams(dimension_semantics=sem, vmem_limit_bytes=VMEM_LIMIT_BYTES)


def _rms(x, g):
    return x * lax.rsqrt(jnp.mean(x * x, axis=-1, keepdims=True) + EPS) * g


def _ln(x, g, b):
    mu = jnp.mean(x, axis=-1, keepdims=True)
    xc = x - mu
    var = jnp.mean(xc * xc, axis=-1, keepdims=True)
    return xc * lax.rsqrt(var + EPS) * g + b


def _bdot(a, b, dims=(((1,), (0,)), ((), ()))):
    return lax.dot_general(a.astype(BF16), b.astype(BF16), dims, preferred_element_type=F32)


def _sgu_piece(zu, zv, lg, lb, w, b):
    u = jax.nn.gelu(zu)
    v = _ln(jax.nn.gelu(zv), lg, lb)
    row = lax.broadcasted_iota(jnp.int32, w.shape, 0)
    col = lax.broadcasted_iota(jnp.int32, w.shape, 1)
    wm = jnp.where(row >= col, w, 0.0)
    return u * (_bdot(wm, v) + b)


def _conf_post(c, g, b):
    return jax.nn.silu(_ln(c, g, b))


def _pool_count(first_pos, rows, w):
    pos = first_pos + lax.broadcasted_iota(jnp.int32, (rows, 1), 0) + 1
    return jnp.minimum(pos, w).astype(F32)


_DIMS = {
    "nn": (((1,), (0,)), ((), ())),
    "nt": (((1,), (1,)), ((), ())),
    "tn": (((0,), (0,)), ((), ())),
}


def _tiles(name, grid, ins, in_specs, outs, out_specs, compute, summed=(), deps=()):
    ni = len(ins)
    nd = len(deps)

    def body(*refs):
        vals = compute(*refs[:ni])
        first = functools.reduce(jnp.logical_and, [pl.program_id(a) == 0 for a in range(len(grid))])
        for idx, (r, v) in enumerate(zip(refs[ni + nd:], vals)):
            if idx in summed:
                @pl.when(first)
                def _(r=r):
                    r[...] = jnp.zeros_like(r)

                r[...] += v
            else:
                r[...] = v.astype(r.dtype)

    sem = ("arbitrary" if summed else "parallel",) * len(grid)
    return pl.pallas_call(
        body, name=name, grid=grid, in_specs=list(in_specs) + [ANY] * nd, out_specs=list(out_specs),
        out_shape=list(outs), compiler_params=_params(*sem),
    )(*ins, *deps)


def _resident(shape):
    return pl.BlockSpec(shape, lambda *_: (0,) * len(shape), pipeline_mode=pl.Buffered(1))


def _sds(shape, dtype):
    return jax.ShapeDtypeStruct(tuple(shape), dtype)


def _tile(n, want):
    t = min(n, want)
    assert n % t == 0, (n, want)
    return t


def _rms_fwd(name, h, g):
    s, d = h.shape
    tm = _tile(s, 512)

    def body(h_ref, g_ref, y_ref):
        y_ref[...] = _rms(h_ref[...], g_ref[...]).astype(BF16)

    return pl.pallas_call(
        body, name=name, grid=(s // tm,),
        in_specs=[pl.BlockSpec((tm, d), lambda i: (i, 0)), pl.BlockSpec((1, d), lambda i: (0, 0))],
        out_specs=pl.BlockSpec((tm, d), lambda i: (i, 0)),
        out_shape=_sds((s, d), BF16),
        compiler_params=_params("parallel"),
    )(h, g)


def _rms_back(h, g, dy, dh_in):
    _, vjp = jax.vjp(_rms, h, g)
    dh, dg = vjp(dy)
    return dh_in + dh, dg


def _rms_bwd(name, h, g, dy, dh_in):
    s, d = h.shape
    tm = _tile(s, 256)

    def body(h_ref, g_ref, dy_ref, dhin_ref, dh_ref, dh16_ref, dg_ref):
        dh, dg = _rms_back(h_ref[...], g_ref[...], dy_ref[...].astype(F32), dhin_ref[...])
        dh_ref[...] = dh
        dh16_ref[...] = dh.astype(BF16)

        @pl.when(pl.program_id(0) == 0)
        def _():
            dg_ref[...] = jnp.zeros_like(dg_ref)

        dg_ref[...] += dg

    tok = pl.BlockSpec((tm, d), lambda i: (i, 0))
    vec = pl.BlockSpec((1, d), lambda i: (0, 0))
    return pl.pallas_call(
        body, name=name, grid=(s // tm,),
        in_specs=[tok, vec, tok, tok],
        out_specs=[tok, tok, vec],
        out_shape=[_sds((s, d), F32), _sds((s, d), BF16), _sds((1, d), F32)],
        compiler_params=_params("arbitrary"),
    )(h, g, dy, dh_in)


def _loss_head(h, g, target):
    s, d = h.shape
    tm = _tile(s, 256)

    def body(h_ref, g_ref, t_ref, loss_ref, dh_ref, dg_ref):
        y, vjp = jax.vjp(_rms, h_ref[...], g_ref[...])
        err = y - t_ref[...]
        dh, dg = vjp(err * (1.0 / d))
        dh_ref[...] = dh
        per_token = jnp.mean(err * err, axis=-1, keepdims=True)
        part = 0.5 * jnp.sum(per_token, axis=0, keepdims=True)

        @pl.when(pl.program_id(0) == 0)
        def _():
            dg_ref[...] = jnp.zeros_like(dg_ref)
            loss_ref[...] = jnp.zeros_like(loss_ref)

        dg_ref[...] += dg
        loss_ref[...] += part

    tok = pl.BlockSpec((tm, d), lambda i: (i, 0))
    vec = pl.BlockSpec((1, d), lambda i: (0, 0))
    return pl.pallas_call(
        body, name="loss_head", grid=(s // tm,),
        in_specs=[tok, vec, tok],
        out_specs=[pl.BlockSpec((1, 1), lambda i: (0, 0)), tok, vec],
        out_shape=[_sds((1, 1), F32), _sds((s, d), F32), _sds((1, d), F32)],
        compiler_params=_params("arbitrary"),
    )(h, g, target)


def _ple_bwd(name, dh, pg_pre, pp):
    s, d = dh.shape
    tm = _tile(s, 512)

    def body(dh_ref, pg_ref, pp_ref, dpg_ref, dpp_ref):
        sg = jax.nn.sigmoid(pg_ref[...].astype(F32))
        dhv = dh_ref[...]
        dpg_ref[...] = (dhv * pp_ref[...].astype(F32) * sg * (1.0 - sg)).astype(BF16)
        dpp_ref[...] = (dhv * sg).astype(BF16)

    tok = pl.BlockSpec((tm, d), lambda i: (i, 0))
    return pl.pallas_call(
        body, name=name, grid=(s // tm,),
        in_specs=[tok, tok, tok], out_specs=[tok, tok],
        out_shape=[_sds((s, d), BF16), _sds((s, d), BF16)],
        compiler_params=_params("parallel"),
    )(dh, pg_pre, pp)


_U, _V, _HB, _BG, _CG, _CA, _CGT, _PD = (GROUP * i for i in range(8))


def _cols(c0, w=GROUP):
    return slice(c0, c0 + w)


_LANE_CHUNKS = tuple(_cols(i * HEAD, HEAD) for i in range(GROUP // HEAD))


def _shifted_copies(sh_ref, ext_ref, length):
    for r in range(1, 8):
        sh_ref[r - 1, 0:length, :] = ext_ref[pl.ds(r, length), :]


def _window(ext_ref, sh_ref, off, rows, cc):
    q, r = divmod(off, 8)
    if sh_ref is None or r == 0:
        return ext_ref[pl.ds(off, rows), cc]
    return sh_ref[r - 1, pl.ds(8 * q, rows), cc]


def _taps(ext_ref, w_ref, nk, start, rows, cc, flip=False, sh_ref=None):
    acc = None
    for k in range(nk):
        kw = nk - 1 - k if flip else k
        term = w_ref[kw:kw + 1, cc] * _window(ext_ref, sh_ref, start + k, rows, cc)
        acc = term if acc is None else acc + term
    return acc


def _mixer_param_specs():
    full = lambda *shape: pl.BlockSpec(shape, lambda i: (0,) * len(shape))
    return [
        full(4, 1, HEAD), full(4, 1, HEAD), full(4, HEAD, HEAD), full(4, HEAD, 1),
        full(8, GROUP), full(32, GROUP), full(1, GROUP), full(1, GROUP), full(1, GROUP),
        full(4, HEAD, HEAD), full(1, GROUP),
    ]


def _mixer_fwd(name, z, prm):
    s = z.shape[0]
    t = _tile(s, 256)
    hb = t // HALO

    def body(zp_ref, zm_ref, lg_ref, lb_ref, sw_ref, sb_ref, scw_ref, cfw_ref, cfb_ref, cg_ref, cb_ref,
             pw_ref, ps_ref, o_ref, ext_ref, sh_ref):
        i = pl.program_id(0)
        keep = (i > 0).astype(F32)
        main = pl.ds(HALO, t)

        for n in range(t // HEAD):
            rows = slice(n * HEAD, (n + 1) * HEAD)
            for hh in range(4):
                cu = _cols(_U + hh * HEAD, HEAD)
                cv = _cols(_V + hh * HEAD, HEAD)
                o_ref[rows, _cols(hh * HEAD, HEAD)] = _sgu_piece(
                    zm_ref[rows, cu], zm_ref[rows, cv], lg_ref[hh], lb_ref[hh], sw_ref[hh], sb_ref[hh]
                ).astype(BF16)

        ext_ref[0:HALO, :] = zp_ref[:, _cols(_CG)] * zp_ref[:, _cols(_HB)] * keep
        ext_ref[main, :] = zm_ref[:, _cols(_CG)] * zm_ref[:, _cols(_HB)]
        y = jnp.zeros((t, GROUP), F32)
        for k in range(SHORT_K):
            y = y + scw_ref[k:k + 1, :] * ext_ref[pl.ds(HALO - (SHORT_K - 1) + k, t), :]
        o_ref[:, _cols(GROUP)] = (zm_ref[:, _cols(_BG)] * y).astype(BF16)

        ext_ref[0:HALO, :] = zp_ref[:, _cols(_CA)] * jax.nn.sigmoid(zp_ref[:, _cols(_CGT)]) * keep
        ext_ref[main, :] = zm_ref[:, _cols(_CA)] * jax.nn.sigmoid(zm_ref[:, _cols(_CGT)])
        _shifted_copies(sh_ref, ext_ref, HALO + t - 8)
        c = jnp.concatenate(
            [cfb_ref[:, cc] + _taps(ext_ref, cfw_ref, CONF_K, HALO - (CONF_K - 1), t, cc, sh_ref=sh_ref)
             for cc in _LANE_CHUNKS], axis=1)
        o_ref[:, _cols(2 * GROUP)] = _conf_post(c, cg_ref[...], cb_ref[...]).astype(BF16)

        ext_ref[0:HALO, :] = zp_ref[:, _cols(_PD)] * keep
        ext_ref[main, :] = zm_ref[:, _cols(_PD)]
        for gi, w in enumerate(POOL_WINDOWS):
            cc = _cols(gi * HEAD, HEAD)
            acc = ext_ref[main, cc]
            for j in range(1, w):
                acc = acc + ext_ref[pl.ds(HALO - j, t), cc]
            q = acc / _pool_count(i * t, t, w) - ext_ref[main, cc]
            o_ref[:, _cols(3 * GROUP + gi * HEAD, HEAD)] = (_bdot(q, pw_ref[gi]) * ps_ref[:, cc]).astype(BF16)

    return pl.pallas_call(
        body, name=name, grid=(s // t,),
        in_specs=[pl.BlockSpec((HALO, 8 * GROUP), lambda i: (jnp.maximum(i * hb - 1, 0), 0)),
                  pl.BlockSpec((t, 8 * GROUP), lambda i: (i, 0)),
                  *_mixer_param_specs()],
        out_specs=pl.BlockSpec((t, 4 * GROUP), lambda i: (i, 0)),
        out_shape=_sds((s, 4 * GROUP), BF16),
        scratch_shapes=[pltpu.VMEM((HALO + t, GROUP), F32), pltpu.VMEM((7, HALO + t - 8, GROUP), F32)],
        compiler_params=_params("parallel"),
    )(z, z, *prm)


def _mixer_bwd(name, z, dmo, prm):
    s = z.shape[0]
    t = _tile(s, 256)
    hb = t // HALO
    nt = s // t
    last_halo = s // HALO - 1

    def body(zp_ref, zm_ref, zn_ref, dm_ref, dn_ref,
             lg_ref, lb_ref, sw_ref, sb_ref, scw_ref, cfw_ref, cfb_ref, cg_ref, cb_ref, pw_ref, ps_ref,
             dz_ref, dlg_ref, dlb_ref, dsw_ref, dsb_ref, dscw_ref, dcfw_ref, dcfb_ref, dcg_ref, dcb_ref,
             dpw_ref, dps_ref, extp_ref, extn_ref, extc_ref, sh_ref):
        i = pl.program_id(0)
        keep_prev = (i > 0).astype(F32)
        keep_next = (i < nt - 1).astype(F32)
        main = pl.ds(HALO, t)

        @pl.when(i == 0)
        def _():
            for r in (dlg_ref, dlb_ref, dsw_ref, dsb_ref, dscw_ref, dcfw_ref, dcfb_ref, dcg_ref, dcb_ref,
                      dpw_ref, dps_ref):
                r[...] = jnp.zeros_like(r)

        def rowsum(v):
            return jnp.sum(v, axis=0, keepdims=True)

        for n in range(t // HEAD):
            rows = slice(n * HEAD, (n + 1) * HEAD)
            for hh in range(4):
                cu = _cols(_U + hh * HEAD, HEAD)
                cv = _cols(_V + hh * HEAD, HEAD)
                _, vjp = jax.vjp(_sgu_piece, zm_ref[rows, cu], zm_ref[rows, cv],
                                 lg_ref[hh], lb_ref[hh], sw_ref[hh], sb_ref[hh])
                dzu, dzv, dlg, dlb, dsw, dsb = vjp(dm_ref[rows, _cols(hh * HEAD, HEAD)].astype(F32))
                dz_ref[rows, cu] = dzu.astype(BF16)
                dz_ref[rows, cv] = dzv.astype(BF16)
                dlg_ref[hh] += dlg
                dlb_ref[hh] += dlb
                dsw_ref[hh] += dsw
                dsb_ref[hh] += dsb

        extp_ref[0:HALO, :] = zp_ref[:, _cols(_CG)] * zp_ref[:, _cols(_HB)] * keep_prev
        extp_ref[main, :] = zm_ref[:, _cols(_CG)] * zm_ref[:, _cols(_HB)]
        dob = dm_ref[:, _cols(GROUP)].astype(F32)
        dy = dob * zm_ref[:, _cols(_BG)]
        extn_ref[0:t, :] = dy
        extn_ref[t:t + HALO, :] = dn_ref[:, _cols(GROUP)].astype(F32) * zn_ref[:, _cols(_BG)] * keep_next
        for cc in _LANE_CHUNKS:
            at = lambda c0: _cols(c0 + cc.start, HEAD)
            y = _taps(extp_ref, scw_ref, SHORT_K, HALO - (SHORT_K - 1), t, cc)
            dx = _taps(extn_ref, scw_ref, SHORT_K, 0, t, cc, flip=True)
            dy_c = extn_ref[0:t, cc]
            for k in range(SHORT_K):
                dscw_ref[k:k + 1, cc] += rowsum(dy_c * extp_ref[pl.ds(HALO - (SHORT_K - 1) + k, t), cc])
            dz_ref[:, at(_BG)] = (dm_ref[:, at(GROUP)].astype(F32) * y).astype(BF16)
            dz_ref[:, at(_CG)] = (dx * zm_ref[:, at(_HB)]).astype(BF16)
            dz_ref[:, at(_HB)] = (dx * zm_ref[:, at(_CG)]).astype(BF16)

        extc_ref[0:HALO, :] = zp_ref[:, _cols(_CA)] * jax.nn.sigmoid(zp_ref[:, _cols(_CGT)]) * keep_prev
        extc_ref[main, :] = zm_ref[:, _cols(_CA)] * jax.nn.sigmoid(zm_ref[:, _cols(_CGT)])
        extc_ref[HALO + t:HALO + t + HALO, :] = zn_ref[:, _cols(_CA)] * jax.nn.sigmoid(zn_ref[:, _cols(_CGT)])
        _shifted_copies(sh_ref, extc_ref, HALO + t + HALO - 8)
        conv = lambda start, rows: jnp.concatenate(
            [cfb_ref[:, cc] + _taps(extc_ref, cfw_ref, CONF_K, start, rows, cc, sh_ref=sh_ref) for cc in _LANE_CHUNKS],
            axis=1)
        c_main = conv(HALO - (CONF_K - 1), t)
        c_next = conv(HALO + t - (CONF_K - 1), HALO)
        _, vjp = jax.vjp(_conf_post, c_main, cg_ref[...], cb_ref[...])
        dc, dcg, dcb = vjp(dm_ref[:, _cols(2 * GROUP)].astype(F32))
        dcg_ref[...] += dcg
        dcb_ref[...] += dcb
        dcfb_ref[...] += rowsum(dc)
        _, vjp_next = jax.vjp(lambda cv: _conf_post(cv, cg_ref[...], cb_ref[...]), c_next)
        (dc_next,) = vjp_next(dn_ref[:, _cols(2 * GROUP)].astype(F32) * keep_next)
        extn_ref[0:t, :] = dc
        extn_ref[t:t + HALO, :] = dc_next
        for cc in _LANE_CHUNKS:
            dc_c = extn_ref[0:t, cc]
            for k in range(CONF_K):
                dcfw_ref[k:k + 1, cc] += rowsum(dc_c * _window(extc_ref, sh_ref, HALO - (CONF_K - 1) + k, t, cc))
        _shifted_copies(sh_ref, extn_ref, t + HALO - 8)
        for cc in _LANE_CHUNKS:
            at = lambda c0: _cols(c0 + cc.start, HEAD)
            dhc = _taps(extn_ref, cfw_ref, CONF_K, 0, t, cc, flip=True, sh_ref=sh_ref)
            sg = jax.nn.sigmoid(zm_ref[:, at(_CGT)])
            dz_ref[:, at(_CA)] = (dhc * sg).astype(BF16)
            dz_ref[:, at(_CGT)] = (dhc * zm_ref[:, at(_CA)] * sg * (1.0 - sg)).astype(BF16)

        extp_ref[0:HALO, :] = zp_ref[:, _cols(_PD)] * keep_prev
        extp_ref[main, :] = zm_ref[:, _cols(_PD)]
        for gi, w in enumerate(POOL_WINDOWS):
            cc = _cols(gi * HEAD, HEAD)
            oc = _cols(3 * GROUP + gi * HEAD, HEAD)
            acc = extp_ref[main, cc]
            for j in range(1, w):
                acc = acc + extp_ref[pl.ds(HALO - j, t), cc]
            count = _pool_count(i * t, t, w)
            q = acc / count - extp_ref[main, cc]
            dod = dm_ref[:, oc].astype(F32)
            dps_ref[:, cc] += rowsum(dod * _bdot(q, pw_ref[gi]))
            ds = dod * ps_ref[:, cc]
            dpw_ref[gi] += _bdot(q, ds, _DIMS["tn"])
            dq = _bdot(ds, pw_ref[gi], _DIMS["nt"])
            ds_next = dn_ref[:, oc].astype(F32) * ps_ref[:, cc] * keep_next
            dq_next = _bdot(ds_next, pw_ref[gi], _DIMS["nt"])
            extn_ref[0:t, cc] = dq / count
            extn_ref[t:t + HALO, cc] = dq_next * (1.0 / w)
            back = extn_ref[0:t, cc]
            for j in range(1, w):
                back = back + extn_ref[pl.ds(j, t), cc]
            dz_ref[:, _cols(_PD + gi * HEAD, HEAD)] = (back - dq).astype(BF16)

    full = lambda *shape: pl.BlockSpec(shape, lambda i: (0,) * len(shape))
    grad_specs = [full(4, 1, HEAD), full(4, 1, HEAD), full(4, HEAD, HEAD), full(4, HEAD, 1),
                  full(8, GROUP), full(32, GROUP), full(1, GROUP), full(1, GROUP), full(1, GROUP),
                  full(4, HEAD, HEAD), full(1, GROUP)]
    grad_shapes = [_sds(sp.block_shape, F32) for sp in grad_specs]
    nxt = lambda i: (jnp.minimum((i + 1) * hb, last_halo), 0)
    return pl.pallas_call(
        body, name=name, grid=(nt,),
        in_specs=[pl.BlockSpec((HALO, 8 * GROUP), lambda i: (jnp.maximum(i * hb - 1, 0), 0)),
                  pl.BlockSpec((t, 8 * GROUP), lambda i: (i, 0)),
                  pl.BlockSpec((HALO, 8 * GROUP), nxt),
                  pl.BlockSpec((t, 4 * GROUP), lambda i: (i, 0)),
                  pl.BlockSpec((HALO, 4 * GROUP), nxt),
                  *_mixer_param_specs()],
        out_specs=[pl.BlockSpec((t, 8 * GROUP), lambda i: (i, 0)), *grad_specs],
        out_shape=[_sds((s, 8 * GROUP), BF16), *grad_shapes],
        scratch_shapes=[pltpu.VMEM((HALO + t, GROUP), F32), pltpu.VMEM((t + HALO, GROUP), F32),
                        pltpu.VMEM((HALO + t + HALO, GROUP), F32), pltpu.VMEM((7, HALO + t + HALO - 8, GROUP), F32)],
        compiler_params=_params("arbitrary"),
    )(z, z, z, dmo, dmo, *prm)


def _dot(a, b, kind="nn"):
    return _bdot(a, b, _DIMS[kind])


def _chain(terms, kind):
    acc = None
    for a, b in terms:
        p = _dot(a, b, kind)
        acc = p if acc is None else acc + p
    return acc


def _after(g, token):
    return g if token is None else g + token[0:1, 0:1]


def _layer_fwd(li, h, p_all, fetch, sm, after_down=None):
    s, d = h.shape
    nb = N_DEV
    pd = p_all.shape[2]
    sv = {"h": h}
    rows = lambda t, w: pl.BlockSpec((t, w), lambda i: (i, 0))

    y1 = _rms_fwd("rms_mix", h, sm["norm_mix_g"])
    tm = _tile(s, 1024)
    (z,) = _tiles(
        "w_in_fwd", (s // tm, nb),
        [y1, fetch("win", y1)],
        [pl.BlockSpec((tm, d), lambda i, j: (i, 0)), pl.BlockSpec((None, d, GROUP), lambda i, j: (j, 0, 0))],
        [_sds((s, nb * GROUP), F32)], [pl.BlockSpec((tm, GROUP), lambda i, j: (i, j))],
        lambda a, w: (_dot(a[...], w[...]),))
    mo = _mixer_fwd("mixer_fwd", z, sm["mixer"])
    tm = _tile(s, 512)
    (h2,) = _tiles(
        "w_out_fwd", (s // tm,),
        [mo, fetch("wout", z), h], [rows(tm, d), _resident((d, d)), rows(tm, d)],
        [_sds((s, d), F32)], [rows(tm, d)],
        lambda a, w, hv: (hv[...] + _dot(a[...], w[...]),))
    sv.update(y1=y1, z=z, mo=mo, h2=h2)

    y2 = _rms_fwd("rms_ffn", h2, sm["norm_ffn_g"])
    gate_pre, up_pre, hmid = _ffn_up("ffn_up_fwd", y2, fetch("wg", z), fetch("wu", z))
    wd = fetch("wd", hmid)
    f8 = wd.shape[1]
    tm, tn = _tile(s, 512), _tile(d, 1024)
    (h3,) = _tiles(
        "w_down_fwd", (d // tn, s // tm),
        [hmid, wd, h2],
        [pl.BlockSpec((nb, tm, f8), lambda j, i: (0, i, 0)), pl.BlockSpec((nb, f8, tn), lambda j, i: (0, 0, j)),
         pl.BlockSpec((tm, tn), lambda j, i: (i, j))],
        [_sds((s, d), F32)], [pl.BlockSpec((tm, tn), lambda j, i: (i, j))],
        lambda a, w, hv: (hv[...] + _chain([(a[k], w[k]) for k in range(nb)], "nn"),))
    sv.update(y2=y2, gate_pre=gate_pre, up_pre=up_pre, hmid=hmid, h3=h3)

    y3 = _rms_fwd("rms_ple", h3, _after(sm["norm_ple_g"], None if after_down is None else after_down(h3)))
    tm = _tile(s, 1024)
    (pp,) = _tiles(
        "w_ple_proj_fwd", (s // tm, nb),
        [p_all, fetch("wpp", hmid)],
        [pl.BlockSpec((None, tm, pd), lambda i, j: (li, i, 0)), pl.BlockSpec((None, pd, pd), lambda i, j: (j, 0, 0))],
        [_sds((s, d), BF16)], [pl.BlockSpec((tm, pd), lambda i, j: (i, j))],
        lambda a, w: (_dot(a[...], w[...]),))
    tm = _tile(s, 256)

    def ple(a, w, hv, ppv):
        pg = _dot(a[...], w[...])
        return hv[...] + jax.nn.sigmoid(pg) * ppv[...].astype(F32), pg

    h4, pg_pre = _tiles(
        "w_ple_gate_fwd", (s // tm,),
        [y3, fetch("wpg", hmid), h3, pp], [rows(tm, d), _resident((d, d)), rows(tm, d), rows(tm, d)],
        [_sds((s, d), F32), _sds((s, d), BF16)], [rows(tm, d), rows(tm, d)], ple)
    sv.update(y3=y3, pp=pp, pg_pre=pg_pre)
    return h4, sv


def _ffn_up(name, y2, wg, wu):
    s, d = y2.shape
    nb, f8, _ = wg.shape
    tm = _tile(s, 1024)

    def compute(y_ref, wg_ref, wu_ref):
        yv = y_ref[...]
        g = _dot(yv, wg_ref[...], "nt")
        u = _dot(yv, wu_ref[...], "nt")
        return g, u, jax.nn.silu(g) * u

    wspec = pl.BlockSpec((None, f8, d), lambda i, j: (j, 0, 0))
    ospec = pl.BlockSpec((None, tm, f8), lambda i, j: (j, i, 0))
    return _tiles(name, (s // tm, nb), [y2, wg, wu], [pl.BlockSpec((tm, d), lambda i, j: (i, 0)), wspec, wspec],
                  [_sds((nb, s, f8), BF16)] * 3, [ospec] * 3, compute)


def _swiglu_bwd(dm, g, u):
    sg = jax.nn.sigmoid(g)
    silu = g * sg
    return dm * u * (sg + silu * (1.0 - sg)), dm * silu


def _layer_bwd(li, dh, p_all, gw, sm, sv, start_token=None, hooks=None):
    hooks = hooks or {}
    run = lambda name, *a: hooks[name](*a) if name in hooks else None
    s, d = dh.shape
    nb, f8, _ = gw["wg"].shape
    pd = p_all.shape[2]
    gr, sg = {}, {}
    rows = lambda t, w: pl.BlockSpec((t, w), lambda i: (i, 0))
    vec = pl.BlockSpec((1, d), lambda i: (0, 0))
    tw = _tile(d, 1024)
    whole = lambda w: pl.BlockSpec((s, w), lambda i, j: (0, i))
    whole_j = lambda w: pl.BlockSpec((s, w), lambda i, j: (0, j))
    tn_dot = lambda a, b: (_dot(a[...], b[...], "tn"),)

    dpg, dpp = _ple_bwd("ple_bwd", dh, sv["pg_pre"], sv["pp"])
    (gr["wpp"],) = _tiles(
        "w_ple_proj_dw", (nb,),
        [p_all, dpp], [pl.BlockSpec((None, s, pd), lambda j: (li, 0, 0)), pl.BlockSpec((s, pd), lambda j: (0, j))],
        [_sds((nb, pd, pd), F32)], [pl.BlockSpec((None, pd, pd), lambda j: (j, 0, 0))], tn_dot)
    (gr["wpg"],) = _tiles(
        "w_ple_gate_dw", (d // tw, d // tw), [sv["y3"], dpg], [whole(tw), whole_j(tw)],
        [_sds((d, d), F32)], [pl.BlockSpec((tw, tw), lambda i, j: (i, j))], tn_dot)
    gr["wpg"] = gr["wpg"].reshape(nb, d // nb, d)
    tm = _tile(s, 256)

    def ple_dx(a, w, hv, gv, dv):
        dh3, dg = _rms_back(hv[...], gv[...], _dot(a[...], w[...], "nt"), dv[...])
        return dh3, dh3, dg

    dh3, dh3_16, sg["norm_ple_g"] = _tiles(
        "w_ple_gate_dx", (s // tm,),
        [dpg, gw["wpg"], sv["h3"], _after(sm["norm_ple_g"], start_token), dh],
        [rows(tm, d), _resident((d, d)), rows(tm, d), vec, rows(tm, d)],
        [_sds((s, d), F32), _sds((s, d), BF16), _sds((1, d), F32)], [rows(tm, d), rows(tm, d), vec],
        ple_dx, summed=(2,))

    tm = _tile(s, 1024)
    blk_rows = pl.BlockSpec((None, tm, f8), lambda i, j: (j, i, 0))
    dgate, dup = _tiles(
        "w_down_dx", (s // tm, nb),
        [dh3_16, gw["wd"], sv["gate_pre"], sv["up_pre"]],
        [pl.BlockSpec((tm, d), lambda i, j: (i, 0)), pl.BlockSpec((None, f8, d), lambda i, j: (j, 0, 0)), blk_rows, blk_rows],
        [_sds((nb, s, f8), BF16)] * 2, [blk_rows] * 2,
        lambda a, w, g, u: _swiglu_bwd(_dot(a[...], w[...], "nt"), g[...].astype(F32), u[...].astype(F32)))
    (gr["wd"],) = _tiles(
        "w_down_dw", (nb, d // tw),
        [sv["hmid"], dh3_16], [pl.BlockSpec((None, s, f8), lambda i, j: (i, 0, 0)), whole_j(tw)],
        [_sds((nb, f8, d), F32)], [pl.BlockSpec((None, f8, tw), lambda i, j: (i, 0, j))], tn_dot)
    mid_token = run("after_down_dx", dgate)
    for nm, dact in (("wg", dgate), ("wu", dup)):
        (gr[nm],) = _tiles(
            "w_" + {"wg": "gate", "wu": "up"}[nm] + "_dw", (d // tw, nb),
            [dact, sv["y2"]], [pl.BlockSpec((None, s, f8), lambda i, j: (j, 0, 0)), whole(tw)],
            [_sds((nb, f8, d), F32)], [pl.BlockSpec((None, f8, tw), lambda i, j: (j, 0, i))], tn_dot)
    ffn_token = run("after_ffn_grads", gr, gr["wu"])
    tm, tn = _tile(s, 512), _tile(d, 1024)
    act_spec = pl.BlockSpec((nb, tm, f8), lambda j, i: (0, i, 0))
    wt_spec = pl.BlockSpec((nb, f8, tn), lambda j, i: (0, 0, j))
    out_spec = pl.BlockSpec((tm, tn), lambda j, i: (i, j))
    (dy2,) = _tiles(
        "w_gate_dx", (d // tn, s // tm), [dgate, gw["wg"]], [act_spec, wt_spec],
        [_sds((s, d), F32)], [out_spec],
        lambda a, w: (_chain([(a[k], w[k]) for k in range(nb)], "nn"),),
        deps=[] if ffn_token is None else [ffn_token])
    (dy2,) = _tiles(
        "w_up_dx", (d // tn, s // tm), [dup, gw["wu"], dy2], [act_spec, wt_spec, out_spec],
        [_sds((s, d), F32)], [out_spec],
        lambda a, w, prev: (prev[...] + _chain([(a[k], w[k]) for k in range(nb)], "nn"),))
    g_ffn = _after(_after(sm["norm_ffn_g"], mid_token), run("before_ffn_norm", dy2))
    dh2, dh2_16, sg["norm_ffn_g"] = _rms_bwd("rms_ffn_bwd", sv["h2"], g_ffn, dy2, dh3)

    tm = _tile(s, 512)
    (dmo,) = _tiles(
        "w_out_dx", (s // tm,), [dh2_16, gw["wout"]], [rows(tm, d), _resident((d, d))],
        [_sds((s, d), BF16)], [rows(tm, d)], lambda a, w: (_dot(a[...], w[...], "nt"),))
    (gr["wout"],) = _tiles(
        "w_out_dw", (d // tw, d // tw), [sv["mo"], dh2_16], [whole(tw), whole_j(tw)],
        [_sds((d, d), F32)], [pl.BlockSpec((tw, tw), lambda i, j: (i, j))], tn_dot)
    gr["wout"] = gr["wout"].reshape(nb, d // nb, d)
    dz, *mix_grads = _mixer_bwd("mixer_bwd", sv["z"], dmo, sm["mixer"])
    sg["mixer"] = mix_grads
    (gr["win"],) = _tiles(
        "w_in_dw", (d // tw, nb), [sv["y1"], dz], [whole(tw), pl.BlockSpec((s, GROUP), lambda i, j: (0, j))],
        [_sds((nb, d, GROUP), F32)], [pl.BlockSpec((None, tw, GROUP), lambda i, j: (j, i, 0))], tn_dot)
    tm = _tile(s, 256)

    def in_dx(a, w, hv, gv, dv):
        dy1 = _chain([(a[:, k * GROUP:(k + 1) * GROUP], w[k]) for k in range(nb)], "nt")
        return _rms_back(hv[...], gv[...], dy1, dv[...])

    dh_in, sg["norm_mix_g"] = _tiles(
        "w_in_dx", (s // tm,),
        [dz, gw["win"], sv["h"], sm["norm_mix_g"], dh2],
        [rows(tm, nb * GROUP), _resident((nb, d, GROUP)), rows(tm, d), vec, rows(tm, d)],
        [_sds((s, d), F32), _sds((1, d), F32)], [rows(tm, d), vec], in_dx, summed=(1,))
    return dh_in, gr, sg


def _place():
    return lax.axis_index("x"), lax.axis_index("y"), lax.axis_index("c")


_HBM = pl.BlockSpec(memory_space=pltpu.HBM)
_SEM = pl.BlockSpec(memory_space=pltpu.SEMAPHORE)


def _split_call(name, bufs, old_sems, n_new, after, body, want_token):
    nb, no = len(bufs), len(old_sems)
    extra = [] if after is None else [after]

    def kbody(*refs):
        new = refs[nb + no + len(extra):nb + no + len(extra) + n_new]
        body(refs[:nb], refs[nb:nb + no], new)
        if want_token:
            refs[-1][...] = jnp.zeros_like(refs[-1])

    outs = pl.pallas_call(
        kbody, name=name,
        out_shape=tuple([pltpu.SemaphoreType.DMA(())] * n_new + [pltpu.HBM(b.shape, b.dtype) for b in bufs]
                        + ([_sds((8, 128), F32)] if want_token else [])),
        in_specs=[_HBM] * nb + [_SEM] * no + [ANY] * len(extra),
        out_specs=tuple([_SEM] * n_new + [_HBM] * nb
                        + ([pl.BlockSpec(memory_space=pltpu.VMEM)] if want_token else [])),
        input_output_aliases={i: n_new + i for i in range(nb)},
        compiler_params=pltpu.CompilerParams(has_side_effects=pltpu.SideEffectType.DATAFLOW_SIDE_EFFECTING),
    )(*[pltpu.with_memory_space_constraint(b, pltpu.HBM) for b in bufs], *old_sems, *extra)
    return list(outs[:n_new]), list(outs[n_new:n_new + nb]), (outs[-1] if want_token else None)


def _remote(ref_src, ref_dst, send_sem, recv_sem, to):
    return pltpu.make_async_remote_copy(src_ref=ref_src, dst_ref=ref_dst, send_sem=send_sem, recv_sem=recv_sem,
                                        device_id=to, device_id_type=MESH)


def _place_shard(name, w):
    n_layers, r, cdim = w.shape
    tr = _row_tile(r, 256)
    nt = r // tr

    def body(w_ref, *rest):
        outs, buf, sem = rest[:n_layers], rest[n_layers], rest[n_layers + 1]
        i = pl.program_id(0)
        x, y, c = _place()
        slot = i % 2

        def writes(step, sl):
            rows = pl.ds(pl.multiple_of(step * tr, 16), tr)
            return [pltpu.make_async_copy(buf.at[sl, l], outs[l].at[4 * x + 2 * y + c, rows, :], sem.at[sl, l])
                    for l in range(n_layers)]

        @pl.when(i >= 2)
        def _():
            for cp in writes(i - 2, slot):
                cp.wait()

        buf[slot] = w_ref[...].astype(BF16)
        for cp in writes(i, slot):
            cp.start()

        @pl.when(i == nt - 1)
        def _():
            for cp in writes(i, slot):
                cp.wait()
            if nt >= 2:
                for cp in writes(i - 1, 1 - slot):
                    cp.wait()

    return pl.pallas_call(
        body, name=name, grid=(nt,),
        in_specs=[pl.BlockSpec((n_layers, tr, cdim), lambda i: (0, i, 0))], out_specs=[ANY] * n_layers,
        out_shape=[_sds((N_DEV, r, cdim), BF16)] * n_layers,
        scratch_shapes=[pltpu.VMEM((2, n_layers, tr, cdim), BF16), pltpu.SemaphoreType.DMA((2, n_layers))],
        compiler_params=_params("arbitrary"),
    )(w)


def _gather_start(li, lands, after):
    nw = len(lands)

    def body(bufs, _, new):
        x, y, c = _place()
        chips = [(1 - x, y), (x, 1 - y), (1 - x, 1 - y)]
        for w in range(nw):
            mine = bufs[w].at[4 * x + 2 * y + c]
            _remote(mine, mine, new[8 * w], new[8 * w + 4], (x, y, 1 - c)).start()
            for j, (px, py) in enumerate(chips):
                _remote(mine, mine, new[8 * w + 1 + j], new[8 * w + 5 + j], (px, py, c)).start()

    return _split_call("gather_start_l%s" % li, lands, [], 8 * nw, after, body, True)


def _gather_forward(li, lands, sems, after):
    nw = len(lands)
    arrivals = [sems[8 * w + 5 + j] for w in range(nw) for j in range(3)]

    def body(bufs, old, new):
        x, y, c = _place()
        chips = [(1 - x, y), (x, 1 - y), (1 - x, 1 - y)]
        for j, (px, py) in enumerate(chips):
            for w in range(nw):
                got = bufs[w].at[4 * px + 2 * py + c]
                _remote(got, got, new[6 * w + j], old[3 * w + j], (px, py, c)).wait_recv()
                _remote(got, got, new[6 * w + j], new[6 * w + 3 + j], (x, y, 1 - c)).start()

    return _split_call("gather_forward_l%s" % li, lands, arrivals, 6 * nw, after, body, True)


def _gather_wait(li, lands, sems, fwd_sems, after):
    nw = len(lands)
    first = [sems[8 * w + k] for w in range(nw) for k in range(5)]

    def body(bufs, old, _):
        x, y, c = _place()
        sib = (x, y, 1 - c)
        chips = [(1 - x, y), (x, 1 - y), (1 - x, 1 - y)]
        n1 = 5 * nw
        for w in range(nw):
            mine = bufs[w].at[4 * x + 2 * y + c]
            theirs = bufs[w].at[4 * x + 2 * y + 1 - c]
            _remote(theirs, theirs, old[5 * w], old[5 * w + 4], sib).wait_recv()
            for k in range(4):
                _remote(mine, mine, old[5 * w + k], old[5 * w + 4], sib).wait_send()
            for j, (px, py) in enumerate(chips):
                sent = bufs[w].at[4 * px + 2 * py + c]
                got = bufs[w].at[4 * px + 2 * py + 1 - c]
                _remote(sent, sent, old[n1 + 6 * w + j], old[n1 + 6 * w + 3 + j], sib).wait_send()
                _remote(got, got, old[n1 + 6 * w + j], old[n1 + 6 * w + 3 + j], sib).wait_recv()

    _, lands, _ = _split_call("gather_wait_l%s" % li, lands, first + list(fwd_sems), 0, after, body, False)
    return lands


def _siblings_start(li, grads, after):
    nw = len(grads)
    lands = [lax.empty((4,) + g.shape[1:], g.dtype) for g in grads]

    def body(bufs, _, new):
        x, y, c = _place()
        for w in range(nw):
            for q in range(4):
                _remote(bufs[w].at[2 * q + (1 - c)], bufs[nw + w].at[q], new[8 * w + q], new[8 * w + 4 + q],
                        (x, y, 1 - c)).start()

    return _split_call("rs_siblings_start_l%s" % li, list(grads) + lands, [], 8 * nw, after, body, True)


def _siblings_wait(li, bufs, sems, after):
    nw = len(bufs) // 2

    def body(refs, old, _):
        x, y, c = _place()
        for w in range(nw):
            for q in range(4):
                _remote(refs[w].at[2 * q + (1 - c)], refs[nw + w].at[q], old[8 * w + q], old[8 * w + 4 + q],
                        (x, y, 1 - c)).wait()

    _, bufs, _ = _split_call("rs_siblings_wait_l%s" % li, bufs, sems, 0, after, body, False)
    return bufs[:nw], bufs[nw:]


_FLIPS = ((1, 0), (0, 1), (1, 1))


def _chips_copies(refs, nw, sems):
    x, y, c = _place()
    for w in range(nw):
        for r, (fx, fy) in enumerate(_FLIPS):
            px = 1 - x if fx else x
            py = 1 - y if fy else y
            yield _remote(refs[w].at[2 * px + py], refs[nw + w].at[r], sems[6 * w + r], sems[6 * w + 3 + r], (px, py, c))


def _chips_start(li, partials, after):
    nw = len(partials)
    lands = [lax.empty((3,) + a.shape[1:], a.dtype) for a in partials]

    def body(bufs, _, new):
        for cp in _chips_copies(bufs, nw, new):
            cp.start()

    return _split_call("rs_chips_start_l%s" % li, list(partials) + lands, [], 6 * nw, after, body, True)


def _chips_wait(li, bufs, sems, after):
    nw = len(bufs) // 2

    def body(refs, old, _):
        for cp in _chips_copies(refs, nw, old):
            cp.wait()

    _, bufs, _ = _split_call("rs_chips_wait_l%s" % li, bufs, sems, 0, after, body, False)
    return bufs[nw:]


def _row_tile(n, want):
    best = None
    for t in range(16, min(n, want) + 1, 16):
        if n % t == 0:
            best = t
    assert best is not None, n
    return best


def _chip_partials(name, grad, from_sibling):
    _, r, cdim = grad.shape
    tr = _row_tile(r, 256)
    nt = r // tr
    steps = 4 * nt

    def body(g_hbm, s_ref, pb_ref, own_ref, buf, sem):
        i, q = pl.program_id(0), pl.program_id(1)
        x, y, c = _place()
        n = 4 * i + q
        slot = n % 2

        def fetch(step, into):
            rows = pl.ds(pl.multiple_of((step // 4) * tr, 16), tr)
            return pltpu.make_async_copy(g_hbm.at[2 * (step % 4) + c, rows, :], buf.at[into], sem.at[into])

        @pl.when(n == 0)
        def _():
            fetch(0, 0).start()

        @pl.when(n + 1 < steps)
        def _():
            fetch(n + 1, 1 - slot).start()

        fetch(n, slot).wait()
        tot = buf[slot] + s_ref[...]
        pb_ref[...] = tot.astype(BF16)

        @pl.when(q == 2 * x + y)
        def _():
            own_ref[...] = tot

    return pl.pallas_call(
        body, name=name, grid=(nt, 4),
        in_specs=[ANY, pl.BlockSpec((None, tr, cdim), lambda i, q: (q, i, 0))],
        out_specs=[pl.BlockSpec((None, tr, cdim), lambda i, q: (q, i, 0)), pl.BlockSpec((tr, cdim), lambda i, q: (i, 0))],
        out_shape=[_sds((4, r, cdim), BF16), _sds((r, cdim), F32)],
        scratch_shapes=[pltpu.VMEM((2, tr, cdim), F32), pltpu.SemaphoreType.DMA((2,))],
        compiler_params=_params("arbitrary", "arbitrary"),
    )(grad, from_sibling)


def _adamw(w, g, m, v):
    m = ADAM_B1 * m + (1.0 - ADAM_B1) * g
    v = ADAM_B2 * v + (1.0 - ADAM_B2) * (g * g)
    m_hat = m / (1.0 - ADAM_B1 ** ADAM_STEP)
    v_hat = v / (1.0 - ADAM_B2 ** ADAM_STEP)
    delta = -ADAM_LR * (m_hat / (jnp.sqrt(v_hat) + ADAM_EPS) + ADAM_WD * w)
    return delta, m, v


def _finish_weight(name, li, own, from_chips, w, m, v, stacked):
    r, cdim = own.shape
    tr = _row_tile(r, 256)
    if stacked is None:
        stacked = [lax.empty(w.shape, F32) for _ in range(4)]

    def body(own_ref, fc_ref, w_ref, m_ref, v_ref, *rest):
        g_out, d_out, m_out, v_out = rest[4:]
        g = own_ref[...] + fc_ref[0].astype(F32) + fc_ref[1].astype(F32) + fc_ref[2].astype(F32)
        delta, mn, vn = _adamw(w_ref[...], g, m_ref[...], v_ref[...])
        g_out[...] = g
        d_out[...] = delta
        m_out[...] = mn
        v_out[...] = vn

    tile = pl.BlockSpec((tr, cdim), lambda i: (i, 0))
    lay = pl.BlockSpec((None, tr, cdim), lambda i: (li, i, 0))
    return pl.pallas_call(
        body, name=name, grid=(r // tr,),
        in_specs=[tile, pl.BlockSpec((3, tr, cdim), lambda i: (0, i, 0)), lay, lay, lay] + [ANY] * 4,
        out_specs=[lay] * 4, out_shape=[_sds(w.shape, F32)] * 4,
        input_output_aliases={5: 0, 6: 1, 7: 2, 8: 3},
        compiler_params=_params("parallel"),
    )(own, from_chips, w, m, v, *stacked)


def _allgather_small(name, v, reduce):
    r = v.shape[0]

    def body(x_ref, out_ref, *rest):
        if reduce:
            sum_ref, send_sems, recv_sems, local_sem = rest
        else:
            send_sems, recv_sems, local_sem = rest
        x, y, c = _place()
        me, sib = (x, y, c), (x, y, 1 - c)
        chips = [(1 - x, y), (x, 1 - y), (1 - x, 1 - y)]

        def rows(px, py, pc):
            return out_ref.at[pl.ds(pl.multiple_of((4 * px + 2 * py + pc) * r, 8), r), :]

        def copy(k, block, to, src=None):
            return pltpu.make_async_remote_copy(
                src_ref=rows(*block) if src is None else src, dst_ref=rows(*block),
                send_sem=send_sems.at[k], recv_sem=recv_sems.at[k], device_id=to, device_id_type=MESH)

        mine = pltpu.make_async_copy(x_ref, rows(*me), local_sem)
        mine.start()
        first = [copy(0, me, sib, src=x_ref)]
        first += [copy(1 + j, me, (*chip, c), src=x_ref) for j, chip in enumerate(chips)]
        for cp in first:
            cp.start()
        passed = [copy(4 + j, (*chip, c), sib) for j, chip in enumerate(chips)]
        for j, chip in enumerate(chips):
            copy(1 + j, (*chip, c), me).wait_recv()
            passed[j].start()
        copy(0, sib, me).wait_recv()
        for j, chip in enumerate(chips):
            copy(4 + j, (*chip, 1 - c), me).wait_recv()
        for cp in first + passed:
            cp.wait_send()
        mine.wait()
        if reduce:
            tot = out_ref[0:r, :]
            for d in range(1, N_DEV):
                tot = tot + out_ref[d * r:(d + 1) * r, :]
            sum_ref[...] = tot

    vm = pl.BlockSpec(memory_space=pltpu.VMEM)
    outs = [_sds((N_DEV * r, 128), F32)] + ([_sds((r, 128), F32)] if reduce else [])
    res = pl.pallas_call(
        body, name=name,
        in_specs=[vm], out_specs=[vm] * len(outs), out_shape=outs,
        scratch_shapes=[pltpu.SemaphoreType.DMA((7,)), pltpu.SemaphoreType.DMA((7,)), pltpu.SemaphoreType.DMA],
        compiler_params=pltpu.CompilerParams(vmem_limit_bytes=VMEM_LIMIT_BYTES),
    )(v)
    return res


def _adamw_small(name, w, g, m, v):
    def body(w_ref, g_ref, m_ref, v_ref, d_out, m_out, v_out):
        delta, mn, vn = _adamw(w_ref[...], g_ref[...], m_ref[...], v_ref[...])
        d_out[...] = delta
        m_out[...] = mn
        v_out[...] = vn

    vm = pl.BlockSpec(memory_space=pltpu.VMEM)
    return pl.pallas_call(
        body, name=name, in_specs=[vm] * 4, out_specs=[vm] * 3, out_shape=[_sds(w.shape, F32)] * 3,
        compiler_params=pltpu.CompilerParams(vmem_limit_bytes=VMEM_LIMIT_BYTES),
    )(w, g, m, v)


def _pack(arrays):
    flat, layout, off = [], [], 0
    for a in arrays:
        flat.append(a.reshape(-1).astype(F32))
        layout.append((off, a.shape))
        off += a.size
    total = -(-off // 1024) * 1024
    if total > off:
        flat.append(jnp.zeros((total - off,), F32))
    return jnp.concatenate(flat).reshape(total // 128, 128), layout


def _unpack(packed, layout):
    flat = packed.reshape(-1)
    return [flat[off:off + math.prod(shape)].reshape(shape) for off, shape in layout]


_BIG = ("win", "wout", "wg", "wu", "wd", "wpg", "wpp")
_BIG_FULL = {"win": "w_in", "wout": "w_out", "wg": "w_gate", "wu": "w_up", "wd": "w_down",
             "wpg": "w_ple_gate", "wpp": "w_ple_proj"}
_SMALL_REPLICATED = ("norm_mix_g", "sgu_ln_g", "sgu_ln_b", "sgu_w", "sgu_b", "cf_conv_b", "cf_ln_g", "cf_ln_b",
                     "pool_w", "pool_scale", "norm_ffn_g", "norm_ple_g", "final_norm_g")
_SMALL_SHARDED = ("sc_conv_w", "cf_conv_w")
_WEIGHTS = ("norm_mix_g", "w_in", "sgu_ln_g", "sgu_ln_b", "sgu_w", "sgu_b", "sc_conv_w", "cf_conv_w", "cf_conv_b",
            "cf_ln_g", "cf_ln_b", "pool_w", "pool_scale", "w_out", "norm_ffn_g", "w_gate", "w_up", "w_down",
            "norm_ple_g", "w_ple_gate", "w_ple_proj", "final_norm_g")


def _pad_rows(a, rows):
    return jnp.concatenate([a, jnp.zeros((rows - a.shape[0],) + a.shape[1:], a.dtype)], axis=0)


def _mixer_params(li, W, sc_full, cf_full):
    return [W["sgu_ln_g"][li][:, None, :], W["sgu_ln_b"][li][:, None, :], W["sgu_w"][li], W["sgu_b"][li][:, :, None],
            _pad_rows(sc_full[li], 8), _pad_rows(cf_full[li], 32),
            W["cf_conv_b"][li][None, :], W["cf_ln_g"][li][None, :], W["cf_ln_b"][li][None, :],
            W["pool_w"][li], W["pool_scale"][li][None, :]]


def _step(W, M, V, x, p, loss_target):
    n_layers = W["w_in"].shape[0]
    h = x[0]
    target = loss_target[0]
    p_all = p[:, 0]
    xi, yi, ci = _place()
    blk = 4 * xi + 2 * yi + ci
    csh = W["sc_conv_w"].shape[2]
    turned = ("w_gate", "w_up")
    W, M, V = ({n: (jnp.swapaxes(a, 1, 2) if n in turned else a) for n, a in t.items()} for t in (W, M, V))

    packed, lay = _pack([W[n] for n in _SMALL_SHARDED])
    (taps,) = _allgather_small("gather_conv_taps", packed, reduce=False)
    per_dev = [_unpack(taps[d * packed.shape[0]:(d + 1) * packed.shape[0]], lay) for d in range(N_DEV)]
    sc_full = jnp.concatenate([pd[0] for pd in per_dev], axis=-1)
    cf_full = jnp.concatenate([pd[1] for pd in per_dev], axis=-1)

    d = h.shape[1]
    placed = [_place_shard("place_" + n, W[_BIG_FULL[n]]) for n in _BIG]
    lands = [[placed[w][li] for w in range(len(_BIG))] for li in range(n_layers)]

    groups = (("win",), ("wout", "wg", "wu"), ("wd", "wpg", "wpp"))
    order = [(li, gi) for li in range(n_layers) for gi in range(len(groups))]
    moving, tok = [], None
    for li in range(n_layers):
        sems, bufs, tok = _gather_start(li, lands[li], tok)
        moving.append({"sems": sems, "bufs": bufs, "got": {}})
    passed = {}

    def pass_on(item, after):
        if item not in passed:
            li, gi = item
            idx = [_BIG.index(n) for n in groups[gi]]
            sems = [s_ for i in idx for s_ in moving[li]["sems"][8 * i:8 * i + 8]]
            tag = "%d_%s" % (li, groups[gi][0])
            passed[item] = (tag, sems) + tuple(_gather_forward(tag, [moving[li]["bufs"][i] for i in idx], sems, after))

    def fetcher(li):
        def fetch(name, after):
            got = moving[li]["got"]
            if name not in got:
                item = (li, next(gi for gi, g_ in enumerate(groups) if name in g_))
                pass_on(item, after)
                ahead = order.index(item) + 1
                if ahead < len(order) and order[ahead][0] >= 1:
                    pass_on(order[ahead], after)
                tag, sems, fwd_sems, bufs, _ = passed[item]
                for n, a in zip(groups[item[1]], _gather_wait(tag, bufs, sems, fwd_sems, after)):
                    got[n] = a.reshape(d, d) if n in ("wout", "wpg") else a
            return got[name]
        return fetch

    saved, gathered, smalls = [], [], []
    for li in range(n_layers):
        sm = {"norm_mix_g": _after(W["norm_mix_g"][li][None, :], tok if li == 0 else None),
              "norm_ffn_g": W["norm_ffn_g"][li][None, :],
              "norm_ple_g": W["norm_ple_g"][li][None, :], "mixer": _mixer_params(li, W, sc_full, cf_full)}
        h, sv = _layer_fwd(li, h, p_all, fetcher(li), sm)
        saved.append(sv)
        gathered.append(moving[li]["got"])
        smalls.append(sm)
    loss, dh, d_final_g = _loss_head(h, W["final_norm_g"][None, :], target)

    big_out = {n: None for n in _BIG}
    small_grads = []

    def scatter_begin(li, tag, names, gr, after):
        sems, bufs, tok = _siblings_start(tag, [gr[n] for n in names], after)
        return {"li": li, "tag": tag, "names": names, "sems": sems, "bufs": bufs, "tok": tok}

    def scatter_middle(st, after):
        grads, from_sib = _siblings_wait(st["tag"], st["bufs"], st["sems"], after)
        parts = [_chip_partials("rs_sum_" + n, g_, s_) for n, g_, s_ in zip(st["names"], grads, from_sib)]
        st["own"] = [own for _, own in parts]
        st["sems"], st["bufs"], tok = _chips_start(st["tag"], [pb for pb, _ in parts], None)
        return tok

    def scatter_end(st, after):
        from_chips = _chips_wait(st["tag"], st["bufs"], st["sems"], after)
        for n, own, fc in zip(st["names"], st["own"], from_chips):
            full = _BIG_FULL[n]
            big_out[n] = _finish_weight("adamw_" + n, st["li"], own, fc, W[full], M[full], V[full], big_out[n])
        return big_out[st["names"][0]][0]

    early, late = ("wpp", "wpg", "wd", "wg", "wu"), ("wout", "win")
    pending, tok = None, None
    for li in reversed(range(n_layers)):
        hooks, first = {}, {}
        if pending is not None:
            hooks["after_down_dx"] = functools.partial(scatter_middle, pending)
        if li == 0:
            def begin_early(gr, after, first=first):
                first.update(scatter_begin(0, "0_early", early, gr, after))
                return first["tok"]

            hooks["after_ffn_grads"] = begin_early
            hooks["before_ffn_norm"] = lambda after, first=first: scatter_middle(first, after)
        dh, gr, sg = _layer_bwd(li, dh, p_all, gathered[li], smalls[li], saved[li], tok, hooks)
        small_grads.append(sg)
        done = None if pending is None else scatter_end(pending, dh)
        if li > 0:
            pending = scatter_begin(li, str(li), _BIG, gr, done)
            tok = pending["tok"]
        else:
            last = scatter_begin(0, "0_late", late, gr, done)
            scatter_middle(last, None)
            scatter_end(first, last["own"][0])
            scatter_end(last, None)
    small_grads.reverse()

    def stacked(fn):
        return jnp.stack([fn(sg) for sg in small_grads])

    mix = lambda i: (lambda sg: sg["mixer"][i])
    grads_small = {
        "norm_mix_g": stacked(lambda sg: sg["norm_mix_g"][0]),
        "sgu_ln_g": stacked(mix(0))[:, :, 0, :], "sgu_ln_b": stacked(mix(1))[:, :, 0, :],
        "sgu_w": stacked(mix(2)), "sgu_b": stacked(mix(3))[:, :, :, 0],
        "sc_conv_w": stacked(mix(4))[:, :SHORT_K], "cf_conv_w": stacked(mix(5))[:, :CONF_K],
        "cf_conv_b": stacked(mix(6))[:, 0], "cf_ln_g": stacked(mix(7))[:, 0], "cf_ln_b": stacked(mix(8))[:, 0],
        "pool_w": stacked(mix(9)), "pool_scale": stacked(mix(10))[:, 0],
        "norm_ffn_g": stacked(lambda sg: sg["norm_ffn_g"][0]), "norm_ple_g": stacked(lambda sg: sg["norm_ple_g"][0]),
        "final_norm_g": d_final_g[0],
    }
    order = _SMALL_REPLICATED + _SMALL_SHARDED
    packed, lay = _pack([grads_small[n] for n in order] + [loss])
    _, summed = _allgather_small("allreduce_small", packed, reduce=True)
    total = dict(zip(order + ("loss",), _unpack(summed, lay)))
    loss_all = total["loss"][0, 0]

    out_g, out_d, out_m, out_v = {}, {}, {}, {}
    pw, lay_r = _pack([W[n] for n in _SMALL_REPLICATED])
    pg, _ = _pack([total[n] for n in _SMALL_REPLICATED])
    pm, _ = _pack([M[n] for n in _SMALL_REPLICATED])
    pv, _ = _pack([V[n] for n in _SMALL_REPLICATED])
    dd, mm, vv = _adamw_small("adamw_small", pw, pg, pm, pv)
    for n, a, b, c_ in zip(_SMALL_REPLICATED, _unpack(dd, lay_r), _unpack(mm, lay_r), _unpack(vv, lay_r)):
        out_g[n], out_d[n], out_m[n], out_v[n] = total[n], a, b, c_
    pick = (jnp.arange(N_DEV) == blk).astype(F32)[None, None, :, None]
    mine = {n: jnp.sum(total[n].reshape(total[n].shape[:2] + (N_DEV, csh)) * pick, axis=2) for n in _SMALL_SHARDED}
    pw, lay_s = _pack([W[n] for n in _SMALL_SHARDED])
    pg, _ = _pack([mine[n] for n in _SMALL_SHARDED])
    pm, _ = _pack([M[n] for n in _SMALL_SHARDED])
    pv, _ = _pack([V[n] for n in _SMALL_SHARDED])
    dd, mm, vv = _adamw_small("adamw_conv_taps", pw, pg, pm, pv)
    for n, a, b, c_ in zip(_SMALL_SHARDED, _unpack(dd, lay_s), _unpack(mm, lay_s), _unpack(vv, lay_s)):
        out_g[n], out_d[n], out_m[n], out_v[n] = mine[n], a, b, c_
    for n in _BIG:
        for k, dst in enumerate((out_g, out_d, out_m, out_v)):
            full = _BIG_FULL[n]
            dst[full] = jnp.swapaxes(big_out[n][k], 1, 2) if full in turned else big_out[n][k]

    return (loss_all, dh[None], *[out_g[n] for n in _WEIGHTS], *[out_d[n] for n in _WEIGHTS],
            *[out_m[n] for n in _WEIGHTS], *[out_v[n] for n in _WEIGHTS])


def kernel(x, p, norm_mix_g, w_in, sgu_ln_g, sgu_ln_b, sgu_w, sgu_b, sc_conv_w, cf_conv_w, cf_conv_b, cf_ln_g, cf_ln_b, pool_w, pool_scale, w_out, norm_ffn_g, w_gate, w_up, w_down, norm_ple_g, w_ple_gate, w_ple_proj, final_norm_g, loss_target, m_norm_mix_g, m_w_in, m_sgu_ln_g, m_sgu_ln_b, m_sgu_w, m_sgu_b, m_sc_conv_w, m_cf_conv_w, m_cf_conv_b, m_cf_ln_g, m_cf_ln_b, m_pool_w, m_pool_scale, m_w_out, m_norm_ffn_g, m_w_gate, m_w_up, m_w_down, m_norm_ple_g, m_w_ple_gate, m_w_ple_proj, m_final_norm_g, v_norm_mix_g, v_w_in, v_sgu_ln_g, v_sgu_ln_b, v_sgu_w, v_sgu_b, v_sc_conv_w, v_cf_conv_w, v_cf_conv_b, v_cf_ln_g, v_cf_ln_b, v_pool_w, v_pool_scale, v_w_out, v_norm_ffn_g, v_w_gate, v_w_up, v_w_down, v_norm_ple_g, v_w_ple_gate, v_w_ple_proj, v_final_norm_g):
    given = dict(locals())
    W = {n: given[n] for n in _WEIGHTS}
    M = {n: given["m_" + n] for n in _WEIGHTS}
    V = {n: given["v_" + n] for n in _WEIGHTS}
    return _step(W, M, V, x, p, loss_target)
```

```python
import functools
import math

import jax
import jax.numpy as jnp
from jax import lax
from jax.experimental import pallas as pl
from jax.experimental.pallas import tpu as pltpu

F32 = jnp.float32
BF16 = jnp.bfloat16
EPS = 1e-6
HEAD = 128
GROUP = 4 * HEAD
HALO = 32
SHORT_K = 3
CONF_K = 31
POOL_WINDOWS = (2, 4, 8, 16)
N_DEV = 8
MESH = pl.DeviceIdType.MESH
VMEM_LIMIT_BYTES = 56 * 1024 * 1024

ADAM_LR = 0.001
ADAM_B1 = 0.9
ADAM_B2 = 0.999
ADAM_EPS = 1e-08
ADAM_WD = 0.01
ADAM_STEP = 10

ANY = pl.BlockSpec(memory_space=pl.ANY)


def _params(*sem):
    return pltpu.CompilerParams(dimension_semantics=sem, vmem_limit_bytes=VMEM_LIMIT_BYTES)


def _rms(x, g):
    return x * lax.rsqrt(jnp.mean(x * x, axis=-1, keepdims=True) + EPS) * g


def _ln(x, g, b):
    mu = jnp.mean(x, axis=-1, keepdims=True)
    xc = x - mu
    var = jnp.mean(xc * xc, axis=-1, keepdims=True)
    return xc * lax.rsqrt(var + EPS) * g + b


def _bdot(a, b, dims=(((1,), (0,)), ((), ()))):
    return lax.dot_general(a.astype(BF16), b.astype(BF16), dims, preferred_element_type=F32)


def _sgu_piece(zu, zv, lg, lb, w, b):
    u = jax.nn.gelu(zu)
    v = _ln(jax.nn.gelu(zv), lg, lb)
    row = lax.broadcasted_iota(jnp.int32, w.shape, 0)
    col = lax.broadcasted_iota(jnp.int32, w.shape, 1)
    wm = jnp.where(row >= col, w, 0.0)
    return u * (_bdot(wm, v) + b)


def _conf_post(c, g, b):
    return jax.nn.silu(_ln(c, g, b))


def _pool_count(first_pos, rows, w):
    pos = first_pos + lax.broadcasted_iota(jnp.int32, (rows, 1), 0) + 1
    return jnp.minimum(pos, w).astype(F32)


_DIMS = {
    "nn": (((1,), (0,)), ((), ())),
    "nt": (((1,), (1,)), ((), ())),
    "tn": (((0,), (0,)), ((), ())),
}


def _tiles(name, grid, ins, in_specs, outs, out_specs, compute, summed=(), deps=()):
    ni = len(ins)
    nd = len(deps)

    def body(*refs):
        vals = compute(*refs[:ni])
        first = functools.reduce(jnp.logical_and, [pl.program_id(a) == 0 for a in range(len(grid))])
        for idx, (r, v) in enumerate(zip(refs[ni + nd:], vals)):
            if idx in summed:
                @pl.when(first)
                def _(r=r):
                    r[...] = jnp.zeros_like(r)

                r[...] += v
            else:
                r[...] = v.astype(r.dtype)

    sem = ("arbitrary" if summed else "parallel",) * len(grid)
    return pl.pallas_call(
        body, name=name, grid=grid, in_specs=list(in_specs) + [ANY] * nd, out_specs=list(out_specs),
        out_shape=list(outs), compiler_params=_params(*sem),
    )(*ins, *deps)


def _resident(shape):
    return pl.BlockSpec(shape, lambda *_: (0,) * len(shape), pipeline_mode=pl.Buffered(1))


def _sds(shape, dtype):
    return jax.ShapeDtypeStruct(tuple(shape), dtype)


def _tile(n, want):
    t = min(n, want)
    assert n % t == 0, (n, want)
    return t


def _rms_fwd(name, h, g):
    s, d = h.shape
    tm = _tile(s, 512)

    def body(h_ref, g_ref, y_ref):
        y_ref[...] = _rms(h_ref[...], g_ref[...]).astype(BF16)

    return pl.pallas_call(
        body, name=name, grid=(s // tm,),
        in_specs=[pl.BlockSpec((tm, d), lambda i: (i, 0)), pl.BlockSpec((1, d), lambda i: (0, 0))],
        out_specs=pl.BlockSpec((tm, d), lambda i: (i, 0)),
        out_shape=_sds((s, d), BF16),
        compiler_params=_params("parallel"),
    )(h, g)


def _rms_back(h, g, dy, dh_in):
    _, vjp = jax.vjp(_rms, h, g)
    dh, dg = vjp(dy)
    return dh_in + dh, dg


def _rms_bwd(name, h, g, dy, dh_in):
    s, d = h.shape
    tm = _tile(s, 256)

    def body(h_ref, g_ref, dy_ref, dhin_ref, dh_ref, dh16_ref, dg_ref):
        dh, dg = _rms_back(h_ref[...], g_ref[...], dy_ref[...].astype(F32), dhin_ref[...])
        dh_ref[...] = dh
        dh16_ref[...] = dh.astype(BF16)

        @pl.when(pl.program_id(0) == 0)
        def _():
            dg_ref[...] = jnp.zeros_like(dg_ref)

        dg_ref[...] += dg

    tok = pl.BlockSpec((tm, d), lambda i: (i, 0))
    vec = pl.BlockSpec((1, d), lambda i: (0, 0))
    return pl.pallas_call(
        body, name=name, grid=(s // tm,),
        in_specs=[tok, vec, tok, tok],
        out_specs=[tok, tok, vec],
        out_shape=[_sds((s, d), F32), _sds((s, d), BF16), _sds((1, d), F32)],
        compiler_params=_params("arbitrary"),
    )(h, g, dy, dh_in)


def _loss_head(h, g, target):
    s, d = h.shape
    tm = _tile(s, 256)

    def body(h_ref, g_ref, t_ref, loss_ref, dh_ref, dg_ref):
        y, vjp = jax.vjp(_rms, h_ref[...], g_ref[...])
        err = y - t_ref[...]
        dh, dg = vjp(err * (1.0 / d))
        dh_ref[...] = dh
        per_token = jnp.mean(err * err, axis=-1, keepdims=True)
        part = 0.5 * jnp.sum(per_token, axis=0, keepdims=True)

        @pl.when(pl.program_id(0) == 0)
        def _():
            dg_ref[...] = jnp.zeros_like(dg_ref)
            loss_ref[...] = jnp.zeros_like(loss_ref)

        dg_ref[...] += dg
        loss_ref[...] += part

    tok = pl.BlockSpec((tm, d), lambda i: (i, 0))
    vec = pl.BlockSpec((1, d), lambda i: (0, 0))
    return pl.pallas_call(
        body, name="loss_head", grid=(s // tm,),
        in_specs=[tok, vec, tok],
        out_specs=[pl.BlockSpec((1, 1), lambda i: (0, 0)), tok, vec],
        out_shape=[_sds((1, 1), F32), _sds((s, d), F32), _sds((1, d), F32)],
        compiler_params=_params("arbitrary"),
    )(h, g, target)


def _ple_bwd(name, dh, pg_pre, pp):
    s, d = dh.shape
    tm = _tile(s, 512)

    def body(dh_ref, pg_ref, pp_ref, dpg_ref, dpp_ref):
        sg = jax.nn.sigmoid(pg_ref[...].astype(F32))
        dhv = dh_ref[...]
        dpg_ref[...] = (dhv * pp_ref[...].astype(F32) * sg * (1.0 - sg)).astype(BF16)
        dpp_ref[...] = (dhv * sg).astype(BF16)

    tok = pl.BlockSpec((tm, d), lambda i: (i, 0))
    return pl.pallas_call(
        body, name=name, grid=(s // tm,),
        in_specs=[tok, tok, tok], out_specs=[tok, tok],
        out_shape=[_sds((s, d), BF16), _sds((s, d), BF16)],
        compiler_params=_params("parallel"),
    )(dh, pg_pre, pp)


_U, _V, _HB, _BG, _CG, _CA, _CGT, _PD = (GROUP * i for i in range(8))


def _cols(c0, w=GROUP):
    return slice(c0, c0 + w)


_LANE_CHUNKS = tuple(_cols(i * HEAD, HEAD) for i in range(GROUP // HEAD))


def _shifted_copies(sh_ref, ext_ref, length):
    for r in range(1, 8):
        sh_ref[r - 1, 0:length, :] = ext_ref[pl.ds(r, length), :]


def _window(ext_ref, sh_ref, off, rows, cc):
    q, r = divmod(off, 8)
    if sh_ref is None or r == 0:
        return ext_ref[pl.ds(off, rows), cc]
    return sh_ref[r - 1, pl.ds(8 * q, rows), cc]


def _taps(ext_ref, w_ref, nk, start, rows, cc, flip=False, sh_ref=None):
    acc = None
    for k in range(nk):
        kw = nk - 1 - k if flip else k
        term = w_ref[kw:kw + 1, cc] * _window(ext_ref, sh_ref, start + k, rows, cc)
        acc = term if acc is None else acc + term
    return acc


def _mixer_param_specs():
    full = lambda *shape: pl.BlockSpec(shape, lambda i: (0,) * len(shape))
    return [
        full(4, 1, HEAD), full(4, 1, HEAD), full(4, HEAD, HEAD), full(4, HEAD, 1),
        full(8, GROUP), full(32, GROUP), full(1, GROUP), full(1, GROUP), full(1, GROUP),
        full(4, HEAD, HEAD), full(1, GROUP),
    ]


def _mixer_fwd(name, z, prm):
    s = z.shape[0]
    t = _tile(s, 256)
    hb = t // HALO

    def body(zp_ref, zm_ref, lg_ref, lb_ref, sw_ref, sb_ref, scw_ref, cfw_ref, cfb_ref, cg_ref, cb_ref,
             pw_ref, ps_ref, o_ref, ext_ref, sh_ref):
        i = pl.program_id(0)
        keep = (i > 0).astype(F32)
        main = pl.ds(HALO, t)

        for n in range(t // HEAD):
            rows = slice(n * HEAD, (n + 1) * HEAD)
            for hh in range(4):
                cu = _cols(_U + hh * HEAD, HEAD)
                cv = _cols(_V + hh * HEAD, HEAD)
                o_ref[rows, _cols(hh * HEAD, HEAD)] = _sgu_piece(
                    zm_ref[rows, cu], zm_ref[rows, cv], lg_ref[hh], lb_ref[hh], sw_ref[hh], sb_ref[hh]
                ).astype(BF16)

        ext_ref[0:HALO, :] = zp_ref[:, _cols(_CG)] * zp_ref[:, _cols(_HB)] * keep
        ext_ref[main, :] = zm_ref[:, _cols(_CG)] * zm_ref[:, _cols(_HB)]
        y = jnp.zeros((t, GROUP), F32)
        for k in range(SHORT_K):
            y = y + scw_ref[k:k + 1, :] * ext_ref[pl.ds(HALO - (SHORT_K - 1) + k, t), :]
        o_ref[:, _cols(GROUP)] = (zm_ref[:, _cols(_BG)] * y).astype(BF16)

        ext_ref[0:HALO, :] = zp_ref[:, _cols(_CA)] * jax.nn.sigmoid(zp_ref[:, _cols(_CGT)]) * keep
        ext_ref[main, :] = zm_ref[:, _cols(_CA)] * jax.nn.sigmoid(zm_ref[:, _cols(_CGT)])
        _shifted_copies(sh_ref, ext_ref, HALO + t - 8)
        c = jnp.concatenate(
            [cfb_ref[:, cc] + _taps(ext_ref, cfw_ref, CONF_K, HALO - (CONF_K - 1), t, cc, sh_ref=sh_ref)
             for cc in _LANE_CHUNKS], axis=1)
        o_ref[:, _cols(2 * GROUP)] = _conf_post(c, cg_ref[...], cb_ref[...]).astype(BF16)

        ext_ref[0:HALO, :] = zp_ref[:, _cols(_PD)] * keep
        ext_ref[main, :] = zm_ref[:, _cols(_PD)]
        for gi, w in enumerate(POOL_WINDOWS):
            cc = _cols(gi * HEAD, HEAD)
            acc = ext_ref[main, cc]
            for j in range(1, w):
                acc = acc + ext_ref[pl.ds(HALO - j, t), cc]
            q = acc / _pool_count(i * t, t, w) - ext_ref[main, cc]
            o_ref[:, _cols(3 * GROUP + gi * HEAD, HEAD)] = (_bdot(q, pw_ref[gi]) * ps_ref[:, cc]).astype(BF16)

    return pl.pallas_call(
        body, name=name, grid=(s // t,),
        in_specs=[pl.BlockSpec((HALO, 8 * GROUP), lambda i: (jnp.maximum(i * hb - 1, 0), 0)),
                  pl.BlockSpec((t, 8 * GROUP), lambda i: (i, 0)),
                  *_mixer_param_specs()],
        out_specs=pl.BlockSpec((t, 4 * GROUP), lambda i: (i, 0)),
        out_shape=_sds((s, 4 * GROUP), BF16),
        scratch_shapes=[pltpu.VMEM((HALO + t, GROUP), F32), pltpu.VMEM((7, HALO + t - 8, GROUP), F32)],
        compiler_params=_params("parallel"),
    )(z, z, *prm)


def _mixer_bwd(name, z, dmo, prm):
    s = z.shape[0]
    t = _tile(s, 256)
    hb = t // HALO
    nt = s // t
    last_halo = s // HALO - 1

    def body(zp_ref, zm_ref, zn_ref, dm_ref, dn_ref,
             lg_ref, lb_ref, sw_ref, sb_ref, scw_ref, cfw_ref, cfb_ref, cg_ref, cb_ref, pw_ref, ps_ref,
             dz_ref, dlg_ref, dlb_ref, dsw_ref, dsb_ref, dscw_ref, dcfw_ref, dcfb_ref, dcg_ref, dcb_ref,
             dpw_ref, dps_ref, extp_ref, extn_ref, extc_ref, sh_ref):
        i = pl.program_id(0)
        keep_prev = (i > 0).astype(F32)
        keep_next = (i < nt - 1).astype(F32)
        main = pl.ds(HALO, t)

        @pl.when(i == 0)
        def _():
            for r in (dlg_ref, dlb_ref, dsw_ref, dsb_ref, dscw_ref, dcfw_ref, dcfb_ref, dcg_ref, dcb_ref,
                      dpw_ref, dps_ref):
                r[...] = jnp.zeros_like(r)

        def rowsum(v):
            return jnp.sum(v, axis=0, keepdims=True)

        for n in range(t // HEAD):
            rows = slice(n * HEAD, (n + 1) * HEAD)
            for hh in range(4):
                cu = _cols(_U + hh * HEAD, HEAD)
                cv = _cols(_V + hh * HEAD, HEAD)
                _, vjp = jax.vjp(_sgu_piece, zm_ref[rows, cu], zm_ref[rows, cv],
                                 lg_ref[hh], lb_ref[hh], sw_ref[hh], sb_ref[hh])
                dzu, dzv, dlg, dlb, dsw, dsb = vjp(dm_ref[rows, _cols(hh * HEAD, HEAD)].astype(F32))
                dz_ref[rows, cu] = dzu.astype(BF16)
                dz_ref[rows, cv] = dzv.astype(BF16)
                dlg_ref[hh] += dlg
                dlb_ref[hh] += dlb
                dsw_ref[hh] += dsw
                dsb_ref[hh] += dsb

        extp_ref[0:HALO, :] = zp_ref[:, _cols(_CG)] * zp_ref[:, _cols(_HB)] * keep_prev
        extp_ref[main, :] = zm_ref[:, _cols(_CG)] * zm_ref[:, _cols(_HB)]
        dob = dm_ref[:, _cols(GROUP)].astype(F32)
        dy = dob * zm_ref[:, _cols(_BG)]
        extn_ref[0:t, :] = dy
        extn_ref[t:t + HALO, :] = dn_ref[:, _cols(GROUP)].astype(F32) * zn_ref[:, _cols(_BG)] * keep_next
        for cc in _LANE_CHUNKS:
            at = lambda c0: _cols(c0 + cc.start, HEAD)
            y = _taps(extp_ref, scw_ref, SHORT_K, HALO - (SHORT_K - 1), t, cc)
            dx = _taps(extn_ref, scw_ref, SHORT_K, 0, t, cc, flip=True)
            dy_c = extn_ref[0:t, cc]
            for k in range(SHORT_K):
                dscw_ref[k:k + 1, cc] += rowsum(dy_c * extp_ref[pl.ds(HALO - (SHORT_K - 1) + k, t), cc])
            dz_ref[:, at(_BG)] = (dm_ref[:, at(GROUP)].astype(F32) * y).astype(BF16)
            dz_ref[:, at(_CG)] = (dx * zm_ref[:, at(_HB)]).astype(BF16)
            dz_ref[:, at(_HB)] = (dx * zm_ref[:, at(_CG)]).astype(BF16)

        extc_ref[0:HALO, :] = zp_ref[:, _cols(_CA)] * jax.nn.sigmoid(zp_ref[:, _cols(_CGT)]) * keep_prev
        extc_ref[main, :] = zm_ref[:, _cols(_CA)] * jax.nn.sigmoid(zm_ref[:, _cols(_CGT)])
        extc_ref[HALO + t:HALO + t + HALO, :] = zn_ref[:, _cols(_CA)] * jax.nn.sigmoid(zn_ref[:, _cols(_CGT)])
        _shifted_copies(sh_ref, extc_ref, HALO + t + HALO - 8)
        conv = lambda start, rows: jnp.concatenate(
            [cfb_ref[:, cc] + _taps(extc_ref, cfw_ref, CONF_K, start, rows, cc, sh_ref=sh_ref) for cc in _LANE_CHUNKS],
            axis=1)
        c_main = conv(HALO - (CONF_K - 1), t)
        c_next = conv(HALO + t - (CONF_K - 1), HALO)
        _, vjp = jax.vjp(_conf_post, c_main, cg_ref[...], cb_ref[...])
        dc, dcg, dcb = vjp(dm_ref[:, _cols(2 * GROUP)].astype(F32))
        dcg_ref[...] += dcg
        dcb_ref[...] += dcb
        dcfb_ref[...] += rowsum(dc)
        _, vjp_next = jax.vjp(lambda cv: _conf_post(cv, cg_ref[...], cb_ref[...]), c_next)
        (dc_next,) = vjp_next(dn_ref[:, _cols(2 * GROUP)].astype(F32) * keep_next)
        extn_ref[0:t, :] = dc
        extn_ref[t:t + HALO, :] = dc_next
        for cc in _LANE_CHUNKS:
            dc_c = extn_ref[0:t, cc]
            for k in range(CONF_K):
                dcfw_ref[k:k + 1, cc] += rowsum(dc_c * _window(extc_ref, sh_ref, HALO - (CONF_K - 1) + k, t, cc))
        _shifted_copies(sh_ref, extn_ref, t + HALO - 8)
        for cc in _LANE_CHUNKS:
            at = lambda c0: _cols(c0 + cc.start, HEAD)
            dhc = _taps(extn_ref, cfw_ref, CONF_K, 0, t, cc, flip=True, sh_ref=sh_ref)
            sg = jax.nn.sigmoid(zm_ref[:, at(_CGT)])
            dz_ref[:, at(_CA)] = (dhc * sg).astype(BF16)
            dz_ref[:, at(_CGT)] = (dhc * zm_ref[:, at(_CA)] * sg * (1.0 - sg)).astype(BF16)

        extp_ref[0:HALO, :] = zp_ref[:, _cols(_PD)] * keep_prev
        extp_ref[main, :] = zm_ref[:, _cols(_PD)]
        for gi, w in enumerate(POOL_WINDOWS):
            cc = _cols(gi * HEAD, HEAD)
            oc = _cols(3 * GROUP + gi * HEAD, HEAD)
            acc = extp_ref[main, cc]
            for j in range(1, w):
                acc = acc + extp_ref[pl.ds(HALO - j, t), cc]
            count = _pool_count(i * t, t, w)
            q = acc / count - extp_ref[main, cc]
            dod = dm_ref[:, oc].astype(F32)
            dps_ref[:, cc] += rowsum(dod * _bdot(q, pw_ref[gi]))
            ds = dod * ps_ref[:, cc]
            dpw_ref[gi] += _bdot(q, ds, _DIMS["tn"])
            dq = _bdot(ds, pw_ref[gi], _DIMS["nt"])
            ds_next = dn_ref[:, oc].astype(F32) * ps_ref[:, cc] * keep_next
            dq_next = _bdot(ds_next, pw_ref[gi], _DIMS["nt"])
            extn_ref[0:t, cc] = dq / count
            extn_ref[t:t + HALO, cc] = dq_next * (1.0 / w)
            back = extn_ref[0:t, cc]
            for j in range(1, w):
                back = back + extn_ref[pl.ds(j, t), cc]
            dz_ref[:, _cols(_PD + gi * HEAD, HEAD)] = (back - dq).astype(BF16)

    full = lambda *shape: pl.BlockSpec(shape, lambda i: (0,) * len(shape))
    grad_specs = [full(4, 1, HEAD), full(4, 1, HEAD), full(4, HEAD, HEAD), full(4, HEAD, 1),
                  full(8, GROUP), full(32, GROUP), full(1, GROUP), full(1, GROUP), full(1, GROUP),
                  full(4, HEAD, HEAD), full(1, GROUP)]
    grad_shapes = [_sds(sp.block_shape, F32) for sp in grad_specs]
    nxt = lambda i: (jnp.minimum((i + 1) * hb, last_halo), 0)
    return pl.pallas_call(
        body, name=name, grid=(nt,),
        in_specs=[pl.BlockSpec((HALO, 8 * GROUP), lambda i: (jnp.maximum(i * hb - 1, 0), 0)),
                  pl.BlockSpec((t, 8 * GROUP), lambda i: (i, 0)),
                  pl.BlockSpec((HALO, 8 * GROUP), nxt),
                  pl.BlockSpec((t, 4 * GROUP), lambda i: (i, 0)),
                  pl.BlockSpec((HALO, 4 * GROUP), nxt),
                  *_mixer_param_specs()],
        out_specs=[pl.BlockSpec((t, 8 * GROUP), lambda i: (i, 0)), *grad_specs],
        out_shape=[_sds((s, 8 * GROUP), BF16), *grad_shapes],
        scratch_shapes=[pltpu.VMEM((HALO + t, GROUP), F32), pltpu.VMEM((t + HALO, GROUP), F32),
                        pltpu.VMEM((HALO + t + HALO, GROUP), F32), pltpu.VMEM((7, HALO + t + HALO - 8, GROUP), F32)],
        compiler_params=_params("arbitrary"),
    )(z, z, z, dmo, dmo, *prm)


def _dot(a, b, kind="nn"):
    return _bdot(a, b, _DIMS[kind])


def _chain(terms, kind):
    acc = None
    for a, b in terms:
        p = _dot(a, b, kind)
        acc = p if acc is None else acc + p
    return acc


def _after(g, token):
    return g if token is None else g + token[0:1, 0:1]


def _layer_fwd(li, h, p_all, fetch, sm, after_down=None):
    s, d = h.shape
    nb = N_DEV
    pd = p_all.shape[2]
    sv = {"h": h}
    rows = lambda t, w: pl.BlockSpec((t, w), lambda i: (i, 0))

    y1 = _rms_fwd("rms_mix", h, sm["norm_mix_g"])
    tm = _tile(s, 1024)
    (z,) = _tiles(
        "w_in_fwd", (s // tm, nb),
        [y1, fetch("win", y1)],
        [pl.BlockSpec((tm, d), lambda i, j: (i, 0)), pl.BlockSpec((None, d, GROUP), lambda i, j: (j, 0, 0))],
        [_sds((s, nb * GROUP), F32)], [pl.BlockSpec((tm, GROUP), lambda i, j: (i, j))],
        lambda a, w: (_dot(a[...], w[...]),))
    mo = _mixer_fwd("mixer_fwd", z, sm["mixer"])
    tm = _tile(s, 512)
    (h2,) = _tiles(
        "w_out_fwd", (s // tm,),
        [mo, fetch("wout", z), h], [rows(tm, d), _resident((d, d)), rows(tm, d)],
        [_sds((s, d), F32)], [rows(tm, d)],
        lambda a, w, hv: (hv[...] + _dot(a[...], w[...]),))
    sv.update(y1=y1, z=z, mo=mo, h2=h2)

    y2 = _rms_fwd("rms_ffn", h2, sm["norm_ffn_g"])
    gate_pre, up_pre, hmid = _ffn_up("ffn_up_fwd", y2, fetch("wg", z), fetch("wu", z))
    wd = fetch("wd", hmid)
    f8 = wd.shape[1]
    tm, tn = _tile(s, 512), _tile(d, 1024)
    (h3,) = _tiles(
        "w_down_fwd", (d // tn, s // tm),
        [hmid, wd, h2],
        [pl.BlockSpec((nb, tm, f8), lambda j, i: (0, i, 0)), pl.BlockSpec((nb, f8, tn), lambda j, i: (0, 0, j)),
         pl.BlockSpec((tm, tn), lambda j, i: (i, j))],
        [_sds((s, d), F32)], [pl.BlockSpec((tm, tn), lambda j, i: (i, j))],
        lambda a, w, hv: (hv[...] + _chain([(a[k], w[k]) for k in range(nb)], "nn"),))
    sv.update(y2=y2, gate_pre=gate_pre, up_pre=up_pre, hmid=hmid, h3=h3)

    y3 = _rms_fwd("rms_ple", h3, _after(sm["norm_ple_g"], None if after_down is None else after_down(h3)))
    tm = _tile(s, 1024)
    (pp,) = _tiles(
        "w_ple_proj_fwd", (s // tm, nb),
        [p_all, fetch("wpp", hmid)],
        [pl.BlockSpec((None, tm, pd), lambda i, j: (li, i, 0)), pl.BlockSpec((None, pd, pd), lambda i, j: (j, 0, 0))],
        [_sds((s, d), BF16)], [pl.BlockSpec((tm, pd), lambda i, j: (i, j))],
        lambda a, w: (_dot(a[...], w[...]),))
    tm = _tile(s, 256)

    def ple(a, w, hv, ppv):
        pg = _dot(a[...], w[...])
        return hv[...] + jax.nn.sigmoid(pg) * ppv[...].astype(F32), pg

    h4, pg_pre = _tiles(
        "w_ple_gate_fwd", (s // tm,),
        [y3, fetch("wpg", hmid), h3, pp], [rows(tm, d), _resident((d, d)), rows(tm, d), rows(tm, d)],
        [_sds((s, d), F32), _sds((s, d), BF16)], [rows(tm, d), rows(tm, d)], ple)
    sv.update(y3=y3, pp=pp, pg_pre=pg_pre)
    return h4, sv


def _ffn_up(name, y2, wg, wu):
    s, d = y2.shape
    nb, f8, _ = wg.shape
    tm = _tile(s, 1024)

    def compute(y_ref, wg_ref, wu_ref):
        yv = y_ref[...]
        g = _dot(yv, wg_ref[...], "nt")
        u = _dot(yv, wu_ref[...], "nt")
        return g, u, jax.nn.silu(g) * u

    wspec = pl.BlockSpec((None, f8, d), lambda i, j: (j, 0, 0))
    ospec = pl.BlockSpec((None, tm, f8), lambda i, j: (j, i, 0))
    return _tiles(name, (s // tm, nb), [y2, wg, wu], [pl.BlockSpec((tm, d), lambda i, j: (i, 0)), wspec, wspec],
                  [_sds((nb, s, f8), BF16)] * 3, [ospec] * 3, compute)


def _swiglu_bwd(dm, g, u):
    sg = jax.nn.sigmoid(g)
    silu = g * sg
    return dm * u * (sg + silu * (1.0 - sg)), dm * silu


def _layer_bwd(li, dh, p_all, gw, sm, sv, start_token=None, hooks=None):
    hooks = hooks or {}
    run = lambda name, *a: hooks[name](*a) if name in hooks else None
    s, d = dh.shape
    nb, f8, _ = gw["wg"].shape
    pd = p_all.shape[2]
    gr, sg = {}, {}
    rows = lambda t, w: pl.BlockSpec((t, w), lambda i: (i, 0))
    vec = pl.BlockSpec((1, d), lambda i: (0, 0))
    tw = _tile(d, 1024)
    whole = lambda w: pl.BlockSpec((s, w), lambda i, j: (0, i))
    whole_j = lambda w: pl.BlockSpec((s, w), lambda i, j: (0, j))
    tn_dot = lambda a, b: (_dot(a[...], b[...], "tn"),)

    dpg, dpp = _ple_bwd("ple_bwd", dh, sv["pg_pre"], sv["pp"])
    (gr["wpp"],) = _tiles(
        "w_ple_proj_dw", (nb,),
        [p_all, dpp], [pl.BlockSpec((None, s, pd), lambda j: (li, 0, 0)), pl.BlockSpec((s, pd), lambda j: (0, j))],
        [_sds((nb, pd, pd), F32)], [pl.BlockSpec((None, pd, pd), lambda j: (j, 0, 0))], tn_dot)
    (gr["wpg"],) = _tiles(
        "w_ple_gate_dw", (d // tw, d // tw), [sv["y3"], dpg], [whole(tw), whole_j(tw)],
        [_sds((d, d), F32)], [pl.BlockSpec((tw, tw), lambda i, j: (i, j))], tn_dot)
    gr["wpg"] = gr["wpg"].reshape(nb, d // nb, d)
    tm = _tile(s, 256)

    def ple_dx(a, w, hv, gv, dv):
        dh3, dg = _rms_back(hv[...], gv[...], _dot(a[...], w[...], "nt"), dv[...])
        return dh3, dh3, dg

    dh3, dh3_16, sg["norm_ple_g"] = _tiles(
        "w_ple_gate_dx", (s // tm,),
        [dpg, gw["wpg"], sv["h3"], _after(sm["norm_ple_g"], start_token), dh],
        [rows(tm, d), _resident((d, d)), rows(tm, d), vec, rows(tm, d)],
        [_sds((s, d), F32), _sds((s, d), BF16), _sds((1, d), F32)], [rows(tm, d), rows(tm, d), vec],
        ple_dx, summed=(2,))

    tm = _tile(s, 1024)
    blk_rows = pl.BlockSpec((None, tm, f8), lambda i, j: (j, i, 0))
    dgate, dup = _tiles(
        "w_down_dx", (s // tm, nb),
        [dh3_16, gw["wd"], sv["gate_pre"], sv["up_pre"]],
        [pl.BlockSpec((tm, d), lambda i, j: (i, 0)), pl.BlockSpec((None, f8, d), lambda i, j: (j, 0, 0)), blk_rows, blk_rows],
        [_sds((nb, s, f8), BF16)] * 2, [blk_rows] * 2,
        lambda a, w, g, u: _swiglu_bwd(_dot(a[...], w[...], "nt"), g[...].astype(F32), u[...].astype(F32)))
    (gr["wd"],) = _tiles(
        "w_down_dw", (nb, d // tw),
        [sv["hmid"], dh3_16], [pl.BlockSpec((None, s, f8), lambda i, j: (i, 0, 0)), whole_j(tw)],
        [_sds((nb, f8, d), F32)], [pl.BlockSpec((None, f8, tw), lambda i, j: (i, 0, j))], tn_dot)
    mid_token = run("after_down_dx", dgate)
    for nm, dact in (("wg", dgate), ("wu", dup)):
        (gr[nm],) = _tiles(
            "w_" + {"wg": "gate", "wu": "up"}[nm] + "_dw", (d // tw, nb),
            [dact, sv["y2"]], [pl.BlockSpec((None, s, f8), lambda i, j: (j, 0, 0)), whole(tw)],
            [_sds((nb, f8, d), F32)], [pl.BlockSpec((None, f8, tw), lambda i, j: (j, 0, i))], tn_dot)
    ffn_token = run("after_ffn_grads", gr, gr["wu"])
    tm, tn = _tile(s, 512), _tile(d, 1024)
    act_spec = pl.BlockSpec((nb, tm, f8), lambda j, i: (0, i, 0))
    wt_spec = pl.BlockSpec((nb, f8, tn), lambda j, i: (0, 0, j))
    out_spec = pl.BlockSpec((tm, tn), lambda j, i: (i, j))
    (dy2,) = _tiles(
        "w_gate_dx", (d // tn, s // tm), [dgate, gw["wg"]], [act_spec, wt_spec],
        [_sds((s, d), F32)], [out_spec],
        lambda a, w: (_chain([(a[k], w[k]) for k in range(nb)], "nn"),),
        deps=[] if ffn_token is None else [ffn_token])
    (dy2,) = _tiles(
        "w_up_dx", (d // tn, s // tm), [dup, gw["wu"], dy2], [act_spec, wt_spec, out_spec],
        [_sds((s, d), F32)], [out_spec],
        lambda a, w, prev: (prev[...] + _chain([(a[k], w[k]) for k in range(nb)], "nn"),))
    g_ffn = _after(_after(sm["norm_ffn_g"], mid_token), run("before_ffn_norm", dy2))
    dh2, dh2_16, sg["norm_ffn_g"] = _rms_bwd("rms_ffn_bwd", sv["h2"], g_ffn, dy2, dh3)

    tm = _tile(s, 512)
    (dmo,) = _tiles(
        "w_out_dx", (s // tm,), [dh2_16, gw["wout"]], [rows(tm, d), _resident((d, d))],
        [_sds((s, d), BF16)], [rows(tm, d)], lambda a, w: (_dot(a[...], w[...], "nt"),))
    (gr["wout"],) = _tiles(
        "w_out_dw", (d // tw, d // tw), [sv["mo"], dh2_16], [whole(tw), whole_j(tw)],
        [_sds((d, d), F32)], [pl.BlockSpec((tw, tw), lambda i, j: (i, j))], tn_dot)
    gr["wout"] = gr["wout"].reshape(nb, d // nb, d)
    dz, *mix_grads = _mixer_bwd("mixer_bwd", sv["z"], dmo, sm["mixer"])
    sg["mixer"] = mix_grads
    (gr["win"],) = _tiles(
        "w_in_dw", (d // tw, nb), [sv["y1"], dz], [whole(tw), pl.BlockSpec((s, GROUP), lambda i, j: (0, j))],
        [_sds((nb, d, GROUP), F32)], [pl.BlockSpec((None, tw, GROUP), lambda i, j: (j, i, 0))], tn_dot)
    tm = _tile(s, 256)

    def in_dx(a, w, hv, gv, dv):
        dy1 = _chain([(a[:, k * GROUP:(k + 1) * GROUP], w[k]) for k in range(nb)], "nt")
        return _rms_back(hv[...], gv[...], dy1, dv[...])

    dh_in, sg["norm_mix_g"] = _tiles(
        "w_in_dx", (s // tm,),
        [dz, gw["win"], sv["h"], sm["norm_mix_g"], dh2],
        [rows(tm, nb * GROUP), _resident((nb, d, GROUP)), rows(tm, d), vec, rows(tm, d)],
        [_sds((s, d), F32), _sds((1, d), F32)], [rows(tm, d), vec], in_dx, summed=(1,))
    return dh_in, gr, sg


def _place():
    return lax.axis_index("x"), lax.axis_index("y"), lax.axis_index("c")


_HBM = pl.BlockSpec(memory_space=pltpu.HBM)
_SEM = pl.BlockSpec(memory_space=pltpu.SEMAPHORE)


def _split_call(name, bufs, old_sems, n_new, after, body, want_token):
    nb, no = len(bufs), len(old_sems)
    extra = [] if after is None else [after]

    def kbody(*refs):
        new = refs[nb + no + len(extra):nb + no + len(extra) + n_new]
        body(refs[:nb], refs[nb:nb + no], new)
        if want_token:
            refs[-1][...] = jnp.zeros_like(refs[-1])

    outs = pl.pallas_call(
        kbody, name=name,
        out_shape=tuple([pltpu.SemaphoreType.DMA(())] * n_new + [pltpu.HBM(b.shape, b.dtype) for b in bufs]
                        + ([_sds((8, 128), F32)] if want_token else [])),
        in_specs=[_HBM] * nb + [_SEM] * no + [ANY] * len(extra),
        out_specs=tuple([_SEM] * n_new + [_HBM] * nb
                        + ([pl.BlockSpec(memory_space=pltpu.VMEM)] if want_token else [])),
        input_output_aliases={i: n_new + i for i in range(nb)},
        compiler_params=pltpu.CompilerParams(has_side_effects=pltpu.SideEffectType.DATAFLOW_SIDE_EFFECTING),
    )(*[pltpu.with_memory_space_constraint(b, pltpu.HBM) for b in bufs], *old_sems, *extra)
    return list(outs[:n_new]), list(outs[n_new:n_new + nb]), (outs[-1] if want_token else None)


def _remote(ref_src, ref_dst, send_sem, recv_sem, to):
    return pltpu.make_async_remote_copy(src_ref=ref_src, dst_ref=ref_dst, send_sem=send_sem, recv_sem=recv_sem,
                                        device_id=to, device_id_type=MESH)


def _place_shard(name, w):
    n_layers, r, cdim = w.shape
    tr = _row_tile(r, 256)
    nt = r // tr

    def body(w_ref, *rest):
        outs, buf, sem = rest[:n_layers], rest[n_layers], rest[n_layers + 1]
        i = pl.program_id(0)
        x, y, c = _place()
        slot = i % 2

        def writes(step, sl):
            rows = pl.ds(pl.multiple_of(step * tr, 16), tr)
            return [pltpu.make_async_copy(buf.at[sl, l], outs[l].at[4 * x + 2 * y + c, rows, :], sem.at[sl, l])
                    for l in range(n_layers)]

        @pl.when(i >= 2)
        def _():
            for cp in writes(i - 2, slot):
                cp.wait()

        buf[slot] = w_ref[...].astype(BF16)
        for cp in writes(i, slot):
            cp.start()

        @pl.when(i == nt - 1)
        def _():
            for cp in writes(i, slot):
                cp.wait()
            if nt >= 2:
                for cp in writes(i - 1, 1 - slot):
                    cp.wait()

    return pl.pallas_call(
        body, name=name, grid=(nt,),
        in_specs=[pl.BlockSpec((n_layers, tr, cdim), lambda i: (0, i, 0))], out_specs=[ANY] * n_layers,
        out_shape=[_sds((N_DEV, r, cdim), BF16)] * n_layers,
        scratch_shapes=[pltpu.VMEM((2, n_layers, tr, cdim), BF16), pltpu.SemaphoreType.DMA((2, n_layers))],
        compiler_params=_params("arbitrary"),
    )(w)


def _gather_start(li, lands, after):
    nw = len(lands)

    def body(bufs, _, new):
        x, y, c = _place()
        chips = [(1 - x, y), (x, 1 - y), (1 - x, 1 - y)]
        for w in range(nw):
            mine = bufs[w].at[4 * x + 2 * y + c]
            _remote(mine, mine, new[8 * w], new[8 * w + 4], (x, y, 1 - c)).start()
            for j, (px, py) in enumerate(chips):
                _remote(mine, mine, new[8 * w + 1 + j], new[8 * w + 5 + j], (px, py, c)).start()

    return _split_call("gather_start_l%s" % li, lands, [], 8 * nw, after, body, True)


def _gather_forward(li, lands, sems, after):
    nw = len(lands)
    arrivals = [sems[8 * w + 5 + j] for w in range(nw) for j in range(3)]

    def body(bufs, old, new):
        x, y, c = _place()
        chips = [(1 - x, y), (x, 1 - y), (1 - x, 1 - y)]
        for j, (px, py) in enumerate(chips):
            for w in range(nw):
                got = bufs[w].at[4 * px + 2 * py + c]
                _remote(got, got, new[6 * w + j], old[3 * w + j], (px, py, c)).wait_recv()
                _remote(got, got, new[6 * w + j], new[6 * w + 3 + j], (x, y, 1 - c)).start()

    return _split_call("gather_forward_l%s" % li, lands, arrivals, 6 * nw, after, body, True)


def _gather_wait(li, lands, sems, fwd_sems, after):
    nw = len(lands)
    first = [sems[8 * w + k] for w in range(nw) for k in range(5)]

    def body(bufs, old, _):
        x, y, c = _place()
        sib = (x, y, 1 - c)
        chips = [(1 - x, y), (x, 1 - y), (1 - x, 1 - y)]
        n1 = 5 * nw
        for w in range(nw):
            mine = bufs[w].at[4 * x + 2 * y + c]
            theirs = bufs[w].at[4 * x + 2 * y + 1 - c]
            _remote(theirs, theirs, old[5 * w], old[5 * w + 4], sib).wait_recv()
            for k in range(4):
                _remote(mine, mine, old[5 * w + k], old[5 * w + 4], sib).wait_send()
            for j, (px, py) in enumerate(chips):
                sent = bufs[w].at[4 * px + 2 * py + c]
                got = bufs[w].at[4 * px + 2 * py + 1 - c]
                _remote(sent, sent, old[n1 + 6 * w + j], old[n1 + 6 * w + 3 + j], sib).wait_send()
                _remote(got, got, old[n1 + 6 * w + j], old[n1 + 6 * w + 3 + j], sib).wait_recv()

    _, lands, _ = _split_call("gather_wait_l%s" % li, lands, first + list(fwd_sems), 0, after, body, False)
    return lands


def _siblings_start(li, grads, after):
    nw = len(grads)
    lands = [lax.empty((4,) + g.shape[1:], g.dtype) for g in grads]

    def body(bufs, _, new):
        x, y, c = _place()
        for w in range(nw):
            for q in range(4):
                _remote(bufs[w].at[2 * q + (1 - c)], bufs[nw + w].at[q], new[8 * w + q], new[8 * w + 4 + q],
                        (x, y, 1 - c)).start()

    return _split_call("rs_siblings_start_l%s" % li, list(grads) + lands, [], 8 * nw, after, body, True)


def _siblings_wait(li, bufs, sems, after):
    nw = len(bufs) // 2

    def body(refs, old, _):
        x, y, c = _place()
        for w in range(nw):
            for q in range(4):
                _remote(refs[w].at[2 * q + (1 - c)], refs[nw + w].at[q], old[8 * w + q], old[8 * w + 4 + q],
                        (x, y, 1 - c)).wait()

    _, bufs, _ = _split_call("rs_siblings_wait_l%s" % li, bufs, sems, 0, after, body, False)
    return bufs[:nw], bufs[nw:]


_FLIPS = ((1, 0), (0, 1), (1, 1))


def _chips_copies(refs, nw, sems):
    x, y, c = _place()
    for w in range(nw):
        for r, (fx, fy) in enumerate(_FLIPS):
            px = 1 - x if fx else x
            py = 1 - y if fy else y
            yield _remote(refs[w].at[2 * px + py], refs[nw + w].at[r], sems[6 * w + r], sems[6 * w + 3 + r], (px, py, c))


def _chips_start(li, partials, after):
    nw = len(partials)
    lands = [lax.empty((3,) + a.shape[1:], a.dtype) for a in partials]

    def body(bufs, _, new):
        for cp in _chips_copies(bufs, nw, new):
            cp.start()

    return _split_call("rs_chips_start_l%s" % li, list(partials) + lands, [], 6 * nw, after, body, True)


def _chips_wait(li, bufs, sems, after):
    nw = len(bufs) // 2

    def body(refs, old, _):
        for cp in _chips_copies(refs, nw, old):
            cp.wait()

    _, bufs, _ = _split_call("rs_chips_wait_l%s" % li, bufs, sems, 0, after, body, False)
    return bufs[nw:]


def _row_tile(n, want):
    best = None
    for t in range(16, min(n, want) + 1, 16):
        if n % t == 0:
            best = t
    assert best is not None, n
    return best


def _chip_partials(name, grad, from_sibling):
    _, r, cdim = grad.shape
    tr = _row_tile(r, 256)
    nt = r // tr
    steps = 4 * nt

    def body(g_hbm, s_ref, pb_ref, own_ref, buf, sem):
        i, q = pl.program_id(0), pl.program_id(1)
        x, y, c = _place()
        n = 4 * i + q
        slot = n % 2

        def fetch(step, into):
            rows = pl.ds(pl.multiple_of((step // 4) * tr, 16), tr)
            return pltpu.make_async_copy(g_hbm.at[2 * (step % 4) + c, rows, :], buf.at[into], sem.at[into])

        @pl.when(n == 0)
        def _():
            fetch(0, 0).start()

        @pl.when(n + 1 < steps)
        def _():
            fetch(n + 1, 1 - slot).start()

        fetch(n, slot).wait()
        tot = buf[slot] + s_ref[...]
        pb_ref[...] = tot.astype(BF16)

        @pl.when(q == 2 * x + y)
        def _():
            own_ref[...] = tot

    return pl.pallas_call(
        body, name=name, grid=(nt, 4),
        in_specs=[ANY, pl.BlockSpec((None, tr, cdim), lambda i, q: (q, i, 0))],
        out_specs=[pl.BlockSpec((None, tr, cdim), lambda i, q: (q, i, 0)), pl.BlockSpec((tr, cdim), lambda i, q: (i, 0))],
        out_shape=[_sds((4, r, cdim), BF16), _sds((r, cdim), F32)],
        scratch_shapes=[pltpu.VMEM((2, tr, cdim), F32), pltpu.SemaphoreType.DMA((2,))],
        compiler_params=_params("arbitrary", "arbitrary"),
    )(grad, from_sibling)


def _adamw(w, g, m, v):
    m = ADAM_B1 * m + (1.0 - ADAM_B1) * g
    v = ADAM_B2 * v + (1.0 - ADAM_B2) * (g * g)
    m_hat = m / (1.0 - ADAM_B1 ** ADAM_STEP)
    v_hat = v / (1.0 - ADAM_B2 ** ADAM_STEP)
    delta = -ADAM_LR * (m_hat / (jnp.sqrt(v_hat) + ADAM_EPS) + ADAM_WD * w)
    return delta, m, v


def _finish_weight(name, li, own, from_chips, w, m, v, stacked):
    r, cdim = own.shape
    tr = _row_tile(r, 256)
    if stacked is None:
        stacked = [lax.empty(w.shape, F32) for _ in range(4)]

    def body(own_ref, fc_ref, w_ref, m_ref, v_ref, *rest):
        g_out, d_out, m_out, v_out = rest[4:]
        g = own_ref[...] + fc_ref[0].astype(F32) + fc_ref[1].astype(F32) + fc_ref[2].astype(F32)
        delta, mn, vn = _adamw(w_ref[...], g, m_ref[...], v_ref[...])
        g_out[...] = g
        d_out[...] = delta
        m_out[...] = mn
        v_out[...] = vn

    tile = pl.BlockSpec((tr, cdim), lambda i: (i, 0))
    lay = pl.BlockSpec((None, tr, cdim), lambda i: (li, i, 0))
    return pl.pallas_call(
        body, name=name, grid=(r // tr,),
        in_specs=[tile, pl.BlockSpec((3, tr, cdim), lambda i: (0, i, 0)), lay, lay, lay] + [ANY] * 4,
        out_specs=[lay] * 4, out_shape=[_sds(w.shape, F32)] * 4,
        input_output_aliases={5: 0, 6: 1, 7: 2, 8: 3},
        compiler_params=_params("parallel"),
    )(own, from_chips, w, m, v, *stacked)


def _allgather_small(name, v, reduce):
    r = v.shape[0]

    def body(x_ref, out_ref, *rest):
        if reduce:
            sum_ref, send_sems, recv_sems, local_sem = rest
        else:
            send_sems, recv_sems, local_sem = rest
        x, y, c = _place()
        me, sib = (x, y, c), (x, y, 1 - c)
        chips = [(1 - x, y), (x, 1 - y), (1 - x, 1 - y)]

        def rows(px, py, pc):
            return out_ref.at[pl.ds(pl.multiple_of((4 * px + 2 * py + pc) * r, 8), r), :]

        def copy(k, block, to, src=None):
            return pltpu.make_async_remote_copy(
                src_ref=rows(*block) if src is None else src, dst_ref=rows(*block),
                send_sem=send_sems.at[k], recv_sem=recv_sems.at[k], device_id=to, device_id_type=MESH)

        mine = pltpu.make_async_copy(x_ref, rows(*me), local_sem)
        mine.start()
        first = [copy(0, me, sib, src=x_ref)]
        first += [copy(1 + j, me, (*chip, c), src=x_ref) for j, chip in enumerate(chips)]
        for cp in first:
            cp.start()
        passed = [copy(4 + j, (*chip, c), sib) for j, chip in enumerate(chips)]
        for j, chip in enumerate(chips):
            copy(1 + j, (*chip, c), me).wait_recv()
            passed[j].start()
        copy(0, sib, me).wait_recv()
        for j, chip in enumerate(chips):
            copy(4 + j, (*chip, 1 - c), me).wait_recv()
        for cp in first + passed:
            cp.wait_send()
        mine.wait()
        if reduce:
            tot = out_ref[0:r, :]
            for d in range(1, N_DEV):
                tot = tot + out_ref[d * r:(d + 1) * r, :]
            sum_ref[...] = tot

    vm = pl.BlockSpec(memory_space=pltpu.VMEM)
    outs = [_sds((N_DEV * r, 128), F32)] + ([_sds((r, 128), F32)] if reduce else [])
    res = pl.pallas_call(
        body, name=name,
        in_specs=[vm], out_specs=[vm] * len(outs), out_shape=outs,
        scratch_shapes=[pltpu.SemaphoreType.DMA((7,)), pltpu.SemaphoreType.DMA((7,)), pltpu.SemaphoreType.DMA],
        compiler_params=pltpu.CompilerParams(vmem_limit_bytes=VMEM_LIMIT_BYTES),
    )(v)
    return res


def _adamw_small(name, w, g, m, v):
    def body(w_ref, g_ref, m_ref, v_ref, d_out, m_out, v_out):
        delta, mn, vn = _adamw(w_ref[...], g_ref[...], m_ref[...], v_ref[...])
        d_out[...] = delta
        m_out[...] = mn
        v_out[...] = vn

    vm = pl.BlockSpec(memory_space=pltpu.VMEM)
    return pl.pallas_call(
        body, name=name, in_specs=[vm] * 4, out_specs=[vm] * 3, out_shape=[_sds(w.shape, F32)] * 3,
        compiler_params=pltpu.CompilerParams(vmem_limit_bytes=VMEM_LIMIT_BYTES),
    )(w, g, m, v)


def _pack(arrays):
    flat, layout, off = [], [], 0
    for a in arrays:
        flat.append(a.reshape(-1).astype(F32))
        layout.append((off, a.shape))
        off += a.size
    total = -(-off // 1024) * 1024
    if total > off:
        flat.append(jnp.zeros((total - off,), F32))
    return jnp.concatenate(flat).reshape(total // 128, 128), layout


def _unpack(packed, layout):
    flat = packed.reshape(-1)
    return [flat[off:off + math.prod(shape)].reshape(shape) for off, shape in layout]


_BIG = ("win", "wout", "wg", "wu", "wd", "wpg", "wpp")
_BIG_FULL = {"win": "w_in", "wout": "w_out", "wg": "w_gate", "wu": "w_up", "wd": "w_down",
             "wpg": "w_ple_gate", "wpp": "w_ple_proj"}
_SMALL_REPLICATED = ("norm_mix_g", "sgu_ln_g", "sgu_ln_b", "sgu_w", "sgu_b", "cf_conv_b", "cf_ln_g", "cf_ln_b",
                     "pool_w", "pool_scale", "norm_ffn_g", "norm_ple_g", "final_norm_g")
_SMALL_SHARDED = ("sc_conv_w", "cf_conv_w")
_WEIGHTS = ("norm_mix_g", "w_in", "sgu_ln_g", "sgu_ln_b", "sgu_w", "sgu_b", "sc_conv_w", "cf_conv_w", "cf_conv_b",
            "cf_ln_g", "cf_ln_b", "pool_w", "pool_scale", "w_out", "norm_ffn_g", "w_gate", "w_up", "w_down",
            "norm_ple_g", "w_ple_gate", "w_ple_proj", "final_norm_g")


def _pad_rows(a, rows):
    return jnp.concatenate([a, jnp.zeros((rows - a.shape[0],) + a.shape[1:], a.dtype)], axis=0)


def _mixer_params(li, W, sc_full, cf_full):
    return [W["sgu_ln_g"][li][:, None, :], W["sgu_ln_b"][li][:, None, :], W["sgu_w"][li], W["sgu_b"][li][:, :, None],
            _pad_rows(sc_full[li], 8), _pad_rows(cf_full[li], 32),
            W["cf_conv_b"][li][None, :], W["cf_ln_g"][li][None, :], W["cf_ln_b"][li][None, :],
            W["pool_w"][li], W["pool_scale"][li][None, :]]


def _step(W, M, V, x, p, loss_target):
    n_layers = W["w_in"].shape[0]
    h = x[0]
    target = loss_target[0]
    p_all = p[:, 0]
    xi, yi, ci = _place()
    blk = 4 * xi + 2 * yi + ci
    csh = W["sc_conv_w"].shape[2]
    turned = ("w_gate", "w_up")
    W, M, V = ({n: (jnp.swapaxes(a, 1, 2) if n in turned else a) for n, a in t.items()} for t in (W, M, V))

    packed, lay = _pack([W[n] for n in _SMALL_SHARDED])
    (taps,) = _allgather_small("gather_conv_taps", packed, reduce=False)
    per_dev = [_unpack(taps[d * packed.shape[0]:(d + 1) * packed.shape[0]], lay) for d in range(N_DEV)]
    sc_full = jnp.concatenate([pd[0] for pd in per_dev], axis=-1)
    cf_full = jnp.concatenate([pd[1] for pd in per_dev], axis=-1)

    d = h.shape[1]
    placed = [_place_shard("place_" + n, W[_BIG_FULL[n]]) for n in _BIG]
    lands = [[placed[w][li] for w in range(len(_BIG))] for li in range(n_layers)]

    groups = (("win",), ("wout", "wg", "wu"), ("wd", "wpg", "wpp"))
    order = [(li, gi) for li in range(n_layers) for gi in range(len(groups))]
    moving, tok = [], taps
    for li in range(n_layers):
        sems, bufs, tok = _gather_start(li, lands[li], tok)
        moving.append({"sems": sems, "bufs": bufs, "got": {}})
    passed = {}

    def pass_on(item, after):
        if item not in passed:
            li, gi = item
            idx = [_BIG.index(n) for n in groups[gi]]
            sems = [s_ for i in idx for s_ in moving[li]["sems"][8 * i:8 * i + 8]]
            tag = "%d_%s" % (li, groups[gi][0])
            passed[item] = (tag, sems) + tuple(_gather_forward(tag, [moving[li]["bufs"][i] for i in idx], sems, after))

    def fetcher(li):
        def fetch(name, after):
            got = moving[li]["got"]
            if name not in got:
                item = (li, next(gi for gi, g_ in enumerate(groups) if name in g_))
                pass_on(item, after)
                ahead = order.index(item) + 1
                if ahead < len(order) and order[ahead][0] >= 1:
                    pass_on(order[ahead], after)
                tag, sems, fwd_sems, bufs, _ = passed[item]
                for n, a in zip(groups[item[1]], _gather_wait(tag, bufs, sems, fwd_sems, after)):
                    got[n] = a.reshape(d, d) if n in ("wout", "wpg") else a
            return got[name]
        return fetch

    saved, gathered, smalls = [], [], []
    for li in range(n_layers):
        sm = {"norm_mix_g": _after(W["norm_mix_g"][li][None, :], tok if li == 0 else None),
              "norm_ffn_g": W["norm_ffn_g"][li][None, :],
              "norm_ple_g": W["norm_ple_g"][li][None, :], "mixer": _mixer_params(li, W, sc_full, cf_full)}
        h, sv = _layer_fwd(li, h, p_all, fetcher(li), sm)
        saved.append(sv)
        gathered.append(moving[li]["got"])
        smalls.append(sm)
    loss, dh, d_final_g = _loss_head(h, W["final_norm_g"][None, :], target)

    big_out = {n: None for n in _BIG}
    small_grads = []

    def scatter_begin(li, tag, names, gr, after):
        sems, bufs, tok = _siblings_start(tag, [gr[n] for n in names], after)
        return {"li": li, "tag": tag, "names": names, "sems": sems, "bufs": bufs, "tok": tok}

    def scatter_middle(st, after):
        grads, from_sib = _siblings_wait(st["tag"], st["bufs"], st["sems"], after)
        parts = [_chip_partials("rs_sum_" + n, g_, s_) for n, g_, s_ in zip(st["names"], grads, from_sib)]
        st["own"] = [own for _, own in parts]
        st["sems"], st["bufs"], tok = _chips_start(st["tag"], [pb for pb, _ in parts], None)
        return tok

    def scatter_end(st, after):
        from_chips = _chips_wait(st["tag"], st["bufs"], st["sems"], after)
        for n, own, fc in zip(st["names"], st["own"], from_chips):
            full = _BIG_FULL[n]
            big_out[n] = _finish_weight("adamw_" + n, st["li"], own, fc, W[full], M[full], V[full], big_out[n])
        return big_out[st["names"][0]][0]

    early, late = ("wpp", "wpg", "wd", "wg", "wu"), ("wout", "win")
    pending, tok = None, None
    for li in reversed(range(n_layers)):
        hooks, first = {}, {}
        if pending is not None:
            hooks["after_down_dx"] = functools.partial(scatter_middle, pending)
        if li == 0:
            def begin_early(gr, after, first=first):
                first.update(scatter_begin(0, "0_early", early, gr, after))
                return first["tok"]

            hooks["after_ffn_grads"] = begin_early
            hooks["before_ffn_norm"] = lambda after, first=first: scatter_middle(first, after)
        dh, gr, sg = _layer_bwd(li, dh, p_all, gathered[li], smalls[li], saved[li], tok, hooks)
        small_grads.append(sg)
        done = None if pending is None else scatter_end(pending, dh)
        if li > 0:
            pending = scatter_begin(li, str(li), _BIG, gr, done)
            tok = pending["tok"]
        else:
            last = scatter_begin(0, "0_late", late, gr, done)
            scatter_middle(last, None)
            scatter_end(first, last["own"][0])
            scatter_end(last, None)
    small_grads.reverse()

    def stacked(fn):
        return jnp.stack([fn(sg) for sg in small_grads])

    mix = lambda i: (lambda sg: sg["mixer"][i])
    grads_small = {
        "norm_mix_g": stacked(lambda sg: sg["norm_mix_g"][0]),
        "sgu_ln_g": stacked(mix(0))[:, :, 0, :], "sgu_ln_b": stacked(mix(1))[:, :, 0, :],
        "sgu_w": stacked(mix(2)), "sgu_b": stacked(mix(3))[:, :, :, 0],
        "sc_conv_w": stacked(mix(4))[:, :SHORT_K], "cf_conv_w": stacked(mix(5))[:, :CONF_K],
        "cf_conv_b": stacked(mix(6))[:, 0], "cf_ln_g": stacked(mix(7))[:, 0], "cf_ln_b": stacked(mix(8))[:, 0],
        "pool_w": stacked(mix(9)), "pool_scale": stacked(mix(10))[:, 0],
        "norm_ffn_g": stacked(lambda sg: sg["norm_ffn_g"][0]), "norm_ple_g": stacked(lambda sg: sg["norm_ple_g"][0]),
        "final_norm_g": d_final_g[0],
    }
    order = _SMALL_REPLICATED + _SMALL_SHARDED
    packed, lay = _pack([grads_small[n] for n in order] + [loss])
    _, summed = _allgather_small("allreduce_small", packed, reduce=True)
    total = dict(zip(order + ("loss",), _unpack(summed, lay)))
    loss_all = total["loss"][0, 0]

    out_g, out_d, out_m, out_v = {}, {}, {}, {}
    pw, lay_r = _pack([W[n] for n in _SMALL_REPLICATED])
    pg, _ = _pack([total[n] for n in _SMALL_REPLICATED])
    pm, _ = _pack([M[n] for n in _SMALL_REPLICATED])
    pv, _ = _pack([V[n] for n in _SMALL_REPLICATED])
    dd, mm, vv = _adamw_small("adamw_small", pw, pg, pm, pv)
    for n, a, b, c_ in zip(_SMALL_REPLICATED, _unpack(dd, lay_r), _unpack(mm, lay_r), _unpack(vv, lay_r)):
        out_g[n], out_d[n], out_m[n], out_v[n] = total[n], a, b, c_
    pick = (jnp.arange(N_DEV) == blk).astype(F32)[None, None, :, None]
    mine = {n: jnp.sum(total[n].reshape(total[n].shape[:2] + (N_DEV, csh)) * pick, axis=2) for n in _SMALL_SHARDED}
    pw, lay_s = _pack([W[n] for n in _SMALL_SHARDED])
    pg, _ = _pack([mine[n] for n in _SMALL_SHARDED])
    pm, _ = _pack([M[n] for n in _SMALL_SHARDED])
    pv, _ = _pack([V[n] for n in _SMALL_SHARDED])
    dd, mm, vv = _adamw_small("adamw_conv_taps", pw, pg, pm, pv)
    for n, a, b, c_ in zip(_SMALL_SHARDED, _unpack(dd, lay_s), _unpack(mm, lay_s), _unpack(vv, lay_s)):
        out_g[n], out_d[n], out_m[n], out_v[n] = mine[n], a, b, c_
    for n in _BIG:
        for k, dst in enumerate((out_g, out_d, out_m, out_v)):
            full = _BIG_FULL[n]
            dst[full] = jnp.swapaxes(big_out[n][k], 1, 2) if full in turned else big_out[n][k]

    return (loss_all, dh[None], *[out_g[n] for n in _WEIGHTS], *[out_d[n] for n in _WEIGHTS],
            *[out_m[n] for n in _WEIGHTS], *[out_v[n] for n in _WEIGHTS])


def kernel(x, p, norm_mix_g, w_in, sgu_ln_g, sgu_ln_b, sgu_w, sgu_b, sc_conv_w, cf_conv_w, cf_conv_b, cf_ln_g, cf_ln_b, pool_w, pool_scale, w_out, norm_ffn_g, w_gate, w_up, w_down, norm_ple_g, w_ple_gate, w_ple_proj, final_norm_g, loss_target, m_norm_mix_g, m_w_in, m_sgu_ln_g, m_sgu_ln_b, m_sgu_w, m_sgu_b, m_sc_conv_w, m_cf_conv_w, m_cf_conv_b, m_cf_ln_g, m_cf_ln_b, m_pool_w, m_pool_scale, m_w_out, m_norm_ffn_g, m_w_gate, m_w_up, m_w_down, m_norm_ple_g, m_w_ple_gate, m_w_ple_proj, m_final_norm_g, v_norm_mix_g, v_w_in, v_sgu_ln_g, v_sgu_ln_b, v_sgu_w, v_sgu_b, v_sc_conv_w, v_cf_conv_w, v_cf_conv_b, v_cf_ln_g, v_cf_ln_b, v_pool_w, v_pool_scale, v_w_out, v_norm_ffn_g, v_w_gate, v_w_up, v_w_down, v_norm_ple_g, v_w_ple_gate, v_w_ple_proj, v_final_norm_g):
    given = dict(locals())
    W = {n: given[n] for n in _WEIGHTS}
    M = {n: given["m_" + n] for n in _WEIGHTS}
    V = {n: given["v_" + n] for n in _WEIGHTS}
    return _step(W, M, V, x, p, loss_target)
```

```python
import functools
import math

import jax
import jax.numpy as jnp
from jax import lax
from jax.experimental import pallas as pl
from jax.experimental.pallas import tpu as pltpu

F32 = jnp.float32
BF16 = jnp.bfloat16
EPS = 1e-6
HEAD = 128
GROUP = 4 * HEAD
HALO = 32
SHORT_K = 3
CONF_K = 31
POOL_WINDOWS = (2, 4, 8, 16)
N_DEV = 8
MESH = pl.DeviceIdType.MESH
VMEM_LIMIT_BYTES = 56 * 1024 * 1024

ADAM_LR = 0.001
ADAM_B1 = 0.9
ADAM_B2 = 0.999
ADAM_EPS = 1e-08
ADAM_WD = 0.01
ADAM_STEP = 10

ANY = pl.BlockSpec(memory_space=pl.ANY)


def _params(*sem):
    return pltpu.CompilerParams(dimension_semantics=sem, vmem_limit_bytes=VMEM_LIMIT_BYTES)


def _rms(x, g):
    return x * lax.rsqrt(jnp.mean(x * x, axis=-1, keepdims=True) + EPS) * g


def _ln(x, g, b):
    mu = jnp.mean(x, axis=-1, keepdims=True)
    xc = x - mu
    var = jnp.mean(xc * xc, axis=-1, keepdims=True)
    return xc * lax.rsqrt(var + EPS) * g + b


def _bdot(a, b, dims=(((1,), (0,)), ((), ()))):
    return lax.dot_general(a.astype(BF16), b.astype(BF16), dims, preferred_element_type=F32)


def _sgu_piece(zu, zv, lg, lb, w, b):
    u = jax.nn.gelu(zu)
    v = _ln(jax.nn.gelu(zv), lg, lb)
    row = lax.broadcasted_iota(jnp.int32, w.shape, 0)
    col = lax.broadcasted_iota(jnp.int32, w.shape, 1)
    wm = jnp.where(row >= col, w, 0.0)
    return u * (_bdot(wm, v) + b)


def _conf_post(c, g, b):
    return jax.nn.silu(_ln(c, g, b))


def _pool_count(first_pos, rows, w):
    pos = first_pos + lax.broadcasted_iota(jnp.int32, (rows, 1), 0) + 1
    return jnp.minimum(pos, w).astype(F32)


_DIMS = {
    "nn": (((1,), (0,)), ((), ())),
    "nt": (((1,), (1,)), ((), ())),
    "tn": (((0,), (0,)), ((), ())),
}


def _tiles(name, grid, ins, in_specs, outs, out_specs, compute, summed=(), deps=()):
    ni = len(ins)
    nd = len(deps)

    def body(*refs):
        vals = compute(*refs[:ni])
        first = functools.reduce(jnp.logical_and, [pl.program_id(a) == 0 for a in range(len(grid))])
        for idx, (r, v) in enumerate(zip(refs[ni + nd:], vals)):
            if idx in summed:
                @pl.when(first)
                def _(r=r):
                    r[...] = jnp.zeros_like(r)

                r[...] += v
            else:
                r[...] = v.astype(r.dtype)

    sem = ("arbitrary" if summed else "parallel",) * len(grid)
    return pl.pallas_call(
        body, name=name, grid=grid, in_specs=list(in_specs) + [ANY] * nd, out_specs=list(out_specs),
        out_shape=list(outs), compiler_params=_params(*sem),
    )(*ins, *deps)


def _resident(shape):
    return pl.BlockSpec(shape, lambda *_: (0,) * len(shape), pipeline_mode=pl.Buffered(1))


def _sds(shape, dtype):
    return jax.ShapeDtypeStruct(tuple(shape), dtype)


def _tile(n, want):
    t = min(n, want)
    assert n % t == 0, (n, want)
    return t


def _rms_fwd(name, h, g):
    s, d = h.shape
    tm = _tile(s, 512)

    def body(h_ref, g_ref, y_ref):
        y_ref[...] = _rms(h_ref[...], g_ref[...]).astype(BF16)

    return pl.pallas_call(
        body, name=name, grid=(s // tm,),
        in_specs=[pl.BlockSpec((tm, d), lambda i: (i, 0)), pl.BlockSpec((1, d), lambda i: (0, 0))],
        out_specs=pl.BlockSpec((tm, d), lambda i: (i, 0)),
        out_shape=_sds((s, d), BF16),
        compiler_params=_params("parallel"),
    )(h, g)


def _rms_back(h, g, dy, dh_in):
    _, vjp = jax.vjp(_rms, h, g)
    dh, dg = vjp(dy)
    return dh_in + dh, dg


def _rms_bwd(name, h, g, dy, dh_in):
    s, d = h.shape
    tm = _tile(s, 256)

    def body(h_ref, g_ref, dy_ref, dhin_ref, dh_ref, dh16_ref, dg_ref):
        dh, dg = _rms_back(h_ref[...], g_ref[...], dy_ref[...].astype(F32), dhin_ref[...])
        dh_ref[...] = dh
        dh16_ref[...] = dh.astype(BF16)

        @pl.when(pl.program_id(0) == 0)
        def _():
            dg_ref[...] = jnp.zeros_like(dg_ref)

        dg_ref[...] += dg

    tok = pl.BlockSpec((tm, d), lambda i: (i, 0))
    vec = pl.BlockSpec((1, d), lambda i: (0, 0))
    return pl.pallas_call(
        body, name=name, grid=(s // tm,),
        in_specs=[tok, vec, tok, tok],
        out_specs=[tok, tok, vec],
        out_shape=[_sds((s, d), F32), _sds((s, d), BF16), _sds((1, d), F32)],
        compiler_params=_params("arbitrary"),
    )(h, g, dy, dh_in)


def _loss_head(h, g, target):
    s, d = h.shape
    tm = _tile(s, 256)

    def body(h_ref, g_ref, t_ref, loss_ref, dh_ref, dg_ref):
        y, vjp = jax.vjp(_rms, h_ref[...], g_ref[...])
        err = y - t_ref[...]
        dh, dg = vjp(err * (1.0 / d))
        dh_ref[...] = dh
        per_token = jnp.mean(err * err, axis=-1, keepdims=True)
        part = 0.5 * jnp.sum(per_token, axis=0, keepdims=True)

        @pl.when(pl.program_id(0) == 0)
        def _():
            dg_ref[...] = jnp.zeros_like(dg_ref)
            loss_ref[...] = jnp.zeros_like(loss_ref)

        dg_ref[...] += dg
        loss_ref[...] += part

    tok = pl.BlockSpec((tm, d), lambda i: (i, 0))
    vec = pl.BlockSpec((1, d), lambda i: (0, 0))
    return pl.pallas_call(
        body, name="loss_head", grid=(s // tm,),
        in_specs=[tok, vec, tok],
        out_specs=[pl.BlockSpec((1, 1), lambda i: (0, 0)), tok, vec],
        out_shape=[_sds((1, 1), F32), _sds((s, d), F32), _sds((1, d), F32)],
        compiler_params=_params("arbitrary"),
    )(h, g, target)


_U, _V, _HB, _BG, _CG, _CA, _CGT, _PD = (GROUP * i for i in range(8))


def _cols(c0, w=GROUP):
    return slice(c0, c0 + w)


_LANE_CHUNKS = tuple(_cols(i * HEAD, HEAD) for i in range(GROUP // HEAD))


def _shifted_copies(sh_ref, ext_ref, length):
    for r in range(1, 8):
        sh_ref[r - 1, 0:length, :] = ext_ref[pl.ds(r, length), :]


def _window(ext_ref, sh_ref, off, rows, cc):
    q, r = divmod(off, 8)
    if sh_ref is None or r == 0:
        return ext_ref[pl.ds(off, rows), cc]
    return sh_ref[r - 1, pl.ds(8 * q, rows), cc]


def _taps(ext_ref, w_ref, nk, start, rows, cc, flip=False, sh_ref=None):
    acc = None
    for k in range(nk):
        kw = nk - 1 - k if flip else k
        term = w_ref[kw:kw + 1, cc] * _window(ext_ref, sh_ref, start + k, rows, cc)
        acc = term if acc is None else acc + term
    return acc


def _mixer_param_specs():
    full = lambda *shape: pl.BlockSpec(shape, lambda i: (0,) * len(shape))
    return [
        full(4, 1, HEAD), full(4, 1, HEAD), full(4, HEAD, HEAD), full(4, HEAD, 1),
        full(8, GROUP), full(32, GROUP), full(1, GROUP), full(1, GROUP), full(1, GROUP),
        full(4, HEAD, HEAD), full(1, GROUP),
    ]


def _mixer_fwd(name, z, prm):
    s = z.shape[0]
    t = _tile(s, 256)
    hb = t // HALO

    def body(zp_ref, zm_ref, lg_ref, lb_ref, sw_ref, sb_ref, scw_ref, cfw_ref, cfb_ref, cg_ref, cb_ref,
             pw_ref, ps_ref, o_ref, ext_ref, sh_ref):
        i = pl.program_id(0)
        keep = (i > 0).astype(F32)
        main = pl.ds(HALO, t)

        for n in range(t // HEAD):
            rows = slice(n * HEAD, (n + 1) * HEAD)
            for hh in range(4):
                cu = _cols(_U + hh * HEAD, HEAD)
                cv = _cols(_V + hh * HEAD, HEAD)
                o_ref[rows, _cols(hh * HEAD, HEAD)] = _sgu_piece(
                    zm_ref[rows, cu], zm_ref[rows, cv], lg_ref[hh], lb_ref[hh], sw_ref[hh], sb_ref[hh]
                ).astype(BF16)

        ext_ref[0:HALO, :] = zp_ref[:, _cols(_CG)] * zp_ref[:, _cols(_HB)] * keep
        ext_ref[main, :] = zm_ref[:, _cols(_CG)] * zm_ref[:, _cols(_HB)]
        y = jnp.zeros((t, GROUP), F32)
        for k in range(SHORT_K):
            y = y + scw_ref[k:k + 1, :] * ext_ref[pl.ds(HALO - (SHORT_K - 1) + k, t), :]
        o_ref[:, _cols(GROUP)] = (zm_ref[:, _cols(_BG)] * y).astype(BF16)

        ext_ref[0:HALO, :] = zp_ref[:, _cols(_CA)] * jax.nn.sigmoid(zp_ref[:, _cols(_CGT)]) * keep
        ext_ref[main, :] = zm_ref[:, _cols(_CA)] * jax.nn.sigmoid(zm_ref[:, _cols(_CGT)])
        _shifted_copies(sh_ref, ext_ref, HALO + t - 8)
        c = jnp.concatenate(
            [cfb_ref[:, cc] + _taps(ext_ref, cfw_ref, CONF_K, HALO - (CONF_K - 1), t, cc, sh_ref=sh_ref)
             for cc in _LANE_CHUNKS], axis=1)
        o_ref[:, _cols(2 * GROUP)] = _conf_post(c, cg_ref[...], cb_ref[...]).astype(BF16)

        ext_ref[0:HALO, :] = zp_ref[:, _cols(_PD)] * keep
        ext_ref[main, :] = zm_ref[:, _cols(_PD)]
        for gi, w in enumerate(POOL_WINDOWS):
            cc = _cols(gi * HEAD, HEAD)
            acc = ext_ref[main, cc]
            for j in range(1, w):
                acc = acc + ext_ref[pl.ds(HALO - j, t), cc]
            q = acc / _pool_count(i * t, t, w) - ext_ref[main, cc]
            o_ref[:, _cols(3 * GROUP + gi * HEAD, HEAD)] = (_bdot(q, pw_ref[gi]) * ps_ref[:, cc]).astype(BF16)

    return pl.pallas_call(
        body, name=name, grid=(s // t,),
        in_specs=[pl.BlockSpec((HALO, 8 * GROUP), lambda i: (jnp.maximum(i * hb - 1, 0), 0)),
                  pl.BlockSpec((t, 8 * GROUP), lambda i: (i, 0)),
                  *_mixer_param_specs()],
        out_specs=pl.BlockSpec((t, 4 * GROUP), lambda i: (i, 0)),
        out_shape=_sds((s, 4 * GROUP), BF16),
        scratch_shapes=[pltpu.VMEM((HALO + t, GROUP), F32), pltpu.VMEM((7, HALO + t - 8, GROUP), F32)],
        compiler_params=_params("parallel"),
    )(z, z, *prm)


def _mixer_bwd(name, z, dmo, prm):
    s = z.shape[0]
    t = _tile(s, 256)
    hb = t // HALO
    nt = s // t
    last_halo = s // HALO - 1

    def body(zp_ref, zm_ref, zn_ref, dm_ref, dn_ref,
             lg_ref, lb_ref, sw_ref, sb_ref, scw_ref, cfw_ref, cfb_ref, cg_ref, cb_ref, pw_ref, ps_ref,
             dz_ref, dlg_ref, dlb_ref, dsw_ref, dsb_ref, dscw_ref, dcfw_ref, dcfb_ref, dcg_ref, dcb_ref,
             dpw_ref, dps_ref, extp_ref, extn_ref, extc_ref, sh_ref):
        i = pl.program_id(0)
        keep_prev = (i > 0).astype(F32)
        keep_next = (i < nt - 1).astype(F32)
        main = pl.ds(HALO, t)

        @pl.when(i == 0)
        def _():
            for r in (dlg_ref, dlb_ref, dsw_ref, dsb_ref, dscw_ref, dcfw_ref, dcfb_ref, dcg_ref, dcb_ref,
                      dpw_ref, dps_ref):
                r[...] = jnp.zeros_like(r)

        def rowsum(v):
            return jnp.sum(v, axis=0, keepdims=True)

        for n in range(t // HEAD):
            rows = slice(n * HEAD, (n + 1) * HEAD)
            for hh in range(4):
                cu = _cols(_U + hh * HEAD, HEAD)
                cv = _cols(_V + hh * HEAD, HEAD)
                _, vjp = jax.vjp(_sgu_piece, zm_ref[rows, cu], zm_ref[rows, cv],
                                 lg_ref[hh], lb_ref[hh], sw_ref[hh], sb_ref[hh])
                dzu, dzv, dlg, dlb, dsw, dsb = vjp(dm_ref[rows, _cols(hh * HEAD, HEAD)].astype(F32))
                dz_ref[rows, cu] = dzu.astype(BF16)
                dz_ref[rows, cv] = dzv.astype(BF16)
                dlg_ref[hh] += dlg
                dlb_ref[hh] += dlb
                dsw_ref[hh] += dsw
                dsb_ref[hh] += dsb

        extp_ref[0:HALO, :] = zp_ref[:, _cols(_CG)] * zp_ref[:, _cols(_HB)] * keep_prev
        extp_ref[main, :] = zm_ref[:, _cols(_CG)] * zm_ref[:, _cols(_HB)]
        dob = dm_ref[:, _cols(GROUP)].astype(F32)
        dy = dob * zm_ref[:, _cols(_BG)]
        extn_ref[0:t, :] = dy
        extn_ref[t:t + HALO, :] = dn_ref[:, _cols(GROUP)].astype(F32) * zn_ref[:, _cols(_BG)] * keep_next
        for cc in _LANE_CHUNKS:
            at = lambda c0: _cols(c0 + cc.start, HEAD)
            y = _taps(extp_ref, scw_ref, SHORT_K, HALO - (SHORT_K - 1), t, cc)
            dx = _taps(extn_ref, scw_ref, SHORT_K, 0, t, cc, flip=True)
            dy_c = extn_ref[0:t, cc]
            for k in range(SHORT_K):
                dscw_ref[k:k + 1, cc] += rowsum(dy_c * extp_ref[pl.ds(HALO - (SHORT_K - 1) + k, t), cc])
            dz_ref[:, at(_BG)] = (dm_ref[:, at(GROUP)].astype(F32) * y).astype(BF16)
            dz_ref[:, at(_CG)] = (dx * zm_ref[:, at(_HB)]).astype(BF16)
            dz_ref[:, at(_HB)] = (dx * zm_ref[:, at(_CG)]).astype(BF16)

        extc_ref[0:HALO, :] = zp_ref[:, _cols(_CA)] * jax.nn.sigmoid(zp_ref[:, _cols(_CGT)]) * keep_prev
        extc_ref[main, :] = zm_ref[:, _cols(_CA)] * jax.nn.sigmoid(zm_ref[:, _cols(_CGT)])
        extc_ref[HALO + t:HALO + t + HALO, :] = zn_ref[:, _cols(_CA)] * jax.nn.sigmoid(zn_ref[:, _cols(_CGT)])
        _shifted_copies(sh_ref, extc_ref, HALO + t + HALO - 8)
        conv = lambda start, rows: jnp.concatenate(
            [cfb_ref[:, cc] + _taps(extc_ref, cfw_ref, CONF_K, start, rows, cc, sh_ref=sh_ref) for cc in _LANE_CHUNKS],
            axis=1)
        c_main = conv(HALO - (CONF_K - 1), t)
        c_next = conv(HALO + t - (CONF_K - 1), HALO)
        _, vjp = jax.vjp(_conf_post, c_main, cg_ref[...], cb_ref[...])
        dc, dcg, dcb = vjp(dm_ref[:, _cols(2 * GROUP)].astype(F32))
        dcg_ref[...] += dcg
        dcb_ref[...] += dcb
        dcfb_ref[...] += rowsum(dc)
        _, vjp_next = jax.vjp(lambda cv: _conf_post(cv, cg_ref[...], cb_ref[...]), c_next)
        (dc_next,) = vjp_next(dn_ref[:, _cols(2 * GROUP)].astype(F32) * keep_next)
        extn_ref[0:t, :] = dc
        extn_ref[t:t + HALO, :] = dc_next
        for cc in _LANE_CHUNKS:
            dc_c = extn_ref[0:t, cc]
            for k in range(CONF_K):
                dcfw_ref[k:k + 1, cc] += rowsum(dc_c * _window(extc_ref, sh_ref, HALO - (CONF_K - 1) + k, t, cc))
        _shifted_copies(sh_ref, extn_ref, t + HALO - 8)
        for cc in _LANE_CHUNKS:
            at = lambda c0: _cols(c0 + cc.start, HEAD)
            dhc = _taps(extn_ref, cfw_ref, CONF_K, 0, t, cc, flip=True, sh_ref=sh_ref)
            sg = jax.nn.sigmoid(zm_ref[:, at(_CGT)])
            dz_ref[:, at(_CA)] = (dhc * sg).astype(BF16)
            dz_ref[:, at(_CGT)] = (dhc * zm_ref[:, at(_CA)] * sg * (1.0 - sg)).astype(BF16)

        extp_ref[0:HALO, :] = zp_ref[:, _cols(_PD)] * keep_prev
        extp_ref[main, :] = zm_ref[:, _cols(_PD)]
        for gi, w in enumerate(POOL_WINDOWS):
            cc = _cols(gi * HEAD, HEAD)
            oc = _cols(3 * GROUP + gi * HEAD, HEAD)
            acc = extp_ref[main, cc]
            for j in range(1, w):
                acc = acc + extp_ref[pl.ds(HALO - j, t), cc]
            count = _pool_count(i * t, t, w)
            q = acc / count - extp_ref[main, cc]
            dod = dm_ref[:, oc].astype(F32)
            dps_ref[:, cc] += rowsum(dod * _bdot(q, pw_ref[gi]))
            ds = dod * ps_ref[:, cc]
            dpw_ref[gi] += _bdot(q, ds, _DIMS["tn"])
            dq = _bdot(ds, pw_ref[gi], _DIMS["nt"])
            ds_next = dn_ref[:, oc].astype(F32) * ps_ref[:, cc] * keep_next
            dq_next = _bdot(ds_next, pw_ref[gi], _DIMS["nt"])
            extn_ref[0:t, cc] = dq / count
            extn_ref[t:t + HALO, cc] = dq_next * (1.0 / w)
            back = extn_ref[0:t, cc]
            for j in range(1, w):
                back = back + extn_ref[pl.ds(j, t), cc]
            dz_ref[:, _cols(_PD + gi * HEAD, HEAD)] = (back - dq).astype(BF16)

    full = lambda *shape: pl.BlockSpec(shape, lambda i: (0,) * len(shape))
    grad_specs = [full(4, 1, HEAD), full(4, 1, HEAD), full(4, HEAD, HEAD), full(4, HEAD, 1),
                  full(8, GROUP), full(32, GROUP), full(1, GROUP), full(1, GROUP), full(1, GROUP),
                  full(4, HEAD, HEAD), full(1, GROUP)]
    grad_shapes = [_sds(sp.block_shape, F32) for sp in grad_specs]
    nxt = lambda i: (jnp.minimum((i + 1) * hb, last_halo), 0)
    return pl.pallas_call(
        body, name=name, grid=(nt,),
        in_specs=[pl.BlockSpec((HALO, 8 * GROUP), lambda i: (jnp.maximum(i * hb - 1, 0), 0)),
                  pl.BlockSpec((t, 8 * GROUP), lambda i: (i, 0)),
                  pl.BlockSpec((HALO, 8 * GROUP), nxt),
                  pl.BlockSpec((t, 4 * GROUP), lambda i: (i, 0)),
                  pl.BlockSpec((HALO, 4 * GROUP), nxt),
                  *_mixer_param_specs()],
        out_specs=[pl.BlockSpec((t, 8 * GROUP), lambda i: (i, 0)), *grad_specs],
        out_shape=[_sds((s, 8 * GROUP), BF16), *grad_shapes],
        scratch_shapes=[pltpu.VMEM((HALO + t, GROUP), F32), pltpu.VMEM((t + HALO, GROUP), F32),
                        pltpu.VMEM((HALO + t + HALO, GROUP), F32), pltpu.VMEM((7, HALO + t + HALO - 8, GROUP), F32)],
        compiler_params=_params("arbitrary"),
    )(z, z, z, dmo, dmo, *prm)


def _dot(a, b, kind="nn"):
    return _bdot(a, b, _DIMS[kind])


def _chain(terms, kind):
    acc = None
    for a, b in terms:
        p = _dot(a, b, kind)
        acc = p if acc is None else acc + p
    return acc


def _after(g, token):
    return g if token is None else g + token[0:1, 0:1]


def _layer_fwd(li, h, p_all, fetch, sm, y1=None, next_mix_g=None):
    s, d = h.shape
    nb = N_DEV
    pd = p_all.shape[2]
    sv = {"h": h}
    rows = lambda t, w: pl.BlockSpec((t, w), lambda i: (i, 0))
    vec = pl.BlockSpec((1, d), lambda i: (0, 0))

    if y1 is None:
        y1 = _rms_fwd("rms_mix", h, sm["norm_mix_g"])
    tm = _tile(s, 1024)
    (z,) = _tiles(
        "w_in_fwd", (s // tm, nb),
        [y1, fetch("win", y1)],
        [pl.BlockSpec((tm, d), lambda i, j: (i, 0)), pl.BlockSpec((None, d, GROUP), lambda i, j: (j, 0, 0))],
        [_sds((s, nb * GROUP), F32)], [pl.BlockSpec((tm, GROUP), lambda i, j: (i, j))],
        lambda a, w: (_dot(a[...], w[...]),))
    mo = _mixer_fwd("mixer_fwd", z, sm["mixer"])
    tm = _tile(s, 512)
    def mix_out(a, w, hv, gv):
        h2v = hv[...] + _dot(a[...], w[...])
        return h2v, _rms(h2v, gv[...])

    h2, y2 = _tiles(
        "w_out_fwd", (s // tm,),
        [mo, fetch("wout", z), h, sm["norm_ffn_g"]], [rows(tm, d), _resident((d, d)), rows(tm, d), vec],
        [_sds((s, d), F32), _sds((s, d), BF16)], [rows(tm, d), rows(tm, d)], mix_out)
    sv.update(y1=y1, z=z, mo=mo, h2=h2)

    gate_pre, up_pre, hmid = _ffn_up("ffn_up_fwd", y2, fetch("wg", z), fetch("wu", z))
    wd = fetch("wd", hmid)
    f8 = wd.shape[1]
    tm, tn = _tile(s, 512), _tile(d, 1024)
    (h3,) = _tiles(
        "w_down_fwd", (d // tn, s // tm),
        [hmid, wd, h2],
        [pl.BlockSpec((nb, tm, f8), lambda j, i: (0, i, 0)), pl.BlockSpec((nb, f8, tn), lambda j, i: (0, 0, j)),
         pl.BlockSpec((tm, tn), lambda j, i: (i, j))],
        [_sds((s, d), F32)], [pl.BlockSpec((tm, tn), lambda j, i: (i, j))],
        lambda a, w, hv: (hv[...] + _chain([(a[k], w[k]) for k in range(nb)], "nn"),))
    sv.update(y2=y2, gate_pre=gate_pre, up_pre=up_pre, hmid=hmid, h3=h3)

    y3 = _rms_fwd("rms_ple", h3, sm["norm_ple_g"])
    tm = _tile(s, 1024)
    (pp,) = _tiles(
        "w_ple_proj_fwd", (s // tm, nb),
        [p_all, fetch("wpp", hmid)],
        [pl.BlockSpec((None, tm, pd), lambda i, j: (li, i, 0)), pl.BlockSpec((None, pd, pd), lambda i, j: (j, 0, 0))],
        [_sds((s, d), BF16)], [pl.BlockSpec((tm, pd), lambda i, j: (i, j))],
        lambda a, w: (_dot(a[...], w[...]),))
    tm = _tile(s, 256)

    def ple(a, w, hv, ppv, *next_g):
        pg = _dot(a[...], w[...])
        h4v = hv[...] + jax.nn.sigmoid(pg) * ppv[...].astype(F32)
        return (h4v, pg) + tuple(_rms(h4v, gv[...]) for gv in next_g)

    more = [] if next_mix_g is None else [next_mix_g]
    h4, pg_pre, *y1_next = _tiles(
        "w_ple_gate_fwd", (s // tm,),
        [y3, fetch("wpg", hmid), h3, pp] + more,
        [rows(tm, d), _resident((d, d)), rows(tm, d), rows(tm, d)] + [vec] * len(more),
        [_sds((s, d), F32), _sds((s, d), BF16)] + [_sds((s, d), BF16)] * len(more),
        [rows(tm, d)] * (2 + len(more)), ple)
    sv.update(y3=y3, pp=pp, pg_pre=pg_pre)
    return h4, sv, (y1_next[0] if y1_next else None)


def _ffn_up(name, y2, wg, wu):
    s, d = y2.shape
    nb, f8, _ = wg.shape
    tm = _tile(s, 1024)

    def compute(y_ref, wg_ref, wu_ref):
        yv = y_ref[...]
        g = _dot(yv, wg_ref[...], "nt")
        u = _dot(yv, wu_ref[...], "nt")
        return g, u, jax.nn.silu(g) * u

    wspec = pl.BlockSpec((None, f8, d), lambda i, j: (j, 0, 0))
    ospec = pl.BlockSpec((None, tm, f8), lambda i, j: (j, i, 0))
    return _tiles(name, (s // tm, nb), [y2, wg, wu], [pl.BlockSpec((tm, d), lambda i, j: (i, 0)), wspec, wspec],
                  [_sds((nb, s, f8), BF16)] * 3, [ospec] * 3, compute)


def _swiglu_bwd(dm, g, u):
    sg = jax.nn.sigmoid(g)
    silu = g * sg
    return dm * u * (sg + silu * (1.0 - sg)), dm * silu


def _layer_bwd(li, dh, p_all, gw, sm, sv, start_token=None, hooks=None):
    hooks = hooks or {}
    run = lambda name, *a: hooks[name](*a) if name in hooks else None
    s, d = dh.shape
    nb, f8, _ = gw["wg"].shape
    pd = p_all.shape[2]
    gr, sg = {}, {}
    rows = lambda t, w: pl.BlockSpec((t, w), lambda i: (i, 0))
    vec = pl.BlockSpec((1, d), lambda i: (0, 0))
    tw = _tile(d, 1024)
    whole = lambda w: pl.BlockSpec((s, w), lambda i, j: (0, i))
    whole_j = lambda w: pl.BlockSpec((s, w), lambda i, j: (0, j))
    tn_dot = lambda a, b: (_dot(a[...], b[...], "tn"),)

    tm = _tile(s, 256)

    def ple_dx(dv, pg, ppv, w, hv, gv):
        dhv = dv[...]
        gate = jax.nn.sigmoid(pg[...].astype(F32))
        dpg = (dhv * ppv[...].astype(F32) * gate * (1.0 - gate)).astype(BF16)
        dh3, dg = _rms_back(hv[...], gv[...], _dot(dpg, w[...], "nt"), dhv)
        return dh3, dh3, dg, dpg, dhv * gate

    dh3, dh3_16, sg["norm_ple_g"], dpg, dpp = _tiles(
        "w_ple_gate_dx", (s // tm,),
        [dh, sv["pg_pre"], sv["pp"], gw["wpg"], sv["h3"], _after(sm["norm_ple_g"], start_token)],
        [rows(tm, d), rows(tm, d), rows(tm, d), _resident((d, d)), rows(tm, d), vec],
        [_sds((s, d), F32), _sds((s, d), BF16), _sds((1, d), F32), _sds((s, d), BF16), _sds((s, d), BF16)],
        [rows(tm, d), rows(tm, d), vec, rows(tm, d), rows(tm, d)], ple_dx, summed=(2,))
    (gr["wpp"],) = _tiles(
        "w_ple_proj_dw", (nb,),
        [p_all, dpp], [pl.BlockSpec((None, s, pd), lambda j: (li, 0, 0)), pl.BlockSpec((s, pd), lambda j: (0, j))],
        [_sds((nb, pd, pd), F32)], [pl.BlockSpec((None, pd, pd), lambda j: (j, 0, 0))], tn_dot)
    (gr["wpg"],) = _tiles(
        "w_ple_gate_dw", (d // tw, d // tw), [sv["y3"], dpg], [whole(tw), whole_j(tw)],
        [_sds((d, d), F32)], [pl.BlockSpec((tw, tw), lambda i, j: (i, j))], tn_dot)
    gr["wpg"] = gr["wpg"].reshape(nb, d // nb, d)

    tm = _tile(s, 1024)
    blk_rows = pl.BlockSpec((None, tm, f8), lambda i, j: (j, i, 0))
    dgate, dup = _tiles(
        "w_down_dx", (s // tm, nb),
        [dh3_16, gw["wd"], sv["gate_pre"], sv["up_pre"]],
        [pl.BlockSpec((tm, d), lambda i, j: (i, 0)), pl.BlockSpec((None, f8, d), lambda i, j: (j, 0, 0)), blk_rows, blk_rows],
        [_sds((nb, s, f8), BF16)] * 2, [blk_rows] * 2,
        lambda a, w, g, u: _swiglu_bwd(_dot(a[...], w[...], "nt"), g[...].astype(F32), u[...].astype(F32)))
    (gr["wd"],) = _tiles(
        "w_down_dw", (nb, d // tw),
        [sv["hmid"], dh3_16], [pl.BlockSpec((None, s, f8), lambda i, j: (i, 0, 0)), whole_j(tw)],
        [_sds((nb, f8, d), F32)], [pl.BlockSpec((None, f8, tw), lambda i, j: (i, 0, j))], tn_dot)
    mid_token = run("after_down_dx", dgate)
    for nm, dact in (("wg", dgate), ("wu", dup)):
        (gr[nm],) = _tiles(
            "w_" + {"wg": "gate", "wu": "up"}[nm] + "_dw", (d // tw, nb),
            [dact, sv["y2"]], [pl.BlockSpec((None, s, f8), lambda i, j: (j, 0, 0)), whole(tw)],
            [_sds((nb, f8, d), F32)], [pl.BlockSpec((None, f8, tw), lambda i, j: (j, 0, i))], tn_dot)
    ffn_token = run("after_ffn_grads", gr, gr["wu"])
    tm, tn = _tile(s, 512), _tile(d, 1024)
    act_spec = pl.BlockSpec((nb, tm, f8), lambda j, i: (0, i, 0))
    wt_spec = pl.BlockSpec((nb, f8, tn), lambda j, i: (0, 0, j))
    out_spec = pl.BlockSpec((tm, tn), lambda j, i: (i, j))
    (dy2,) = _tiles(
        "w_gate_dx", (d // tn, s // tm), [dgate, gw["wg"]], [act_spec, wt_spec],
        [_sds((s, d), F32)], [out_spec],
        lambda a, w: (_chain([(a[k], w[k]) for k in range(nb)], "nn"),),
        deps=[tk for tk in (mid_token, ffn_token) if tk is not None])
    (dy2,) = _tiles(
        "w_up_dx", (d // tn, s // tm), [dup, gw["wu"], dy2], [act_spec, wt_spec, out_spec],
        [_sds((s, d), F32)], [out_spec],
        lambda a, w, prev: (prev[...] + _chain([(a[k], w[k]) for k in range(nb)], "nn"),))
    g_ffn = _after(sm["norm_ffn_g"], run("before_ffn_norm", dy2))
    dh2, dh2_16, sg["norm_ffn_g"] = _rms_bwd("rms_ffn_bwd", sv["h2"], g_ffn, dy2, dh3)

    tm = _tile(s, 512)
    (dmo,) = _tiles(
        "w_out_dx", (s // tm,), [dh2_16, gw["wout"]], [rows(tm, d), _resident((d, d))],
        [_sds((s, d), BF16)], [rows(tm, d)], lambda a, w: (_dot(a[...], w[...], "nt"),))
    (gr["wout"],) = _tiles(
        "w_out_dw", (d // tw, d // tw), [sv["mo"], dh2_16], [whole(tw), whole_j(tw)],
        [_sds((d, d), F32)], [pl.BlockSpec((tw, tw), lambda i, j: (i, j))], tn_dot)
    gr["wout"] = gr["wout"].reshape(nb, d // nb, d)
    dz, *mix_grads = _mixer_bwd("mixer_bwd", sv["z"], dmo, sm["mixer"])
    sg["mixer"] = mix_grads
    (gr["win"],) = _tiles(
        "w_in_dw", (d // tw, nb), [sv["y1"], dz], [whole(tw), pl.BlockSpec((s, GROUP), lambda i, j: (0, j))],
        [_sds((nb, d, GROUP), F32)], [pl.BlockSpec((None, tw, GROUP), lambda i, j: (j, i, 0))], tn_dot)
    tm = _tile(s, 256)

    def in_dx(a, w, hv, gv, dv):
        dy1 = _chain([(a[:, k * GROUP:(k + 1) * GROUP], w[k]) for k in range(nb)], "nt")
        return _rms_back(hv[...], gv[...], dy1, dv[...])

    dh_in, sg["norm_mix_g"] = _tiles(
        "w_in_dx", (s // tm,),
        [dz, gw["win"], sv["h"], sm["norm_mix_g"], dh2],
        [rows(tm, nb * GROUP), _resident((nb, d, GROUP)), rows(tm, d), vec, rows(tm, d)],
        [_sds((s, d), F32), _sds((1, d), F32)], [rows(tm, d), vec], in_dx, summed=(1,))
    return dh_in, gr, sg


def _place():
    return lax.axis_index("x"), lax.axis_index("y"), lax.axis_index("c")


_HBM = pl.BlockSpec(memory_space=pltpu.HBM)
_SEM = pl.BlockSpec(memory_space=pltpu.SEMAPHORE)


def _split_call(name, bufs, old_sems, n_new, after, body, want_token):
    nb, no = len(bufs), len(old_sems)
    extra = [] if after is None else [after]

    def kbody(*refs):
        new = refs[nb + no + len(extra):nb + no + len(extra) + n_new]
        body(refs[:nb], refs[nb:nb + no], new)
        if want_token:
            refs[-1][...] = jnp.zeros_like(refs[-1])

    outs = pl.pallas_call(
        kbody, name=name,
        out_shape=tuple([pltpu.SemaphoreType.DMA(())] * n_new + [pltpu.HBM(b.shape, b.dtype) for b in bufs]
                        + ([_sds((8, 128), F32)] if want_token else [])),
        in_specs=[_HBM] * nb + [_SEM] * no + [ANY] * len(extra),
        out_specs=tuple([_SEM] * n_new + [_HBM] * nb
                        + ([pl.BlockSpec(memory_space=pltpu.VMEM)] if want_token else [])),
        input_output_aliases={i: n_new + i for i in range(nb)},
        compiler_params=pltpu.CompilerParams(has_side_effects=pltpu.SideEffectType.DATAFLOW_SIDE_EFFECTING),
    )(*[pltpu.with_memory_space_constraint(b, pltpu.HBM) for b in bufs], *old_sems, *extra)
    return list(outs[:n_new]), list(outs[n_new:n_new + nb]), (outs[-1] if want_token else None)


def _remote(ref_src, ref_dst, send_sem, recv_sem, to):
    return pltpu.make_async_remote_copy(src_ref=ref_src, dst_ref=ref_dst, send_sem=send_sem, recv_sem=recv_sem,
                                        device_id=to, device_id_type=MESH)


def _place_shard(name, w):
    n_layers, r, cdim = w.shape
    tr = _row_tile(r, 256)
    nt = r // tr

    def body(w_ref, *rest):
        outs, buf, sem = rest[:n_layers], rest[n_layers], rest[n_layers + 1]
        i = pl.program_id(0)
        x, y, c = _place()
        slot = i % 2

        def writes(step, sl):
            rows = pl.ds(pl.multiple_of(step * tr, 16), tr)
            return [pltpu.make_async_copy(buf.at[sl, l], outs[l].at[4 * x + 2 * y + c, rows, :], sem.at[sl, l])
                    for l in range(n_layers)]

        @pl.when(i >= 2)
        def _():
            for cp in writes(i - 2, slot):
                cp.wait()

        buf[slot] = w_ref[...].astype(BF16)
        for cp in writes(i, slot):
            cp.start()

        @pl.when(i == nt - 1)
        def _():
            for cp in writes(i, slot):
                cp.wait()
            if nt >= 2:
                for cp in writes(i - 1, 1 - slot):
                    cp.wait()

    return pl.pallas_call(
        body, name=name, grid=(nt,),
        in_specs=[pl.BlockSpec((n_layers, tr, cdim), lambda i: (0, i, 0))], out_specs=[ANY] * n_layers,
        out_shape=[_sds((N_DEV, r, cdim), BF16)] * n_layers,
        scratch_shapes=[pltpu.VMEM((2, n_layers, tr, cdim), BF16), pltpu.SemaphoreType.DMA((2, n_layers))],
        compiler_params=_params("arbitrary"),
    )(w)


def _gather_start(li, lands, after):
    nw = len(lands)

    def body(bufs, _, new):
        x, y, c = _place()
        chips = [(1 - x, y), (x, 1 - y), (1 - x, 1 - y)]
        for w in range(nw):
            mine = bufs[w].at[4 * x + 2 * y + c]
            _remote(mine, mine, new[8 * w], new[8 * w + 4], (x, y, 1 - c)).start()
            for j, (px, py) in enumerate(chips):
                _remote(mine, mine, new[8 * w + 1 + j], new[8 * w + 5 + j], (px, py, c)).start()

    return _split_call("gather_start_l%s" % li, lands, [], 8 * nw, after, body, True)


def _gather_forward(li, lands, sems, after):
    nw = len(lands)
    arrivals = [sems[8 * w + 5 + j] for w in range(nw) for j in range(3)]

    def body(bufs, old, new):
        x, y, c = _place()
        chips = [(1 - x, y), (x, 1 - y), (1 - x, 1 - y)]
        for j, (px, py) in enumerate(chips):
            for w in range(nw):
                got = bufs[w].at[4 * px + 2 * py + c]
                _remote(got, got, new[6 * w + j], old[3 * w + j], (px, py, c)).wait_recv()
                _remote(got, got, new[6 * w + j], new[6 * w + 3 + j], (x, y, 1 - c)).start()

    return _split_call("gather_forward_l%s" % li, lands, arrivals, 6 * nw, after, body, True)


def _gather_wait(li, lands, sems, fwd_sems, after):
    nw = len(lands)
    first = [sems[8 * w + k] for w in range(nw) for k in range(5)]

    def body(bufs, old, _):
        x, y, c = _place()
        sib = (x, y, 1 - c)
        chips = [(1 - x, y), (x, 1 - y), (1 - x, 1 - y)]
        n1 = 5 * nw
        for w in range(nw):
            mine = bufs[w].at[4 * x + 2 * y + c]
            theirs = bufs[w].at[4 * x + 2 * y + 1 - c]
            _remote(theirs, theirs, old[5 * w], old[5 * w + 4], sib).wait_recv()
            for k in range(4):
                _remote(mine, mine, old[5 * w + k], old[5 * w + 4], sib).wait_send()
            for j, (px, py) in enumerate(chips):
                sent = bufs[w].at[4 * px + 2 * py + c]
                got = bufs[w].at[4 * px + 2 * py + 1 - c]
                _remote(sent, sent, old[n1 + 6 * w + j], old[n1 + 6 * w + 3 + j], sib).wait_send()
                _remote(got, got, old[n1 + 6 * w + j], old[n1 + 6 * w + 3 + j], sib).wait_recv()

    _, lands, _ = _split_call("gather_wait_l%s" % li, lands, first + list(fwd_sems), 0, after, body, False)
    return lands


def _siblings_start(li, grads, after):
    nw = len(grads)
    lands = [lax.empty((4,) + g.shape[1:], g.dtype) for g in grads]

    def body(bufs, _, new):
        x, y, c = _place()
        for w in range(nw):
            for q in range(4):
                _remote(bufs[w].at[2 * q + (1 - c)], bufs[nw + w].at[q], new[8 * w + q], new[8 * w + 4 + q],
                        (x, y, 1 - c)).start()

    return _split_call("rs_siblings_start_l%s" % li, list(grads) + lands, [], 8 * nw, after, body, True)


def _siblings_wait(li, bufs, sems, after):
    nw = len(bufs) // 2

    def body(refs, old, _):
        x, y, c = _place()
        for w in range(nw):
            for q in range(4):
                _remote(refs[w].at[2 * q + (1 - c)], refs[nw + w].at[q], old[8 * w + q], old[8 * w + 4 + q],
                        (x, y, 1 - c)).wait()

    _, bufs, _ = _split_call("rs_siblings_wait_l%s" % li, bufs, sems, 0, after, body, False)
    return bufs[:nw], bufs[nw:]


_FLIPS = ((1, 0), (0, 1), (1, 1))


def _chips_copies(refs, nw, sems):
    x, y, c = _place()
    for w in range(nw):
        for r, (fx, fy) in enumerate(_FLIPS):
            px = 1 - x if fx else x
            py = 1 - y if fy else y
            yield _remote(refs[w].at[2 * px + py], refs[nw + w].at[r], sems[6 * w + r], sems[6 * w + 3 + r], (px, py, c))


def _chips_start(li, partials, after):
    nw = len(partials)
    lands = [lax.empty((3,) + a.shape[1:], a.dtype) for a in partials]

    def body(bufs, _, new):
        for cp in _chips_copies(bufs, nw, new):
            cp.start()

    return _split_call("rs_chips_start_l%s" % li, list(partials) + lands, [], 6 * nw, after, body, True)


def _chips_wait(li, bufs, sems, after):
    nw = len(bufs) // 2

    def body(refs, old, _):
        for cp in _chips_copies(refs, nw, old):
            cp.wait()

    _, bufs, _ = _split_call("rs_chips_wait_l%s" % li, bufs, sems, 0, after, body, False)
    return bufs[nw:]


def _row_tile(n, want):
    best = None
    for t in range(16, min(n, want) + 1, 16):
        if n % t == 0:
            best = t
    assert best is not None, n
    return best


def _chip_partials(name, grad, from_sibling):
    _, r, cdim = grad.shape
    tr = _row_tile(r, 256)
    nt = r // tr
    steps = 4 * nt

    def body(g_hbm, s_ref, pb_ref, own_ref, buf, sem):
        i, q = pl.program_id(0), pl.program_id(1)
        x, y, c = _place()
        n = 4 * i + q
        slot = n % 2

        def fetch(step, into):
            rows = pl.ds(pl.multiple_of((step // 4) * tr, 16), tr)
            return pltpu.make_async_copy(g_hbm.at[2 * (step % 4) + c, rows, :], buf.at[into], sem.at[into])

        @pl.when(n == 0)
        def _():
            fetch(0, 0).start()

        @pl.when(n + 1 < steps)
        def _():
            fetch(n + 1, 1 - slot).start()

        fetch(n, slot).wait()
        tot = buf[slot] + s_ref[...]
        pb_ref[...] = tot.astype(BF16)

        @pl.when(q == 2 * x + y)
        def _():
            own_ref[...] = tot

    return pl.pallas_call(
        body, name=name, grid=(nt, 4),
        in_specs=[ANY, pl.BlockSpec((None, tr, cdim), lambda i, q: (q, i, 0))],
        out_specs=[pl.BlockSpec((None, tr, cdim), lambda i, q: (q, i, 0)), pl.BlockSpec((tr, cdim), lambda i, q: (i, 0))],
        out_shape=[_sds((4, r, cdim), BF16), _sds((r, cdim), F32)],
        scratch_shapes=[pltpu.VMEM((2, tr, cdim), F32), pltpu.SemaphoreType.DMA((2,))],
        compiler_params=_params("arbitrary", "arbitrary"),
    )(grad, from_sibling)


def _adamw(w, g, m, v):
    m = ADAM_B1 * m + (1.0 - ADAM_B1) * g
    v = ADAM_B2 * v + (1.0 - ADAM_B2) * (g * g)
    m_hat = m / (1.0 - ADAM_B1 ** ADAM_STEP)
    v_hat = v / (1.0 - ADAM_B2 ** ADAM_STEP)
    delta = -ADAM_LR * (m_hat / (jnp.sqrt(v_hat) + ADAM_EPS) + ADAM_WD * w)
    return delta, m, v


def _finish_weight(name, li, own, from_chips, w, m, v, stacked):
    r, cdim = own.shape
    tr = _row_tile(r, 256)
    if stacked is None:
        stacked = [lax.empty(w.shape, F32) for _ in range(4)]

    def body(own_ref, fc_ref, w_ref, m_ref, v_ref, *rest):
        g_out, d_out, m_out, v_out = rest[4:]
        g = own_ref[...] + fc_ref[0].astype(F32) + fc_ref[1].astype(F32) + fc_ref[2].astype(F32)
        delta, mn, vn = _adamw(w_ref[...], g, m_ref[...], v_ref[...])
        g_out[...] = g
        d_out[...] = delta
        m_out[...] = mn
        v_out[...] = vn

    tile = pl.BlockSpec((tr, cdim), lambda i: (i, 0))
    lay = pl.BlockSpec((None, tr, cdim), lambda i: (li, i, 0))
    return pl.pallas_call(
        body, name=name, grid=(r // tr,),
        in_specs=[tile, pl.BlockSpec((3, tr, cdim), lambda i: (0, i, 0)), lay, lay, lay] + [ANY] * 4,
        out_specs=[lay] * 4, out_shape=[_sds(w.shape, F32)] * 4,
        input_output_aliases={5: 0, 6: 1, 7: 2, 8: 3},
        compiler_params=_params("parallel"),
    )(own, from_chips, w, m, v, *stacked)


def _allgather_small(name, v, reduce):
    r = v.shape[0]

    def body(x_ref, out_ref, *rest):
        if reduce:
            sum_ref, send_sems, recv_sems, local_sem = rest
        else:
            send_sems, recv_sems, local_sem = rest
        x, y, c = _place()
        me, sib = (x, y, c), (x, y, 1 - c)
        chips = [(1 - x, y), (x, 1 - y), (1 - x, 1 - y)]

        def rows(px, py, pc):
            return out_ref.at[pl.ds(pl.multiple_of((4 * px + 2 * py + pc) * r, 8), r), :]

        def copy(k, block, to, src=None):
            return pltpu.make_async_remote_copy(
                src_ref=rows(*block) if src is None else src, dst_ref=rows(*block),
                send_sem=send_sems.at[k], recv_sem=recv_sems.at[k], device_id=to, device_id_type=MESH)

        mine = pltpu.make_async_copy(x_ref, rows(*me), local_sem)
        mine.start()
        first = [copy(0, me, sib, src=x_ref)]
        first += [copy(1 + j, me, (*chip, c), src=x_ref) for j, chip in enumerate(chips)]
        for cp in first:
            cp.start()
        passed = [copy(4 + j, (*chip, c), sib) for j, chip in enumerate(chips)]
        for j, chip in enumerate(chips):
            copy(1 + j, (*chip, c), me).wait_recv()
            passed[j].start()
        copy(0, sib, me).wait_recv()
        for j, chip in enumerate(chips):
            copy(4 + j, (*chip, 1 - c), me).wait_recv()
        for cp in first + passed:
            cp.wait_send()
        mine.wait()
        if reduce:
            tot = out_ref[0:r, :]
            for d in range(1, N_DEV):
                tot = tot + out_ref[d * r:(d + 1) * r, :]
            sum_ref[...] = tot

    vm = pl.BlockSpec(memory_space=pltpu.VMEM)
    outs = [_sds((N_DEV * r, 128), F32)] + ([_sds((r, 128), F32)] if reduce else [])
    res = pl.pallas_call(
        body, name=name,
        in_specs=[vm], out_specs=[vm] * len(outs), out_shape=outs,
        scratch_shapes=[pltpu.SemaphoreType.DMA((7,)), pltpu.SemaphoreType.DMA((7,)), pltpu.SemaphoreType.DMA],
        compiler_params=pltpu.CompilerParams(vmem_limit_bytes=VMEM_LIMIT_BYTES),
    )(v)
    return res


def _adamw_small(name, w, g, m, v):
    def body(w_ref, g_ref, m_ref, v_ref, d_out, m_out, v_out):
        delta, mn, vn = _adamw(w_ref[...], g_ref[...], m_ref[...], v_ref[...])
        d_out[...] = delta
        m_out[...] = mn
        v_out[...] = vn

    vm = pl.BlockSpec(memory_space=pltpu.VMEM)
    return pl.pallas_call(
        body, name=name, in_specs=[vm] * 4, out_specs=[vm] * 3, out_shape=[_sds(w.shape, F32)] * 3,
        compiler_params=pltpu.CompilerParams(vmem_limit_bytes=VMEM_LIMIT_BYTES),
    )(w, g, m, v)


def _pack(arrays):
    flat, layout, off = [], [], 0
    for a in arrays:
        flat.append(a.reshape(-1).astype(F32))
        layout.append((off, a.shape))
        off += a.size
    total = -(-off // 1024) * 1024
    if total > off:
        flat.append(jnp.zeros((total - off,), F32))
    return jnp.concatenate(flat).reshape(total // 128, 128), layout


def _unpack(packed, layout):
    flat = packed.reshape(-1)
    return [flat[off:off + math.prod(shape)].reshape(shape) for off, shape in layout]


_BIG = ("win", "wout", "wg", "wu", "wd", "wpg", "wpp")
_BIG_FULL = {"win": "w_in", "wout": "w_out", "wg": "w_gate", "wu": "w_up", "wd": "w_down",
             "wpg": "w_ple_gate", "wpp": "w_ple_proj"}
_SMALL_REPLICATED = ("norm_mix_g", "sgu_ln_g", "sgu_ln_b", "sgu_w", "sgu_b", "cf_conv_b", "cf_ln_g", "cf_ln_b",
                     "pool_w", "pool_scale", "norm_ffn_g", "norm_ple_g", "final_norm_g")
_SMALL_SHARDED = ("sc_conv_w", "cf_conv_w")
_WEIGHTS = ("norm_mix_g", "w_in", "sgu_ln_g", "sgu_ln_b", "sgu_w", "sgu_b", "sc_conv_w", "cf_conv_w", "cf_conv_b",
            "cf_ln_g", "cf_ln_b", "pool_w", "pool_scale", "w_out", "norm_ffn_g", "w_gate", "w_up", "w_down",
            "norm_ple_g", "w_ple_gate", "w_ple_proj", "final_norm_g")


def _pad_rows(a, rows):
    return jnp.concatenate([a, jnp.zeros((rows - a.shape[0],) + a.shape[1:], a.dtype)], axis=0)


def _mixer_params(li, W, sc_full, cf_full):
    return [W["sgu_ln_g"][li][:, None, :], W["sgu_ln_b"][li][:, None, :], W["sgu_w"][li], W["sgu_b"][li][:, :, None],
            _pad_rows(sc_full[li], 8), _pad_rows(cf_full[li], 32),
            W["cf_conv_b"][li][None, :], W["cf_ln_g"][li][None, :], W["cf_ln_b"][li][None, :],
            W["pool_w"][li], W["pool_scale"][li][None, :]]


def _step(W, M, V, x, p, loss_target):
    n_layers = W["w_in"].shape[0]
    h = x[0]
    target = loss_target[0]
    p_all = p[:, 0]
    xi, yi, ci = _place()
    blk = 4 * xi + 2 * yi + ci
    csh = W["sc_conv_w"].shape[2]
    turned = ("w_gate", "w_up")
    W, M, V = ({n: (jnp.swapaxes(a, 1, 2) if n in turned else a) for n, a in t.items()} for t in (W, M, V))

    packed, lay = _pack([W[n] for n in _SMALL_SHARDED])
    (taps,) = _allgather_small("gather_conv_taps", packed, reduce=False)
    per_dev = [_unpack(taps[d * packed.shape[0]:(d + 1) * packed.shape[0]], lay) for d in range(N_DEV)]
    sc_full = jnp.concatenate([pd[0] for pd in per_dev], axis=-1)
    cf_full = jnp.concatenate([pd[1] for pd in per_dev], axis=-1)

    d = h.shape[1]
    placed = [_place_shard("place_" + n, W[_BIG_FULL[n]]) for n in _BIG]
    lands = [[placed[w][li] for w in range(len(_BIG))] for li in range(n_layers)]

    groups = (("win",), ("wout", "wg", "wu"), ("wd", "wpg", "wpp"))
    order = [(li, gi) for li in range(n_layers) for gi in range(len(groups))]
    moving, tok = [], taps
    for li in range(n_layers):
        sems, bufs, tok = _gather_start(li, lands[li], tok)
        moving.append({"sems": sems, "bufs": bufs, "got": {}})
    passed = {}

    def pass_on(item, after):
        if item not in passed:
            li, gi = item
            idx = [_BIG.index(n) for n in groups[gi]]
            sems = [s_ for i in idx for s_ in moving[li]["sems"][8 * i:8 * i + 8]]
            tag = "%d_%s" % (li, groups[gi][0])
            passed[item] = (tag, sems) + tuple(_gather_forward(tag, [moving[li]["bufs"][i] for i in idx], sems, after))

    def fetcher(li):
        def fetch(name, after):
            got = moving[li]["got"]
            if name not in got:
                item = (li, next(gi for gi, g_ in enumerate(groups) if name in g_))
                pass_on(item, after)
                ahead = order.index(item) + 1
                if ahead < len(order) and order[ahead][0] >= 1:
                    pass_on(order[ahead], after)
                tag, sems, fwd_sems, bufs, _ = passed[item]
                for n, a in zip(groups[item[1]], _gather_wait(tag, bufs, sems, fwd_sems, after)):
                    got[n] = a.reshape(d, d) if n in ("wout", "wpg") else a
            return got[name]
        return fetch

    saved, gathered, smalls = [], [], []
    y1 = None
    for li in range(n_layers):
        sm = {"norm_mix_g": _after(W["norm_mix_g"][li][None, :], tok if li == 0 else None),
              "norm_ffn_g": W["norm_ffn_g"][li][None, :],
              "norm_ple_g": W["norm_ple_g"][li][None, :], "mixer": _mixer_params(li, W, sc_full, cf_full)}
        next_mix_g = W["norm_mix_g"][li + 1][None, :] if li + 1 < n_layers else None
        h, sv, y1 = _layer_fwd(li, h, p_all, fetcher(li), sm, y1, next_mix_g)
        saved.append(sv)
        gathered.append(moving[li]["got"])
        smalls.append(sm)
    loss, dh, d_final_g = _loss_head(h, W["final_norm_g"][None, :], target)

    big_out = {n: None for n in _BIG}
    small_grads = []

    def scatter_begin(li, tag, names, gr, after):
        sems, bufs, tok = _siblings_start(tag, [gr[n] for n in names], after)
        return {"li": li, "tag": tag, "names": names, "sems": sems, "bufs": bufs, "tok": tok}

    def scatter_middle(st, after):
        grads, from_sib = _siblings_wait(st["tag"], st["bufs"], st["sems"], after)
        parts = [_chip_partials("rs_sum_" + n, g_, s_) for n, g_, s_ in zip(st["names"], grads, from_sib)]
        st["own"] = [own for _, own in parts]
        st["sems"], st["bufs"], tok = _chips_start(st["tag"], [pb for pb, _ in parts], None)
        return tok

    def scatter_end(st, after):
        from_chips = _chips_wait(st["tag"], st["bufs"], st["sems"], after)
        for n, own, fc in zip(st["names"], st["own"], from_chips):
            full = _BIG_FULL[n]
            big_out[n] = _finish_weight("adamw_" + n, st["li"], own, fc, W[full], M[full], V[full], big_out[n])
        return big_out[st["names"][0]][0]

    early, late = ("wpp", "wpg", "wd", "wg", "wu"), ("wout", "win")
    pending, tok = None, None
    for li in reversed(range(n_layers)):
        hooks, first = {}, {}
        if pending is not None:
            hooks["after_down_dx"] = functools.partial(scatter_middle, pending)
        if li == 0:
            def begin_early(gr, after, first=first):
                first.update(scatter_begin(0, "0_early", early, gr, after))
                return first["tok"]

            hooks["after_ffn_grads"] = begin_early
            hooks["before_ffn_norm"] = lambda after, first=first: scatter_middle(first, after)
        dh, gr, sg = _layer_bwd(li, dh, p_all, gathered[li], smalls[li], saved[li], tok, hooks)
        small_grads.append(sg)
        done = None if pending is None else scatter_end(pending, dh)
        if li > 0:
            pending = scatter_begin(li, str(li), _BIG, gr, done)
            tok = pending["tok"]
        else:
            last = scatter_begin(0, "0_late", late, gr, done)
            scatter_middle(last, None)
            scatter_end(first, last["own"][0])
            scatter_end(last, None)
    small_grads.reverse()

    def stacked(fn):
        return jnp.stack([fn(sg) for sg in small_grads])

    mix = lambda i: (lambda sg: sg["mixer"][i])
    grads_small = {
        "norm_mix_g": stacked(lambda sg: sg["norm_mix_g"][0]),
        "sgu_ln_g": stacked(mix(0))[:, :, 0, :], "sgu_ln_b": stacked(mix(1))[:, :, 0, :],
        "sgu_w": stacked(mix(2)), "sgu_b": stacked(mix(3))[:, :, :, 0],
        "sc_conv_w": stacked(mix(4))[:, :SHORT_K], "cf_conv_w": stacked(mix(5))[:, :CONF_K],
        "cf_conv_b": stacked(mix(6))[:, 0], "cf_ln_g": stacked(mix(7))[:, 0], "cf_ln_b": stacked(mix(8))[:, 0],
        "pool_w": stacked(mix(9)), "pool_scale": stacked(mix(10))[:, 0],
        "norm_ffn_g": stacked(lambda sg: sg["norm_ffn_g"][0]), "norm_ple_g": stacked(lambda sg: sg["norm_ple_g"][0]),
        "final_norm_g": d_final_g[0],
    }
    order = _SMALL_REPLICATED + _SMALL_SHARDED
    packed, lay = _pack([grads_small[n] for n in order] + [loss])
    _, summed = _allgather_small("allreduce_small", packed, reduce=True)
    total = dict(zip(order + ("loss",), _unpack(summed, lay)))
    loss_all = total["loss"][0, 0]

    out_g, out_d, out_m, out_v = {}, {}, {}, {}
    pw, lay_r = _pack([W[n] for n in _SMALL_REPLICATED])
    pg, _ = _pack([total[n] for n in _SMALL_REPLICATED])
    pm, _ = _pack([M[n] for n in _SMALL_REPLICATED])
    pv, _ = _pack([V[n] for n in _SMALL_REPLICATED])
    dd, mm, vv = _adamw_small("adamw_small", pw, pg, pm, pv)
    for n, a, b, c_ in zip(_SMALL_REPLICATED, _unpack(dd, lay_r), _unpack(mm, lay_r), _unpack(vv, lay_r)):
        out_g[n], out_d[n], out_m[n], out_v[n] = total[n], a, b, c_
    pick = (jnp.arange(N_DEV) == blk).astype(F32)[None, None, :, None]
    mine = {n: jnp.sum(total[n].reshape(total[n].shape[:2] + (N_DEV, csh)) * pick, axis=2) for n in _SMALL_SHARDED}
    pw, lay_s = _pack([W[n] for n in _SMALL_SHARDED])
    pg, _ = _pack([mine[n] for n in _SMALL_SHARDED])
    pm, _ = _pack([M[n] for n in _SMALL_SHARDED])
    pv, _ = _pack([V[n] for n in _SMALL_SHARDED])
    dd, mm, vv = _adamw_small("adamw_conv_taps", pw, pg, pm, pv)
    for n, a, b, c_ in zip(_SMALL_SHARDED, _unpack(dd, lay_s), _unpack(mm, lay_s), _unpack(vv, lay_s)):
        out_g[n], out_d[n], out_m[n], out_v[n] = mine[n], a, b, c_
    for n in _BIG:
        for k, dst in enumerate((out_g, out_d, out_m, out_v)):
            full = _BIG_FULL[n]
            dst[full] = jnp.swapaxes(big_out[n][k], 1, 2) if full in turned else big_out[n][k]

    return (loss_all, dh[None], *[out_g[n] for n in _WEIGHTS], *[out_d[n] for n in _WEIGHTS],
            *[out_m[n] for n in _WEIGHTS], *[out_v[n] for n in _WEIGHTS])


def kernel(x, p, norm_mix_g, w_in, sgu_ln_g, sgu_ln_b, sgu_w, sgu_b, sc_conv_w, cf_conv_w, cf_conv_b, cf_ln_g, cf_ln_b, pool_w, pool_scale, w_out, norm_ffn_g, w_gate, w_up, w_down, norm_ple_g, w_ple_gate, w_ple_proj, final_norm_g, loss_target, m_norm_mix_g, m_w_in, m_sgu_ln_g, m_sgu_ln_b, m_sgu_w, m_sgu_b, m_sc_conv_w, m_cf_conv_w, m_cf_conv_b, m_cf_ln_g, m_cf_ln_b, m_pool_w, m_pool_scale, m_w_out, m_norm_ffn_g, m_w_gate, m_w_up, m_w_down, m_norm_ple_g, m_w_ple_gate, m_w_ple_proj, m_final_norm_g, v_norm_mix_g, v_w_in, v_sgu_ln_g, v_sgu_ln_b, v_sgu_w, v_sgu_b, v_sc_conv_w, v_cf_conv_w, v_cf_conv_b, v_cf_ln_g, v_cf_ln_b, v_pool_w, v_pool_scale, v_w_out, v_norm_ffn_g, v_w_gate, v_w_up, v_w_down, v_norm_ple_g, v_w_ple_gate, v_w_ple_proj, v_final_norm_g):
    given = dict(locals())
    W = {n: given[n] for n in _WEIGHTS}
    M = {n: given["m_" + n] for n in _WEIGHTS}
    V = {n: given["v_" + n] for n in _WEIGHTS}
    return _step(W, M, V, x, p, loss_target)
```

```python
import functools
import math

import jax
import jax.numpy as jnp
from jax import lax
from jax.experimental import pallas as pl
from jax.experimental.pallas import tpu as pltpu

F32 = jnp.float32
BF16 = jnp.bfloat16
EPS = 1e-6
HEAD = 128
GROUP = 4 * HEAD
HALO = 32
SHORT_K = 3
CONF_K = 31
POOL_WINDOWS = (2, 4, 8, 16)
N_DEV = 8
MESH = pl.DeviceIdType.MESH
VMEM_LIMIT_BYTES = 56 * 1024 * 1024

ADAM_LR = 0.001
ADAM_B1 = 0.9
ADAM_B2 = 0.999
ADAM_EPS = 1e-08
ADAM_WD = 0.01
ADAM_STEP = 10

ANY = pl.BlockSpec(memory_space=pl.ANY)


def _params(*sem):
    return pltpu.CompilerParams(dimension_semantics=sem, vmem_limit_bytes=VMEM_LIMIT_BYTES)


def _rms(x, g):
    return x * lax.rsqrt(jnp.mean(x * x, axis=-1, keepdims=True) + EPS) * g


def _ln(x, g, b):
    mu = jnp.mean(x, axis=-1, keepdims=True)
    xc = x - mu
    var = jnp.mean(xc * xc, axis=-1, keepdims=True)
    return xc * lax.rsqrt(var + EPS) * g + b


def _bdot(a, b, dims=(((1,), (0,)), ((), ()))):
    return lax.dot_general(a.astype(BF16), b.astype(BF16), dims, preferred_element_type=F32)


def _sgu_piece(zu, zv, lg, lb, w, b):
    u = jax.nn.gelu(zu)
    v = _ln(jax.nn.gelu(zv), lg, lb)
    row = lax.broadcasted_iota(jnp.int32, w.shape, 0)
    col = lax.broadcasted_iota(jnp.int32, w.shape, 1)
    wm = jnp.where(row >= col, w, 0.0)
    return u * (_bdot(wm, v) + b)


def _conf_post(c, g, b):
    return jax.nn.silu(_ln(c, g, b))


def _pool_count(first_pos, rows, w):
    pos = first_pos + lax.broadcasted_iota(jnp.int32, (rows, 1), 0) + 1
    return jnp.minimum(pos, w).astype(F32)


_DIMS = {
    "nn": (((1,), (0,)), ((), ())),
    "nt": (((1,), (1,)), ((), ())),
    "tn": (((0,), (0,)), ((), ())),
}


def _tiles(name, grid, ins, in_specs, outs, out_specs, compute, summed=(), deps=()):
    ni = len(ins)
    nd = len(deps)

    def body(*refs):
        vals = compute(*refs[:ni])
        first = functools.reduce(jnp.logical_and, [pl.program_id(a) == 0 for a in range(len(grid))])
        for idx, (r, v) in enumerate(zip(refs[ni + nd:], vals)):
            if idx in summed:
                @pl.when(first)
                def _(r=r):
                    r[...] = jnp.zeros_like(r)

                r[...] += v
            else:
                r[...] = v.astype(r.dtype)

    sem = ("arbitrary" if summed else "parallel",) * len(grid)
    return pl.pallas_call(
        body, name=name, grid=grid, in_specs=list(in_specs) + [ANY] * nd, out_specs=list(out_specs),
        out_shape=list(outs), compiler_params=_params(*sem),
    )(*ins, *deps)


def _resident(shape):
    return pl.BlockSpec(shape, lambda *_: (0,) * len(shape), pipeline_mode=pl.Buffered(1))


def _sds(shape, dtype):
    return jax.ShapeDtypeStruct(tuple(shape), dtype)


def _tile(n, want):
    t = min(n, want)
    assert n % t == 0, (n, want)
    return t


def _rms_fwd(name, h, g):
    s, d = h.shape
    tm = _tile(s, 512)

    def body(h_ref, g_ref, y_ref):
        y_ref[...] = _rms(h_ref[...], g_ref[...]).astype(BF16)

    return pl.pallas_call(
        body, name=name, grid=(s // tm,),
        in_specs=[pl.BlockSpec((tm, d), lambda i: (i, 0)), pl.BlockSpec((1, d), lambda i: (0, 0))],
        out_specs=pl.BlockSpec((tm, d), lambda i: (i, 0)),
        out_shape=_sds((s, d), BF16),
        compiler_params=_params("parallel"),
    )(h, g)


def _rms_back(h, g, dy, dh_in):
    _, vjp = jax.vjp(_rms, h, g)
    dh, dg = vjp(dy)
    return dh_in + dh, dg


def _loss_head(h, g, target):
    s, d = h.shape
    tm = _tile(s, 256)

    def body(h_ref, g_ref, t_ref, loss_ref, dh_ref, dg_ref):
        y, vjp = jax.vjp(_rms, h_ref[...], g_ref[...])
        err = y - t_ref[...]
        dh, dg = vjp(err * (1.0 / d))
        dh_ref[...] = dh
        per_token = jnp.mean(err * err, axis=-1, keepdims=True)
        part = 0.5 * jnp.sum(per_token, axis=0, keepdims=True)

        @pl.when(pl.program_id(0) == 0)
        def _():
            dg_ref[...] = jnp.zeros_like(dg_ref)
            loss_ref[...] = jnp.zeros_like(loss_ref)

        dg_ref[...] += dg
        loss_ref[...] += part

    tok = pl.BlockSpec((tm, d), lambda i: (i, 0))
    vec = pl.BlockSpec((1, d), lambda i: (0, 0))
    return pl.pallas_call(
        body, name="loss_head", grid=(s // tm,),
        in_specs=[tok, vec, tok],
        out_specs=[pl.BlockSpec((1, 1), lambda i: (0, 0)), tok, vec],
        out_shape=[_sds((1, 1), F32), _sds((s, d), F32), _sds((1, d), F32)],
        compiler_params=_params("arbitrary"),
    )(h, g, target)


_U, _V, _HB, _BG, _CG, _CA, _CGT, _PD = (GROUP * i for i in range(8))


def _cols(c0, w=GROUP):
    return slice(c0, c0 + w)


_LANE_CHUNKS = tuple(_cols(i * HEAD, HEAD) for i in range(GROUP // HEAD))


def _shifted_copies(sh_ref, ext_ref, length):
    for r in range(1, 8):
        sh_ref[r - 1, 0:length, :] = ext_ref[pl.ds(r, length), :]


def _window(ext_ref, sh_ref, off, rows, cc):
    q, r = divmod(off, 8)
    if sh_ref is None or r == 0:
        return ext_ref[pl.ds(off, rows), cc]
    return sh_ref[r - 1, pl.ds(8 * q, rows), cc]


def _taps(ext_ref, w_ref, nk, start, rows, cc, flip=False, sh_ref=None):
    acc = None
    for k in range(nk):
        kw = nk - 1 - k if flip else k
        term = w_ref[kw:kw + 1, cc] * _window(ext_ref, sh_ref, start + k, rows, cc)
        acc = term if acc is None else acc + term
    return acc


def _mixer_param_specs():
    full = lambda *shape: pl.BlockSpec(shape, lambda i: (0,) * len(shape))
    return [
        full(4, 1, HEAD), full(4, 1, HEAD), full(4, HEAD, HEAD), full(4, HEAD, 1),
        full(8, GROUP), full(32, GROUP), full(1, GROUP), full(1, GROUP), full(1, GROUP),
        full(4, HEAD, HEAD), full(1, GROUP),
    ]


def _mixer_fwd(name, z, prm):
    s = z.shape[0]
    t = _tile(s, 256)
    hb = t // HALO

    def body(zp_ref, zm_ref, lg_ref, lb_ref, sw_ref, sb_ref, scw_ref, cfw_ref, cfb_ref, cg_ref, cb_ref,
             pw_ref, ps_ref, o_ref, ext_ref, sh_ref):
        i = pl.program_id(0)
        keep = (i > 0).astype(F32)
        main = pl.ds(HALO, t)

        for n in range(t // HEAD):
            rows = slice(n * HEAD, (n + 1) * HEAD)
            for hh in range(4):
                cu = _cols(_U + hh * HEAD, HEAD)
                cv = _cols(_V + hh * HEAD, HEAD)
                o_ref[rows, _cols(hh * HEAD, HEAD)] = _sgu_piece(
                    zm_ref[rows, cu], zm_ref[rows, cv], lg_ref[hh], lb_ref[hh], sw_ref[hh], sb_ref[hh]
                ).astype(BF16)

        ext_ref[0:HALO, :] = zp_ref[:, _cols(_CG)] * zp_ref[:, _cols(_HB)] * keep
        ext_ref[main, :] = zm_ref[:, _cols(_CG)] * zm_ref[:, _cols(_HB)]
        y = jnp.zeros((t, GROUP), F32)
        for k in range(SHORT_K):
            y = y + scw_ref[k:k + 1, :] * ext_ref[pl.ds(HALO - (SHORT_K - 1) + k, t), :]
        o_ref[:, _cols(GROUP)] = (zm_ref[:, _cols(_BG)] * y).astype(BF16)

        ext_ref[0:HALO, :] = zp_ref[:, _cols(_CA)] * jax.nn.sigmoid(zp_ref[:, _cols(_CGT)]) * keep
        ext_ref[main, :] = zm_ref[:, _cols(_CA)] * jax.nn.sigmoid(zm_ref[:, _cols(_CGT)])
        _shifted_copies(sh_ref, ext_ref, HALO + t - 8)
        c = jnp.concatenate(
            [cfb_ref[:, cc] + _taps(ext_ref, cfw_ref, CONF_K, HALO - (CONF_K - 1), t, cc, sh_ref=sh_ref)
             for cc in _LANE_CHUNKS], axis=1)
        o_ref[:, _cols(2 * GROUP)] = _conf_post(c, cg_ref[...], cb_ref[...]).astype(BF16)

        ext_ref[0:HALO, :] = zp_ref[:, _cols(_PD)] * keep
        ext_ref[main, :] = zm_ref[:, _cols(_PD)]
        for gi, w in enumerate(POOL_WINDOWS):
            cc = _cols(gi * HEAD, HEAD)
            acc = ext_ref[main, cc]
            for j in range(1, w):
                acc = acc + ext_ref[pl.ds(HALO - j, t), cc]
            q = acc / _pool_count(i * t, t, w) - ext_ref[main, cc]
            o_ref[:, _cols(3 * GROUP + gi * HEAD, HEAD)] = (_bdot(q, pw_ref[gi]) * ps_ref[:, cc]).astype(BF16)

    return pl.pallas_call(
        body, name=name, grid=(s // t,),
        in_specs=[pl.BlockSpec((HALO, 8 * GROUP), lambda i: (jnp.maximum(i * hb - 1, 0), 0)),
                  pl.BlockSpec((t, 8 * GROUP), lambda i: (i, 0)),
                  *_mixer_param_specs()],
        out_specs=pl.BlockSpec((t, 4 * GROUP), lambda i: (i, 0)),
        out_shape=_sds((s, 4 * GROUP), BF16),
        scratch_shapes=[pltpu.VMEM((HALO + t, GROUP), F32), pltpu.VMEM((7, HALO + t - 8, GROUP), F32)],
        compiler_params=_params("parallel"),
    )(z, z, *prm)


def _mixer_bwd(name, z, dmo, prm):
    s = z.shape[0]
    t = _tile(s, 256)
    hb = t // HALO
    nt = s // t
    last_halo = s // HALO - 1

    def body(zp_ref, zm_ref, zn_ref, dm_ref, dn_ref,
             lg_ref, lb_ref, sw_ref, sb_ref, scw_ref, cfw_ref, cfb_ref, cg_ref, cb_ref, pw_ref, ps_ref,
             dz_ref, dlg_ref, dlb_ref, dsw_ref, dsb_ref, dscw_ref, dcfw_ref, dcfb_ref, dcg_ref, dcb_ref,
             dpw_ref, dps_ref, extp_ref, extn_ref, extc_ref, sh_ref):
        i = pl.program_id(0)
        keep_prev = (i > 0).astype(F32)
        keep_next = (i < nt - 1).astype(F32)
        main = pl.ds(HALO, t)

        @pl.when(i == 0)
        def _():
            for r in (dlg_ref, dlb_ref, dsw_ref, dsb_ref, dscw_ref, dcfw_ref, dcfb_ref, dcg_ref, dcb_ref,
                      dpw_ref, dps_ref):
                r[...] = jnp.zeros_like(r)

        def rowsum(v):
            return jnp.sum(v, axis=0, keepdims=True)

        for n in range(t // HEAD):
            rows = slice(n * HEAD, (n + 1) * HEAD)
            for hh in range(4):
                cu = _cols(_U + hh * HEAD, HEAD)
                cv = _cols(_V + hh * HEAD, HEAD)
                _, vjp = jax.vjp(_sgu_piece, zm_ref[rows, cu], zm_ref[rows, cv],
                                 lg_ref[hh], lb_ref[hh], sw_ref[hh], sb_ref[hh])
                dzu, dzv, dlg, dlb, dsw, dsb = vjp(dm_ref[rows, _cols(hh * HEAD, HEAD)].astype(F32))
                dz_ref[rows, cu] = dzu.astype(BF16)
                dz_ref[rows, cv] = dzv.astype(BF16)
                dlg_ref[hh] += dlg
                dlb_ref[hh] += dlb
                dsw_ref[hh] += dsw
                dsb_ref[hh] += dsb

        extp_ref[0:HALO, :] = zp_ref[:, _cols(_CG)] * zp_ref[:, _cols(_HB)] * keep_prev
        extp_ref[main, :] = zm_ref[:, _cols(_CG)] * zm_ref[:, _cols(_HB)]
        dob = dm_ref[:, _cols(GROUP)].astype(F32)
        dy = dob * zm_ref[:, _cols(_BG)]
        extn_ref[0:t, :] = dy
        extn_ref[t:t + HALO, :] = dn_ref[:, _cols(GROUP)].astype(F32) * zn_ref[:, _cols(_BG)] * keep_next
        for cc in _LANE_CHUNKS:
            at = lambda c0: _cols(c0 + cc.start, HEAD)
            y = _taps(extp_ref, scw_ref, SHORT_K, HALO - (SHORT_K - 1), t, cc)
            dx = _taps(extn_ref, scw_ref, SHORT_K, 0, t, cc, flip=True)
            dy_c = extn_ref[0:t, cc]
            for k in range(SHORT_K):
                dscw_ref[k:k + 1, cc] += rowsum(dy_c * extp_ref[pl.ds(HALO - (SHORT_K - 1) + k, t), cc])
            dz_ref[:, at(_BG)] = (dm_ref[:, at(GROUP)].astype(F32) * y).astype(BF16)
            dz_ref[:, at(_CG)] = (dx * zm_ref[:, at(_HB)]).astype(BF16)
            dz_ref[:, at(_HB)] = (dx * zm_ref[:, at(_CG)]).astype(BF16)

        extc_ref[0:HALO, :] = zp_ref[:, _cols(_CA)] * jax.nn.sigmoid(zp_ref[:, _cols(_CGT)]) * keep_prev
        extc_ref[main, :] = zm_ref[:, _cols(_CA)] * jax.nn.sigmoid(zm_ref[:, _cols(_CGT)])
        extc_ref[HALO + t:HALO + t + HALO, :] = zn_ref[:, _cols(_CA)] * jax.nn.sigmoid(zn_ref[:, _cols(_CGT)])
        _shifted_copies(sh_ref, extc_ref, HALO + t + HALO - 8)
        conv = lambda start, rows: jnp.concatenate(
            [cfb_ref[:, cc] + _taps(extc_ref, cfw_ref, CONF_K, start, rows, cc, sh_ref=sh_ref) for cc in _LANE_CHUNKS],
            axis=1)
        c_main = conv(HALO - (CONF_K - 1), t)
        c_next = conv(HALO + t - (CONF_K - 1), HALO)
        _, vjp = jax.vjp(_conf_post, c_main, cg_ref[...], cb_ref[...])
        dc, dcg, dcb = vjp(dm_ref[:, _cols(2 * GROUP)].astype(F32))
        dcg_ref[...] += dcg
        dcb_ref[...] += dcb
        dcfb_ref[...] += rowsum(dc)
        _, vjp_next = jax.vjp(lambda cv: _conf_post(cv, cg_ref[...], cb_ref[...]), c_next)
        (dc_next,) = vjp_next(dn_ref[:, _cols(2 * GROUP)].astype(F32) * keep_next)
        extn_ref[0:t, :] = dc
        extn_ref[t:t + HALO, :] = dc_next
        for cc in _LANE_CHUNKS:
            dc_c = extn_ref[0:t, cc]
            for k in range(CONF_K):
                dcfw_ref[k:k + 1, cc] += rowsum(dc_c * _window(extc_ref, sh_ref, HALO - (CONF_K - 1) + k, t, cc))
        _shifted_copies(sh_ref, extn_ref, t + HALO - 8)
        for cc in _LANE_CHUNKS:
            at = lambda c0: _cols(c0 + cc.start, HEAD)
            dhc = _taps(extn_ref, cfw_ref, CONF_K, 0, t, cc, flip=True, sh_ref=sh_ref)
            sg = jax.nn.sigmoid(zm_ref[:, at(_CGT)])
            dz_ref[:, at(_CA)] = (dhc * sg).astype(BF16)
            dz_ref[:, at(_CGT)] = (dhc * zm_ref[:, at(_CA)] * sg * (1.0 - sg)).astype(BF16)

        extp_ref[0:HALO, :] = zp_ref[:, _cols(_PD)] * keep_prev
        extp_ref[main, :] = zm_ref[:, _cols(_PD)]
        for gi, w in enumerate(POOL_WINDOWS):
            cc = _cols(gi * HEAD, HEAD)
            oc = _cols(3 * GROUP + gi * HEAD, HEAD)
            acc = extp_ref[main, cc]
            for j in range(1, w):
                acc = acc + extp_ref[pl.ds(HALO - j, t), cc]
            count = _pool_count(i * t, t, w)
            q = acc / count - extp_ref[main, cc]
            dod = dm_ref[:, oc].astype(F32)
            dps_ref[:, cc] += rowsum(dod * _bdot(q, pw_ref[gi]))
            ds = dod * ps_ref[:, cc]
            dpw_ref[gi] += _bdot(q, ds, _DIMS["tn"])
            dq = _bdot(ds, pw_ref[gi], _DIMS["nt"])
            ds_next = dn_ref[:, oc].astype(F32) * ps_ref[:, cc] * keep_next
            dq_next = _bdot(ds_next, pw_ref[gi], _DIMS["nt"])
            extn_ref[0:t, cc] = dq / count
            extn_ref[t:t + HALO, cc] = dq_next * (1.0 / w)
            back = extn_ref[0:t, cc]
            for j in range(1, w):
                back = back + extn_ref[pl.ds(j, t), cc]
            dz_ref[:, _cols(_PD + gi * HEAD, HEAD)] = (back - dq).astype(BF16)

    full = lambda *shape: pl.BlockSpec(shape, lambda i: (0,) * len(shape))
    grad_specs = [full(4, 1, HEAD), full(4, 1, HEAD), full(4, HEAD, HEAD), full(4, HEAD, 1),
                  full(8, GROUP), full(32, GROUP), full(1, GROUP), full(1, GROUP), full(1, GROUP),
                  full(4, HEAD, HEAD), full(1, GROUP)]
    grad_shapes = [_sds(sp.block_shape, F32) for sp in grad_specs]
    nxt = lambda i: (jnp.minimum((i + 1) * hb, last_halo), 0)
    return pl.pallas_call(
        body, name=name, grid=(nt,),
        in_specs=[pl.BlockSpec((HALO, 8 * GROUP), lambda i: (jnp.maximum(i * hb - 1, 0), 0)),
                  pl.BlockSpec((t, 8 * GROUP), lambda i: (i, 0)),
                  pl.BlockSpec((HALO, 8 * GROUP), nxt),
                  pl.BlockSpec((t, 4 * GROUP), lambda i: (i, 0)),
                  pl.BlockSpec((HALO, 4 * GROUP), nxt),
                  *_mixer_param_specs()],
        out_specs=[pl.BlockSpec((t, 8 * GROUP), lambda i: (i, 0)), *grad_specs],
        out_shape=[_sds((s, 8 * GROUP), BF16), *grad_shapes],
        scratch_shapes=[pltpu.VMEM((HALO + t, GROUP), F32), pltpu.VMEM((t + HALO, GROUP), F32),
                        pltpu.VMEM((HALO + t + HALO, GROUP), F32), pltpu.VMEM((7, HALO + t + HALO - 8, GROUP), F32)],
        compiler_params=_params("arbitrary"),
    )(z, z, z, dmo, dmo, *prm)


def _dot(a, b, kind="nn"):
    return _bdot(a, b, _DIMS[kind])


def _chain(terms, kind):
    acc = None
    for a, b in terms:
        p = _dot(a, b, kind)
        acc = p if acc is None else acc + p
    return acc


def _after(g, token):
    return g if token is None else g + token[0:1, 0:1]


def _layer_fwd(li, h, p_all, fetch, sm, y1=None, next_mix_g=None):
    s, d = h.shape
    nb = N_DEV
    pd = p_all.shape[2]
    sv = {"h": h}
    rows = lambda t, w: pl.BlockSpec((t, w), lambda i: (i, 0))
    vec = pl.BlockSpec((1, d), lambda i: (0, 0))

    if y1 is None:
        y1 = _rms_fwd("rms_mix", h, sm["norm_mix_g"])
    tm = _tile(s, 1024)
    (z,) = _tiles(
        "w_in_fwd", (s // tm, nb),
        [y1, fetch("win", y1)],
        [pl.BlockSpec((tm, d), lambda i, j: (i, 0)), pl.BlockSpec((None, d, GROUP), lambda i, j: (j, 0, 0))],
        [_sds((s, nb * GROUP), F32)], [pl.BlockSpec((tm, GROUP), lambda i, j: (i, j))],
        lambda a, w: (_dot(a[...], w[...]),))
    mo = _mixer_fwd("mixer_fwd", z, sm["mixer"])
    tm = _tile(s, 512)
    def mix_out(a, w, hv, gv):
        h2v = hv[...] + _dot(a[...], w[...])
        return h2v, _rms(h2v, gv[...])

    h2, y2 = _tiles(
        "w_out_fwd", (s // tm,),
        [mo, fetch("wout", z), h, sm["norm_ffn_g"]], [rows(tm, d), _resident((d, d)), rows(tm, d), vec],
        [_sds((s, d), F32), _sds((s, d), BF16)], [rows(tm, d), rows(tm, d)], mix_out)
    sv.update(y1=y1, z=z, mo=mo, h2=h2)

    gate_pre, up_pre, hmid = _ffn_up("ffn_up_fwd", y2, fetch("wg", z), fetch("wu", z))
    wd = fetch("wd", hmid)
    f8 = wd.shape[1]
    tm, tn = _tile(s, 512), _tile(d, 1024)
    (h3,) = _tiles(
        "w_down_fwd", (d // tn, s // tm),
        [hmid, wd, h2],
        [pl.BlockSpec((nb, tm, f8), lambda j, i: (0, i, 0)), pl.BlockSpec((nb, f8, tn), lambda j, i: (0, 0, j)),
         pl.BlockSpec((tm, tn), lambda j, i: (i, j))],
        [_sds((s, d), F32)], [pl.BlockSpec((tm, tn), lambda j, i: (i, j))],
        lambda a, w, hv: (hv[...] + _chain([(a[k], w[k]) for k in range(nb)], "nn"),))
    sv.update(y2=y2, gate_pre=gate_pre, up_pre=up_pre, hmid=hmid, h3=h3)

    y3 = _rms_fwd("rms_ple", h3, sm["norm_ple_g"])
    tm = _tile(s, 1024)
    (pp,) = _tiles(
        "w_ple_proj_fwd", (s // tm, nb),
        [p_all, fetch("wpp", hmid)],
        [pl.BlockSpec((None, tm, pd), lambda i, j: (li, i, 0)), pl.BlockSpec((None, pd, pd), lambda i, j: (j, 0, 0))],
        [_sds((s, d), BF16)], [pl.BlockSpec((tm, pd), lambda i, j: (i, j))],
        lambda a, w: (_dot(a[...], w[...]),))
    tm = _tile(s, 256)

    def ple(a, w, hv, ppv, *next_g):
        pg = _dot(a[...], w[...])
        h4v = hv[...] + jax.nn.sigmoid(pg) * ppv[...].astype(F32)
        return (h4v, pg) + tuple(_rms(h4v, gv[...]) for gv in next_g)

    more = [] if next_mix_g is None else [next_mix_g]
    h4, pg_pre, *y1_next = _tiles(
        "w_ple_gate_fwd", (s // tm,),
        [y3, fetch("wpg", hmid), h3, pp] + more,
        [rows(tm, d), _resident((d, d)), rows(tm, d), rows(tm, d)] + [vec] * len(more),
        [_sds((s, d), F32), _sds((s, d), BF16)] + [_sds((s, d), BF16)] * len(more),
        [rows(tm, d)] * (2 + len(more)), ple)
    sv.update(y3=y3, pp=pp, pg_pre=pg_pre)
    return h4, sv, (y1_next[0] if y1_next else None)


def _ffn_up(name, y2, wg, wu):
    s, d = y2.shape
    nb, f8, _ = wg.shape
    tm = _tile(s, 1024)

    def compute(y_ref, wg_ref, wu_ref):
        yv = y_ref[...]
        g = _dot(yv, wg_ref[...], "nt")
        u = _dot(yv, wu_ref[...], "nt")
        return g, u, jax.nn.silu(g) * u

    wspec = pl.BlockSpec((None, f8, d), lambda i, j: (j, 0, 0))
    ospec = pl.BlockSpec((None, tm, f8), lambda i, j: (j, i, 0))
    return _tiles(name, (s // tm, nb), [y2, wg, wu], [pl.BlockSpec((tm, d), lambda i, j: (i, 0)), wspec, wspec],
                  [_sds((nb, s, f8), BF16)] * 3, [ospec] * 3, compute)


def _swiglu_bwd(dm, g, u):
    sg = jax.nn.sigmoid(g)
    silu = g * sg
    return dm * u * (sg + silu * (1.0 - sg)), dm * silu


def _layer_bwd(li, dh, p_all, gw, sm, sv, start_token=None, hooks=None):
    hooks = hooks or {}
    run = lambda name, *a: hooks[name](*a) if name in hooks else None
    s, d = dh.shape
    nb, f8, _ = gw["wg"].shape
    pd = p_all.shape[2]
    gr, sg = {}, {}
    rows = lambda t, w: pl.BlockSpec((t, w), lambda i: (i, 0))
    vec = pl.BlockSpec((1, d), lambda i: (0, 0))
    tw = _tile(d, 1024)
    whole = lambda w: pl.BlockSpec((s, w), lambda i, j: (0, i))
    whole_j = lambda w: pl.BlockSpec((s, w), lambda i, j: (0, j))
    tn_dot = lambda a, b: (_dot(a[...], b[...], "tn"),)

    tm = _tile(s, 256)

    def ple_dx(dv, pg, ppv, w, hv, gv):
        dhv = dv[...]
        gate = jax.nn.sigmoid(pg[...].astype(F32))
        dpg = (dhv * ppv[...].astype(F32) * gate * (1.0 - gate)).astype(BF16)
        dh3, dg = _rms_back(hv[...], gv[...], _dot(dpg, w[...], "nt"), dhv)
        return dh3, dh3, dg, dpg, dhv * gate

    dh3, dh3_16, sg["norm_ple_g"], dpg, dpp = _tiles(
        "w_ple_gate_dx", (s // tm,),
        [dh, sv["pg_pre"], sv["pp"], gw["wpg"], sv["h3"], _after(sm["norm_ple_g"], start_token)],
        [rows(tm, d), rows(tm, d), rows(tm, d), _resident((d, d)), rows(tm, d), vec],
        [_sds((s, d), F32), _sds((s, d), BF16), _sds((1, d), F32), _sds((s, d), BF16), _sds((s, d), BF16)],
        [rows(tm, d), rows(tm, d), vec, rows(tm, d), rows(tm, d)], ple_dx, summed=(2,))
    (gr["wpp"],) = _tiles(
        "w_ple_proj_dw", (nb,),
        [p_all, dpp], [pl.BlockSpec((None, s, pd), lambda j: (li, 0, 0)), pl.BlockSpec((s, pd), lambda j: (0, j))],
        [_sds((nb, pd, pd), F32)], [pl.BlockSpec((None, pd, pd), lambda j: (j, 0, 0))], tn_dot)
    (gr["wpg"],) = _tiles(
        "w_ple_gate_dw", (d // tw, d // tw), [sv["y3"], dpg], [whole(tw), whole_j(tw)],
        [_sds((d, d), F32)], [pl.BlockSpec((tw, tw), lambda i, j: (i, j))], tn_dot)
    gr["wpg"] = gr["wpg"].reshape(nb, d // nb, d)

    tm = _tile(s, 1024)
    blk_rows = pl.BlockSpec((None, tm, f8), lambda i, j: (j, i, 0))
    dgate, dup = _tiles(
        "w_down_dx", (s // tm, nb),
        [dh3_16, gw["wd"], sv["gate_pre"], sv["up_pre"]],
        [pl.BlockSpec((tm, d), lambda i, j: (i, 0)), pl.BlockSpec((None, f8, d), lambda i, j: (j, 0, 0)), blk_rows, blk_rows],
        [_sds((nb, s, f8), BF16)] * 2, [blk_rows] * 2,
        lambda a, w, g, u: _swiglu_bwd(_dot(a[...], w[...], "nt"), g[...].astype(F32), u[...].astype(F32)))
    (gr["wd"],) = _tiles(
        "w_down_dw", (nb, d // tw),
        [sv["hmid"], dh3_16], [pl.BlockSpec((None, s, f8), lambda i, j: (i, 0, 0)), whole_j(tw)],
        [_sds((nb, f8, d), F32)], [pl.BlockSpec((None, f8, tw), lambda i, j: (i, 0, j))], tn_dot)
    mid_token = run("after_down_dx", dgate)
    for nm, dact in (("wg", dgate), ("wu", dup)):
        (gr[nm],) = _tiles(
            "w_" + {"wg": "gate", "wu": "up"}[nm] + "_dw", (d // tw, nb),
            [dact, sv["y2"]], [pl.BlockSpec((None, s, f8), lambda i, j: (j, 0, 0)), whole(tw)],
            [_sds((nb, f8, d), F32)], [pl.BlockSpec((None, f8, tw), lambda i, j: (j, 0, i))], tn_dot)
    ffn_token = run("after_ffn_grads", gr, gr["wu"])
    tm, tn = _tile(s, 512), _tile(d, 1024)
    act_spec = pl.BlockSpec((nb, tm, f8), lambda j, i: (0, i, 0))
    wt_spec = pl.BlockSpec((nb, f8, tn), lambda j, i: (0, 0, j))
    out_spec = pl.BlockSpec((tm, tn), lambda j, i: (i, j))
    (dy2,) = _tiles(
        "w_gate_dx", (d // tn, s // tm), [dgate, gw["wg"]], [act_spec, wt_spec],
        [_sds((s, d), F32)], [out_spec],
        lambda a, w: (_chain([(a[k], w[k]) for k in range(nb)], "nn"),),
        deps=[tk for tk in (mid_token, ffn_token) if tk is not None])
    (dy2,) = _tiles(
        "w_up_dx", (d // tn, s // tm), [dup, gw["wu"], dy2], [act_spec, wt_spec, out_spec],
        [_sds((s, d), F32)], [out_spec],
        lambda a, w, prev: (prev[...] + _chain([(a[k], w[k]) for k in range(nb)], "nn"),))
    g_ffn = _after(sm["norm_ffn_g"], run("before_ffn_norm", dy2))

    tm = _tile(s, 256)

    def out_dx(dy, hv, gv, dv, w):
        dh2v, dg = _rms_back(hv[...], gv[...], dy[...], dv[...])
        return dh2v, dh2v, dg, _dot(dh2v, w[...], "nt")

    dh2, dh2_16, sg["norm_ffn_g"], dmo = _tiles(
        "w_out_dx", (s // tm,), [dy2, sv["h2"], g_ffn, dh3, gw["wout"]],
        [rows(tm, d), rows(tm, d), vec, rows(tm, d), _resident((d, d))],
        [_sds((s, d), F32), _sds((s, d), BF16), _sds((1, d), F32), _sds((s, d), BF16)],
        [rows(tm, d), rows(tm, d), vec, rows(tm, d)], out_dx, summed=(2,))
    (gr["wout"],) = _tiles(
        "w_out_dw", (d // tw, d // tw), [sv["mo"], dh2_16], [whole(tw), whole_j(tw)],
        [_sds((d, d), F32)], [pl.BlockSpec((tw, tw), lambda i, j: (i, j))], tn_dot)
    gr["wout"] = gr["wout"].reshape(nb, d // nb, d)
    dz, *mix_grads = _mixer_bwd("mixer_bwd", sv["z"], dmo, sm["mixer"])
    sg["mixer"] = mix_grads
    (gr["win"],) = _tiles(
        "w_in_dw", (d // tw, nb), [sv["y1"], dz], [whole(tw), pl.BlockSpec((s, GROUP), lambda i, j: (0, j))],
        [_sds((nb, d, GROUP), F32)], [pl.BlockSpec((None, tw, GROUP), lambda i, j: (j, i, 0))], tn_dot)
    tm = _tile(s, 256)

    def in_dx(a, w, hv, gv, dv):
        dy1 = _chain([(a[:, k * GROUP:(k + 1) * GROUP], w[k]) for k in range(nb)], "nt")
        return _rms_back(hv[...], gv[...], dy1, dv[...])

    dh_in, sg["norm_mix_g"] = _tiles(
        "w_in_dx", (s // tm,),
        [dz, gw["win"], sv["h"], sm["norm_mix_g"], dh2],
        [rows(tm, nb * GROUP), _resident((nb, d, GROUP)), rows(tm, d), vec, rows(tm, d)],
        [_sds((s, d), F32), _sds((1, d), F32)], [rows(tm, d), vec], in_dx, summed=(1,))
    return dh_in, gr, sg


def _place():
    return lax.axis_index("x"), lax.axis_index("y"), lax.axis_index("c")


_HBM = pl.BlockSpec(memory_space=pltpu.HBM)
_SEM = pl.BlockSpec(memory_space=pltpu.SEMAPHORE)


def _split_call(name, bufs, old_sems, n_new, after, body, want_token):
    nb, no = len(bufs), len(old_sems)
    extra = [] if after is None else [after]

    def kbody(*refs):
        new = refs[nb + no + len(extra):nb + no + len(extra) + n_new]
        body(refs[:nb], refs[nb:nb + no], new)
        if want_token:
            refs[-1][...] = jnp.zeros_like(refs[-1])

    outs = pl.pallas_call(
        kbody, name=name,
        out_shape=tuple([pltpu.SemaphoreType.DMA(())] * n_new + [pltpu.HBM(b.shape, b.dtype) for b in bufs]
                        + ([_sds((8, 128), F32)] if want_token else [])),
        in_specs=[_HBM] * nb + [_SEM] * no + [ANY] * len(extra),
        out_specs=tuple([_SEM] * n_new + [_HBM] * nb
                        + ([pl.BlockSpec(memory_space=pltpu.VMEM)] if want_token else [])),
        input_output_aliases={i: n_new + i for i in range(nb)},
        compiler_params=pltpu.CompilerParams(has_side_effects=pltpu.SideEffectType.DATAFLOW_SIDE_EFFECTING),
    )(*[pltpu.with_memory_space_constraint(b, pltpu.HBM) for b in bufs], *old_sems, *extra)
    return list(outs[:n_new]), list(outs[n_new:n_new + nb]), (outs[-1] if want_token else None)


def _remote(ref_src, ref_dst, send_sem, recv_sem, to):
    return pltpu.make_async_remote_copy(src_ref=ref_src, dst_ref=ref_dst, send_sem=send_sem, recv_sem=recv_sem,
                                        device_id=to, device_id_type=MESH)


def _place_shard(name, w):
    n_layers, r, cdim = w.shape
    tr = _row_tile(r, 256)
    nt = r // tr

    def body(w_ref, *rest):
        outs, buf, sem = rest[:n_layers], rest[n_layers], rest[n_layers + 1]
        i = pl.program_id(0)
        x, y, c = _place()
        slot = i % 2

        def writes(step, sl):
            rows = pl.ds(pl.multiple_of(step * tr, 16), tr)
            return [pltpu.make_async_copy(buf.at[sl, l], outs[l].at[4 * x + 2 * y + c, rows, :], sem.at[sl, l])
                    for l in range(n_layers)]

        @pl.when(i >= 2)
        def _():
            for cp in writes(i - 2, slot):
                cp.wait()

        buf[slot] = w_ref[...].astype(BF16)
        for cp in writes(i, slot):
            cp.start()

        @pl.when(i == nt - 1)
        def _():
            for cp in writes(i, slot):
                cp.wait()
            if nt >= 2:
                for cp in writes(i - 1, 1 - slot):
                    cp.wait()

    return pl.pallas_call(
        body, name=name, grid=(nt,),
        in_specs=[pl.BlockSpec((n_layers, tr, cdim), lambda i: (0, i, 0))], out_specs=[ANY] * n_layers,
        out_shape=[_sds((N_DEV, r, cdim), BF16)] * n_layers,
        scratch_shapes=[pltpu.VMEM((2, n_layers, tr, cdim), BF16), pltpu.SemaphoreType.DMA((2, n_layers))],
        compiler_params=_params("arbitrary"),
    )(w)


def _gather_start(li, lands, after):
    nw = len(lands)

    def body(bufs, _, new):
        x, y, c = _place()
        chips = [(1 - x, y), (x, 1 - y), (1 - x, 1 - y)]
        for w in range(nw):
            mine = bufs[w].at[4 * x + 2 * y + c]
            _remote(mine, mine, new[8 * w], new[8 * w + 4], (x, y, 1 - c)).start()
            for j, (px, py) in enumerate(chips):
                _remote(mine, mine, new[8 * w + 1 + j], new[8 * w + 5 + j], (px, py, c)).start()

    return _split_call("gather_start_l%s" % li, lands, [], 8 * nw, after, body, True)


def _gather_forward(li, lands, sems, after):
    nw = len(lands)
    arrivals = [sems[8 * w + 5 + j] for w in range(nw) for j in range(3)]

    def body(bufs, old, new):
        x, y, c = _place()
        chips = [(1 - x, y), (x, 1 - y), (1 - x, 1 - y)]
        for j, (px, py) in enumerate(chips):
            for w in range(nw):
                got = bufs[w].at[4 * px + 2 * py + c]
                _remote(got, got, new[6 * w + j], old[3 * w + j], (px, py, c)).wait_recv()
                _remote(got, got, new[6 * w + j], new[6 * w + 3 + j], (x, y, 1 - c)).start()

    return _split_call("gather_forward_l%s" % li, lands, arrivals, 6 * nw, after, body, True)


def _gather_wait(li, lands, sems, fwd_sems, after):
    nw = len(lands)
    first = [sems[8 * w + k] for w in range(nw) for k in range(5)]

    def body(bufs, old, _):
        x, y, c = _place()
        sib = (x, y, 1 - c)
        chips = [(1 - x, y), (x, 1 - y), (1 - x, 1 - y)]
        n1 = 5 * nw
        for w in range(nw):
            mine = bufs[w].at[4 * x + 2 * y + c]
            theirs = bufs[w].at[4 * x + 2 * y + 1 - c]
            _remote(theirs, theirs, old[5 * w], old[5 * w + 4], sib).wait_recv()
            for k in range(4):
                _remote(mine, mine, old[5 * w + k], old[5 * w + 4], sib).wait_send()
            for j, (px, py) in enumerate(chips):
                sent = bufs[w].at[4 * px + 2 * py + c]
                got = bufs[w].at[4 * px + 2 * py + 1 - c]
                _remote(sent, sent, old[n1 + 6 * w + j], old[n1 + 6 * w + 3 + j], sib).wait_send()
                _remote(got, got, old[n1 + 6 * w + j], old[n1 + 6 * w + 3 + j], sib).wait_recv()

    _, lands, _ = _split_call("gather_wait_l%s" % li, lands, first + list(fwd_sems), 0, after, body, False)
    return lands


def _siblings_start(li, grads, after):
    nw = len(grads)
    lands = [lax.empty((4,) + g.shape[1:], g.dtype) for g in grads]

    def body(bufs, _, new):
        x, y, c = _place()
        for w in range(nw):
            for q in range(4):
                _remote(bufs[w].at[2 * q + (1 - c)], bufs[nw + w].at[q], new[8 * w + q], new[8 * w + 4 + q],
                        (x, y, 1 - c)).start()

    return _split_call("rs_siblings_start_l%s" % li, list(grads) + lands, [], 8 * nw, after, body, True)


def _siblings_wait(li, bufs, sems, after):
    nw = len(bufs) // 2

    def body(refs, old, _):
        x, y, c = _place()
        for w in range(nw):
            for q in range(4):
                _remote(refs[w].at[2 * q + (1 - c)], refs[nw + w].at[q], old[8 * w + q], old[8 * w + 4 + q],
                        (x, y, 1 - c)).wait()

    _, bufs, _ = _split_call("rs_siblings_wait_l%s" % li, bufs, sems, 0, after, body, False)
    return bufs[:nw], bufs[nw:]


_FLIPS = ((1, 0), (0, 1), (1, 1))


def _chips_copies(refs, nw, sems):
    x, y, c = _place()
    for w in range(nw):
        for r, (fx, fy) in enumerate(_FLIPS):
            px = 1 - x if fx else x
            py = 1 - y if fy else y
            yield _remote(refs[w].at[2 * px + py], refs[nw + w].at[r], sems[6 * w + r], sems[6 * w + 3 + r], (px, py, c))


def _chips_start(li, partials, after):
    nw = len(partials)
    lands = [lax.empty((3,) + a.shape[1:], a.dtype) for a in partials]

    def body(bufs, _, new):
        for cp in _chips_copies(bufs, nw, new):
            cp.start()

    return _split_call("rs_chips_start_l%s" % li, list(partials) + lands, [], 6 * nw, after, body, True)


def _chips_wait(li, bufs, sems, after):
    nw = len(bufs) // 2

    def body(refs, old, _):
        for cp in _chips_copies(refs, nw, old):
            cp.wait()

    _, bufs, _ = _split_call("rs_chips_wait_l%s" % li, bufs, sems, 0, after, body, False)
    return bufs[nw:]


def _row_tile(n, want):
    best = None
    for t in range(16, min(n, want) + 1, 16):
        if n % t == 0:
            best = t
    assert best is not None, n
    return best


def _chip_partials(name, grad, from_sibling):
    _, r, cdim = grad.shape
    tr = _row_tile(r, 256)
    nt = r // tr
    steps = 4 * nt

    def body(g_hbm, s_ref, pb_ref, own_ref, buf, sem):
        i, q = pl.program_id(0), pl.program_id(1)
        x, y, c = _place()
        n = 4 * i + q
        slot = n % 2

        def fetch(step, into):
            rows = pl.ds(pl.multiple_of((step // 4) * tr, 16), tr)
            return pltpu.make_async_copy(g_hbm.at[2 * (step % 4) + c, rows, :], buf.at[into], sem.at[into])

        @pl.when(n == 0)
        def _():
            fetch(0, 0).start()

        @pl.when(n + 1 < steps)
        def _():
            fetch(n + 1, 1 - slot).start()

        fetch(n, slot).wait()
        tot = buf[slot] + s_ref[...]
        pb_ref[...] = tot.astype(BF16)

        @pl.when(q == 2 * x + y)
        def _():
            own_ref[...] = tot

    return pl.pallas_call(
        body, name=name, grid=(nt, 4),
        in_specs=[ANY, pl.BlockSpec((None, tr, cdim), lambda i, q: (q, i, 0))],
        out_specs=[pl.BlockSpec((None, tr, cdim), lambda i, q: (q, i, 0)), pl.BlockSpec((tr, cdim), lambda i, q: (i, 0))],
        out_shape=[_sds((4, r, cdim), BF16), _sds((r, cdim), F32)],
        scratch_shapes=[pltpu.VMEM((2, tr, cdim), F32), pltpu.SemaphoreType.DMA((2,))],
        compiler_params=_params("arbitrary", "arbitrary"),
    )(grad, from_sibling)


def _adamw(w, g, m, v):
    m = ADAM_B1 * m + (1.0 - ADAM_B1) * g
    v = ADAM_B2 * v + (1.0 - ADAM_B2) * (g * g)
    m_hat = m / (1.0 - ADAM_B1 ** ADAM_STEP)
    v_hat = v / (1.0 - ADAM_B2 ** ADAM_STEP)
    delta = -ADAM_LR * (m_hat / (jnp.sqrt(v_hat) + ADAM_EPS) + ADAM_WD * w)
    return delta, m, v


def _finish_weight(name, li, own, from_chips, w, m, v, stacked):
    r, cdim = own.shape
    tr = _row_tile(r, 256)
    if stacked is None:
        stacked = [lax.empty(w.shape, F32) for _ in range(4)]

    def body(own_ref, fc_ref, w_ref, m_ref, v_ref, *rest):
        g_out, d_out, m_out, v_out = rest[4:]
        g = own_ref[...] + fc_ref[0].astype(F32) + fc_ref[1].astype(F32) + fc_ref[2].astype(F32)
        delta, mn, vn = _adamw(w_ref[...], g, m_ref[...], v_ref[...])
        g_out[...] = g
        d_out[...] = delta
        m_out[...] = mn
        v_out[...] = vn

    tile = pl.BlockSpec((tr, cdim), lambda i: (i, 0))
    lay = pl.BlockSpec((None, tr, cdim), lambda i: (li, i, 0))
    return pl.pallas_call(
        body, name=name, grid=(r // tr,),
        in_specs=[tile, pl.BlockSpec((3, tr, cdim), lambda i: (0, i, 0)), lay, lay, lay] + [ANY] * 4,
        out_specs=[lay] * 4, out_shape=[_sds(w.shape, F32)] * 4,
        input_output_aliases={5: 0, 6: 1, 7: 2, 8: 3},
        compiler_params=_params("parallel"),
    )(own, from_chips, w, m, v, *stacked)


def _allgather_small(name, v, reduce):
    r = v.shape[0]

    def body(x_ref, out_ref, *rest):
        if reduce:
            sum_ref, send_sems, recv_sems, local_sem = rest
        else:
            send_sems, recv_sems, local_sem = rest
        x, y, c = _place()
        me, sib = (x, y, c), (x, y, 1 - c)
        chips = [(1 - x, y), (x, 1 - y), (1 - x, 1 - y)]

        def rows(px, py, pc):
            return out_ref.at[pl.ds(pl.multiple_of((4 * px + 2 * py + pc) * r, 8), r), :]

        def copy(k, block, to, src=None):
            return pltpu.make_async_remote_copy(
                src_ref=rows(*block) if src is None else src, dst_ref=rows(*block),
                send_sem=send_sems.at[k], recv_sem=recv_sems.at[k], device_id=to, device_id_type=MESH)

        mine = pltpu.make_async_copy(x_ref, rows(*me), local_sem)
        mine.start()
        first = [copy(0, me, sib, src=x_ref)]
        first += [copy(1 + j, me, (*chip, c), src=x_ref) for j, chip in enumerate(chips)]
        for cp in first:
            cp.start()
        passed = [copy(4 + j, (*chip, c), sib) for j, chip in enumerate(chips)]
        for j, chip in enumerate(chips):
            copy(1 + j, (*chip, c), me).wait_recv()
            passed[j].start()
        copy(0, sib, me).wait_recv()
        for j, chip in enumerate(chips):
            copy(4 + j, (*chip, 1 - c), me).wait_recv()
        for cp in first + passed:
            cp.wait_send()
        mine.wait()
        if reduce:
            tot = out_ref[0:r, :]
            for d in range(1, N_DEV):
                tot = tot + out_ref[d * r:(d + 1) * r, :]
            sum_ref[...] = tot

    vm = pl.BlockSpec(memory_space=pltpu.VMEM)
    outs = [_sds((N_DEV * r, 128), F32)] + ([_sds((r, 128), F32)] if reduce else [])
    res = pl.pallas_call(
        body, name=name,
        in_specs=[vm], out_specs=[vm] * len(outs), out_shape=outs,
        scratch_shapes=[pltpu.SemaphoreType.DMA((7,)), pltpu.SemaphoreType.DMA((7,)), pltpu.SemaphoreType.DMA],
        compiler_params=pltpu.CompilerParams(vmem_limit_bytes=VMEM_LIMIT_BYTES),
    )(v)
    return res


def _adamw_small(name, w, g, m, v):
    def body(w_ref, g_ref, m_ref, v_ref, d_out, m_out, v_out):
        delta, mn, vn = _adamw(w_ref[...], g_ref[...], m_ref[...], v_ref[...])
        d_out[...] = delta
        m_out[...] = mn
        v_out[...] = vn

    vm = pl.BlockSpec(memory_space=pltpu.VMEM)
    return pl.pallas_call(
        body, name=name, in_specs=[vm] * 4, out_specs=[vm] * 3, out_shape=[_sds(w.shape, F32)] * 3,
        compiler_params=pltpu.CompilerParams(vmem_limit_bytes=VMEM_LIMIT_BYTES),
    )(w, g, m, v)


def _pack(arrays):
    flat, layout, off = [], [], 0
    for a in arrays:
        flat.append(a.reshape(-1).astype(F32))
        layout.append((off, a.shape))
        off += a.size
    total = -(-off // 1024) * 1024
    if total > off:
        flat.append(jnp.zeros((total - off,), F32))
    return jnp.concatenate(flat).reshape(total // 128, 128), layout


def _unpack(packed, layout):
    flat = packed.reshape(-1)
    return [flat[off:off + math.prod(shape)].reshape(shape) for off, shape in layout]


_BIG = ("win", "wout", "wg", "wu", "wd", "wpg", "wpp")
_BIG_FULL = {"win": "w_in", "wout": "w_out", "wg": "w_gate", "wu": "w_up", "wd": "w_down",
             "wpg": "w_ple_gate", "wpp": "w_ple_proj"}
_SMALL_REPLICATED = ("norm_mix_g", "sgu_ln_g", "sgu_ln_b", "sgu_w", "sgu_b", "cf_conv_b", "cf_ln_g", "cf_ln_b",
                     "pool_w", "pool_scale", "norm_ffn_g", "norm_ple_g", "final_norm_g")
_SMALL_SHARDED = ("sc_conv_w", "cf_conv_w")
_WEIGHTS = ("norm_mix_g", "w_in", "sgu_ln_g", "sgu_ln_b", "sgu_w", "sgu_b", "sc_conv_w", "cf_conv_w", "cf_conv_b",
            "cf_ln_g", "cf_ln_b", "pool_w", "pool_scale", "w_out", "norm_ffn_g", "w_gate", "w_up", "w_down",
            "norm_ple_g", "w_ple_gate", "w_ple_proj", "final_norm_g")


def _pad_rows(a, rows):
    return jnp.concatenate([a, jnp.zeros((rows - a.shape[0],) + a.shape[1:], a.dtype)], axis=0)


def _mixer_params(li, W, sc_full, cf_full):
    return [W["sgu_ln_g"][li][:, None, :], W["sgu_ln_b"][li][:, None, :], W["sgu_w"][li], W["sgu_b"][li][:, :, None],
            _pad_rows(sc_full[li], 8), _pad_rows(cf_full[li], 32),
            W["cf_conv_b"][li][None, :], W["cf_ln_g"][li][None, :], W["cf_ln_b"][li][None, :],
            W["pool_w"][li], W["pool_scale"][li][None, :]]


def _step(W, M, V, x, p, loss_target):
    n_layers = W["w_in"].shape[0]
    h = x[0]
    target = loss_target[0]
    p_all = p[:, 0]
    xi, yi, ci = _place()
    blk = 4 * xi + 2 * yi + ci
    csh = W["sc_conv_w"].shape[2]
    turned = ("w_gate", "w_up")
    W, M, V = ({n: (jnp.swapaxes(a, 1, 2) if n in turned else a) for n, a in t.items()} for t in (W, M, V))

    packed, lay = _pack([W[n] for n in _SMALL_SHARDED])
    (taps,) = _allgather_small("gather_conv_taps", packed, reduce=False)
    per_dev = [_unpack(taps[d * packed.shape[0]:(d + 1) * packed.shape[0]], lay) for d in range(N_DEV)]
    sc_full = jnp.concatenate([pd[0] for pd in per_dev], axis=-1)
    cf_full = jnp.concatenate([pd[1] for pd in per_dev], axis=-1)

    d = h.shape[1]
    placed = [_place_shard("place_" + n, W[_BIG_FULL[n]]) for n in _BIG]
    lands = [[placed[w][li] for w in range(len(_BIG))] for li in range(n_layers)]

    groups = (("win",), ("wout", "wg", "wu"), ("wd", "wpg", "wpp"))
    order = [(li, gi) for li in range(n_layers) for gi in range(len(groups))]
    moving, tok = [], taps
    for li in range(n_layers):
        sems, bufs, tok = _gather_start(li, lands[li], tok)
        moving.append({"sems": sems, "bufs": bufs, "got": {}})
    passed = {}

    def pass_on(item, after):
        if item not in passed:
            li, gi = item
            idx = [_BIG.index(n) for n in groups[gi]]
            sems = [s_ for i in idx for s_ in moving[li]["sems"][8 * i:8 * i + 8]]
            tag = "%d_%s" % (li, groups[gi][0])
            passed[item] = (tag, sems) + tuple(_gather_forward(tag, [moving[li]["bufs"][i] for i in idx], sems, after))

    def fetcher(li):
        def fetch(name, after):
            got = moving[li]["got"]
            if name not in got:
                item = (li, next(gi for gi, g_ in enumerate(groups) if name in g_))
                pass_on(item, after)
                ahead = order.index(item) + 1
                if ahead < len(order) and order[ahead][0] >= 1:
                    pass_on(order[ahead], after)
                tag, sems, fwd_sems, bufs, _ = passed[item]
                for n, a in zip(groups[item[1]], _gather_wait(tag, bufs, sems, fwd_sems, after)):
                    got[n] = a.reshape(d, d) if n in ("wout", "wpg") else a
            return got[name]
        return fetch

    saved, gathered, smalls = [], [], []
    y1 = None
    for li in range(n_layers):
        sm = {"norm_mix_g": _after(W["norm_mix_g"][li][None, :], tok if li == 0 else None),
              "norm_ffn_g": W["norm_ffn_g"][li][None, :],
              "norm_ple_g": W["norm_ple_g"][li][None, :], "mixer": _mixer_params(li, W, sc_full, cf_full)}
        next_mix_g = W["norm_mix_g"][li + 1][None, :] if li + 1 < n_layers else None
        h, sv, y1 = _layer_fwd(li, h, p_all, fetcher(li), sm, y1, next_mix_g)
        saved.append(sv)
        gathered.append(moving[li]["got"])
        smalls.append(sm)
    loss, dh, d_final_g = _loss_head(h, W["final_norm_g"][None, :], target)

    big_out = {n: None for n in _BIG}
    small_grads = []

    def scatter_begin(li, tag, names, gr, after):
        sems, bufs, tok = _siblings_start(tag, [gr[n] for n in names], after)
        return {"li": li, "tag": tag, "names": names, "sems": sems, "bufs": bufs, "tok": tok}

    def scatter_middle(st, after):
        grads, from_sib = _siblings_wait(st["tag"], st["bufs"], st["sems"], after)
        parts = [_chip_partials("rs_sum_" + n, g_, s_) for n, g_, s_ in zip(st["names"], grads, from_sib)]
        st["own"] = [own for _, own in parts]
        st["sems"], st["bufs"], tok = _chips_start(st["tag"], [pb for pb, _ in parts], None)
        return tok

    def scatter_end(st, after):
        from_chips = _chips_wait(st["tag"], st["bufs"], st["sems"], after)
        for n, own, fc in zip(st["names"], st["own"], from_chips):
            full = _BIG_FULL[n]
            big_out[n] = _finish_weight("adamw_" + n, st["li"], own, fc, W[full], M[full], V[full], big_out[n])
        return big_out[st["names"][0]][0]

    order = _SMALL_REPLICATED + _SMALL_SHARDED

    def reduce_small(token):
        by_layer = small_grads[::-1]
        stacked = lambda fn: jnp.stack([fn(sg) for sg in by_layer])
        mix = lambda i: (lambda sg: sg["mixer"][i])
        grads_small = {
            "norm_mix_g": stacked(lambda sg: sg["norm_mix_g"][0]),
            "sgu_ln_g": stacked(mix(0))[:, :, 0, :], "sgu_ln_b": stacked(mix(1))[:, :, 0, :],
            "sgu_w": stacked(mix(2)), "sgu_b": stacked(mix(3))[:, :, :, 0],
            "sc_conv_w": stacked(mix(4))[:, :SHORT_K], "cf_conv_w": stacked(mix(5))[:, :CONF_K],
            "cf_conv_b": stacked(mix(6))[:, 0], "cf_ln_g": stacked(mix(7))[:, 0], "cf_ln_b": stacked(mix(8))[:, 0],
            "pool_w": stacked(mix(9)), "pool_scale": stacked(mix(10))[:, 0],
            "norm_ffn_g": stacked(lambda sg: sg["norm_ffn_g"][0]),
            "norm_ple_g": stacked(lambda sg: sg["norm_ple_g"][0]),
            "final_norm_g": d_final_g[0],
        }
        packed, lay = _pack([grads_small[n] for n in order] + [_after(loss, token)])
        _, summed = _allgather_small("allreduce_small", packed, reduce=True)
        return dict(zip(order + ("loss",), _unpack(summed, lay))), summed

    early, late = ("wpp", "wpg", "wd", "wg", "wu"), ("wout", "win")
    pending, tok = None, None
    for li in reversed(range(n_layers)):
        hooks, first = {}, {}
        if pending is not None:
            hooks["after_down_dx"] = functools.partial(scatter_middle, pending)
        if li == 0:
            def begin_early(gr, after, first=first):
                first.update(scatter_begin(0, "0_early", early, gr, after))
                return first["tok"]

            hooks["after_ffn_grads"] = begin_early
            hooks["before_ffn_norm"] = lambda after, first=first: scatter_middle(first, after)
        dh, gr, sg = _layer_bwd(li, dh, p_all, gathered[li], smalls[li], saved[li], tok, hooks)
        small_grads.append(sg)
        done = None if pending is None else scatter_end(pending, dh)
        if li > 0:
            pending = scatter_begin(li, str(li), _BIG, gr, done)
            tok = pending["tok"]
        else:
            last = scatter_begin(0, "0_late", late, gr, done)
            total, summed = reduce_small(last["tok"])
            scatter_middle(last, summed)
            scatter_end(first, last["own"][0])
            scatter_end(last, None)
    loss_all = total["loss"][0, 0]

    out_g, out_d, out_m, out_v = {}, {}, {}, {}
    pw, lay_r = _pack([W[n] for n in _SMALL_REPLICATED])
    pg, _ = _pack([total[n] for n in _SMALL_REPLICATED])
    pm, _ = _pack([M[n] for n in _SMALL_REPLICATED])
    pv, _ = _pack([V[n] for n in _SMALL_REPLICATED])
    dd, mm, vv = _adamw_small("adamw_small", pw, pg, pm, pv)
    for n, a, b, c_ in zip(_SMALL_REPLICATED, _unpack(dd, lay_r), _unpack(mm, lay_r), _unpack(vv, lay_r)):
        out_g[n], out_d[n], out_m[n], out_v[n] = total[n], a, b, c_
    pick = (jnp.arange(N_DEV) == blk).astype(F32)[None, None, :, None]
    mine = {n: jnp.sum(total[n].reshape(total[n].shape[:2] + (N_DEV, csh)) * pick, axis=2) for n in _SMALL_SHARDED}
    pw, lay_s = _pack([W[n] for n in _SMALL_SHARDED])
    pg, _ = _pack([mine[n] for n in _SMALL_SHARDED])
    pm, _ = _pack([M[n] for n in _SMALL_SHARDED])
    pv, _ = _pack([V[n] for n in _SMALL_SHARDED])
    dd, mm, vv = _adamw_small("adamw_conv_taps", pw, pg, pm, pv)
    for n, a, b, c_ in zip(_SMALL_SHARDED, _unpack(dd, lay_s), _unpack(mm, lay_s), _unpack(vv, lay_s)):
        out_g[n], out_d[n], out_m[n], out_v[n] = mine[n], a, b, c_
    for n in _BIG:
        for k, dst in enumerate((out_g, out_d, out_m, out_v)):
            full = _BIG_FULL[n]
            dst[full] = jnp.swapaxes(big_out[n][k], 1, 2) if full in turned else big_out[n][k]

    return (loss_all, dh[None], *[out_g[n] for n in _WEIGHTS], *[out_d[n] for n in _WEIGHTS],
            *[out_m[n] for n in _WEIGHTS], *[out_v[n] for n in _WEIGHTS])


def kernel(x, p, norm_mix_g, w_in, sgu_ln_g, sgu_ln_b, sgu_w, sgu_b, sc_conv_w, cf_conv_w, cf_conv_b, cf_ln_g, cf_ln_b, pool_w, pool_scale, w_out, norm_ffn_g, w_gate, w_up, w_down, norm_ple_g, w_ple_gate, w_ple_proj, final_norm_g, loss_target, m_norm_mix_g, m_w_in, m_sgu_ln_g, m_sgu_ln_b, m_sgu_w, m_sgu_b, m_sc_conv_w, m_cf_conv_w, m_cf_conv_b, m_cf_ln_g, m_cf_ln_b, m_pool_w, m_pool_scale, m_w_out, m_norm_ffn_g, m_w_gate, m_w_up, m_w_down, m_norm_ple_g, m_w_ple_gate, m_w_ple_proj, m_final_norm_g, v_norm_mix_g, v_w_in, v_sgu_ln_g, v_sgu_ln_b, v_sgu_w, v_sgu_b, v_sc_conv_w, v_cf_conv_w, v_cf_conv_b, v_cf_ln_g, v_cf_ln_b, v_pool_w, v_pool_scale, v_w_out, v_norm_ffn_g, v_w_gate, v_w_up, v_w_down, v_norm_ple_g, v_w_ple_gate, v_w_ple_proj, v_final_norm_g):
    given = dict(locals())
    W = {n: given[n] for n in _WEIGHTS}
    M = {n: given["m_" + n] for n in _WEIGHTS}
    V = {n: given["v_" + n] for n in _WEIGHTS}
    return _step(W, M, V, x, p, loss_target)
```

```python
import functools
import math

import jax
import jax.numpy as jnp
from jax import lax
from jax.experimental import pallas as pl
from jax.experimental.pallas import tpu as pltpu

F32 = jnp.float32
BF16 = jnp.bfloat16
EPS = 1e-6
HEAD = 128
GROUP = 4 * HEAD
HALO = 32
SHORT_K = 3
CONF_K = 31
POOL_WINDOWS = (2, 4, 8, 16)
N_DEV = 8
MESH = pl.DeviceIdType.MESH
VMEM_LIMIT_BYTES = 56 * 1024 * 1024

ADAM_LR = 0.001
ADAM_B1 = 0.9
ADAM_B2 = 0.999
ADAM_EPS = 1e-08
ADAM_WD = 0.01
ADAM_STEP = 10

ANY = pl.BlockSpec(memory_space=pl.ANY)


def _params(*sem):
    return pltpu.CompilerParams(dimension_semantics=sem, vmem_limit_bytes=VMEM_LIMIT_BYTES)


def _rms(x, g):
    return x * lax.rsqrt(jnp.mean(x * x, axis=-1, keepdims=True) + EPS) * g


def _ln(x, g, b):
    mu = jnp.mean(x, axis=-1, keepdims=True)
    xc = x - mu
    var = jnp.mean(xc * xc, axis=-1, keepdims=True)
    return xc * lax.rsqrt(var + EPS) * g + b


def _bdot(a, b, dims=(((1,), (0,)), ((), ()))):
    return lax.dot_general(a.astype(BF16), b.astype(BF16), dims, preferred_element_type=F32)


def _sgu_piece(zu, zv, lg, lb, w, b):
    u = jax.nn.gelu(zu)
    v = _ln(jax.nn.gelu(zv), lg, lb)
    row = lax.broadcasted_iota(jnp.int32, w.shape, 0)
    col = lax.broadcasted_iota(jnp.int32, w.shape, 1)
    wm = jnp.where(row >= col, w, 0.0)
    return u * (_bdot(wm, v) + b)


def _conf_post(c, g, b):
    return jax.nn.silu(_ln(c, g, b))


def _pool_count(first_pos, rows, w):
    pos = first_pos + lax.broadcasted_iota(jnp.int32, (rows, 1), 0) + 1
    return jnp.minimum(pos, w).astype(F32)


_DIMS = {
    "nn": (((1,), (0,)), ((), ())),
    "nt": (((1,), (1,)), ((), ())),
    "tn": (((0,), (0,)), ((), ())),
}


def _tiles(name, grid, ins, in_specs, outs, out_specs, compute, summed=(), deps=()):
    ni = len(ins)
    nd = len(deps)

    def body(*refs):
        vals = compute(*refs[:ni])
        first = functools.reduce(jnp.logical_and, [pl.program_id(a) == 0 for a in range(len(grid))])
        for idx, (r, v) in enumerate(zip(refs[ni + nd:], vals)):
            if idx in summed:
                @pl.when(first)
                def _(r=r):
                    r[...] = jnp.zeros_like(r)

                r[...] += v
            else:
                r[...] = v.astype(r.dtype)

    sem = ("arbitrary" if summed else "parallel",) * len(grid)
    return pl.pallas_call(
        body, name=name, grid=grid, in_specs=list(in_specs) + [ANY] * nd, out_specs=list(out_specs),
        out_shape=list(outs), compiler_params=_params(*sem),
    )(*ins, *deps)


def _resident(shape):
    return pl.BlockSpec(shape, lambda *_: (0,) * len(shape), pipeline_mode=pl.Buffered(1))


def _sds(shape, dtype):
    return jax.ShapeDtypeStruct(tuple(shape), dtype)


def _tile(n, want):
    t = min(n, want)
    assert n % t == 0, (n, want)
    return t


def _rms_fwd(name, h, g):
    s, d = h.shape
    tm = _tile(s, 512)

    def body(h_ref, g_ref, y_ref):
        y_ref[...] = _rms(h_ref[...], g_ref[...]).astype(BF16)

    return pl.pallas_call(
        body, name=name, grid=(s // tm,),
        in_specs=[pl.BlockSpec((tm, d), lambda i: (i, 0)), pl.BlockSpec((1, d), lambda i: (0, 0))],
        out_specs=pl.BlockSpec((tm, d), lambda i: (i, 0)),
        out_shape=_sds((s, d), BF16),
        compiler_params=_params("parallel"),
    )(h, g)


def _rms_back(h, g, dy, dh_in):
    inv = lax.rsqrt(jnp.mean(h * h, axis=-1, keepdims=True) + EPS)
    xhat = h * inv
    dxhat = dy * g
    dh = inv * (dxhat - xhat * jnp.mean(dxhat * xhat, axis=-1, keepdims=True))
    return dh_in + dh, jnp.sum(dy * xhat, axis=0, keepdims=True)


def _loss_head(h, g, target):
    s, d = h.shape
    tm = _tile(s, 256)

    def body(h_ref, g_ref, t_ref, loss_ref, dh_ref, dg_ref):
        y, vjp = jax.vjp(_rms, h_ref[...], g_ref[...])
        err = y - t_ref[...]
        dh, dg = vjp(err * (1.0 / d))
        dh_ref[...] = dh
        per_token = jnp.mean(err * err, axis=-1, keepdims=True)
        part = 0.5 * jnp.sum(per_token, axis=0, keepdims=True)

        @pl.when(pl.program_id(0) == 0)
        def _():
            dg_ref[...] = jnp.zeros_like(dg_ref)
            loss_ref[...] = jnp.zeros_like(loss_ref)

        dg_ref[...] += dg
        loss_ref[...] += part

    tok = pl.BlockSpec((tm, d), lambda i: (i, 0))
    vec = pl.BlockSpec((1, d), lambda i: (0, 0))
    return pl.pallas_call(
        body, name="loss_head", grid=(s // tm,),
        in_specs=[tok, vec, tok],
        out_specs=[pl.BlockSpec((1, 1), lambda i: (0, 0)), tok, vec],
        out_shape=[_sds((1, 1), F32), _sds((s, d), F32), _sds((1, d), F32)],
        compiler_params=_params("arbitrary"),
    )(h, g, target)


_U, _V, _HB, _BG, _CG, _CA, _CGT, _PD = (GROUP * i for i in range(8))


def _cols(c0, w=GROUP):
    return slice(c0, c0 + w)


_LANE_CHUNKS = tuple(_cols(i * HEAD, HEAD) for i in range(GROUP // HEAD))


def _shifted_copies(sh_ref, ext_ref, length):
    for r in range(1, 8):
        sh_ref[r - 1, 0:length, :] = ext_ref[pl.ds(r, length), :]


def _window(ext_ref, sh_ref, off, rows, cc):
    q, r = divmod(off, 8)
    if sh_ref is None or r == 0:
        return ext_ref[pl.ds(off, rows), cc]
    return sh_ref[r - 1, pl.ds(8 * q, rows), cc]


def _taps(ext_ref, w_ref, nk, start, rows, cc, flip=False, sh_ref=None):
    acc = None
    for k in range(nk):
        kw = nk - 1 - k if flip else k
        term = w_ref[kw:kw + 1, cc] * _window(ext_ref, sh_ref, start + k, rows, cc)
        acc = term if acc is None else acc + term
    return acc


def _mixer_param_specs():
    full = lambda *shape: pl.BlockSpec(shape, lambda i: (0,) * len(shape))
    return [
        full(4, 1, HEAD), full(4, 1, HEAD), full(4, HEAD, HEAD), full(4, HEAD, 1),
        full(8, GROUP), full(32, GROUP), full(1, GROUP), full(1, GROUP), full(1, GROUP),
        full(4, HEAD, HEAD), full(1, GROUP),
    ]


def _mixer_fwd(name, z, prm):
    s = z.shape[0]
    t = _tile(s, 256)
    hb = t // HALO

    def body(zp_ref, zm_ref, lg_ref, lb_ref, sw_ref, sb_ref, scw_ref, cfw_ref, cfb_ref, cg_ref, cb_ref,
             pw_ref, ps_ref, o_ref, ext_ref, sh_ref):
        i = pl.program_id(0)
        keep = (i > 0).astype(F32)
        main = pl.ds(HALO, t)

        for n in range(t // HEAD):
            rows = slice(n * HEAD, (n + 1) * HEAD)
            for hh in range(4):
                cu = _cols(_U + hh * HEAD, HEAD)
                cv = _cols(_V + hh * HEAD, HEAD)
                o_ref[rows, _cols(hh * HEAD, HEAD)] = _sgu_piece(
                    zm_ref[rows, cu], zm_ref[rows, cv], lg_ref[hh], lb_ref[hh], sw_ref[hh], sb_ref[hh]
                ).astype(BF16)

        ext_ref[0:HALO, :] = zp_ref[:, _cols(_CG)] * zp_ref[:, _cols(_HB)] * keep
        ext_ref[main, :] = zm_ref[:, _cols(_CG)] * zm_ref[:, _cols(_HB)]
        y = jnp.zeros((t, GROUP), F32)
        for k in range(SHORT_K):
            y = y + scw_ref[k:k + 1, :] * ext_ref[pl.ds(HALO - (SHORT_K - 1) + k, t), :]
        o_ref[:, _cols(GROUP)] = (zm_ref[:, _cols(_BG)] * y).astype(BF16)

        ext_ref[0:HALO, :] = zp_ref[:, _cols(_CA)] * jax.nn.sigmoid(zp_ref[:, _cols(_CGT)]) * keep
        ext_ref[main, :] = zm_ref[:, _cols(_CA)] * jax.nn.sigmoid(zm_ref[:, _cols(_CGT)])
        _shifted_copies(sh_ref, ext_ref, HALO + t - 8)
        c = jnp.concatenate(
            [cfb_ref[:, cc] + _taps(ext_ref, cfw_ref, CONF_K, HALO - (CONF_K - 1), t, cc, sh_ref=sh_ref)
             for cc in _LANE_CHUNKS], axis=1)
        o_ref[:, _cols(2 * GROUP)] = _conf_post(c, cg_ref[...], cb_ref[...]).astype(BF16)

        ext_ref[0:HALO, :] = zp_ref[:, _cols(_PD)] * keep
        ext_ref[main, :] = zm_ref[:, _cols(_PD)]
        for gi, w in enumerate(POOL_WINDOWS):
            cc = _cols(gi * HEAD, HEAD)
            acc = ext_ref[main, cc]
            for j in range(1, w):
                acc = acc + ext_ref[pl.ds(HALO - j, t), cc]
            q = acc / _pool_count(i * t, t, w) - ext_ref[main, cc]
            o_ref[:, _cols(3 * GROUP + gi * HEAD, HEAD)] = (_bdot(q, pw_ref[gi]) * ps_ref[:, cc]).astype(BF16)

    return pl.pallas_call(
        body, name=name, grid=(s // t,),
        in_specs=[pl.BlockSpec((HALO, 8 * GROUP), lambda i: (jnp.maximum(i * hb - 1, 0), 0)),
                  pl.BlockSpec((t, 8 * GROUP), lambda i: (i, 0)),
                  *_mixer_param_specs()],
        out_specs=pl.BlockSpec((t, 4 * GROUP), lambda i: (i, 0)),
        out_shape=_sds((s, 4 * GROUP), BF16),
        scratch_shapes=[pltpu.VMEM((HALO + t, GROUP), F32), pltpu.VMEM((7, HALO + t - 8, GROUP), F32)],
        compiler_params=_params("parallel"),
    )(z, z, *prm)


def _mixer_bwd(name, z, dmo, prm):
    s = z.shape[0]
    t = _tile(s, 256)
    hb = t // HALO
    nt = s // t
    last_halo = s // HALO - 1

    def body(zp_ref, zm_ref, zn_ref, dm_ref, dn_ref,
             lg_ref, lb_ref, sw_ref, sb_ref, scw_ref, cfw_ref, cfb_ref, cg_ref, cb_ref, pw_ref, ps_ref,
             dz_ref, dlg_ref, dlb_ref, dsw_ref, dsb_ref, dscw_ref, dcfw_ref, dcfb_ref, dcg_ref, dcb_ref,
             dpw_ref, dps_ref, extp_ref, extn_ref, extc_ref, sh_ref):
        i = pl.program_id(0)
        keep_prev = (i > 0).astype(F32)
        keep_next = (i < nt - 1).astype(F32)
        main = pl.ds(HALO, t)

        @pl.when(i == 0)
        def _():
            for r in (dlg_ref, dlb_ref, dsw_ref, dsb_ref, dscw_ref, dcfw_ref, dcfb_ref, dcg_ref, dcb_ref,
                      dpw_ref, dps_ref):
                r[...] = jnp.zeros_like(r)

        def rowsum(v):
            return jnp.sum(v, axis=0, keepdims=True)

        for n in range(t // HEAD):
            rows = slice(n * HEAD, (n + 1) * HEAD)
            for hh in range(4):
                cu = _cols(_U + hh * HEAD, HEAD)
                cv = _cols(_V + hh * HEAD, HEAD)
                _, vjp = jax.vjp(_sgu_piece, zm_ref[rows, cu], zm_ref[rows, cv],
                                 lg_ref[hh], lb_ref[hh], sw_ref[hh], sb_ref[hh])
                dzu, dzv, dlg, dlb, dsw, dsb = vjp(dm_ref[rows, _cols(hh * HEAD, HEAD)].astype(F32))
                dz_ref[rows, cu] = dzu.astype(BF16)
                dz_ref[rows, cv] = dzv.astype(BF16)
                dlg_ref[hh] += dlg
                dlb_ref[hh] += dlb
                dsw_ref[hh] += dsw
                dsb_ref[hh] += dsb

        extp_ref[0:HALO, :] = zp_ref[:, _cols(_CG)] * zp_ref[:, _cols(_HB)] * keep_prev
        extp_ref[main, :] = zm_ref[:, _cols(_CG)] * zm_ref[:, _cols(_HB)]
        dob = dm_ref[:, _cols(GROUP)].astype(F32)
        dy = dob * zm_ref[:, _cols(_BG)]
        extn_ref[0:t, :] = dy
        extn_ref[t:t + HALO, :] = dn_ref[:, _cols(GROUP)].astype(F32) * zn_ref[:, _cols(_BG)] * keep_next
        for cc in _LANE_CHUNKS:
            at = lambda c0: _cols(c0 + cc.start, HEAD)
            y = _taps(extp_ref, scw_ref, SHORT_K, HALO - (SHORT_K - 1), t, cc)
            dx = _taps(extn_ref, scw_ref, SHORT_K, 0, t, cc, flip=True)
            dy_c = extn_ref[0:t, cc]
            for k in range(SHORT_K):
                dscw_ref[k:k + 1, cc] += rowsum(dy_c * extp_ref[pl.ds(HALO - (SHORT_K - 1) + k, t), cc])
            dz_ref[:, at(_BG)] = (dm_ref[:, at(GROUP)].astype(F32) * y).astype(BF16)
            dz_ref[:, at(_CG)] = (dx * zm_ref[:, at(_HB)]).astype(BF16)
            dz_ref[:, at(_HB)] = (dx * zm_ref[:, at(_CG)]).astype(BF16)

        extc_ref[0:HALO, :] = zp_ref[:, _cols(_CA)] * jax.nn.sigmoid(zp_ref[:, _cols(_CGT)]) * keep_prev
        extc_ref[main, :] = zm_ref[:, _cols(_CA)] * jax.nn.sigmoid(zm_ref[:, _cols(_CGT)])
        extc_ref[HALO + t:HALO + t + HALO, :] = zn_ref[:, _cols(_CA)] * jax.nn.sigmoid(zn_ref[:, _cols(_CGT)])
        _shifted_copies(sh_ref, extc_ref, HALO + t + HALO - 8)
        conv = lambda start, rows: jnp.concatenate(
            [cfb_ref[:, cc] + _taps(extc_ref, cfw_ref, CONF_K, start, rows, cc, sh_ref=sh_ref) for cc in _LANE_CHUNKS],
            axis=1)
        c_main = conv(HALO - (CONF_K - 1), t)
        c_next = conv(HALO + t - (CONF_K - 1), HALO)
        _, vjp = jax.vjp(_conf_post, c_main, cg_ref[...], cb_ref[...])
        dc, dcg, dcb = vjp(dm_ref[:, _cols(2 * GROUP)].astype(F32))
        dcg_ref[...] += dcg
        dcb_ref[...] += dcb
        dcfb_ref[...] += rowsum(dc)
        _, vjp_next = jax.vjp(lambda cv: _conf_post(cv, cg_ref[...], cb_ref[...]), c_next)
        (dc_next,) = vjp_next(dn_ref[:, _cols(2 * GROUP)].astype(F32) * keep_next)
        extn_ref[0:t, :] = dc
        extn_ref[t:t + HALO, :] = dc_next
        for cc in _LANE_CHUNKS:
            dc_c = extn_ref[0:t, cc]
            for k in range(CONF_K):
                dcfw_ref[k:k + 1, cc] += rowsum(dc_c * _window(extc_ref, sh_ref, HALO - (CONF_K - 1) + k, t, cc))
        _shifted_copies(sh_ref, extn_ref, t + HALO - 8)
        for cc in _LANE_CHUNKS:
            at = lambda c0: _cols(c0 + cc.start, HEAD)
            dhc = _taps(extn_ref, cfw_ref, CONF_K, 0, t, cc, flip=True, sh_ref=sh_ref)
            sg = jax.nn.sigmoid(zm_ref[:, at(_CGT)])
            dz_ref[:, at(_CA)] = (dhc * sg).astype(BF16)
            dz_ref[:, at(_CGT)] = (dhc * zm_ref[:, at(_CA)] * sg * (1.0 - sg)).astype(BF16)

        extp_ref[0:HALO, :] = zp_ref[:, _cols(_PD)] * keep_prev
        extp_ref[main, :] = zm_ref[:, _cols(_PD)]
        for gi, w in enumerate(POOL_WINDOWS):
            cc = _cols(gi * HEAD, HEAD)
            oc = _cols(3 * GROUP + gi * HEAD, HEAD)
            acc = extp_ref[main, cc]
            for j in range(1, w):
                acc = acc + extp_ref[pl.ds(HALO - j, t), cc]
            count = _pool_count(i * t, t, w)
            q = acc / count - extp_ref[main, cc]
            dod = dm_ref[:, oc].astype(F32)
            dps_ref[:, cc] += rowsum(dod * _bdot(q, pw_ref[gi]))
            ds = dod * ps_ref[:, cc]
            dpw_ref[gi] += _bdot(q, ds, _DIMS["tn"])
            dq = _bdot(ds, pw_ref[gi], _DIMS["nt"])
            ds_next = dn_ref[:, oc].astype(F32) * ps_ref[:, cc] * keep_next
            dq_next = _bdot(ds_next, pw_ref[gi], _DIMS["nt"])
            extn_ref[0:t, cc] = dq / count
            extn_ref[t:t + HALO, cc] = dq_next * (1.0 / w)
            back = extn_ref[0:t, cc]
            for j in range(1, w):
                back = back + extn_ref[pl.ds(j, t), cc]
            dz_ref[:, _cols(_PD + gi * HEAD, HEAD)] = (back - dq).astype(BF16)

    full = lambda *shape: pl.BlockSpec(shape, lambda i: (0,) * len(shape))
    grad_specs = [full(4, 1, HEAD), full(4, 1, HEAD), full(4, HEAD, HEAD), full(4, HEAD, 1),
                  full(8, GROUP), full(32, GROUP), full(1, GROUP), full(1, GROUP), full(1, GROUP),
                  full(4, HEAD, HEAD), full(1, GROUP)]
    grad_shapes = [_sds(sp.block_shape, F32) for sp in grad_specs]
    nxt = lambda i: (jnp.minimum((i + 1) * hb, last_halo), 0)
    return pl.pallas_call(
        body, name=name, grid=(nt,),
        in_specs=[pl.BlockSpec((HALO, 8 * GROUP), lambda i: (jnp.maximum(i * hb - 1, 0), 0)),
                  pl.BlockSpec((t, 8 * GROUP), lambda i: (i, 0)),
                  pl.BlockSpec((HALO, 8 * GROUP), nxt),
                  pl.BlockSpec((t, 4 * GROUP), lambda i: (i, 0)),
                  pl.BlockSpec((HALO, 4 * GROUP), nxt),
                  *_mixer_param_specs()],
        out_specs=[pl.BlockSpec((t, 8 * GROUP), lambda i: (i, 0)), *grad_specs],
        out_shape=[_sds((s, 8 * GROUP), BF16), *grad_shapes],
        scratch_shapes=[pltpu.VMEM((HALO + t, GROUP), F32), pltpu.VMEM((t + HALO, GROUP), F32),
                        pltpu.VMEM((HALO + t + HALO, GROUP), F32), pltpu.VMEM((7, HALO + t + HALO - 8, GROUP), F32)],
        compiler_params=_params("arbitrary"),
    )(z, z, z, dmo, dmo, *prm)


def _dot(a, b, kind="nn"):
    return _bdot(a, b, _DIMS[kind])


def _chain(terms, kind):
    acc = None
    for a, b in terms:
        p = _dot(a, b, kind)
        acc = p if acc is None else acc + p
    return acc


def _after(g, token):
    return g if token is None else g + token[0:1, 0:1]


def _layer_fwd(li, h, p_all, fetch, sm, y1=None, next_mix_g=None):
    s, d = h.shape
    nb = N_DEV
    pd = p_all.shape[2]
    sv = {"h": h}
    rows = lambda t, w: pl.BlockSpec((t, w), lambda i: (i, 0))
    vec = pl.BlockSpec((1, d), lambda i: (0, 0))

    if y1 is None:
        y1 = _rms_fwd("rms_mix", h, sm["norm_mix_g"])
    tm = _tile(s, 1024)
    (z,) = _tiles(
        "w_in_fwd", (s // tm, nb),
        [y1, fetch("win", y1)],
        [pl.BlockSpec((tm, d), lambda i, j: (i, 0)), pl.BlockSpec((None, d, GROUP), lambda i, j: (j, 0, 0))],
        [_sds((s, nb * GROUP), F32)], [pl.BlockSpec((tm, GROUP), lambda i, j: (i, j))],
        lambda a, w: (_dot(a[...], w[...]),))
    mo = _mixer_fwd("mixer_fwd", z, sm["mixer"])
    tm = _tile(s, 512)
    def mix_out(a, w, hv, gv):
        h2v = hv[...] + _dot(a[...], w[...])
        return h2v, _rms(h2v, gv[...])

    h2, y2 = _tiles(
        "w_out_fwd", (s // tm,),
        [mo, fetch("wout", z), h, sm["norm_ffn_g"]], [rows(tm, d), _resident((d, d)), rows(tm, d), vec],
        [_sds((s, d), F32), _sds((s, d), BF16)], [rows(tm, d), rows(tm, d)], mix_out)
    sv.update(y1=y1, z=z, mo=mo, h2=h2)

    gate_pre, up_pre, hmid = _ffn_up("ffn_up_fwd", y2, fetch("wg", z), fetch("wu", z))
    wd = fetch("wd", hmid)
    f8 = wd.shape[1]
    tm, tn = _tile(s, 512), _tile(d, 1024)
    (h3,) = _tiles(
        "w_down_fwd", (d // tn, s // tm),
        [hmid, wd, h2],
        [pl.BlockSpec((nb, tm, f8), lambda j, i: (0, i, 0)), pl.BlockSpec((nb, f8, tn), lambda j, i: (0, 0, j)),
         pl.BlockSpec((tm, tn), lambda j, i: (i, j))],
        [_sds((s, d), F32)], [pl.BlockSpec((tm, tn), lambda j, i: (i, j))],
        lambda a, w, hv: (hv[...] + _chain([(a[k], w[k]) for k in range(nb)], "nn"),))
    sv.update(y2=y2, gate_pre=gate_pre, up_pre=up_pre, hmid=hmid, h3=h3)

    y3 = _rms_fwd("rms_ple", h3, sm["norm_ple_g"])
    tm = _tile(s, 1024)
    (pp,) = _tiles(
        "w_ple_proj_fwd", (s // tm, nb),
        [p_all, fetch("wpp", hmid)],
        [pl.BlockSpec((None, tm, pd), lambda i, j: (li, i, 0)), pl.BlockSpec((None, pd, pd), lambda i, j: (j, 0, 0))],
        [_sds((s, d), BF16)], [pl.BlockSpec((tm, pd), lambda i, j: (i, j))],
        lambda a, w: (_dot(a[...], w[...]),))
    tm = _tile(s, 256)

    def ple(a, w, hv, ppv, *next_g):
        pg = _dot(a[...], w[...])
        h4v = hv[...] + jax.nn.sigmoid(pg) * ppv[...].astype(F32)
        return (h4v, pg) + tuple(_rms(h4v, gv[...]) for gv in next_g)

    more = [] if next_mix_g is None else [next_mix_g]
    h4, pg_pre, *y1_next = _tiles(
        "w_ple_gate_fwd", (s // tm,),
        [y3, fetch("wpg", hmid), h3, pp] + more,
        [rows(tm, d), _resident((d, d)), rows(tm, d), rows(tm, d)] + [vec] * len(more),
        [_sds((s, d), F32), _sds((s, d), BF16)] + [_sds((s, d), BF16)] * len(more),
        [rows(tm, d)] * (2 + len(more)), ple)
    sv.update(y3=y3, pp=pp, pg_pre=pg_pre)
    return h4, sv, (y1_next[0] if y1_next else None)


def _ffn_up(name, y2, wg, wu):
    s, d = y2.shape
    nb, f8, _ = wg.shape
    tm = _tile(s, 1024)

    def compute(y_ref, wg_ref, wu_ref):
        yv = y_ref[...]
        g = _dot(yv, wg_ref[...], "nt")
        u = _dot(yv, wu_ref[...], "nt")
        return g, u, jax.nn.silu(g) * u

    wspec = pl.BlockSpec((None, f8, d), lambda i, j: (j, 0, 0))
    ospec = pl.BlockSpec((None, tm, f8), lambda i, j: (j, i, 0))
    return _tiles(name, (s // tm, nb), [y2, wg, wu], [pl.BlockSpec((tm, d), lambda i, j: (i, 0)), wspec, wspec],
                  [_sds((nb, s, f8), BF16)] * 3, [ospec] * 3, compute)


def _swiglu_bwd(dm, g, u):
    sg = jax.nn.sigmoid(g)
    silu = g * sg
    return dm * u * (sg + silu * (1.0 - sg)), dm * silu


def _layer_bwd(li, dh, p_all, gw, sm, sv, start_token=None, hooks=None):
    hooks = hooks or {}
    run = lambda name, *a: hooks[name](*a) if name in hooks else None
    s, d = dh.shape
    nb, f8, _ = gw["wg"].shape
    pd = p_all.shape[2]
    gr, sg = {}, {}
    rows = lambda t, w: pl.BlockSpec((t, w), lambda i: (i, 0))
    vec = pl.BlockSpec((1, d), lambda i: (0, 0))
    tw = _tile(d, 1024)
    whole = lambda w: pl.BlockSpec((s, w), lambda i, j: (0, i))
    whole_j = lambda w: pl.BlockSpec((s, w), lambda i, j: (0, j))
    tn_dot = lambda a, b: (_dot(a[...], b[...], "tn"),)

    tm = _tile(s, 256)

    def ple_dx(dv, pg, ppv, w, hv, gv):
        dhv = dv[...]
        gate = jax.nn.sigmoid(pg[...].astype(F32))
        dpg = (dhv * ppv[...].astype(F32) * gate * (1.0 - gate)).astype(BF16)
        dh3, dg = _rms_back(hv[...], gv[...], _dot(dpg, w[...], "nt"), dhv)
        return dh3, dh3, dg, dpg, dhv * gate

    dh3, dh3_16, sg["norm_ple_g"], dpg, dpp = _tiles(
        "w_ple_gate_dx", (s // tm,),
        [dh, sv["pg_pre"], sv["pp"], gw["wpg"], sv["h3"], _after(sm["norm_ple_g"], start_token)],
        [rows(tm, d), rows(tm, d), rows(tm, d), _resident((d, d)), rows(tm, d), vec],
        [_sds((s, d), F32), _sds((s, d), BF16), _sds((1, d), F32), _sds((s, d), BF16), _sds((s, d), BF16)],
        [rows(tm, d), rows(tm, d), vec, rows(tm, d), rows(tm, d)], ple_dx, summed=(2,))
    (gr["wpp"],) = _tiles(
        "w_ple_proj_dw", (nb,),
        [p_all, dpp], [pl.BlockSpec((None, s, pd), lambda j: (li, 0, 0)), pl.BlockSpec((s, pd), lambda j: (0, j))],
        [_sds((nb, pd, pd), F32)], [pl.BlockSpec((None, pd, pd), lambda j: (j, 0, 0))], tn_dot)
    (gr["wpg"],) = _tiles(
        "w_ple_gate_dw", (d // tw, d // tw), [sv["y3"], dpg], [whole(tw), whole_j(tw)],
        [_sds((d, d), F32)], [pl.BlockSpec((tw, tw), lambda i, j: (i, j))], tn_dot)
    gr["wpg"] = gr["wpg"].reshape(nb, d // nb, d)

    tm = _tile(s, 1024)
    blk_rows = pl.BlockSpec((None, tm, f8), lambda i, j: (j, i, 0))
    dgate, dup = _tiles(
        "w_down_dx", (s // tm, nb),
        [dh3_16, gw["wd"], sv["gate_pre"], sv["up_pre"]],
        [pl.BlockSpec((tm, d), lambda i, j: (i, 0)), pl.BlockSpec((None, f8, d), lambda i, j: (j, 0, 0)), blk_rows, blk_rows],
        [_sds((nb, s, f8), BF16)] * 2, [blk_rows] * 2,
        lambda a, w, g, u: _swiglu_bwd(_dot(a[...], w[...], "nt"), g[...].astype(F32), u[...].astype(F32)))
    (gr["wd"],) = _tiles(
        "w_down_dw", (nb, d // tw),
        [sv["hmid"], dh3_16], [pl.BlockSpec((None, s, f8), lambda i, j: (i, 0, 0)), whole_j(tw)],
        [_sds((nb, f8, d), F32)], [pl.BlockSpec((None, f8, tw), lambda i, j: (i, 0, j))], tn_dot)
    mid_token = run("after_down_dx", dgate)
    for nm, dact in (("wg", dgate), ("wu", dup)):
        (gr[nm],) = _tiles(
            "w_" + {"wg": "gate", "wu": "up"}[nm] + "_dw", (d // tw, nb),
            [dact, sv["y2"]], [pl.BlockSpec((None, s, f8), lambda i, j: (j, 0, 0)), whole(tw)],
            [_sds((nb, f8, d), F32)], [pl.BlockSpec((None, f8, tw), lambda i, j: (j, 0, i))], tn_dot)
    ffn_token = run("after_ffn_grads", gr, gr["wu"])
    tm, tn = _tile(s, 512), _tile(d, 1024)
    act_spec = pl.BlockSpec((nb, tm, f8), lambda j, i: (0, i, 0))
    wt_spec = pl.BlockSpec((nb, f8, tn), lambda j, i: (0, 0, j))
    out_spec = pl.BlockSpec((tm, tn), lambda j, i: (i, j))
    (dy2,) = _tiles(
        "w_gate_dx", (d // tn, s // tm), [dgate, gw["wg"]], [act_spec, wt_spec],
        [_sds((s, d), F32)], [out_spec],
        lambda a, w: (_chain([(a[k], w[k]) for k in range(nb)], "nn"),),
        deps=[tk for tk in (mid_token, ffn_token) if tk is not None])
    (dy2,) = _tiles(
        "w_up_dx", (d // tn, s // tm), [dup, gw["wu"], dy2], [act_spec, wt_spec, out_spec],
        [_sds((s, d), F32)], [out_spec],
        lambda a, w, prev: (prev[...] + _chain([(a[k], w[k]) for k in range(nb)], "nn"),))
    g_ffn = _after(sm["norm_ffn_g"], run("before_ffn_norm", dy2))

    tm = _tile(s, 256)

    def out_dx(dy, hv, gv, dv, w):
        dh2v, dg = _rms_back(hv[...], gv[...], dy[...], dv[...])
        return dh2v, dh2v, dg, _dot(dh2v, w[...], "nt")

    dh2, dh2_16, sg["norm_ffn_g"], dmo = _tiles(
        "w_out_dx", (s // tm,), [dy2, sv["h2"], g_ffn, dh3, gw["wout"]],
        [rows(tm, d), rows(tm, d), vec, rows(tm, d), _resident((d, d))],
        [_sds((s, d), F32), _sds((s, d), BF16), _sds((1, d), F32), _sds((s, d), BF16)],
        [rows(tm, d), rows(tm, d), vec, rows(tm, d)], out_dx, summed=(2,))
    (gr["wout"],) = _tiles(
        "w_out_dw", (d // tw, d // tw), [sv["mo"], dh2_16], [whole(tw), whole_j(tw)],
        [_sds((d, d), F32)], [pl.BlockSpec((tw, tw), lambda i, j: (i, j))], tn_dot)
    gr["wout"] = gr["wout"].reshape(nb, d // nb, d)
    dz, *mix_grads = _mixer_bwd("mixer_bwd", sv["z"], dmo, sm["mixer"])
    sg["mixer"] = mix_grads
    (gr["win"],) = _tiles(
        "w_in_dw", (d // tw, nb), [sv["y1"], dz], [whole(tw), pl.BlockSpec((s, GROUP), lambda i, j: (0, j))],
        [_sds((nb, d, GROUP), F32)], [pl.BlockSpec((None, tw, GROUP), lambda i, j: (j, i, 0))], tn_dot)
    in_token = run("after_in_dw", gr)
    tm = _tile(s, 256)

    def in_dx(a, w, hv, gv, dv):
        dy1 = _chain([(a[:, k * GROUP:(k + 1) * GROUP], w[k]) for k in range(nb)], "nt")
        return _rms_back(hv[...], gv[...], dy1, dv[...])

    dh_in, sg["norm_mix_g"] = _tiles(
        "w_in_dx", (s // tm,),
        [dz, gw["win"], sv["h"], sm["norm_mix_g"], dh2],
        [rows(tm, nb * GROUP), _resident((nb, d, GROUP)), rows(tm, d), vec, rows(tm, d)],
        [_sds((s, d), F32), _sds((1, d), F32)], [rows(tm, d), vec], in_dx, summed=(1,),
        deps=[] if in_token is None else [in_token])
    return dh_in, gr, sg


def _place():
    return lax.axis_index("x"), lax.axis_index("y"), lax.axis_index("c")


_HBM = pl.BlockSpec(memory_space=pltpu.HBM)
_SEM = pl.BlockSpec(memory_space=pltpu.SEMAPHORE)


def _split_call(name, bufs, old_sems, n_new, after, body, want_token):
    nb, no = len(bufs), len(old_sems)
    extra = [] if after is None else [after]

    def kbody(*refs):
        new = refs[nb + no + len(extra):nb + no + len(extra) + n_new]
        body(refs[:nb], refs[nb:nb + no], new)
        if want_token:
            refs[-1][...] = jnp.zeros_like(refs[-1])

    outs = pl.pallas_call(
        kbody, name=name,
        out_shape=tuple([pltpu.SemaphoreType.DMA(())] * n_new + [pltpu.HBM(b.shape, b.dtype) for b in bufs]
                        + ([_sds((8, 128), F32)] if want_token else [])),
        in_specs=[_HBM] * nb + [_SEM] * no + [ANY] * len(extra),
        out_specs=tuple([_SEM] * n_new + [_HBM] * nb
                        + ([pl.BlockSpec(memory_space=pltpu.VMEM)] if want_token else [])),
        input_output_aliases={i: n_new + i for i in range(nb)},
        compiler_params=pltpu.CompilerParams(has_side_effects=pltpu.SideEffectType.DATAFLOW_SIDE_EFFECTING),
    )(*[pltpu.with_memory_space_constraint(b, pltpu.HBM) for b in bufs], *old_sems, *extra)
    return list(outs[:n_new]), list(outs[n_new:n_new + nb]), (outs[-1] if want_token else None)


def _remote(ref_src, ref_dst, send_sem, recv_sem, to):
    return pltpu.make_async_remote_copy(src_ref=ref_src, dst_ref=ref_dst, send_sem=send_sem, recv_sem=recv_sem,
                                        device_id=to, device_id_type=MESH)


def _place_shard(name, w):
    n_layers, r, cdim = w.shape
    tr = _row_tile(r, 256)
    nt = r // tr

    def body(w_ref, *rest):
        outs, buf, sem = rest[:n_layers], rest[n_layers], rest[n_layers + 1]
        i = pl.program_id(0)
        x, y, c = _place()
        slot = i % 2

        def writes(step, sl):
            rows = pl.ds(pl.multiple_of(step * tr, 16), tr)
            return [pltpu.make_async_copy(buf.at[sl, l], outs[l].at[4 * x + 2 * y + c, rows, :], sem.at[sl, l])
                    for l in range(n_layers)]

        @pl.when(i >= 2)
        def _():
            for cp in writes(i - 2, slot):
                cp.wait()

        buf[slot] = w_ref[...].astype(BF16)
        for cp in writes(i, slot):
            cp.start()

        @pl.when(i == nt - 1)
        def _():
            for cp in writes(i, slot):
                cp.wait()
            if nt >= 2:
                for cp in writes(i - 1, 1 - slot):
                    cp.wait()

    return pl.pallas_call(
        body, name=name, grid=(nt,),
        in_specs=[pl.BlockSpec((n_layers, tr, cdim), lambda i: (0, i, 0))], out_specs=[ANY] * n_layers,
        out_shape=[_sds((N_DEV, r, cdim), BF16)] * n_layers,
        scratch_shapes=[pltpu.VMEM((2, n_layers, tr, cdim), BF16), pltpu.SemaphoreType.DMA((2, n_layers))],
        compiler_params=_params("arbitrary"),
    )(w)


def _gather_start(li, lands, after):
    nw = len(lands)

    def body(bufs, _, new):
        x, y, c = _place()
        chips = [(1 - x, y), (x, 1 - y), (1 - x, 1 - y)]
        for w in range(nw):
            mine = bufs[w].at[4 * x + 2 * y + c]
            _remote(mine, mine, new[8 * w], new[8 * w + 4], (x, y, 1 - c)).start()
            for j, (px, py) in enumerate(chips):
                _remote(mine, mine, new[8 * w + 1 + j], new[8 * w + 5 + j], (px, py, c)).start()

    return _split_call("gather_start_l%s" % li, lands, [], 8 * nw, after, body, True)


def _gather_forward(li, lands, sems, after):
    nw = len(lands)
    arrivals = [sems[8 * w + 5 + j] for w in range(nw) for j in range(3)]

    def body(bufs, old, new):
        x, y, c = _place()
        chips = [(1 - x, y), (x, 1 - y), (1 - x, 1 - y)]
        for j, (px, py) in enumerate(chips):
            for w in range(nw):
                got = bufs[w].at[4 * px + 2 * py + c]
                _remote(got, got, new[6 * w + j], old[3 * w + j], (px, py, c)).wait_recv()
                _remote(got, got, new[6 * w + j], new[6 * w + 3 + j], (x, y, 1 - c)).start()

    return _split_call("gather_forward_l%s" % li, lands, arrivals, 6 * nw, after, body, True)


def _gather_wait(li, lands, sems, fwd_sems, after):
    nw = len(lands)
    first = [sems[8 * w + k] for w in range(nw) for k in range(5)]

    def body(bufs, old, _):
        x, y, c = _place()
        sib = (x, y, 1 - c)
        chips = [(1 - x, y), (x, 1 - y), (1 - x, 1 - y)]
        n1 = 5 * nw
        for w in range(nw):
            mine = bufs[w].at[4 * x + 2 * y + c]
            theirs = bufs[w].at[4 * x + 2 * y + 1 - c]
            _remote(theirs, theirs, old[5 * w], old[5 * w + 4], sib).wait_recv()
            for k in range(4):
                _remote(mine, mine, old[5 * w + k], old[5 * w + 4], sib).wait_send()
            for j, (px, py) in enumerate(chips):
                sent = bufs[w].at[4 * px + 2 * py + c]
                got = bufs[w].at[4 * px + 2 * py + 1 - c]
                _remote(sent, sent, old[n1 + 6 * w + j], old[n1 + 6 * w + 3 + j], sib).wait_send()
                _remote(got, got, old[n1 + 6 * w + j], old[n1 + 6 * w + 3 + j], sib).wait_recv()

    _, lands, _ = _split_call("gather_wait_l%s" % li, lands, first + list(fwd_sems), 0, after, body, False)
    return lands


def _siblings_start(li, grads, after):
    nw = len(grads)
    lands = [lax.empty((4,) + g.shape[1:], g.dtype) for g in grads]

    def body(bufs, _, new):
        x, y, c = _place()
        for w in range(nw):
            for q in range(4):
                _remote(bufs[w].at[2 * q + (1 - c)], bufs[nw + w].at[q], new[8 * w + q], new[8 * w + 4 + q],
                        (x, y, 1 - c)).start()

    return _split_call("rs_siblings_start_l%s" % li, list(grads) + lands, [], 8 * nw, after, body, True)


def _siblings_wait(li, bufs, sems, after):
    nw = len(bufs) // 2

    def body(refs, old, _):
        x, y, c = _place()
        for w in range(nw):
            for q in range(4):
                _remote(refs[w].at[2 * q + (1 - c)], refs[nw + w].at[q], old[8 * w + q], old[8 * w + 4 + q],
                        (x, y, 1 - c)).wait()

    _, bufs, _ = _split_call("rs_siblings_wait_l%s" % li, bufs, sems, 0, after, body, False)
    return bufs[:nw], bufs[nw:]


_FLIPS = ((1, 0), (0, 1), (1, 1))


def _chips_copies(refs, nw, sems):
    x, y, c = _place()
    for w in range(nw):
        for r, (fx, fy) in enumerate(_FLIPS):
            px = 1 - x if fx else x
            py = 1 - y if fy else y
            yield _remote(refs[w].at[2 * px + py], refs[nw + w].at[r], sems[6 * w + r], sems[6 * w + 3 + r], (px, py, c))


def _chips_start(li, partials, after):
    nw = len(partials)
    lands = [lax.empty((3,) + a.shape[1:], a.dtype) for a in partials]

    def body(bufs, _, new):
        for cp in _chips_copies(bufs, nw, new):
            cp.start()

    return _split_call("rs_chips_start_l%s" % li, list(partials) + lands, [], 6 * nw, after, body, True)


def _chips_wait(li, bufs, sems, after):
    nw = len(bufs) // 2

    def body(refs, old, _):
        for cp in _chips_copies(refs, nw, old):
            cp.wait()

    _, bufs, _ = _split_call("rs_chips_wait_l%s" % li, bufs, sems, 0, after, body, False)
    return bufs[nw:]


def _row_tile(n, want):
    best = None
    for t in range(16, min(n, want) + 1, 16):
        if n % t == 0:
            best = t
    assert best is not None, n
    return best


def _chip_partials(name, grad, from_sibling):
    _, r, cdim = grad.shape
    tr = _row_tile(r, 256)
    nt = r // tr
    steps = 4 * nt

    def body(g_hbm, s_ref, pb_ref, own_ref, buf, sem):
        i, q = pl.program_id(0), pl.program_id(1)
        x, y, c = _place()
        n = 4 * i + q
        slot = n % 2

        def fetch(step, into):
            rows = pl.ds(pl.multiple_of((step // 4) * tr, 16), tr)
            return pltpu.make_async_copy(g_hbm.at[2 * (step % 4) + c, rows, :], buf.at[into], sem.at[into])

        @pl.when(n == 0)
        def _():
            fetch(0, 0).start()

        @pl.when(n + 1 < steps)
        def _():
            fetch(n + 1, 1 - slot).start()

        fetch(n, slot).wait()
        tot = buf[slot] + s_ref[...]
        pb_ref[...] = tot.astype(BF16)

        @pl.when(q == 2 * x + y)
        def _():
            own_ref[...] = tot

    return pl.pallas_call(
        body, name=name, grid=(nt, 4),
        in_specs=[ANY, pl.BlockSpec((None, tr, cdim), lambda i, q: (q, i, 0))],
        out_specs=[pl.BlockSpec((None, tr, cdim), lambda i, q: (q, i, 0)), pl.BlockSpec((tr, cdim), lambda i, q: (i, 0))],
        out_shape=[_sds((4, r, cdim), BF16), _sds((r, cdim), F32)],
        scratch_shapes=[pltpu.VMEM((2, tr, cdim), F32), pltpu.SemaphoreType.DMA((2,))],
        compiler_params=_params("arbitrary", "arbitrary"),
    )(grad, from_sibling)


def _adamw(w, g, m, v):
    m = ADAM_B1 * m + (1.0 - ADAM_B1) * g
    v = ADAM_B2 * v + (1.0 - ADAM_B2) * (g * g)
    m_hat = m / (1.0 - ADAM_B1 ** ADAM_STEP)
    v_hat = v / (1.0 - ADAM_B2 ** ADAM_STEP)
    delta = -ADAM_LR * (m_hat / (jnp.sqrt(v_hat) + ADAM_EPS) + ADAM_WD * w)
    return delta, m, v


def _finish_weight(name, li, own, from_chips, w, m, v, stacked):
    r, cdim = own.shape
    tr = _row_tile(r, 256)
    if stacked is None:
        stacked = [lax.empty(w.shape, F32) for _ in range(4)]

    def body(own_ref, fc_ref, w_ref, m_ref, v_ref, *rest):
        g_out, d_out, m_out, v_out = rest[4:]
        g = own_ref[...] + fc_ref[0].astype(F32) + fc_ref[1].astype(F32) + fc_ref[2].astype(F32)
        delta, mn, vn = _adamw(w_ref[...], g, m_ref[...], v_ref[...])
        g_out[...] = g
        d_out[...] = delta
        m_out[...] = mn
        v_out[...] = vn

    tile = pl.BlockSpec((tr, cdim), lambda i: (i, 0))
    lay = pl.BlockSpec((None, tr, cdim), lambda i: (li, i, 0))
    return pl.pallas_call(
        body, name=name, grid=(r // tr,),
        in_specs=[tile, pl.BlockSpec((3, tr, cdim), lambda i: (0, i, 0)), lay, lay, lay] + [ANY] * 4,
        out_specs=[lay] * 4, out_shape=[_sds(w.shape, F32)] * 4,
        input_output_aliases={5: 0, 6: 1, 7: 2, 8: 3},
        compiler_params=_params("parallel"),
    )(own, from_chips, w, m, v, *stacked)


def _allgather_small(name, v, reduce):
    r = v.shape[0]

    def body(x_ref, out_ref, *rest):
        if reduce:
            sum_ref, send_sems, recv_sems, local_sem = rest
        else:
            send_sems, recv_sems, local_sem = rest
        x, y, c = _place()
        me, sib = (x, y, c), (x, y, 1 - c)
        chips = [(1 - x, y), (x, 1 - y), (1 - x, 1 - y)]

        def rows(px, py, pc):
            return out_ref.at[pl.ds(pl.multiple_of((4 * px + 2 * py + pc) * r, 8), r), :]

        def copy(k, block, to, src=None):
            return pltpu.make_async_remote_copy(
                src_ref=rows(*block) if src is None else src, dst_ref=rows(*block),
                send_sem=send_sems.at[k], recv_sem=recv_sems.at[k], device_id=to, device_id_type=MESH)

        mine = pltpu.make_async_copy(x_ref, rows(*me), local_sem)
        mine.start()
        first = [copy(0, me, sib, src=x_ref)]
        first += [copy(1 + j, me, (*chip, c), src=x_ref) for j, chip in enumerate(chips)]
        for cp in first:
            cp.start()
        passed = [copy(4 + j, (*chip, c), sib) for j, chip in enumerate(chips)]
        for j, chip in enumerate(chips):
            copy(1 + j, (*chip, c), me).wait_recv()
            passed[j].start()
        copy(0, sib, me).wait_recv()
        for j, chip in enumerate(chips):
            copy(4 + j, (*chip, 1 - c), me).wait_recv()
        for cp in first + passed:
            cp.wait_send()
        mine.wait()
        if reduce:
            tot = out_ref[0:r, :]
            for d in range(1, N_DEV):
                tot = tot + out_ref[d * r:(d + 1) * r, :]
            sum_ref[...] = tot

    vm = pl.BlockSpec(memory_space=pltpu.VMEM)
    outs = [_sds((N_DEV * r, 128), F32)] + ([_sds((r, 128), F32)] if reduce else [])
    res = pl.pallas_call(
        body, name=name,
        in_specs=[vm], out_specs=[vm] * len(outs), out_shape=outs,
        scratch_shapes=[pltpu.SemaphoreType.DMA((7,)), pltpu.SemaphoreType.DMA((7,)), pltpu.SemaphoreType.DMA],
        compiler_params=pltpu.CompilerParams(vmem_limit_bytes=VMEM_LIMIT_BYTES),
    )(v)
    return res


def _adamw_small(name, w, g, m, v):
    def body(w_ref, g_ref, m_ref, v_ref, d_out, m_out, v_out):
        delta, mn, vn = _adamw(w_ref[...], g_ref[...], m_ref[...], v_ref[...])
        d_out[...] = delta
        m_out[...] = mn
        v_out[...] = vn

    vm = pl.BlockSpec(memory_space=pltpu.VMEM)
    return pl.pallas_call(
        body, name=name, in_specs=[vm] * 4, out_specs=[vm] * 3, out_shape=[_sds(w.shape, F32)] * 3,
        compiler_params=pltpu.CompilerParams(vmem_limit_bytes=VMEM_LIMIT_BYTES),
    )(w, g, m, v)


def _pack(arrays):
    flat, layout, off = [], [], 0
    for a in arrays:
        flat.append(a.reshape(-1).astype(F32))
        layout.append((off, a.shape))
        off += a.size
    total = -(-off // 1024) * 1024
    if total > off:
        flat.append(jnp.zeros((total - off,), F32))
    return jnp.concatenate(flat).reshape(total // 128, 128), layout


def _unpack(packed, layout):
    flat = packed.reshape(-1)
    return [flat[off:off + math.prod(shape)].reshape(shape) for off, shape in layout]


_BIG = ("win", "wout", "wg", "wu", "wd", "wpg", "wpp")
_BIG_FULL = {"win": "w_in", "wout": "w_out", "wg": "w_gate", "wu": "w_up", "wd": "w_down",
             "wpg": "w_ple_gate", "wpp": "w_ple_proj"}
_SMALL_REPLICATED = ("norm_mix_g", "sgu_ln_g", "sgu_ln_b", "sgu_w", "sgu_b", "cf_conv_b", "cf_ln_g", "cf_ln_b",
                     "pool_w", "pool_scale", "norm_ffn_g", "norm_ple_g", "final_norm_g")
_SMALL_SHARDED = ("sc_conv_w", "cf_conv_w")
_WEIGHTS = ("norm_mix_g", "w_in", "sgu_ln_g", "sgu_ln_b", "sgu_w", "sgu_b", "sc_conv_w", "cf_conv_w", "cf_conv_b",
            "cf_ln_g", "cf_ln_b", "pool_w", "pool_scale", "w_out", "norm_ffn_g", "w_gate", "w_up", "w_down",
            "norm_ple_g", "w_ple_gate", "w_ple_proj", "final_norm_g")


def _pad_rows(a, rows):
    return jnp.concatenate([a, jnp.zeros((rows - a.shape[0],) + a.shape[1:], a.dtype)], axis=0)


def _mixer_params(li, W, sc_full, cf_full):
    return [W["sgu_ln_g"][li][:, None, :], W["sgu_ln_b"][li][:, None, :], W["sgu_w"][li], W["sgu_b"][li][:, :, None],
            _pad_rows(sc_full[li], 8), _pad_rows(cf_full[li], 32),
            W["cf_conv_b"][li][None, :], W["cf_ln_g"][li][None, :], W["cf_ln_b"][li][None, :],
            W["pool_w"][li], W["pool_scale"][li][None, :]]


def _step(W, M, V, x, p, loss_target):
    n_layers = W["w_in"].shape[0]
    h = x[0]
    target = loss_target[0]
    p_all = p[:, 0]
    xi, yi, ci = _place()
    blk = 4 * xi + 2 * yi + ci
    csh = W["sc_conv_w"].shape[2]
    turned = ("w_gate", "w_up")
    W, M, V = ({n: (jnp.swapaxes(a, 1, 2) if n in turned else a) for n, a in t.items()} for t in (W, M, V))

    packed, lay = _pack([W[n] for n in _SMALL_SHARDED])
    (taps,) = _allgather_small("gather_conv_taps", packed, reduce=False)
    per_dev = [_unpack(taps[d * packed.shape[0]:(d + 1) * packed.shape[0]], lay) for d in range(N_DEV)]
    sc_full = jnp.concatenate([pd[0] for pd in per_dev], axis=-1)
    cf_full = jnp.concatenate([pd[1] for pd in per_dev], axis=-1)

    d = h.shape[1]
    placed = [_place_shard("place_" + n, W[_BIG_FULL[n]]) for n in _BIG]
    lands = [[placed[w][li] for w in range(len(_BIG))] for li in range(n_layers)]

    groups = (("win",), ("wout", "wg", "wu"), ("wd", "wpg", "wpp"))
    order = [(li, gi) for li in range(n_layers) for gi in range(len(groups))]
    moving, tok = [], taps
    for li in range(n_layers):
        sems, bufs, tok = _gather_start(li, lands[li], tok)
        moving.append({"sems": sems, "bufs": bufs, "got": {}})
    passed = {}

    def pass_on(item, after):
        if item not in passed:
            li, gi = item
            idx = [_BIG.index(n) for n in groups[gi]]
            sems = [s_ for i in idx for s_ in moving[li]["sems"][8 * i:8 * i + 8]]
            tag = "%d_%s" % (li, groups[gi][0])
            passed[item] = (tag, sems) + tuple(_gather_forward(tag, [moving[li]["bufs"][i] for i in idx], sems, after))

    def fetcher(li):
        def fetch(name, after):
            got = moving[li]["got"]
            if name not in got:
                item = (li, next(gi for gi, g_ in enumerate(groups) if name in g_))
                pass_on(item, after)
                ahead = order.index(item) + 1
                if ahead < len(order) and order[ahead][0] >= 1:
                    pass_on(order[ahead], after)
                tag, sems, fwd_sems, bufs, _ = passed[item]
                for n, a in zip(groups[item[1]], _gather_wait(tag, bufs, sems, fwd_sems, after)):
                    got[n] = a.reshape(d, d) if n in ("wout", "wpg") else a
            return got[name]
        return fetch

    saved, gathered, smalls = [], [], []
    y1 = None
    for li in range(n_layers):
        sm = {"norm_mix_g": _after(W["norm_mix_g"][li][None, :], tok if li == 0 else None),
              "norm_ffn_g": W["norm_ffn_g"][li][None, :],
              "norm_ple_g": W["norm_ple_g"][li][None, :], "mixer": _mixer_params(li, W, sc_full, cf_full)}
        next_mix_g = W["norm_mix_g"][li + 1][None, :] if li + 1 < n_layers else None
        h, sv, y1 = _layer_fwd(li, h, p_all, fetcher(li), sm, y1, next_mix_g)
        saved.append(sv)
        gathered.append(moving[li]["got"])
        smalls.append(sm)
    loss, dh, d_final_g = _loss_head(h, W["final_norm_g"][None, :], target)

    big_out = {n: None for n in _BIG}
    small_grads = []

    def scatter_begin(li, tag, names, gr, after):
        sems, bufs, tok = _siblings_start(tag, [gr[n] for n in names], after)
        return {"li": li, "tag": tag, "names": names, "sems": sems, "bufs": bufs, "tok": tok}

    def scatter_middle(st, after):
        grads, from_sib = _siblings_wait(st["tag"], st["bufs"], st["sems"], after)
        parts = [_chip_partials("rs_sum_" + n, g_, s_) for n, g_, s_ in zip(st["names"], grads, from_sib)]
        st["own"] = [own for _, own in parts]
        st["sems"], st["bufs"], tok = _chips_start(st["tag"], [pb for pb, _ in parts], None)
        return tok

    def scatter_end(st, after):
        from_chips = _chips_wait(st["tag"], st["bufs"], st["sems"], after)
        for n, own, fc in zip(st["names"], st["own"], from_chips):
            full = _BIG_FULL[n]
            big_out[n] = _finish_weight("adamw_" + n, st["li"], own, fc, W[full], M[full], V[full], big_out[n])
        return big_out[st["names"][0]][0]

    order = _SMALL_REPLICATED + _SMALL_SHARDED

    def reduce_small(token):
        by_layer = small_grads[::-1]
        stacked = lambda fn: jnp.stack([fn(sg) for sg in by_layer])
        mix = lambda i: (lambda sg: sg["mixer"][i])
        grads_small = {
            "norm_mix_g": stacked(lambda sg: sg["norm_mix_g"][0]),
            "sgu_ln_g": stacked(mix(0))[:, :, 0, :], "sgu_ln_b": stacked(mix(1))[:, :, 0, :],
            "sgu_w": stacked(mix(2)), "sgu_b": stacked(mix(3))[:, :, :, 0],
            "sc_conv_w": stacked(mix(4))[:, :SHORT_K], "cf_conv_w": stacked(mix(5))[:, :CONF_K],
            "cf_conv_b": stacked(mix(6))[:, 0], "cf_ln_g": stacked(mix(7))[:, 0], "cf_ln_b": stacked(mix(8))[:, 0],
            "pool_w": stacked(mix(9)), "pool_scale": stacked(mix(10))[:, 0],
            "norm_ffn_g": stacked(lambda sg: sg["norm_ffn_g"][0]),
            "norm_ple_g": stacked(lambda sg: sg["norm_ple_g"][0]),
            "final_norm_g": d_final_g[0],
        }
        packed, lay = _pack([grads_small[n] for n in order] + [_after(loss, token)])
        _, summed = _allgather_small("allreduce_small", packed, reduce=True)
        return dict(zip(order + ("loss",), _unpack(summed, lay))), summed

    early, late = ("wpp", "wpg", "wd", "wg", "wu"), ("wout", "win")
    pending, tok = None, None
    for li in reversed(range(n_layers)):
        hooks, first, last = {}, {}, {}
        if pending is not None:
            hooks["after_down_dx"] = functools.partial(scatter_middle, pending)
        if li == 0:
            def begin_early(gr, after, first=first):
                first.update(scatter_begin(0, "0_early", early, gr, after))
                return first["tok"]

            def begin_late(gr, last=last):
                last.update(scatter_begin(0, "0_late", late, gr, None))
                return last["tok"]

            hooks["after_ffn_grads"] = begin_early
            hooks["before_ffn_norm"] = lambda after, first=first: scatter_middle(first, after)
            hooks["after_in_dw"] = begin_late
        dh, gr, sg = _layer_bwd(li, dh, p_all, gathered[li], smalls[li], saved[li], tok, hooks)
        small_grads.append(sg)
        done = None if pending is None else scatter_end(pending, dh)
        if li > 0:
            pending = scatter_begin(li, str(li), _BIG, gr, done)
            tok = pending["tok"]
        else:
            total, summed = reduce_small(last["tok"])
            scatter_middle(last, summed)
            scatter_end(first, last["own"][0])
            scatter_end(last, None)
    loss_all = total["loss"][0, 0]

    out_g, out_d, out_m, out_v = {}, {}, {}, {}
    pw, lay_r = _pack([W[n] for n in _SMALL_REPLICATED])
    pg, _ = _pack([total[n] for n in _SMALL_REPLICATED])
    pm, _ = _pack([M[n] for n in _SMALL_REPLICATED])
    pv, _ = _pack([V[n] for n in _SMALL_REPLICATED])
    dd, mm, vv = _adamw_small("adamw_small", pw, pg, pm, pv)
    for n, a, b, c_ in zip(_SMALL_REPLICATED, _unpack(dd, lay_r), _unpack(mm, lay_r), _unpack(vv, lay_r)):
        out_g[n], out_d[n], out_m[n], out_v[n] = total[n], a, b, c_
    pick = (jnp.arange(N_DEV) == blk).astype(F32)[None, None, :, None]
    mine = {n: jnp.sum(total[n].reshape(total[n].shape[:2] + (N_DEV, csh)) * pick, axis=2) for n in _SMALL_SHARDED}
    pw, lay_s = _pack([W[n] for n in _SMALL_SHARDED])
    pg, _ = _pack([mine[n] for n in _SMALL_SHARDED])
    pm, _ = _pack([M[n] for n in _SMALL_SHARDED])
    pv, _ = _pack([V[n] for n in _SMALL_SHARDED])
    dd, mm, vv = _adamw_small("adamw_conv_taps", pw, pg, pm, pv)
    for n, a, b, c_ in zip(_SMALL_SHARDED, _unpack(dd, lay_s), _unpack(mm, lay_s), _unpack(vv, lay_s)):
        out_g[n], out_d[n], out_m[n], out_v[n] = mine[n], a, b, c_
    for n in _BIG:
        for k, dst in enumerate((out_g, out_d, out_m, out_v)):
            full = _BIG_FULL[n]
            dst[full] = jnp.swapaxes(big_out[n][k], 1, 2) if full in turned else big_out[n][k]

    return (loss_all, dh[None], *[out_g[n] for n in _WEIGHTS], *[out_d[n] for n in _WEIGHTS],
            *[out_m[n] for n in _WEIGHTS], *[out_v[n] for n in _WEIGHTS])


def kernel(x, p, norm_mix_g, w_in, sgu_ln_g, sgu_ln_b, sgu_w, sgu_b, sc_conv_w, cf_conv_w, cf_conv_b, cf_ln_g, cf_ln_b, pool_w, pool_scale, w_out, norm_ffn_g, w_gate, w_up, w_down, norm_ple_g, w_ple_gate, w_ple_proj, final_norm_g, loss_target, m_norm_mix_g, m_w_in, m_sgu_ln_g, m_sgu_ln_b, m_sgu_w, m_sgu_b, m_sc_conv_w, m_cf_conv_w, m_cf_conv_b, m_cf_ln_g, m_cf_ln_b, m_pool_w, m_pool_scale, m_w_out, m_norm_ffn_g, m_w_gate, m_w_up, m_w_down, m_norm_ple_g, m_w_ple_gate, m_w_ple_proj, m_final_norm_g, v_norm_mix_g, v_w_in, v_sgu_ln_g, v_sgu_ln_b, v_sgu_w, v_sgu_b, v_sc_conv_w, v_cf_conv_w, v_cf_conv_b, v_cf_ln_g, v_cf_ln_b, v_pool_w, v_pool_scale, v_w_out, v_norm_ffn_g, v_w_gate, v_w_up, v_w_down, v_norm_ple_g, v_w_ple_gate, v_w_ple_proj, v_final_norm_g):
    given = dict(locals())
    W = {n: given[n] for n in _WEIGHTS}
    M = {n: given["m_" + n] for n in _WEIGHTS}
    V = {n: given["v_" + n] for n in _WEIGHTS}
    return _step(W, M, V, x, p, loss_target)
```

```python
import functools
import math

import jax
import jax.numpy as jnp
from jax import lax
from jax.experimental import pallas as pl
from jax.experimental.pallas import tpu as pltpu

F32 = jnp.float32
BF16 = jnp.bfloat16
EPS = 1e-6
HEAD = 128
GROUP = 4 * HEAD
HALO = 32
SHORT_K = 3
CONF_K = 31
POOL_WINDOWS = (2, 4, 8, 16)
N_DEV = 8
MESH = pl.DeviceIdType.MESH
VMEM_LIMIT_BYTES = 56 * 1024 * 1024

ADAM_LR = 0.001
ADAM_B1 = 0.9
ADAM_B2 = 0.999
ADAM_EPS = 1e-08
ADAM_WD = 0.01
ADAM_STEP = 10

ANY = pl.BlockSpec(memory_space=pl.ANY)


def _params(*sem):
    return pltpu.CompilerParams(dimension_semantics=sem, vmem_limit_bytes=VMEM_LIMIT_BYTES)


def _rms(x, g):
    return x * lax.rsqrt(jnp.mean(x * x, axis=-1, keepdims=True) + EPS) * g


def _ln(x, g, b):
    mu = jnp.mean(x, axis=-1, keepdims=True)
    xc = x - mu
    var = jnp.mean(xc * xc, axis=-1, keepdims=True)
    return xc * lax.rsqrt(var + EPS) * g + b


def _bdot(a, b, dims=(((1,), (0,)), ((), ()))):
    return lax.dot_general(a.astype(BF16), b.astype(BF16), dims, preferred_element_type=F32)


def _sgu_piece(zu, zv, lg, lb, w, b):
    u = jax.nn.gelu(zu)
    v = _ln(jax.nn.gelu(zv), lg, lb)
    row = lax.broadcasted_iota(jnp.int32, w.shape, 0)
    col = lax.broadcasted_iota(jnp.int32, w.shape, 1)
    wm = jnp.where(row >= col, w, 0.0)
    return u * (_bdot(wm, v) + b)


def _conf_post(c, g, b):
    return jax.nn.silu(_ln(c, g, b))


def _pool_count(first_pos, rows, w):
    pos = first_pos + lax.broadcasted_iota(jnp.int32, (rows, 1), 0) + 1
    return jnp.minimum(pos, w).astype(F32)


_DIMS = {
    "nn": (((1,), (0,)), ((), ())),
    "nt": (((1,), (1,)), ((), ())),
    "tn": (((0,), (0,)), ((), ())),
}


def _tiles(name, grid, ins, in_specs, outs, out_specs, compute, summed=(), deps=()):
    ni = len(ins)
    nd = len(deps)

    def body(*refs):
        vals = compute(*refs[:ni])
        first = functools.reduce(jnp.logical_and, [pl.program_id(a) == 0 for a in range(len(grid))])
        for idx, (r, v) in enumerate(zip(refs[ni + nd:], vals)):
            if idx in summed:
                @pl.when(first)
                def _(r=r):
                    r[...] = jnp.zeros_like(r)

                r[...] += v
            else:
                r[...] = v.astype(r.dtype)

    sem = ("arbitrary" if summed else "parallel",) * len(grid)
    return pl.pallas_call(
        body, name=name, grid=grid, in_specs=list(in_specs) + [ANY] * nd, out_specs=list(out_specs),
        out_shape=list(outs), compiler_params=_params(*sem),
    )(*ins, *deps)


def _resident(shape):
    return pl.BlockSpec(shape, lambda *_: (0,) * len(shape), pipeline_mode=pl.Buffered(1))


def _sds(shape, dtype):
    return jax.ShapeDtypeStruct(tuple(shape), dtype)


def _tile(n, want):
    t = min(n, want)
    assert n % t == 0, (n, want)
    return t


def _rms_fwd(name, h, g):
    s, d = h.shape
    tm = _tile(s, 512)

    def body(h_ref, g_ref, y_ref):
        y_ref[...] = _rms(h_ref[...], g_ref[...]).astype(BF16)

    return pl.pallas_call(
        body, name=name, grid=(s // tm,),
        in_specs=[pl.BlockSpec((tm, d), lambda i: (i, 0)), pl.BlockSpec((1, d), lambda i: (0, 0))],
        out_specs=pl.BlockSpec((tm, d), lambda i: (i, 0)),
        out_shape=_sds((s, d), BF16),
        compiler_params=_params("parallel"),
    )(h, g)


def _rms_back(h, g, dy, dh_in):
    inv = lax.rsqrt(jnp.mean(h * h, axis=-1, keepdims=True) + EPS)
    xhat = h * inv
    dxhat = dy * g
    dh = inv * (dxhat - xhat * jnp.mean(dxhat * xhat, axis=-1, keepdims=True))
    return dh_in + dh, jnp.sum(dy * xhat, axis=0, keepdims=True)


def _loss_head(h, g, target):
    s, d = h.shape
    tm = _tile(s, 256)

    def body(h_ref, g_ref, t_ref, loss_ref, dh_ref, dg_ref):
        y, vjp = jax.vjp(_rms, h_ref[...], g_ref[...])
        err = y - t_ref[...]
        dh, dg = vjp(err * (1.0 / d))
        dh_ref[...] = dh
        per_token = jnp.mean(err * err, axis=-1, keepdims=True)
        part = 0.5 * jnp.sum(per_token, axis=0, keepdims=True)

        @pl.when(pl.program_id(0) == 0)
        def _():
            dg_ref[...] = jnp.zeros_like(dg_ref)
            loss_ref[...] = jnp.zeros_like(loss_ref)

        dg_ref[...] += dg
        loss_ref[...] += part

    tok = pl.BlockSpec((tm, d), lambda i: (i, 0))
    vec = pl.BlockSpec((1, d), lambda i: (0, 0))
    return pl.pallas_call(
        body, name="loss_head", grid=(s // tm,),
        in_specs=[tok, vec, tok],
        out_specs=[pl.BlockSpec((1, 1), lambda i: (0, 0)), tok, vec],
        out_shape=[_sds((1, 1), F32), _sds((s, d), F32), _sds((1, d), F32)],
        compiler_params=_params("arbitrary"),
    )(h, g, target)


_U, _V, _HB, _BG, _CG, _CA, _CGT, _PD = (GROUP * i for i in range(8))


def _cols(c0, w=GROUP):
    return slice(c0, c0 + w)


_LANE_CHUNKS = tuple(_cols(i * HEAD, HEAD) for i in range(GROUP // HEAD))


def _shifted_copies(sh_ref, ext_ref, length):
    for r in range(1, 8):
        sh_ref[r - 1, 0:length, :] = ext_ref[pl.ds(r, length), :]


def _window(ext_ref, sh_ref, off, rows, cc):
    q, r = divmod(off, 8)
    if sh_ref is None or r == 0:
        return ext_ref[pl.ds(off, rows), cc]
    return sh_ref[r - 1, pl.ds(8 * q, rows), cc]


def _taps(ext_ref, w_ref, nk, start, rows, cc, flip=False, sh_ref=None):
    acc = None
    for k in range(nk):
        kw = nk - 1 - k if flip else k
        term = w_ref[kw:kw + 1, cc] * _window(ext_ref, sh_ref, start + k, rows, cc)
        acc = term if acc is None else acc + term
    return acc


def _mixer_param_specs():
    full = lambda *shape: pl.BlockSpec(shape, lambda i: (0,) * len(shape))
    return [
        full(4, 1, HEAD), full(4, 1, HEAD), full(4, HEAD, HEAD), full(4, HEAD, 1),
        full(8, GROUP), full(32, GROUP), full(1, GROUP), full(1, GROUP), full(1, GROUP),
        full(4, HEAD, HEAD), full(1, GROUP),
    ]


def _mixer_fwd(name, z, prm):
    s = z.shape[0]
    t = _tile(s, 256)
    hb = t // HALO

    def body(zp_ref, zm_ref, lg_ref, lb_ref, sw_ref, sb_ref, scw_ref, cfw_ref, cfb_ref, cg_ref, cb_ref,
             pw_ref, ps_ref, o_ref, ext_ref, sh_ref):
        i = pl.program_id(0)
        keep = (i > 0).astype(F32)
        main = pl.ds(HALO, t)

        for n in range(t // HEAD):
            rows = slice(n * HEAD, (n + 1) * HEAD)
            for hh in range(4):
                cu = _cols(_U + hh * HEAD, HEAD)
                cv = _cols(_V + hh * HEAD, HEAD)
                o_ref[rows, _cols(hh * HEAD, HEAD)] = _sgu_piece(
                    zm_ref[rows, cu], zm_ref[rows, cv], lg_ref[hh], lb_ref[hh], sw_ref[hh], sb_ref[hh]
                ).astype(BF16)

        ext_ref[0:HALO, :] = zp_ref[:, _cols(_CG)] * zp_ref[:, _cols(_HB)] * keep
        ext_ref[main, :] = zm_ref[:, _cols(_CG)] * zm_ref[:, _cols(_HB)]
        y = jnp.zeros((t, GROUP), F32)
        for k in range(SHORT_K):
            y = y + scw_ref[k:k + 1, :] * ext_ref[pl.ds(HALO - (SHORT_K - 1) + k, t), :]
        o_ref[:, _cols(GROUP)] = (zm_ref[:, _cols(_BG)] * y).astype(BF16)

        ext_ref[0:HALO, :] = zp_ref[:, _cols(_CA)] * jax.nn.sigmoid(zp_ref[:, _cols(_CGT)]) * keep
        ext_ref[main, :] = zm_ref[:, _cols(_CA)] * jax.nn.sigmoid(zm_ref[:, _cols(_CGT)])
        _shifted_copies(sh_ref, ext_ref, HALO + t - 8)
        c = jnp.concatenate(
            [cfb_ref[:, cc] + _taps(ext_ref, cfw_ref, CONF_K, HALO - (CONF_K - 1), t, cc, sh_ref=sh_ref)
             for cc in _LANE_CHUNKS], axis=1)
        o_ref[:, _cols(2 * GROUP)] = _conf_post(c, cg_ref[...], cb_ref[...]).astype(BF16)

        ext_ref[0:HALO, :] = zp_ref[:, _cols(_PD)] * keep
        ext_ref[main, :] = zm_ref[:, _cols(_PD)]
        for gi, w in enumerate(POOL_WINDOWS):
            cc = _cols(gi * HEAD, HEAD)
            acc = ext_ref[main, cc]
            for j in range(1, w):
                acc = acc + ext_ref[pl.ds(HALO - j, t), cc]
            q = acc / _pool_count(i * t, t, w) - ext_ref[main, cc]
            o_ref[:, _cols(3 * GROUP + gi * HEAD, HEAD)] = (_bdot(q, pw_ref[gi]) * ps_ref[:, cc]).astype(BF16)

    return pl.pallas_call(
        body, name=name, grid=(s // t,),
        in_specs=[pl.BlockSpec((HALO, 8 * GROUP), lambda i: (jnp.maximum(i * hb - 1, 0), 0)),
                  pl.BlockSpec((t, 8 * GROUP), lambda i: (i, 0)),
                  *_mixer_param_specs()],
        out_specs=pl.BlockSpec((t, 4 * GROUP), lambda i: (i, 0)),
        out_shape=_sds((s, 4 * GROUP), BF16),
        scratch_shapes=[pltpu.VMEM((HALO + t, GROUP), F32), pltpu.VMEM((7, HALO + t - 8, GROUP), F32)],
        compiler_params=_params("parallel"),
    )(z, z, *prm)


def _mixer_bwd(name, z, dmo, prm):
    s = z.shape[0]
    t = _tile(s, 256)
    hb = t // HALO
    nt = s // t
    last_halo = s // HALO - 1

    def body(zp_ref, zm_ref, zn_ref, dm_ref, dn_ref,
             lg_ref, lb_ref, sw_ref, sb_ref, scw_ref, cfw_ref, cfb_ref, cg_ref, cb_ref, pw_ref, ps_ref,
             dz_ref, dlg_ref, dlb_ref, dsw_ref, dsb_ref, dscw_ref, dcfw_ref, dcfb_ref, dcg_ref, dcb_ref,
             dpw_ref, dps_ref, extp_ref, extn_ref, extc_ref, sh_ref):
        i = pl.program_id(0)
        keep_prev = (i > 0).astype(F32)
        keep_next = (i < nt - 1).astype(F32)
        main = pl.ds(HALO, t)

        @pl.when(i == 0)
        def _():
            for r in (dlg_ref, dlb_ref, dsw_ref, dsb_ref, dscw_ref, dcfw_ref, dcfb_ref, dcg_ref, dcb_ref,
                      dpw_ref, dps_ref):
                r[...] = jnp.zeros_like(r)

        def rowsum(v):
            return jnp.sum(v, axis=0, keepdims=True)

        for n in range(t // HEAD):
            rows = slice(n * HEAD, (n + 1) * HEAD)
            for hh in range(4):
                cu = _cols(_U + hh * HEAD, HEAD)
                cv = _cols(_V + hh * HEAD, HEAD)
                _, vjp = jax.vjp(_sgu_piece, zm_ref[rows, cu], zm_ref[rows, cv],
                                 lg_ref[hh], lb_ref[hh], sw_ref[hh], sb_ref[hh])
                dzu, dzv, dlg, dlb, dsw, dsb = vjp(dm_ref[rows, _cols(hh * HEAD, HEAD)].astype(F32))
                dz_ref[rows, cu] = dzu.astype(BF16)
                dz_ref[rows, cv] = dzv.astype(BF16)
                dlg_ref[hh] += dlg
                dlb_ref[hh] += dlb
                dsw_ref[hh] += dsw
                dsb_ref[hh] += dsb

        extp_ref[0:HALO, :] = zp_ref[:, _cols(_CG)] * zp_ref[:, _cols(_HB)] * keep_prev
        extp_ref[main, :] = zm_ref[:, _cols(_CG)] * zm_ref[:, _cols(_HB)]
        dob = dm_ref[:, _cols(GROUP)].astype(F32)
        dy = dob * zm_ref[:, _cols(_BG)]
        extn_ref[0:t, :] = dy
        extn_ref[t:t + HALO, :] = dn_ref[:, _cols(GROUP)].astype(F32) * zn_ref[:, _cols(_BG)] * keep_next
        for cc in _LANE_CHUNKS:
            at = lambda c0: _cols(c0 + cc.start, HEAD)
            y = _taps(extp_ref, scw_ref, SHORT_K, HALO - (SHORT_K - 1), t, cc)
            dx = _taps(extn_ref, scw_ref, SHORT_K, 0, t, cc, flip=True)
            dy_c = extn_ref[0:t, cc]
            for k in range(SHORT_K):
                dscw_ref[k:k + 1, cc] += rowsum(dy_c * extp_ref[pl.ds(HALO - (SHORT_K - 1) + k, t), cc])
            dz_ref[:, at(_BG)] = (dm_ref[:, at(GROUP)].astype(F32) * y).astype(BF16)
            dz_ref[:, at(_CG)] = (dx * zm_ref[:, at(_HB)]).astype(BF16)
            dz_ref[:, at(_HB)] = (dx * zm_ref[:, at(_CG)]).astype(BF16)

        extc_ref[0:HALO, :] = zp_ref[:, _cols(_CA)] * jax.nn.sigmoid(zp_ref[:, _cols(_CGT)]) * keep_prev
        extc_ref[main, :] = zm_ref[:, _cols(_CA)] * jax.nn.sigmoid(zm_ref[:, _cols(_CGT)])
        extc_ref[HALO + t:HALO + t + HALO, :] = zn_ref[:, _cols(_CA)] * jax.nn.sigmoid(zn_ref[:, _cols(_CGT)])
        _shifted_copies(sh_ref, extc_ref, HALO + t + HALO - 8)
        conv = lambda start, rows: jnp.concatenate(
            [cfb_ref[:, cc] + _taps(extc_ref, cfw_ref, CONF_K, start, rows, cc, sh_ref=sh_ref) for cc in _LANE_CHUNKS],
            axis=1)
        c_main = conv(HALO - (CONF_K - 1), t)
        c_next = conv(HALO + t - (CONF_K - 1), HALO)
        _, vjp = jax.vjp(_conf_post, c_main, cg_ref[...], cb_ref[...])
        dc, dcg, dcb = vjp(dm_ref[:, _cols(2 * GROUP)].astype(F32))
        dcg_ref[...] += dcg
        dcb_ref[...] += dcb
        dcfb_ref[...] += rowsum(dc)
        _, vjp_next = jax.vjp(lambda cv: _conf_post(cv, cg_ref[...], cb_ref[...]), c_next)
        (dc_next,) = vjp_next(dn_ref[:, _cols(2 * GROUP)].astype(F32) * keep_next)
        extn_ref[0:t, :] = dc
        extn_ref[t:t + HALO, :] = dc_next
        for cc in _LANE_CHUNKS:
            dc_c = extn_ref[0:t, cc]
            for k in range(CONF_K):
                dcfw_ref[k:k + 1, cc] += rowsum(dc_c * _window(extc_ref, sh_ref, HALO - (CONF_K - 1) + k, t, cc))
        _shifted_copies(sh_ref, extn_ref, t + HALO - 8)
        for cc in _LANE_CHUNKS:
            at = lambda c0: _cols(c0 + cc.start, HEAD)
            dhc = _taps(extn_ref, cfw_ref, CONF_K, 0, t, cc, flip=True, sh_ref=sh_ref)
            sg = jax.nn.sigmoid(zm_ref[:, at(_CGT)])
            dz_ref[:, at(_CA)] = (dhc * sg).astype(BF16)
            dz_ref[:, at(_CGT)] = (dhc * zm_ref[:, at(_CA)] * sg * (1.0 - sg)).astype(BF16)

        extp_ref[0:HALO, :] = zp_ref[:, _cols(_PD)] * keep_prev
        extp_ref[main, :] = zm_ref[:, _cols(_PD)]
        for gi, w in enumerate(POOL_WINDOWS):
            cc = _cols(gi * HEAD, HEAD)
            oc = _cols(3 * GROUP + gi * HEAD, HEAD)
            acc = extp_ref[main, cc]
            for j in range(1, w):
                acc = acc + extp_ref[pl.ds(HALO - j, t), cc]
            count = _pool_count(i * t, t, w)
            q = acc / count - extp_ref[main, cc]
            dod = dm_ref[:, oc].astype(F32)
            dps_ref[:, cc] += rowsum(dod * _bdot(q, pw_ref[gi]))
            ds = dod * ps_ref[:, cc]
            dpw_ref[gi] += _bdot(q, ds, _DIMS["tn"])
            dq = _bdot(ds, pw_ref[gi], _DIMS["nt"])
            ds_next = dn_ref[:, oc].astype(F32) * ps_ref[:, cc] * keep_next
            dq_next = _bdot(ds_next, pw_ref[gi], _DIMS["nt"])
            extn_ref[0:t, cc] = dq / count
            extn_ref[t:t + HALO, cc] = dq_next * (1.0 / w)
            back = extn_ref[0:t, cc]
            for j in range(1, w):
                back = back + extn_ref[pl.ds(j, t), cc]
            dz_ref[:, _cols(_PD + gi * HEAD, HEAD)] = (back - dq).astype(BF16)

    full = lambda *shape: pl.BlockSpec(shape, lambda i: (0,) * len(shape))
    grad_specs = [full(4, 1, HEAD), full(4, 1, HEAD), full(4, HEAD, HEAD), full(4, HEAD, 1),
                  full(8, GROUP), full(32, GROUP), full(1, GROUP), full(1, GROUP), full(1, GROUP),
                  full(4, HEAD, HEAD), full(1, GROUP)]
    grad_shapes = [_sds(sp.block_shape, F32) for sp in grad_specs]
    nxt = lambda i: (jnp.minimum((i + 1) * hb, last_halo), 0)
    return pl.pallas_call(
        body, name=name, grid=(nt,),
        in_specs=[pl.BlockSpec((HALO, 8 * GROUP), lambda i: (jnp.maximum(i * hb - 1, 0), 0)),
                  pl.BlockSpec((t, 8 * GROUP), lambda i: (i, 0)),
                  pl.BlockSpec((HALO, 8 * GROUP), nxt),
                  pl.BlockSpec((t, 4 * GROUP), lambda i: (i, 0)),
                  pl.BlockSpec((HALO, 4 * GROUP), nxt),
                  *_mixer_param_specs()],
        out_specs=[pl.BlockSpec((t, 8 * GROUP), lambda i: (i, 0)), *grad_specs],
        out_shape=[_sds((s, 8 * GROUP), BF16), *grad_shapes],
        scratch_shapes=[pltpu.VMEM((HALO + t, GROUP), F32), pltpu.VMEM((t + HALO, GROUP), F32),
                        pltpu.VMEM((HALO + t + HALO, GROUP), F32), pltpu.VMEM((7, HALO + t + HALO - 8, GROUP), F32)],
        compiler_params=_params("arbitrary"),
    )(z, z, z, dmo, dmo, *prm)


def _dot(a, b, kind="nn"):
    return _bdot(a, b, _DIMS[kind])


def _chain(terms, kind):
    acc = None
    for a, b in terms:
        p = _dot(a, b, kind)
        acc = p if acc is None else acc + p
    return acc


def _after(g, token):
    return g if token is None else g + token[0:1, 0:1]


def _layer_fwd(li, h, p_all, fetch, sm, y1=None, next_mix_g=None):
    s, d = h.shape
    nb = N_DEV
    pd = p_all.shape[2]
    sv = {"h": h}
    rows = lambda t, w: pl.BlockSpec((t, w), lambda i: (i, 0))
    vec = pl.BlockSpec((1, d), lambda i: (0, 0))

    if y1 is None:
        y1 = _rms_fwd("rms_mix", h, sm["norm_mix_g"])
    tm = _tile(s, 1024)
    (z,) = _tiles(
        "w_in_fwd", (s // tm, nb),
        [y1, fetch("win", y1)],
        [pl.BlockSpec((tm, d), lambda i, j: (i, 0)), pl.BlockSpec((None, d, GROUP), lambda i, j: (j, 0, 0))],
        [_sds((s, nb * GROUP), F32)], [pl.BlockSpec((tm, GROUP), lambda i, j: (i, j))],
        lambda a, w: (_dot(a[...], w[...]),))
    mo = _mixer_fwd("mixer_fwd", z, sm["mixer"])
    tm = _tile(s, 512)
    def mix_out(a, w, hv, gv):
        h2v = hv[...] + _dot(a[...], w[...])
        return h2v, _rms(h2v, gv[...])

    h2, y2 = _tiles(
        "w_out_fwd", (s // tm,),
        [mo, fetch("wout", z), h, sm["norm_ffn_g"]], [rows(tm, d), _resident((d, d)), rows(tm, d), vec],
        [_sds((s, d), F32), _sds((s, d), BF16)], [rows(tm, d), rows(tm, d)], mix_out)
    sv.update(y1=y1, z=z, mo=mo, h2=h2)

    gate_pre, up_pre, hmid = _ffn_up("ffn_up_fwd", y2, fetch("wg", z), fetch("wu", z))
    wd = fetch("wd", hmid)
    f8 = wd.shape[1]
    tm, tn = _tile(s, 512), _tile(d, 1024)
    (h3,) = _tiles(
        "w_down_fwd", (d // tn, s // tm),
        [hmid, wd, h2],
        [pl.BlockSpec((nb, tm, f8), lambda j, i: (0, i, 0)), pl.BlockSpec((nb, f8, tn), lambda j, i: (0, 0, j)),
         pl.BlockSpec((tm, tn), lambda j, i: (i, j))],
        [_sds((s, d), F32)], [pl.BlockSpec((tm, tn), lambda j, i: (i, j))],
        lambda a, w, hv: (hv[...] + _chain([(a[k], w[k]) for k in range(nb)], "nn"),))
    sv.update(y2=y2, gate_pre=gate_pre, up_pre=up_pre, hmid=hmid, h3=h3)

    y3 = _rms_fwd("rms_ple", h3, sm["norm_ple_g"])
    tm = _tile(s, 1024)
    (pp,) = _tiles(
        "w_ple_proj_fwd", (s // tm, nb),
        [p_all, fetch("wpp", hmid)],
        [pl.BlockSpec((None, tm, pd), lambda i, j: (li, i, 0)), pl.BlockSpec((None, pd, pd), lambda i, j: (j, 0, 0))],
        [_sds((s, d), BF16)], [pl.BlockSpec((tm, pd), lambda i, j: (i, j))],
        lambda a, w: (_dot(a[...], w[...]),))
    tm = _tile(s, 256)

    def ple(a, w, hv, ppv, *next_g):
        pg = _dot(a[...], w[...])
        h4v = hv[...] + jax.nn.sigmoid(pg) * ppv[...].astype(F32)
        return (h4v, pg) + tuple(_rms(h4v, gv[...]) for gv in next_g)

    more = [] if next_mix_g is None else [next_mix_g]
    h4, pg_pre, *y1_next = _tiles(
        "w_ple_gate_fwd", (s // tm,),
        [y3, fetch("wpg", hmid), h3, pp] + more,
        [rows(tm, d), _resident((d, d)), rows(tm, d), rows(tm, d)] + [vec] * len(more),
        [_sds((s, d), F32), _sds((s, d), BF16)] + [_sds((s, d), BF16)] * len(more),
        [rows(tm, d)] * (2 + len(more)), ple)
    sv.update(y3=y3, pp=pp, pg_pre=pg_pre)
    return h4, sv, (y1_next[0] if y1_next else None)


def _ffn_up(name, y2, wg, wu):
    s, d = y2.shape
    nb, f8, _ = wg.shape
    tm = _tile(s, 1024)

    def compute(y_ref, wg_ref, wu_ref):
        yv = y_ref[...]
        g = _dot(yv, wg_ref[...], "nt")
        u = _dot(yv, wu_ref[...], "nt")
        return g, u, jax.nn.silu(g) * u

    wspec = pl.BlockSpec((None, f8, d), lambda i, j: (j, 0, 0))
    ospec = pl.BlockSpec((None, tm, f8), lambda i, j: (j, i, 0))
    return _tiles(name, (s // tm, nb), [y2, wg, wu], [pl.BlockSpec((tm, d), lambda i, j: (i, 0)), wspec, wspec],
                  [_sds((nb, s, f8), BF16)] * 3, [ospec] * 3, compute)


def _swiglu_bwd(dm, g, u):
    sg = jax.nn.sigmoid(g)
    silu = g * sg
    return dm * u * (sg + silu * (1.0 - sg)), dm * silu


def _layer_bwd(li, dh, p_all, gw, sm, sv, start_token=None, hooks=None):
    hooks = hooks or {}
    run = lambda name, *a: hooks[name](*a) if name in hooks else None
    s, d = dh.shape
    nb, f8, _ = gw["wg"].shape
    pd = p_all.shape[2]
    gr, sg = {}, {}
    rows = lambda t, w: pl.BlockSpec((t, w), lambda i: (i, 0))
    vec = pl.BlockSpec((1, d), lambda i: (0, 0))
    tw = _tile(d, 1024)
    whole = lambda w: pl.BlockSpec((s, w), lambda i, j: (0, i))
    whole_j = lambda w: pl.BlockSpec((s, w), lambda i, j: (0, j))
    tn_dot = lambda a, b: (_dot(a[...], b[...], "tn"),)

    tm = _tile(s, 256)

    def ple_dx(dv, pg, ppv, w, hv, gv):
        dhv = dv[...]
        gate = jax.nn.sigmoid(pg[...].astype(F32))
        dpg = (dhv * ppv[...].astype(F32) * gate * (1.0 - gate)).astype(BF16)
        dh3, dg = _rms_back(hv[...], gv[...], _dot(dpg, w[...], "nt"), dhv)
        return dh3, dh3, dg, dpg, dhv * gate

    dh3, dh3_16, sg["norm_ple_g"], dpg, dpp = _tiles(
        "w_ple_gate_dx", (s // tm,),
        [dh, sv["pg_pre"], sv["pp"], gw["wpg"], sv["h3"], _after(sm["norm_ple_g"], start_token)],
        [rows(tm, d), rows(tm, d), rows(tm, d), _resident((d, d)), rows(tm, d), vec],
        [_sds((s, d), F32), _sds((s, d), BF16), _sds((1, d), F32), _sds((s, d), BF16), _sds((s, d), BF16)],
        [rows(tm, d), rows(tm, d), vec, rows(tm, d), rows(tm, d)], ple_dx, summed=(2,))
    (gr["wpp"],) = _tiles(
        "w_ple_proj_dw", (nb,),
        [p_all, dpp], [pl.BlockSpec((None, s, pd), lambda j: (li, 0, 0)), pl.BlockSpec((s, pd), lambda j: (0, j))],
        [_sds((nb, pd, pd), F32)], [pl.BlockSpec((None, pd, pd), lambda j: (j, 0, 0))], tn_dot)
    (gr["wpg"],) = _tiles(
        "w_ple_gate_dw", (d // tw, d // tw), [sv["y3"], dpg], [whole(tw), whole_j(tw)],
        [_sds((d, d), F32)], [pl.BlockSpec((tw, tw), lambda i, j: (i, j))], tn_dot)
    gr["wpg"] = gr["wpg"].reshape(nb, d // nb, d)

    tm = _tile(s, 1024)
    blk_rows = pl.BlockSpec((None, tm, f8), lambda i, j: (j, i, 0))
    dgate, dup = _tiles(
        "w_down_dx", (s // tm, nb),
        [dh3_16, gw["wd"], sv["gate_pre"], sv["up_pre"]],
        [pl.BlockSpec((tm, d), lambda i, j: (i, 0)), pl.BlockSpec((None, f8, d), lambda i, j: (j, 0, 0)), blk_rows, blk_rows],
        [_sds((nb, s, f8), BF16)] * 2, [blk_rows] * 2,
        lambda a, w, g, u: _swiglu_bwd(_dot(a[...], w[...], "nt"), g[...].astype(F32), u[...].astype(F32)))
    (gr["wd"],) = _tiles(
        "w_down_dw", (nb, d // tw),
        [sv["hmid"], dh3_16], [pl.BlockSpec((None, s, f8), lambda i, j: (i, 0, 0)), whole_j(tw)],
        [_sds((nb, f8, d), F32)], [pl.BlockSpec((None, f8, tw), lambda i, j: (i, 0, j))], tn_dot)
    mid_token = run("after_down_dx", dgate)
    for nm, dact in (("wg", dgate), ("wu", dup)):
        (gr[nm],) = _tiles(
            "w_" + {"wg": "gate", "wu": "up"}[nm] + "_dw", (d // tw, nb),
            [dact, sv["y2"]], [pl.BlockSpec((None, s, f8), lambda i, j: (j, 0, 0)), whole(tw)],
            [_sds((nb, f8, d), F32)], [pl.BlockSpec((None, f8, tw), lambda i, j: (j, 0, i))], tn_dot)
    ffn_token = run("after_ffn_grads", gr, gr["wu"])
    tm, tn = _tile(s, 512), _tile(d, 1024)
    act_spec = pl.BlockSpec((nb, tm, f8), lambda j, i: (0, i, 0))
    wt_spec = pl.BlockSpec((nb, f8, tn), lambda j, i: (0, 0, j))
    out_spec = pl.BlockSpec((tm, tn), lambda j, i: (i, j))
    (dy2,) = _tiles(
        "w_gate_dx", (d // tn, s // tm), [dgate, gw["wg"]], [act_spec, wt_spec],
        [_sds((s, d), F32)], [out_spec],
        lambda a, w: (_chain([(a[k], w[k]) for k in range(nb)], "nn"),),
        deps=[tk for tk in (mid_token, ffn_token) if tk is not None])
    (dy2,) = _tiles(
        "w_up_dx", (d // tn, s // tm), [dup, gw["wu"], dy2], [act_spec, wt_spec, out_spec],
        [_sds((s, d), F32)], [out_spec],
        lambda a, w, prev: (prev[...] + _chain([(a[k], w[k]) for k in range(nb)], "nn"),))
    g_ffn = _after(sm["norm_ffn_g"], run("before_ffn_norm", dy2))

    tm = _tile(s, 256)

    def out_dx(dy, hv, gv, dv, w):
        dh2v, dg = _rms_back(hv[...], gv[...], dy[...], dv[...])
        return dh2v, dh2v, dg, _dot(dh2v, w[...], "nt")

    dh2, dh2_16, sg["norm_ffn_g"], dmo = _tiles(
        "w_out_dx", (s // tm,), [dy2, sv["h2"], g_ffn, dh3, gw["wout"]],
        [rows(tm, d), rows(tm, d), vec, rows(tm, d), _resident((d, d))],
        [_sds((s, d), F32), _sds((s, d), BF16), _sds((1, d), F32), _sds((s, d), BF16)],
        [rows(tm, d), rows(tm, d), vec, rows(tm, d)], out_dx, summed=(2,))
    (gr["wout"],) = _tiles(
        "w_out_dw", (d // tw, d // tw), [sv["mo"], dh2_16], [whole(tw), whole_j(tw)],
        [_sds((d, d), F32)], [pl.BlockSpec((tw, tw), lambda i, j: (i, j))], tn_dot)
    gr["wout"] = gr["wout"].reshape(nb, d // nb, d)
    dz, *mix_grads = _mixer_bwd("mixer_bwd", sv["z"], dmo, sm["mixer"])
    sg["mixer"] = mix_grads
    (gr["win"],) = _tiles(
        "w_in_dw", (d // tw, nb), [sv["y1"], dz], [whole(tw), pl.BlockSpec((s, GROUP), lambda i, j: (0, j))],
        [_sds((nb, d, GROUP), F32)], [pl.BlockSpec((None, tw, GROUP), lambda i, j: (j, i, 0))], tn_dot)
    in_token = run("after_in_dw", gr)
    tm = _tile(s, 256)

    def in_dx(a, w, hv, gv, dv):
        dy1 = _chain([(a[:, k * GROUP:(k + 1) * GROUP], w[k]) for k in range(nb)], "nt")
        return _rms_back(hv[...], gv[...], dy1, dv[...])

    dh_in, sg["norm_mix_g"] = _tiles(
        "w_in_dx", (s // tm,),
        [dz, gw["win"], sv["h"], sm["norm_mix_g"], dh2],
        [rows(tm, nb * GROUP), _resident((nb, d, GROUP)), rows(tm, d), vec, rows(tm, d)],
        [_sds((s, d), F32), _sds((1, d), F32)], [rows(tm, d), vec], in_dx, summed=(1,),
        deps=[] if in_token is None else [in_token])
    return dh_in, gr, sg


def _place():
    return lax.axis_index("x"), lax.axis_index("y"), lax.axis_index("c")


_HBM = pl.BlockSpec(memory_space=pltpu.HBM)
_SEM = pl.BlockSpec(memory_space=pltpu.SEMAPHORE)


def _split_call(name, bufs, old_sems, n_new, after, body, want_token):
    nb, no = len(bufs), len(old_sems)
    extra = [] if after is None else [after]

    def kbody(*refs):
        new = refs[nb + no + len(extra):nb + no + len(extra) + n_new]
        body(refs[:nb], refs[nb:nb + no], new)
        if want_token:
            refs[-1][...] = jnp.zeros_like(refs[-1])

    outs = pl.pallas_call(
        kbody, name=name,
        out_shape=tuple([pltpu.SemaphoreType.DMA(())] * n_new + [pltpu.HBM(b.shape, b.dtype) for b in bufs]
                        + ([_sds((8, 128), F32)] if want_token else [])),
        in_specs=[_HBM] * nb + [_SEM] * no + [ANY] * len(extra),
        out_specs=tuple([_SEM] * n_new + [_HBM] * nb
                        + ([pl.BlockSpec(memory_space=pltpu.VMEM)] if want_token else [])),
        input_output_aliases={i: n_new + i for i in range(nb)},
        compiler_params=pltpu.CompilerParams(has_side_effects=pltpu.SideEffectType.DATAFLOW_SIDE_EFFECTING),
    )(*[pltpu.with_memory_space_constraint(b, pltpu.HBM) for b in bufs], *old_sems, *extra)
    return list(outs[:n_new]), list(outs[n_new:n_new + nb]), (outs[-1] if want_token else None)


def _remote(ref_src, ref_dst, send_sem, recv_sem, to):
    return pltpu.make_async_remote_copy(src_ref=ref_src, dst_ref=ref_dst, send_sem=send_sem, recv_sem=recv_sem,
                                        device_id=to, device_id_type=MESH)


def _place_shard(name, w):
    n_layers, r, cdim = w.shape
    tr = _row_tile(r, 256)
    nt = r // tr

    def body(w_ref, *rest):
        outs, buf, sem = rest[:n_layers], rest[n_layers], rest[n_layers + 1]
        i = pl.program_id(0)
        x, y, c = _place()
        slot = i % 2

        def writes(step, sl):
            rows = pl.ds(pl.multiple_of(step * tr, 16), tr)
            return [pltpu.make_async_copy(buf.at[sl, l], outs[l].at[4 * x + 2 * y + c, rows, :], sem.at[sl, l])
                    for l in range(n_layers)]

        @pl.when(i >= 2)
        def _():
            for cp in writes(i - 2, slot):
                cp.wait()

        buf[slot] = w_ref[...].astype(BF16)
        for cp in writes(i, slot):
            cp.start()

        @pl.when(i == nt - 1)
        def _():
            for cp in writes(i, slot):
                cp.wait()
            if nt >= 2:
                for cp in writes(i - 1, 1 - slot):
                    cp.wait()

    return pl.pallas_call(
        body, name=name, grid=(nt,),
        in_specs=[pl.BlockSpec((n_layers, tr, cdim), lambda i: (0, i, 0))], out_specs=[ANY] * n_layers,
        out_shape=[_sds((N_DEV, r, cdim), BF16)] * n_layers,
        scratch_shapes=[pltpu.VMEM((2, n_layers, tr, cdim), BF16), pltpu.SemaphoreType.DMA((2, n_layers))],
        compiler_params=_params("arbitrary"),
    )(w)


def _gather_start(li, lands, after):
    nw = len(lands)

    def body(bufs, _, new):
        x, y, c = _place()
        chips = [(1 - x, y), (x, 1 - y), (1 - x, 1 - y)]
        for w in range(nw):
            mine = bufs[w].at[4 * x + 2 * y + c]
            _remote(mine, mine, new[8 * w], new[8 * w + 4], (x, y, 1 - c)).start()
            for j, (px, py) in enumerate(chips):
                _remote(mine, mine, new[8 * w + 1 + j], new[8 * w + 5 + j], (px, py, c)).start()

    return _split_call("gather_start_l%s" % li, lands, [], 8 * nw, after, body, True)


def _gather_forward(li, lands, sems, after):
    nw = len(lands)
    arrivals = [sems[8 * w + 5 + j] for w in range(nw) for j in range(3)]

    def body(bufs, old, new):
        x, y, c = _place()
        chips = [(1 - x, y), (x, 1 - y), (1 - x, 1 - y)]
        for j, (px, py) in enumerate(chips):
            for w in range(nw):
                got = bufs[w].at[4 * px + 2 * py + c]
                _remote(got, got, new[6 * w + j], old[3 * w + j], (px, py, c)).wait_recv()
                _remote(got, got, new[6 * w + j], new[6 * w + 3 + j], (x, y, 1 - c)).start()

    return _split_call("gather_forward_l%s" % li, lands, arrivals, 6 * nw, after, body, True)


def _gather_wait(li, lands, sems, fwd_sems, after):
    nw = len(lands)
    first = [sems[8 * w + k] for w in range(nw) for k in range(5)]

    def body(bufs, old, _):
        x, y, c = _place()
        sib = (x, y, 1 - c)
        chips = [(1 - x, y), (x, 1 - y), (1 - x, 1 - y)]
        n1 = 5 * nw
        for w in range(nw):
            mine = bufs[w].at[4 * x + 2 * y + c]
            theirs = bufs[w].at[4 * x + 2 * y + 1 - c]
            _remote(theirs, theirs, old[5 * w], old[5 * w + 4], sib).wait_recv()
            for k in range(4):
                _remote(mine, mine, old[5 * w + k], old[5 * w + 4], sib).wait_send()
            for j, (px, py) in enumerate(chips):
                sent = bufs[w].at[4 * px + 2 * py + c]
                got = bufs[w].at[4 * px + 2 * py + 1 - c]
                _remote(sent, sent, old[n1 + 6 * w + j], old[n1 + 6 * w + 3 + j], sib).wait_send()
                _remote(got, got, old[n1 + 6 * w + j], old[n1 + 6 * w + 3 + j], sib).wait_recv()

    _, lands, _ = _split_call("gather_wait_l%s" % li, lands, first + list(fwd_sems), 0, after, body, False)
    return lands


def _siblings_start(li, grads, after):
    nw = len(grads)
    lands = [lax.empty((4,) + g.shape[1:], g.dtype) for g in grads]

    def body(bufs, _, new):
        x, y, c = _place()
        for w in range(nw):
            for q in range(4):
                _remote(bufs[w].at[2 * q + (1 - c)], bufs[nw + w].at[q], new[8 * w + q], new[8 * w + 4 + q],
                        (x, y, 1 - c)).start()

    return _split_call("rs_siblings_start_l%s" % li, list(grads) + lands, [], 8 * nw, after, body, True)


def _siblings_wait(li, bufs, sems, after):
    nw = len(bufs) // 2

    def body(refs, old, _):
        x, y, c = _place()
        for w in range(nw):
            for q in range(4):
                _remote(refs[w].at[2 * q + (1 - c)], refs[nw + w].at[q], old[8 * w + q], old[8 * w + 4 + q],
                        (x, y, 1 - c)).wait()

    _, bufs, _ = _split_call("rs_siblings_wait_l%s" % li, bufs, sems, 0, after, body, False)
    return bufs[:nw], bufs[nw:]


_FLIPS = ((1, 0), (0, 1), (1, 1))


def _chips_copies(refs, nw, sems):
    x, y, c = _place()
    for w in range(nw):
        for r, (fx, fy) in enumerate(_FLIPS):
            px = 1 - x if fx else x
            py = 1 - y if fy else y
            yield _remote(refs[w].at[2 * px + py], refs[nw + w].at[r], sems[6 * w + r], sems[6 * w + 3 + r], (px, py, c))


def _chips_start(li, partials, after):
    nw = len(partials)
    lands = [lax.empty((3,) + a.shape[1:], a.dtype) for a in partials]

    def body(bufs, _, new):
        for cp in _chips_copies(bufs, nw, new):
            cp.start()

    return _split_call("rs_chips_start_l%s" % li, list(partials) + lands, [], 6 * nw, after, body, True)


def _chips_wait(li, bufs, sems, after):
    nw = len(bufs) // 2

    def body(refs, old, _):
        for cp in _chips_copies(refs, nw, old):
            cp.wait()

    _, bufs, _ = _split_call("rs_chips_wait_l%s" % li, bufs, sems, 0, after, body, False)
    return bufs[nw:]


def _row_tile(n, want):
    best = None
    for t in range(16, min(n, want) + 1, 16):
        if n % t == 0:
            best = t
    assert best is not None, n
    return best


def _chip_partials(name, grad, from_sibling):
    _, r, cdim = grad.shape
    tr = _row_tile(r, 256)
    nt = r // tr
    steps = 4 * nt

    def body(g_hbm, s_ref, pb_ref, own_ref, buf, sem):
        i, q = pl.program_id(0), pl.program_id(1)
        x, y, c = _place()
        n = 4 * i + q
        slot = n % 2

        def fetch(step, into):
            rows = pl.ds(pl.multiple_of((step // 4) * tr, 16), tr)
            return pltpu.make_async_copy(g_hbm.at[2 * (step % 4) + c, rows, :], buf.at[into], sem.at[into])

        @pl.when(n == 0)
        def _():
            fetch(0, 0).start()

        @pl.when(n + 1 < steps)
        def _():
            fetch(n + 1, 1 - slot).start()

        fetch(n, slot).wait()
        tot = buf[slot] + s_ref[...]
        pb_ref[...] = tot.astype(BF16)

        @pl.when(q == 2 * x + y)
        def _():
            own_ref[...] = tot

    return pl.pallas_call(
        body, name=name, grid=(nt, 4),
        in_specs=[ANY, pl.BlockSpec((None, tr, cdim), lambda i, q: (q, i, 0))],
        out_specs=[pl.BlockSpec((None, tr, cdim), lambda i, q: (q, i, 0)), pl.BlockSpec((tr, cdim), lambda i, q: (i, 0))],
        out_shape=[_sds((4, r, cdim), BF16), _sds((r, cdim), F32)],
        scratch_shapes=[pltpu.VMEM((2, tr, cdim), F32), pltpu.SemaphoreType.DMA((2,))],
        compiler_params=_params("arbitrary", "arbitrary"),
    )(grad, from_sibling)


def _adamw(w, g, m, v):
    m = ADAM_B1 * m + (1.0 - ADAM_B1) * g
    v = ADAM_B2 * v + (1.0 - ADAM_B2) * (g * g)
    m_hat = m / (1.0 - ADAM_B1 ** ADAM_STEP)
    v_hat = v / (1.0 - ADAM_B2 ** ADAM_STEP)
    delta = -ADAM_LR * (m_hat / (jnp.sqrt(v_hat) + ADAM_EPS) + ADAM_WD * w)
    return delta, m, v


def _finish_weight(name, li, own, from_chips, w, m, v, stacked):
    r, cdim = own.shape
    tr = _row_tile(r, 256)
    if stacked is None:
        stacked = [lax.empty(w.shape, F32) for _ in range(4)]

    def body(own_ref, fc_ref, w_ref, m_ref, v_ref, *rest):
        g_out, d_out, m_out, v_out = rest[4:]
        g = own_ref[...] + fc_ref[0].astype(F32) + fc_ref[1].astype(F32) + fc_ref[2].astype(F32)
        delta, mn, vn = _adamw(w_ref[...], g, m_ref[...], v_ref[...])
        g_out[...] = g
        d_out[...] = delta
        m_out[...] = mn
        v_out[...] = vn

    tile = pl.BlockSpec((tr, cdim), lambda i: (i, 0))
    lay = pl.BlockSpec((None, tr, cdim), lambda i: (li, i, 0))
    return pl.pallas_call(
        body, name=name, grid=(r // tr,),
        in_specs=[tile, pl.BlockSpec((3, tr, cdim), lambda i: (0, i, 0)), lay, lay, lay] + [ANY] * 4,
        out_specs=[lay] * 4, out_shape=[_sds(w.shape, F32)] * 4,
        input_output_aliases={5: 0, 6: 1, 7: 2, 8: 3},
        compiler_params=_params("parallel"),
    )(own, from_chips, w, m, v, *stacked)


def _allgather_small(name, v, reduce, after=None):
    r = v.shape[0]
    deps = [] if after is None else [after]

    def body(x_ref, *rest):
        out_ref, rest = rest[len(deps)], rest[len(deps) + 1:]
        if reduce:
            sum_ref, send_sems, recv_sems, local_sem = rest
        else:
            send_sems, recv_sems, local_sem = rest
        x, y, c = _place()
        me, sib = (x, y, c), (x, y, 1 - c)
        chips = [(1 - x, y), (x, 1 - y), (1 - x, 1 - y)]

        def rows(px, py, pc):
            return out_ref.at[pl.ds(pl.multiple_of((4 * px + 2 * py + pc) * r, 8), r), :]

        def copy(k, block, to, src=None):
            return pltpu.make_async_remote_copy(
                src_ref=rows(*block) if src is None else src, dst_ref=rows(*block),
                send_sem=send_sems.at[k], recv_sem=recv_sems.at[k], device_id=to, device_id_type=MESH)

        mine = pltpu.make_async_copy(x_ref, rows(*me), local_sem)
        mine.start()
        first = [copy(0, me, sib, src=x_ref)]
        first += [copy(1 + j, me, (*chip, c), src=x_ref) for j, chip in enumerate(chips)]
        for cp in first:
            cp.start()
        passed = [copy(4 + j, (*chip, c), sib) for j, chip in enumerate(chips)]
        for j, chip in enumerate(chips):
            copy(1 + j, (*chip, c), me).wait_recv()
            passed[j].start()
        copy(0, sib, me).wait_recv()
        for j, chip in enumerate(chips):
            copy(4 + j, (*chip, 1 - c), me).wait_recv()
        for cp in first + passed:
            cp.wait_send()
        mine.wait()
        if reduce:
            tot = out_ref[0:r, :]
            for d in range(1, N_DEV):
                tot = tot + out_ref[d * r:(d + 1) * r, :]
            sum_ref[...] = tot

    vm = pl.BlockSpec(memory_space=pltpu.VMEM)
    outs = [_sds((N_DEV * r, 128), F32)] + ([_sds((r, 128), F32)] if reduce else [])
    res = pl.pallas_call(
        body, name=name,
        in_specs=[vm] + [ANY] * len(deps), out_specs=[vm] * len(outs), out_shape=outs,
        scratch_shapes=[pltpu.SemaphoreType.DMA((7,)), pltpu.SemaphoreType.DMA((7,)), pltpu.SemaphoreType.DMA],
        compiler_params=pltpu.CompilerParams(vmem_limit_bytes=VMEM_LIMIT_BYTES),
    )(v, *deps)
    return res


def _adamw_small(name, w, g, m, v):
    def body(w_ref, g_ref, m_ref, v_ref, d_out, m_out, v_out):
        delta, mn, vn = _adamw(w_ref[...], g_ref[...], m_ref[...], v_ref[...])
        d_out[...] = delta
        m_out[...] = mn
        v_out[...] = vn

    vm = pl.BlockSpec(memory_space=pltpu.VMEM)
    return pl.pallas_call(
        body, name=name, in_specs=[vm] * 4, out_specs=[vm] * 3, out_shape=[_sds(w.shape, F32)] * 3,
        compiler_params=pltpu.CompilerParams(vmem_limit_bytes=VMEM_LIMIT_BYTES),
    )(w, g, m, v)


def _pack(arrays):
    flat, layout, off = [], [], 0
    for a in arrays:
        flat.append(a.reshape(-1).astype(F32))
        layout.append((off, a.shape))
        off += a.size
    total = -(-off // 1024) * 1024
    if total > off:
        flat.append(jnp.zeros((total - off,), F32))
    return jnp.concatenate(flat).reshape(total // 128, 128), layout


def _unpack(packed, layout):
    flat = packed.reshape(-1)
    return [flat[off:off + math.prod(shape)].reshape(shape) for off, shape in layout]


_BIG = ("win", "wout", "wg", "wu", "wd", "wpg", "wpp")
_BIG_FULL = {"win": "w_in", "wout": "w_out", "wg": "w_gate", "wu": "w_up", "wd": "w_down",
             "wpg": "w_ple_gate", "wpp": "w_ple_proj"}
_SMALL_REPLICATED = ("norm_mix_g", "sgu_ln_g", "sgu_ln_b", "sgu_w", "sgu_b", "cf_conv_b", "cf_ln_g", "cf_ln_b",
                     "pool_w", "pool_scale", "norm_ffn_g", "norm_ple_g", "final_norm_g")
_SMALL_SHARDED = ("sc_conv_w", "cf_conv_w")
_WEIGHTS = ("norm_mix_g", "w_in", "sgu_ln_g", "sgu_ln_b", "sgu_w", "sgu_b", "sc_conv_w", "cf_conv_w", "cf_conv_b",
            "cf_ln_g", "cf_ln_b", "pool_w", "pool_scale", "w_out", "norm_ffn_g", "w_gate", "w_up", "w_down",
            "norm_ple_g", "w_ple_gate", "w_ple_proj", "final_norm_g")


def _pad_rows(a, rows):
    return jnp.concatenate([a, jnp.zeros((rows - a.shape[0],) + a.shape[1:], a.dtype)], axis=0)


def _mixer_params(li, W, sc_full, cf_full):
    return [W["sgu_ln_g"][li][:, None, :], W["sgu_ln_b"][li][:, None, :], W["sgu_w"][li], W["sgu_b"][li][:, :, None],
            _pad_rows(sc_full[li], 8), _pad_rows(cf_full[li], 32),
            W["cf_conv_b"][li][None, :], W["cf_ln_g"][li][None, :], W["cf_ln_b"][li][None, :],
            W["pool_w"][li], W["pool_scale"][li][None, :]]


def _step(W, M, V, x, p, loss_target):
    n_layers = W["w_in"].shape[0]
    h = x[0]
    target = loss_target[0]
    p_all = p[:, 0]
    xi, yi, ci = _place()
    blk = 4 * xi + 2 * yi + ci
    csh = W["sc_conv_w"].shape[2]
    turned = ("w_gate", "w_up")
    W, M, V = ({n: (jnp.swapaxes(a, 1, 2) if n in turned else a) for n, a in t.items()} for t in (W, M, V))

    packed, lay = _pack([W[n] for n in _SMALL_SHARDED])
    (taps,) = _allgather_small("gather_conv_taps", packed, reduce=False)
    per_dev = [_unpack(taps[d * packed.shape[0]:(d + 1) * packed.shape[0]], lay) for d in range(N_DEV)]
    sc_full = jnp.concatenate([pd[0] for pd in per_dev], axis=-1)
    cf_full = jnp.concatenate([pd[1] for pd in per_dev], axis=-1)

    d = h.shape[1]
    placed = [_place_shard("place_" + n, W[_BIG_FULL[n]]) for n in _BIG]
    lands = [[placed[w][li] for w in range(len(_BIG))] for li in range(n_layers)]

    groups = (("win",), ("wout", "wg", "wu"), ("wd", "wpg", "wpp"))
    order = [(li, gi) for li in range(n_layers) for gi in range(len(groups))]
    moving, tok = [], taps
    for li in range(n_layers):
        sems, bufs, tok = _gather_start(li, lands[li], tok)
        moving.append({"sems": sems, "bufs": bufs, "got": {}})
    passed = {}

    def pass_on(item, after):
        if item not in passed:
            li, gi = item
            idx = [_BIG.index(n) for n in groups[gi]]
            sems = [s_ for i in idx for s_ in moving[li]["sems"][8 * i:8 * i + 8]]
            tag = "%d_%s" % (li, groups[gi][0])
            passed[item] = (tag, sems) + tuple(_gather_forward(tag, [moving[li]["bufs"][i] for i in idx], sems, after))

    def fetcher(li):
        def fetch(name, after):
            got = moving[li]["got"]
            if name not in got:
                item = (li, next(gi for gi, g_ in enumerate(groups) if name in g_))
                pass_on(item, after)
                ahead = order.index(item) + 1
                if ahead < len(order) and order[ahead][0] >= 1:
                    pass_on(order[ahead], after)
                tag, sems, fwd_sems, bufs, _ = passed[item]
                for n, a in zip(groups[item[1]], _gather_wait(tag, bufs, sems, fwd_sems, after)):
                    got[n] = a.reshape(d, d) if n in ("wout", "wpg") else a
            return got[name]
        return fetch

    saved, gathered, smalls = [], [], []
    y1 = None
    for li in range(n_layers):
        sm = {"norm_mix_g": _after(W["norm_mix_g"][li][None, :], tok if li == 0 else None),
              "norm_ffn_g": W["norm_ffn_g"][li][None, :],
              "norm_ple_g": W["norm_ple_g"][li][None, :], "mixer": _mixer_params(li, W, sc_full, cf_full)}
        next_mix_g = W["norm_mix_g"][li + 1][None, :] if li + 1 < n_layers else None
        h, sv, y1 = _layer_fwd(li, h, p_all, fetcher(li), sm, y1, next_mix_g)
        saved.append(sv)
        gathered.append(moving[li]["got"])
        smalls.append(sm)
    loss, dh, d_final_g = _loss_head(h, W["final_norm_g"][None, :], target)

    big_out = {n: None for n in _BIG}
    small_grads = []

    def scatter_begin(li, tag, names, gr, after):
        sems, bufs, tok = _siblings_start(tag, [gr[n] for n in names], after)
        return {"li": li, "tag": tag, "names": names, "sems": sems, "bufs": bufs, "tok": tok}

    def scatter_middle(st, after):
        grads, from_sib = _siblings_wait(st["tag"], st["bufs"], st["sems"], after)
        parts = [_chip_partials("rs_sum_" + n, g_, s_) for n, g_, s_ in zip(st["names"], grads, from_sib)]
        st["own"] = [own for _, own in parts]
        st["sems"], st["bufs"], tok = _chips_start(st["tag"], [pb for pb, _ in parts], None)
        return tok

    def scatter_end(st, after):
        from_chips = _chips_wait(st["tag"], st["bufs"], st["sems"], after)
        for n, own, fc in zip(st["names"], st["own"], from_chips):
            full = _BIG_FULL[n]
            big_out[n] = _finish_weight("adamw_" + n, st["li"], own, fc, W[full], M[full], V[full], big_out[n])
        return big_out[st["names"][0]][0]

    order = _SMALL_REPLICATED + _SMALL_SHARDED

    def reduce_small(after):
        by_layer = small_grads[::-1]
        stacked = lambda fn: jnp.stack([fn(sg) for sg in by_layer])
        mix = lambda i: (lambda sg: sg["mixer"][i])
        grads_small = {
            "norm_mix_g": stacked(lambda sg: sg["norm_mix_g"][0]),
            "sgu_ln_g": stacked(mix(0))[:, :, 0, :], "sgu_ln_b": stacked(mix(1))[:, :, 0, :],
            "sgu_w": stacked(mix(2)), "sgu_b": stacked(mix(3))[:, :, :, 0],
            "sc_conv_w": stacked(mix(4))[:, :SHORT_K], "cf_conv_w": stacked(mix(5))[:, :CONF_K],
            "cf_conv_b": stacked(mix(6))[:, 0], "cf_ln_g": stacked(mix(7))[:, 0], "cf_ln_b": stacked(mix(8))[:, 0],
            "pool_w": stacked(mix(9)), "pool_scale": stacked(mix(10))[:, 0],
            "norm_ffn_g": stacked(lambda sg: sg["norm_ffn_g"][0]),
            "norm_ple_g": stacked(lambda sg: sg["norm_ple_g"][0]),
            "final_norm_g": d_final_g[0],
        }
        packed, lay = _pack([grads_small[n] for n in order] + [loss])
        _, summed = _allgather_small("allreduce_small", packed, reduce=True, after=after)
        return dict(zip(order + ("loss",), _unpack(summed, lay)))

    early, late = ("wpp", "wpg", "wd", "wg", "wu"), ("wout", "win")
    pending, tok = None, None
    for li in reversed(range(n_layers)):
        hooks, first, last = {}, {}, {}
        if pending is not None:
            hooks["after_down_dx"] = functools.partial(scatter_middle, pending)
        if li == 0:
            def begin_early(gr, after, first=first):
                first.update(scatter_begin(0, "0_early", early, gr, after))
                return first["tok"]

            def begin_late(gr, last=last):
                last.update(scatter_begin(0, "0_late", late, gr, None))
                return last["tok"]

            hooks["after_ffn_grads"] = begin_early
            hooks["before_ffn_norm"] = lambda after, first=first: scatter_middle(first, after)
            hooks["after_in_dw"] = begin_late
        dh, gr, sg = _layer_bwd(li, dh, p_all, gathered[li], smalls[li], saved[li], tok, hooks)
        small_grads.append(sg)
        if li > 0:
            done = None if pending is None else scatter_end(pending, dh)
            pending = scatter_begin(li, str(li), _BIG, gr, done)
            tok = pending["tok"]
        else:
            scatter_middle(last, dh)
            done = last["own"][0] if pending is None else scatter_end(pending, last["own"][0])
            done = scatter_end(first, done)
            total = reduce_small(scatter_end(last, done))
    loss_all = total["loss"][0, 0]

    out_g, out_d, out_m, out_v = {}, {}, {}, {}
    pw, lay_r = _pack([W[n] for n in _SMALL_REPLICATED])
    pg, _ = _pack([total[n] for n in _SMALL_REPLICATED])
    pm, _ = _pack([M[n] for n in _SMALL_REPLICATED])
    pv, _ = _pack([V[n] for n in _SMALL_REPLICATED])
    dd, mm, vv = _adamw_small("adamw_small", pw, pg, pm, pv)
    for n, a, b, c_ in zip(_SMALL_REPLICATED, _unpack(dd, lay_r), _unpack(mm, lay_r), _unpack(vv, lay_r)):
        out_g[n], out_d[n], out_m[n], out_v[n] = total[n], a, b, c_
    pick = (jnp.arange(N_DEV) == blk).astype(F32)[None, None, :, None]
    mine = {n: jnp.sum(total[n].reshape(total[n].shape[:2] + (N_DEV, csh)) * pick, axis=2) for n in _SMALL_SHARDED}
    pw, lay_s = _pack([W[n] for n in _SMALL_SHARDED])
    pg, _ = _pack([mine[n] for n in _SMALL_SHARDED])
    pm, _ = _pack([M[n] for n in _SMALL_SHARDED])
    pv, _ = _pack([V[n] for n in _SMALL_SHARDED])
    dd, mm, vv = _adamw_small("adamw_conv_taps", pw, pg, pm, pv)
    for n, a, b, c_ in zip(_SMALL_SHARDED, _unpack(dd, lay_s), _unpack(mm, lay_s), _unpack(vv, lay_s)):
        out_g[n], out_d[n], out_m[n], out_v[n] = mine[n], a, b, c_
    for n in _BIG:
        for k, dst in enumerate((out_g, out_d, out_m, out_v)):
            full = _BIG_FULL[n]
            dst[full] = jnp.swapaxes(big_out[n][k], 1, 2) if full in turned else big_out[n][k]

    return (loss_all, dh[None], *[out_g[n] for n in _WEIGHTS], *[out_d[n] for n in _WEIGHTS],
            *[out_m[n] for n in _WEIGHTS], *[out_v[n] for n in _WEIGHTS])


def kernel(x, p, norm_mix_g, w_in, sgu_ln_g, sgu_ln_b, sgu_w, sgu_b, sc_conv_w, cf_conv_w, cf_conv_b, cf_ln_g, cf_ln_b, pool_w, pool_scale, w_out, norm_ffn_g, w_gate, w_up, w_down, norm_ple_g, w_ple_gate, w_ple_proj, final_norm_g, loss_target, m_norm_mix_g, m_w_in, m_sgu_ln_g, m_sgu_ln_b, m_sgu_w, m_sgu_b, m_sc_conv_w, m_cf_conv_w, m_cf_conv_b, m_cf_ln_g, m_cf_ln_b, m_pool_w, m_pool_scale, m_w_out, m_norm_ffn_g, m_w_gate, m_w_up, m_w_down, m_norm_ple_g, m_w_ple_gate, m_w_ple_proj, m_final_norm_g, v_norm_mix_g, v_w_in, v_sgu_ln_g, v_sgu_ln_b, v_sgu_w, v_sgu_b, v_sc_conv_w, v_cf_conv_w, v_cf_conv_b, v_cf_ln_g, v_cf_ln_b, v_pool_w, v_pool_scale, v_w_out, v_norm_ffn_g, v_w_gate, v_w_up, v_w_down, v_norm_ple_g, v_w_ple_gate, v_w_ple_proj, v_final_norm_g):
    given = dict(locals())
    W = {n: given[n] for n in _WEIGHTS}
    M = {n: given["m_" + n] for n in _WEIGHTS}
    V = {n: given["v_" + n] for n in _WEIGHTS}
    return _step(W, M, V, x, p, loss_target)
```

```python
import functools
import math

import jax
import jax.numpy as jnp
from jax import lax
from jax.experimental import pallas as pl
from jax.experimental.pallas import tpu as pltpu

F32 = jnp.float32
BF16 = jnp.bfloat16
EPS = 1e-6
HEAD = 128
GROUP = 4 * HEAD
HALO = 32
SHORT_K = 3
CONF_K = 31
POOL_WINDOWS = (2, 4, 8, 16)
N_DEV = 8
MESH = pl.DeviceIdType.MESH
VMEM_LIMIT_BYTES = 56 * 1024 * 1024

ADAM_LR = 0.001
ADAM_B1 = 0.9
ADAM_B2 = 0.999
ADAM_EPS = 1e-08
ADAM_WD = 0.01
ADAM_STEP = 10

ANY = pl.BlockSpec(memory_space=pl.ANY)


def _params(*sem):
    return pltpu.CompilerParams(dimension_semantics=sem, vmem_limit_bytes=VMEM_LIMIT_BYTES)


def _rms(x, g):
    return x * lax.rsqrt(jnp.mean(x * x, axis=-1, keepdims=True) + EPS) * g


def _ln(x, g, b):
    mu = jnp.mean(x, axis=-1, keepdims=True)
    xc = x - mu
    var = jnp.mean(xc * xc, axis=-1, keepdims=True)
    return xc * lax.rsqrt(var + EPS) * g + b


def _bdot(a, b, dims=(((1,), (0,)), ((), ()))):
    return lax.dot_general(a.astype(BF16), b.astype(BF16), dims, preferred_element_type=F32)


def _sgu_piece(zu, zv, lg, lb, w, b):
    u = jax.nn.gelu(zu)
    v = _ln(jax.nn.gelu(zv), lg, lb)
    row = lax.broadcasted_iota(jnp.int32, w.shape, 0)
    col = lax.broadcasted_iota(jnp.int32, w.shape, 1)
    wm = jnp.where(row >= col, w, 0.0)
    return u * (_bdot(wm, v) + b)


def _conf_post(c, g, b):
    return jax.nn.silu(_ln(c, g, b))


def _pool_count(first_pos, rows, w):
    pos = first_pos + lax.broadcasted_iota(jnp.int32, (rows, 1), 0) + 1
    return jnp.minimum(pos, w).astype(F32)


_DIMS = {
    "nn": (((1,), (0,)), ((), ())),
    "nt": (((1,), (1,)), ((), ())),
    "tn": (((0,), (0,)), ((), ())),
}


def _tiles(name, grid, ins, in_specs, outs, out_specs, compute, summed=(), deps=()):
    ni = len(ins)
    nd = len(deps)

    def body(*refs):
        vals = compute(*refs[:ni])
        first = functools.reduce(jnp.logical_and, [pl.program_id(a) == 0 for a in range(len(grid))])
        for idx, (r, v) in enumerate(zip(refs[ni + nd:], vals)):
            if idx in summed:
                @pl.when(first)
                def _(r=r):
                    r[...] = jnp.zeros_like(r)

                r[...] += v
            else:
                r[...] = v.astype(r.dtype)

    sem = ("arbitrary" if summed else "parallel",) * len(grid)
    return pl.pallas_call(
        body, name=name, grid=grid, in_specs=list(in_specs) + [ANY] * nd, out_specs=list(out_specs),
        out_shape=list(outs), compiler_params=_params(*sem),
    )(*ins, *deps)


def _resident(shape):
    return pl.BlockSpec(shape, lambda *_: (0,) * len(shape), pipeline_mode=pl.Buffered(1))


def _sds(shape, dtype):
    return jax.ShapeDtypeStruct(tuple(shape), dtype)


def _tile(n, want):
    t = min(n, want)
    assert n % t == 0, (n, want)
    return t


def _rms_fwd(name, h, g):
    s, d = h.shape
    tm = _tile(s, 512)

    def body(h_ref, g_ref, y_ref):
        y_ref[...] = _rms(h_ref[...], g_ref[...]).astype(BF16)

    return pl.pallas_call(
        body, name=name, grid=(s // tm,),
        in_specs=[pl.BlockSpec((tm, d), lambda i: (i, 0)), pl.BlockSpec((1, d), lambda i: (0, 0))],
        out_specs=pl.BlockSpec((tm, d), lambda i: (i, 0)),
        out_shape=_sds((s, d), BF16),
        compiler_params=_params("parallel"),
    )(h, g)


def _rms_back(h, g, dy, dh_in):
    inv = lax.rsqrt(jnp.mean(h * h, axis=-1, keepdims=True) + EPS)
    xhat = h * inv
    dxhat = dy * g
    dh = inv * (dxhat - xhat * jnp.mean(dxhat * xhat, axis=-1, keepdims=True))
    return dh_in + dh, jnp.sum(dy * xhat, axis=0, keepdims=True)


def _loss_head(h, g, target):
    s, d = h.shape
    tm = _tile(s, 256)

    def body(h_ref, g_ref, t_ref, loss_ref, dh_ref, dg_ref):
        y, vjp = jax.vjp(_rms, h_ref[...], g_ref[...])
        err = y - t_ref[...]
        dh, dg = vjp(err * (1.0 / d))
        dh_ref[...] = dh
        per_token = jnp.mean(err * err, axis=-1, keepdims=True)
        part = 0.5 * jnp.sum(per_token, axis=0, keepdims=True)

        @pl.when(pl.program_id(0) == 0)
        def _():
            dg_ref[...] = jnp.zeros_like(dg_ref)
            loss_ref[...] = jnp.zeros_like(loss_ref)

        dg_ref[...] += dg
        loss_ref[...] += part

    tok = pl.BlockSpec((tm, d), lambda i: (i, 0))
    vec = pl.BlockSpec((1, d), lambda i: (0, 0))
    return pl.pallas_call(
        body, name="loss_head", grid=(s // tm,),
        in_specs=[tok, vec, tok],
        out_specs=[pl.BlockSpec((1, 1), lambda i: (0, 0)), tok, vec],
        out_shape=[_sds((1, 1), F32), _sds((s, d), F32), _sds((1, d), F32)],
        compiler_params=_params("arbitrary"),
    )(h, g, target)


_U, _V, _HB, _BG, _CG, _CA, _CGT, _PD = (GROUP * i for i in range(8))


def _cols(c0, w=GROUP):
    return slice(c0, c0 + w)


_LANE_CHUNKS = tuple(_cols(i * HEAD, HEAD) for i in range(GROUP // HEAD))


def _shifted_copies(sh_ref, ext_ref, length):
    for r in range(1, 8):
        sh_ref[r - 1, 0:length, :] = ext_ref[pl.ds(r, length), :]


def _window(ext_ref, sh_ref, off, rows, cc):
    q, r = divmod(off, 8)
    if sh_ref is None or r == 0:
        return ext_ref[pl.ds(off, rows), cc]
    return sh_ref[r - 1, pl.ds(8 * q, rows), cc]


def _taps(ext_ref, w_ref, nk, start, rows, cc, flip=False, sh_ref=None):
    acc = None
    for k in range(nk):
        kw = nk - 1 - k if flip else k
        term = w_ref[kw:kw + 1, cc] * _window(ext_ref, sh_ref, start + k, rows, cc)
        acc = term if acc is None else acc + term
    return acc


def _mixer_param_specs():
    full = lambda *shape: pl.BlockSpec(shape, lambda i: (0,) * len(shape))
    return [
        full(4, 1, HEAD), full(4, 1, HEAD), full(4, HEAD, HEAD), full(4, HEAD, 1),
        full(8, GROUP), full(32, GROUP), full(1, GROUP), full(1, GROUP), full(1, GROUP),
        full(4, HEAD, HEAD), full(1, GROUP),
    ]


def _mixer_fwd(name, z, prm):
    s = z.shape[0]
    t = _tile(s, 256)
    hb = t // HALO

    def body(zp_ref, zm_ref, lg_ref, lb_ref, sw_ref, sb_ref, scw_ref, cfw_ref, cfb_ref, cg_ref, cb_ref,
             pw_ref, ps_ref, o_ref, ext_ref, sh_ref):
        i = pl.program_id(0)
        keep = (i > 0).astype(F32)
        main = pl.ds(HALO, t)

        for n in range(t // HEAD):
            rows = slice(n * HEAD, (n + 1) * HEAD)
            for hh in range(4):
                cu = _cols(_U + hh * HEAD, HEAD)
                cv = _cols(_V + hh * HEAD, HEAD)
                o_ref[rows, _cols(hh * HEAD, HEAD)] = _sgu_piece(
                    zm_ref[rows, cu], zm_ref[rows, cv], lg_ref[hh], lb_ref[hh], sw_ref[hh], sb_ref[hh]
                ).astype(BF16)

        ext_ref[0:HALO, :] = zp_ref[:, _cols(_CG)] * zp_ref[:, _cols(_HB)] * keep
        ext_ref[main, :] = zm_ref[:, _cols(_CG)] * zm_ref[:, _cols(_HB)]
        y = jnp.zeros((t, GROUP), F32)
        for k in range(SHORT_K):
            y = y + scw_ref[k:k + 1, :] * ext_ref[pl.ds(HALO - (SHORT_K - 1) + k, t), :]
        o_ref[:, _cols(GROUP)] = (zm_ref[:, _cols(_BG)] * y).astype(BF16)

        ext_ref[0:HALO, :] = zp_ref[:, _cols(_CA)] * jax.nn.sigmoid(zp_ref[:, _cols(_CGT)]) * keep
        ext_ref[main, :] = zm_ref[:, _cols(_CA)] * jax.nn.sigmoid(zm_ref[:, _cols(_CGT)])
        _shifted_copies(sh_ref, ext_ref, HALO + t - 8)
        c = jnp.concatenate(
            [cfb_ref[:, cc] + _taps(ext_ref, cfw_ref, CONF_K, HALO - (CONF_K - 1), t, cc, sh_ref=sh_ref)
             for cc in _LANE_CHUNKS], axis=1)
        o_ref[:, _cols(2 * GROUP)] = _conf_post(c, cg_ref[...], cb_ref[...]).astype(BF16)

        ext_ref[0:HALO, :] = zp_ref[:, _cols(_PD)] * keep
        ext_ref[main, :] = zm_ref[:, _cols(_PD)]
        for gi, w in enumerate(POOL_WINDOWS):
            cc = _cols(gi * HEAD, HEAD)
            acc = ext_ref[main, cc]
            for j in range(1, w):
                acc = acc + ext_ref[pl.ds(HALO - j, t), cc]
            q = acc / _pool_count(i * t, t, w) - ext_ref[main, cc]
            o_ref[:, _cols(3 * GROUP + gi * HEAD, HEAD)] = (_bdot(q, pw_ref[gi]) * ps_ref[:, cc]).astype(BF16)

    return pl.pallas_call(
        body, name=name, grid=(s // t,),
        in_specs=[pl.BlockSpec((HALO, 8 * GROUP), lambda i: (jnp.maximum(i * hb - 1, 0), 0)),
                  pl.BlockSpec((t, 8 * GROUP), lambda i: (i, 0)),
                  *_mixer_param_specs()],
        out_specs=pl.BlockSpec((t, 4 * GROUP), lambda i: (i, 0)),
        out_shape=_sds((s, 4 * GROUP), BF16),
        scratch_shapes=[pltpu.VMEM((HALO + t, GROUP), F32), pltpu.VMEM((7, HALO + t - 8, GROUP), F32)],
        compiler_params=_params("parallel"),
    )(z, z, *prm)


def _mixer_bwd(name, z, dmo, prm):
    s = z.shape[0]
    t = _tile(s, 256)
    hb = t // HALO
    nt = s // t
    last_halo = s // HALO - 1

    def body(zp_ref, zm_ref, zn_ref, dm_ref, dn_ref,
             lg_ref, lb_ref, sw_ref, sb_ref, scw_ref, cfw_ref, cfb_ref, cg_ref, cb_ref, pw_ref, ps_ref,
             dz_ref, dlg_ref, dlb_ref, dsw_ref, dsb_ref, dscw_ref, dcfw_ref, dcfb_ref, dcg_ref, dcb_ref,
             dpw_ref, dps_ref, extp_ref, extn_ref, extc_ref, sh_ref):
        i = pl.program_id(0)
        keep_prev = (i > 0).astype(F32)
        keep_next = (i < nt - 1).astype(F32)
        main = pl.ds(HALO, t)

        @pl.when(i == 0)
        def _():
            for r in (dlg_ref, dlb_ref, dsw_ref, dsb_ref, dscw_ref, dcfw_ref, dcfb_ref, dcg_ref, dcb_ref,
                      dpw_ref, dps_ref):
                r[...] = jnp.zeros_like(r)

        def rowsum(v):
            return jnp.sum(v, axis=0, keepdims=True)

        for n in range(t // HEAD):
            rows = slice(n * HEAD, (n + 1) * HEAD)
            for hh in range(4):
                cu = _cols(_U + hh * HEAD, HEAD)
                cv = _cols(_V + hh * HEAD, HEAD)
                _, vjp = jax.vjp(_sgu_piece, zm_ref[rows, cu], zm_ref[rows, cv],
                                 lg_ref[hh], lb_ref[hh], sw_ref[hh], sb_ref[hh])
                dzu, dzv, dlg, dlb, dsw, dsb = vjp(dm_ref[rows, _cols(hh * HEAD, HEAD)].astype(F32))
                dz_ref[rows, cu] = dzu.astype(BF16)
                dz_ref[rows, cv] = dzv.astype(BF16)
                dlg_ref[hh] += dlg
                dlb_ref[hh] += dlb
                dsw_ref[hh] += dsw
                dsb_ref[hh] += dsb

        extp_ref[0:HALO, :] = zp_ref[:, _cols(_CG)] * zp_ref[:, _cols(_HB)] * keep_prev
        extp_ref[main, :] = zm_ref[:, _cols(_CG)] * zm_ref[:, _cols(_HB)]
        dob = dm_ref[:, _cols(GROUP)].astype(F32)
        dy = dob * zm_ref[:, _cols(_BG)]
        extn_ref[0:t, :] = dy
        extn_ref[t:t + HALO, :] = dn_ref[:, _cols(GROUP)].astype(F32) * zn_ref[:, _cols(_BG)] * keep_next
        for cc in _LANE_CHUNKS:
            at = lambda c0: _cols(c0 + cc.start, HEAD)
            y = _taps(extp_ref, scw_ref, SHORT_K, HALO - (SHORT_K - 1), t, cc)
            dx = _taps(extn_ref, scw_ref, SHORT_K, 0, t, cc, flip=True)
            dy_c = extn_ref[0:t, cc]
            for k in range(SHORT_K):
                dscw_ref[k:k + 1, cc] += rowsum(dy_c * extp_ref[pl.ds(HALO - (SHORT_K - 1) + k, t), cc])
            dz_ref[:, at(_BG)] = (dm_ref[:, at(GROUP)].astype(F32) * y).astype(BF16)
            dz_ref[:, at(_CG)] = (dx * zm_ref[:, at(_HB)]).astype(BF16)
            dz_ref[:, at(_HB)] = (dx * zm_ref[:, at(_CG)]).astype(BF16)

        extc_ref[0:HALO, :] = zp_ref[:, _cols(_CA)] * jax.nn.sigmoid(zp_ref[:, _cols(_CGT)]) * keep_prev
        extc_ref[main, :] = zm_ref[:, _cols(_CA)] * jax.nn.sigmoid(zm_ref[:, _cols(_CGT)])
        extc_ref[HALO + t:HALO + t + HALO, :] = zn_ref[:, _cols(_CA)] * jax.nn.sigmoid(zn_ref[:, _cols(_CGT)])
        _shifted_copies(sh_ref, extc_ref, HALO + t + HALO - 8)
        conv = lambda start, rows: jnp.concatenate(
            [cfb_ref[:, cc] + _taps(extc_ref, cfw_ref, CONF_K, start, rows, cc, sh_ref=sh_ref) for cc in _LANE_CHUNKS],
            axis=1)
        c_main = conv(HALO - (CONF_K - 1), t)
        c_next = conv(HALO + t - (CONF_K - 1), HALO)
        _, vjp = jax.vjp(_conf_post, c_main, cg_ref[...], cb_ref[...])
        dc, dcg, dcb = vjp(dm_ref[:, _cols(2 * GROUP)].astype(F32))
        dcg_ref[...] += dcg
        dcb_ref[...] += dcb
        dcfb_ref[...] += rowsum(dc)
        _, vjp_next = jax.vjp(lambda cv: _conf_post(cv, cg_ref[...], cb_ref[...]), c_next)
        (dc_next,) = vjp_next(dn_ref[:, _cols(2 * GROUP)].astype(F32) * keep_next)
        extn_ref[0:t, :] = dc
        extn_ref[t:t + HALO, :] = dc_next
        for cc in _LANE_CHUNKS:
            dc_c = extn_ref[0:t, cc]
            for k in range(CONF_K):
                dcfw_ref[k:k + 1, cc] += rowsum(dc_c * _window(extc_ref, sh_ref, HALO - (CONF_K - 1) + k, t, cc))
        _shifted_copies(sh_ref, extn_ref, t + HALO - 8)
        for cc in _LANE_CHUNKS:
            at = lambda c0: _cols(c0 + cc.start, HEAD)
            dhc = _taps(extn_ref, cfw_ref, CONF_K, 0, t, cc, flip=True, sh_ref=sh_ref)
            sg = jax.nn.sigmoid(zm_ref[:, at(_CGT)])
            dz_ref[:, at(_CA)] = (dhc * sg).astype(BF16)
            dz_ref[:, at(_CGT)] = (dhc * zm_ref[:, at(_CA)] * sg * (1.0 - sg)).astype(BF16)

        extp_ref[0:HALO, :] = zp_ref[:, _cols(_PD)] * keep_prev
        extp_ref[main, :] = zm_ref[:, _cols(_PD)]
        for gi, w in enumerate(POOL_WINDOWS):
            cc = _cols(gi * HEAD, HEAD)
            oc = _cols(3 * GROUP + gi * HEAD, HEAD)
            acc = extp_ref[main, cc]
            for j in range(1, w):
                acc = acc + extp_ref[pl.ds(HALO - j, t), cc]
            count = _pool_count(i * t, t, w)
            q = acc / count - extp_ref[main, cc]
            dod = dm_ref[:, oc].astype(F32)
            dps_ref[:, cc] += rowsum(dod * _bdot(q, pw_ref[gi]))
            ds = dod * ps_ref[:, cc]
            dpw_ref[gi] += _bdot(q, ds, _DIMS["tn"])
            dq = _bdot(ds, pw_ref[gi], _DIMS["nt"])
            ds_next = dn_ref[:, oc].astype(F32) * ps_ref[:, cc] * keep_next
            dq_next = _bdot(ds_next, pw_ref[gi], _DIMS["nt"])
            extn_ref[0:t, cc] = dq / count
            extn_ref[t:t + HALO, cc] = dq_next * (1.0 / w)
            back = extn_ref[0:t, cc]
            for j in range(1, w):
                back = back + extn_ref[pl.ds(j, t), cc]
            dz_ref[:, _cols(_PD + gi * HEAD, HEAD)] = (back - dq).astype(BF16)

    full = lambda *shape: pl.BlockSpec(shape, lambda i: (0,) * len(shape))
    grad_specs = [full(4, 1, HEAD), full(4, 1, HEAD), full(4, HEAD, HEAD), full(4, HEAD, 1),
                  full(8, GROUP), full(32, GROUP), full(1, GROUP), full(1, GROUP), full(1, GROUP),
                  full(4, HEAD, HEAD), full(1, GROUP)]
    grad_shapes = [_sds(sp.block_shape, F32) for sp in grad_specs]
    nxt = lambda i: (jnp.minimum((i + 1) * hb, last_halo), 0)
    return pl.pallas_call(
        body, name=name, grid=(nt,),
        in_specs=[pl.BlockSpec((HALO, 8 * GROUP), lambda i: (jnp.maximum(i * hb - 1, 0), 0)),
                  pl.BlockSpec((t, 8 * GROUP), lambda i: (i, 0)),
                  pl.BlockSpec((HALO, 8 * GROUP), nxt),
                  pl.BlockSpec((t, 4 * GROUP), lambda i: (i, 0)),
                  pl.BlockSpec((HALO, 4 * GROUP), nxt),
                  *_mixer_param_specs()],
        out_specs=[pl.BlockSpec((t, 8 * GROUP), lambda i: (i, 0)), *grad_specs],
        out_shape=[_sds((s, 8 * GROUP), BF16), *grad_shapes],
        scratch_shapes=[pltpu.VMEM((HALO + t, GROUP), F32), pltpu.VMEM((t + HALO, GROUP), F32),
                        pltpu.VMEM((HALO + t + HALO, GROUP), F32), pltpu.VMEM((7, HALO + t + HALO - 8, GROUP), F32)],
        compiler_params=_params("arbitrary"),
    )(z, z, z, dmo, dmo, *prm)


def _dot(a, b, kind="nn"):
    return _bdot(a, b, _DIMS[kind])


def _chain(terms, kind):
    acc = None
    for a, b in terms:
        p = _dot(a, b, kind)
        acc = p if acc is None else acc + p
    return acc


def _after(g, token):
    return g if token is None else g + token[0:1, 0:1]


def _layer_fwd(li, h, p_all, fetch, sm, y1=None, next_mix_g=None):
    s, d = h.shape
    nb = N_DEV
    pd = p_all.shape[2]
    sv = {"h": h}
    rows = lambda t, w: pl.BlockSpec((t, w), lambda i: (i, 0))
    vec = pl.BlockSpec((1, d), lambda i: (0, 0))

    if y1 is None:
        y1 = _rms_fwd("rms_mix", h, sm["norm_mix_g"])
    tm = _tile(s, 1024)
    (z,) = _tiles(
        "w_in_fwd", (s // tm, nb),
        [y1, fetch("win", y1)],
        [pl.BlockSpec((tm, d), lambda i, j: (i, 0)), pl.BlockSpec((None, d, GROUP), lambda i, j: (j, 0, 0))],
        [_sds((s, nb * GROUP), F32)], [pl.BlockSpec((tm, GROUP), lambda i, j: (i, j))],
        lambda a, w: (_dot(a[...], w[...]),))
    mo = _mixer_fwd("mixer_fwd", z, sm["mixer"])
    tm = _tile(s, 512)
    def mix_out(a, w, hv, gv):
        h2v = hv[...] + _dot(a[...], w[...])
        return h2v, _rms(h2v, gv[...])

    h2, y2 = _tiles(
        "w_out_fwd", (s // tm,),
        [mo, fetch("wout", z), h, sm["norm_ffn_g"]], [rows(tm, d), _resident((d, d)), rows(tm, d), vec],
        [_sds((s, d), F32), _sds((s, d), BF16)], [rows(tm, d), rows(tm, d)], mix_out)
    sv.update(y1=y1, z=z, mo=mo, h2=h2)

    gate_pre, up_pre, hmid = _ffn_up("ffn_up_fwd", y2, fetch("wg", z), fetch("wu", z))
    wd = fetch("wd", hmid)
    f8 = wd.shape[1]
    tm, tn = _tile(s, 512), _tile(d, 1024)
    (h3,) = _tiles(
        "w_down_fwd", (d // tn, s // tm),
        [hmid, wd, h2],
        [pl.BlockSpec((nb, tm, f8), lambda j, i: (0, i, 0)), pl.BlockSpec((nb, f8, tn), lambda j, i: (0, 0, j)),
         pl.BlockSpec((tm, tn), lambda j, i: (i, j))],
        [_sds((s, d), F32)], [pl.BlockSpec((tm, tn), lambda j, i: (i, j))],
        lambda a, w, hv: (hv[...] + _chain([(a[k], w[k]) for k in range(nb)], "nn"),))
    sv.update(y2=y2, gate_pre=gate_pre, up_pre=up_pre, hmid=hmid, h3=h3)

    y3 = _rms_fwd("rms_ple", h3, sm["norm_ple_g"])
    tm = _tile(s, 1024)
    (pp,) = _tiles(
        "w_ple_proj_fwd", (s // tm, nb),
        [p_all, fetch("wpp", hmid)],
        [pl.BlockSpec((None, tm, pd), lambda i, j: (li, i, 0)), pl.BlockSpec((None, pd, pd), lambda i, j: (j, 0, 0))],
        [_sds((s, d), BF16)], [pl.BlockSpec((tm, pd), lambda i, j: (i, j))],
        lambda a, w: (_dot(a[...], w[...]),))
    tm = _tile(s, 256)

    def ple(a, w, hv, ppv, *next_g):
        pg = _dot(a[...], w[...])
        h4v = hv[...] + jax.nn.sigmoid(pg) * ppv[...].astype(F32)
        return (h4v, pg) + tuple(_rms(h4v, gv[...]) for gv in next_g)

    more = [] if next_mix_g is None else [next_mix_g]
    h4, pg_pre, *y1_next = _tiles(
        "w_ple_gate_fwd", (s // tm,),
        [y3, fetch("wpg", hmid), h3, pp] + more,
        [rows(tm, d), _resident((d, d)), rows(tm, d), rows(tm, d)] + [vec] * len(more),
        [_sds((s, d), F32), _sds((s, d), BF16)] + [_sds((s, d), BF16)] * len(more),
        [rows(tm, d)] * (2 + len(more)), ple)
    sv.update(y3=y3, pp=pp, pg_pre=pg_pre)
    return h4, sv, (y1_next[0] if y1_next else None)


def _ffn_up(name, y2, wg, wu):
    s, d = y2.shape
    nb, f8, _ = wg.shape
    tm = _tile(s, 1024)

    def compute(y_ref, wg_ref, wu_ref):
        yv = y_ref[...]
        g = _dot(yv, wg_ref[...], "nt")
        u = _dot(yv, wu_ref[...], "nt")
        return g, u, jax.nn.silu(g) * u

    wspec = pl.BlockSpec((None, f8, d), lambda i, j: (j, 0, 0))
    ospec = pl.BlockSpec((None, tm, f8), lambda i, j: (j, i, 0))
    return _tiles(name, (s // tm, nb), [y2, wg, wu], [pl.BlockSpec((tm, d), lambda i, j: (i, 0)), wspec, wspec],
                  [_sds((nb, s, f8), BF16)] * 3, [ospec] * 3, compute)


def _swiglu_bwd(dm, g, u):
    sg = jax.nn.sigmoid(g)
    silu = g * sg
    return dm * u * (sg + silu * (1.0 - sg)), dm * silu


def _layer_bwd(li, dh, p_all, gw, sm, sv, start_token=None, hooks=None):
    hooks = hooks or {}
    run = lambda name, *a: hooks[name](*a) if name in hooks else None
    s, d = dh.shape
    nb, f8, _ = gw["wg"].shape
    pd = p_all.shape[2]
    gr, sg = {}, {}
    rows = lambda t, w: pl.BlockSpec((t, w), lambda i: (i, 0))
    vec = pl.BlockSpec((1, d), lambda i: (0, 0))
    tw = _tile(d, 1024)
    whole = lambda w: pl.BlockSpec((s, w), lambda i, j: (0, i))
    whole_j = lambda w: pl.BlockSpec((s, w), lambda i, j: (0, j))
    tn_dot = lambda a, b: (_dot(a[...], b[...], "tn"),)

    tm = _tile(s, 256)

    def ple_dx(dv, pg, ppv, w, hv, gv):
        dhv = dv[...]
        gate = jax.nn.sigmoid(pg[...].astype(F32))
        dpg = (dhv * ppv[...].astype(F32) * gate * (1.0 - gate)).astype(BF16)
        dh3, dg = _rms_back(hv[...], gv[...], _dot(dpg, w[...], "nt"), dhv)
        return dh3, dh3, dg, dpg, dhv * gate

    dh3, dh3_16, sg["norm_ple_g"], dpg, dpp = _tiles(
        "w_ple_gate_dx", (s // tm,),
        [dh, sv["pg_pre"], sv["pp"], gw["wpg"], sv["h3"], _after(sm["norm_ple_g"], start_token)],
        [rows(tm, d), rows(tm, d), rows(tm, d), _resident((d, d)), rows(tm, d), vec],
        [_sds((s, d), F32), _sds((s, d), BF16), _sds((1, d), F32), _sds((s, d), BF16), _sds((s, d), BF16)],
        [rows(tm, d), rows(tm, d), vec, rows(tm, d), rows(tm, d)], ple_dx, summed=(2,))
    (gr["wpp"],) = _tiles(
        "w_ple_proj_dw", (nb,),
        [p_all, dpp], [pl.BlockSpec((None, s, pd), lambda j: (li, 0, 0)), pl.BlockSpec((s, pd), lambda j: (0, j))],
        [_sds((nb, pd, pd), F32)], [pl.BlockSpec((None, pd, pd), lambda j: (j, 0, 0))], tn_dot)
    (gr["wpg"],) = _tiles(
        "w_ple_gate_dw", (d // tw, d // tw), [sv["y3"], dpg], [whole(tw), whole_j(tw)],
        [_sds((d, d), F32)], [pl.BlockSpec((tw, tw), lambda i, j: (i, j))], tn_dot)
    gr["wpg"] = gr["wpg"].reshape(nb, d // nb, d)

    tm = _tile(s, 1024)
    blk_rows = pl.BlockSpec((None, tm, f8), lambda i, j: (j, i, 0))
    dgate, dup = _tiles(
        "w_down_dx", (s // tm, nb),
        [dh3_16, gw["wd"], sv["gate_pre"], sv["up_pre"]],
        [pl.BlockSpec((tm, d), lambda i, j: (i, 0)), pl.BlockSpec((None, f8, d), lambda i, j: (j, 0, 0)), blk_rows, blk_rows],
        [_sds((nb, s, f8), BF16)] * 2, [blk_rows] * 2,
        lambda a, w, g, u: _swiglu_bwd(_dot(a[...], w[...], "nt"), g[...].astype(F32), u[...].astype(F32)))
    (gr["wd"],) = _tiles(
        "w_down_dw", (nb, d // tw),
        [sv["hmid"], dh3_16], [pl.BlockSpec((None, s, f8), lambda i, j: (i, 0, 0)), whole_j(tw)],
        [_sds((nb, f8, d), F32)], [pl.BlockSpec((None, f8, tw), lambda i, j: (i, 0, j))], tn_dot)
    mid_token = run("after_down_dx", dgate)
    for nm, dact in (("wg", dgate), ("wu", dup)):
        (gr[nm],) = _tiles(
            "w_" + {"wg": "gate", "wu": "up"}[nm] + "_dw", (d // tw, nb),
            [dact, sv["y2"]], [pl.BlockSpec((None, s, f8), lambda i, j: (j, 0, 0)), whole(tw)],
            [_sds((nb, f8, d), F32)], [pl.BlockSpec((None, f8, tw), lambda i, j: (j, 0, i))], tn_dot)
    ffn_token = run("after_ffn_grads", gr, gr["wu"])
    tm, tn = _tile(s, 512), _tile(d, 1024)
    act_spec = pl.BlockSpec((nb, tm, f8), lambda j, i: (0, i, 0))
    wt_spec = pl.BlockSpec((nb, f8, tn), lambda j, i: (0, 0, j))
    out_spec = pl.BlockSpec((tm, tn), lambda j, i: (i, j))
    (dy2,) = _tiles(
        "w_gate_dx", (d // tn, s // tm), [dgate, gw["wg"]], [act_spec, wt_spec],
        [_sds((s, d), F32)], [out_spec],
        lambda a, w: (_chain([(a[k], w[k]) for k in range(nb)], "nn"),),
        deps=[tk for tk in (mid_token, ffn_token) if tk is not None])
    (dy2,) = _tiles(
        "w_up_dx", (d // tn, s // tm), [dup, gw["wu"], dy2], [act_spec, wt_spec, out_spec],
        [_sds((s, d), F32)], [out_spec],
        lambda a, w, prev: (prev[...] + _chain([(a[k], w[k]) for k in range(nb)], "nn"),))
    g_ffn = _after(sm["norm_ffn_g"], run("before_ffn_norm", dy2))

    tm = _tile(s, 256)

    def out_dx(dy, hv, gv, dv, w):
        dh2v, dg = _rms_back(hv[...], gv[...], dy[...], dv[...])
        return dh2v, dh2v, dg, _dot(dh2v, w[...], "nt")

    dh2, dh2_16, sg["norm_ffn_g"], dmo = _tiles(
        "w_out_dx", (s // tm,), [dy2, sv["h2"], g_ffn, dh3, gw["wout"]],
        [rows(tm, d), rows(tm, d), vec, rows(tm, d), _resident((d, d))],
        [_sds((s, d), F32), _sds((s, d), BF16), _sds((1, d), F32), _sds((s, d), BF16)],
        [rows(tm, d), rows(tm, d), vec, rows(tm, d)], out_dx, summed=(2,))
    (gr["wout"],) = _tiles(
        "w_out_dw", (d // tw, d // tw), [sv["mo"], dh2_16], [whole(tw), whole_j(tw)],
        [_sds((d, d), F32)], [pl.BlockSpec((tw, tw), lambda i, j: (i, j))], tn_dot)
    gr["wout"] = gr["wout"].reshape(nb, d // nb, d)
    dz, *mix_grads = _mixer_bwd("mixer_bwd", sv["z"], dmo, sm["mixer"])
    sg["mixer"] = mix_grads
    (gr["win"],) = _tiles(
        "w_in_dw", (d // tw, nb), [sv["y1"], dz], [whole(tw), pl.BlockSpec((s, GROUP), lambda i, j: (0, j))],
        [_sds((nb, d, GROUP), F32)], [pl.BlockSpec((None, tw, GROUP), lambda i, j: (j, i, 0))], tn_dot)
    in_token = run("after_in_dw", gr)
    tm = _tile(s, 256)

    def in_dx(a, w, hv, gv, dv):
        dy1 = _chain([(a[:, k * GROUP:(k + 1) * GROUP], w[k]) for k in range(nb)], "nt")
        return _rms_back(hv[...], gv[...], dy1, dv[...])

    dh_in, sg["norm_mix_g"] = _tiles(
        "w_in_dx", (s // tm,),
        [dz, gw["win"], sv["h"], sm["norm_mix_g"], dh2],
        [rows(tm, nb * GROUP), _resident((nb, d, GROUP)), rows(tm, d), vec, rows(tm, d)],
        [_sds((s, d), F32), _sds((1, d), F32)], [rows(tm, d), vec], in_dx, summed=(1,),
        deps=[] if in_token is None else [in_token])
    return dh_in, gr, sg


def _place():
    return lax.axis_index("x"), lax.axis_index("y"), lax.axis_index("c")


_HBM = pl.BlockSpec(memory_space=pltpu.HBM)
_SEM = pl.BlockSpec(memory_space=pltpu.SEMAPHORE)


def _split_call(name, bufs, old_sems, n_new, after, body, want_token):
    nb, no = len(bufs), len(old_sems)
    extra = [] if after is None else [after]

    def kbody(*refs):
        new = refs[nb + no + len(extra):nb + no + len(extra) + n_new]
        body(refs[:nb], refs[nb:nb + no], new)
        if want_token:
            refs[-1][...] = jnp.zeros_like(refs[-1])

    outs = pl.pallas_call(
        kbody, name=name,
        out_shape=tuple([pltpu.SemaphoreType.DMA(())] * n_new + [pltpu.HBM(b.shape, b.dtype) for b in bufs]
                        + ([_sds((8, 128), F32)] if want_token else [])),
        in_specs=[_HBM] * nb + [_SEM] * no + [ANY] * len(extra),
        out_specs=tuple([_SEM] * n_new + [_HBM] * nb
                        + ([pl.BlockSpec(memory_space=pltpu.VMEM)] if want_token else [])),
        input_output_aliases={i: n_new + i for i in range(nb)},
        compiler_params=pltpu.CompilerParams(has_side_effects=pltpu.SideEffectType.DATAFLOW_SIDE_EFFECTING),
    )(*[pltpu.with_memory_space_constraint(b, pltpu.HBM) for b in bufs], *old_sems, *extra)
    return list(outs[:n_new]), list(outs[n_new:n_new + nb]), (outs[-1] if want_token else None)


def _remote(ref_src, ref_dst, send_sem, recv_sem, to):
    return pltpu.make_async_remote_copy(src_ref=ref_src, dst_ref=ref_dst, send_sem=send_sem, recv_sem=recv_sem,
                                        device_id=to, device_id_type=MESH)


def _place_shard(name, w):
    n_layers, r, cdim = w.shape
    tr = _row_tile(r, 256)
    nt = r // tr

    def body(w_ref, *rest):
        outs, buf, sem = rest[:n_layers], rest[n_layers], rest[n_layers + 1]
        i = pl.program_id(0)
        x, y, c = _place()
        slot = i % 2

        def writes(step, sl):
            rows = pl.ds(pl.multiple_of(step * tr, 16), tr)
            return [pltpu.make_async_copy(buf.at[sl, l], outs[l].at[4 * x + 2 * y + c, rows, :], sem.at[sl, l])
                    for l in range(n_layers)]

        @pl.when(i >= 2)
        def _():
            for cp in writes(i - 2, slot):
                cp.wait()

        buf[slot] = w_ref[...].astype(BF16)
        for cp in writes(i, slot):
            cp.start()

        @pl.when(i == nt - 1)
        def _():
            for cp in writes(i, slot):
                cp.wait()
            if nt >= 2:
                for cp in writes(i - 1, 1 - slot):
                    cp.wait()

    return pl.pallas_call(
        body, name=name, grid=(nt,),
        in_specs=[pl.BlockSpec((n_layers, tr, cdim), lambda i: (0, i, 0))], out_specs=[ANY] * n_layers,
        out_shape=[_sds((N_DEV, r, cdim), BF16)] * n_layers,
        scratch_shapes=[pltpu.VMEM((2, n_layers, tr, cdim), BF16), pltpu.SemaphoreType.DMA((2, n_layers))],
        compiler_params=_params("arbitrary"),
    )(w)


def _gather_start(li, lands, after):
    nw = len(lands)

    def body(bufs, _, new):
        x, y, c = _place()
        chips = [(1 - x, y), (x, 1 - y), (1 - x, 1 - y)]
        for w in range(nw):
            mine = bufs[w].at[4 * x + 2 * y + c]
            _remote(mine, mine, new[8 * w], new[8 * w + 4], (x, y, 1 - c)).start()
            for j, (px, py) in enumerate(chips):
                _remote(mine, mine, new[8 * w + 1 + j], new[8 * w + 5 + j], (px, py, c)).start()

    return _split_call("gather_start_l%s" % li, lands, [], 8 * nw, after, body, True)


def _arrivals(sems, nw):
    return [sems[8 * w + 5 + j] for w in range(nw) for j in range(3)]


def _pass_on(bufs, arrived, new):
    x, y, c = _place()
    for j, (px, py) in enumerate([(1 - x, y), (x, 1 - y), (1 - x, 1 - y)]):
        for w in range(len(bufs)):
            got = bufs[w].at[4 * px + 2 * py + c]
            _remote(got, got, new[6 * w + j], arrived[3 * w + j], (px, py, c)).wait_recv()
            _remote(got, got, new[6 * w + j], new[6 * w + 3 + j], (x, y, 1 - c)).start()


def _wait_rest(bufs, first, fwd):
    x, y, c = _place()
    sib = (x, y, 1 - c)
    for w in range(len(bufs)):
        mine = bufs[w].at[4 * x + 2 * y + c]
        theirs = bufs[w].at[4 * x + 2 * y + 1 - c]
        _remote(theirs, theirs, first[5 * w], first[5 * w + 4], sib).wait_recv()
        for k in range(4):
            _remote(mine, mine, first[5 * w + k], first[5 * w + 4], sib).wait_send()
        for j, (px, py) in enumerate([(1 - x, y), (x, 1 - y), (1 - x, 1 - y)]):
            sent = bufs[w].at[4 * px + 2 * py + c]
            got = bufs[w].at[4 * px + 2 * py + 1 - c]
            _remote(sent, sent, fwd[6 * w + j], fwd[6 * w + 3 + j], sib).wait_send()
            _remote(got, got, fwd[6 * w + j], fwd[6 * w + 3 + j], sib).wait_recv()


def _gather_forward(li, lands, sems, after):
    nw = len(lands)
    return _split_call("gather_forward_l%s" % li, lands, _arrivals(sems, nw), 6 * nw, after,
                       lambda bufs, old, new: _pass_on(bufs, old, new), True)


def _gather_wait(li, lands, sems, fwd_sems, after, ahead=None):
    nw = len(lands)
    first = [sems[8 * w + k] for w in range(nw) for k in range(5)]
    n1, n2 = len(first), len(first) + len(fwd_sems)
    more, arrived = ([], []) if ahead is None else (list(ahead[0]), _arrivals(ahead[1], len(ahead[0])))

    def body(bufs, old, new):
        if ahead is not None:
            _pass_on(bufs[nw:], old[n2:], new)
        _wait_rest(bufs[:nw], old[:n1], old[n1:n2])

    new, bufs, _ = _split_call("gather_wait_l%s" % li, list(lands) + more, first + list(fwd_sems) + arrived,
                               6 * len(more), after, body, False)
    return bufs[:nw], (None if ahead is None else (new, bufs[nw:]))


def _siblings_start(li, grads, after):
    nw = len(grads)
    lands = [lax.empty((4,) + g.shape[1:], g.dtype) for g in grads]

    def body(bufs, _, new):
        x, y, c = _place()
        for w in range(nw):
            for q in range(4):
                _remote(bufs[w].at[2 * q + (1 - c)], bufs[nw + w].at[q], new[8 * w + q], new[8 * w + 4 + q],
                        (x, y, 1 - c)).start()

    return _split_call("rs_siblings_start_l%s" % li, list(grads) + lands, [], 8 * nw, after, body, True)


def _siblings_wait(li, bufs, sems, after):
    nw = len(bufs) // 2

    def body(refs, old, _):
        x, y, c = _place()
        for w in range(nw):
            for q in range(4):
                _remote(refs[w].at[2 * q + (1 - c)], refs[nw + w].at[q], old[8 * w + q], old[8 * w + 4 + q],
                        (x, y, 1 - c)).wait()

    _, bufs, _ = _split_call("rs_siblings_wait_l%s" % li, bufs, sems, 0, after, body, False)
    return bufs[:nw], bufs[nw:]


_FLIPS = ((1, 0), (0, 1), (1, 1))


def _chips_copies(refs, nw, sems):
    x, y, c = _place()
    for w in range(nw):
        for r, (fx, fy) in enumerate(_FLIPS):
            px = 1 - x if fx else x
            py = 1 - y if fy else y
            yield _remote(refs[w].at[2 * px + py], refs[nw + w].at[r], sems[6 * w + r], sems[6 * w + 3 + r], (px, py, c))


def _chips_start(li, partials, after):
    nw = len(partials)
    lands = [lax.empty((3,) + a.shape[1:], a.dtype) for a in partials]

    def body(bufs, _, new):
        for cp in _chips_copies(bufs, nw, new):
            cp.start()

    return _split_call("rs_chips_start_l%s" % li, list(partials) + lands, [], 6 * nw, after, body, True)


def _chips_wait(li, bufs, sems, after):
    nw = len(bufs) // 2

    def body(refs, old, _):
        for cp in _chips_copies(refs, nw, old):
            cp.wait()

    _, bufs, _ = _split_call("rs_chips_wait_l%s" % li, bufs, sems, 0, after, body, False)
    return bufs[nw:]


def _row_tile(n, want):
    best = None
    for t in range(16, min(n, want) + 1, 16):
        if n % t == 0:
            best = t
    assert best is not None, n
    return best


def _chip_partials(name, grad, from_sibling):
    _, r, cdim = grad.shape
    tr = _row_tile(r, 256)
    nt = r // tr
    steps = 4 * nt

    def body(g_hbm, s_ref, pb_ref, own_ref, buf, sem):
        i, q = pl.program_id(0), pl.program_id(1)
        x, y, c = _place()
        n = 4 * i + q
        slot = n % 2

        def fetch(step, into):
            rows = pl.ds(pl.multiple_of((step // 4) * tr, 16), tr)
            return pltpu.make_async_copy(g_hbm.at[2 * (step % 4) + c, rows, :], buf.at[into], sem.at[into])

        @pl.when(n == 0)
        def _():
            fetch(0, 0).start()

        @pl.when(n + 1 < steps)
        def _():
            fetch(n + 1, 1 - slot).start()

        fetch(n, slot).wait()
        tot = buf[slot] + s_ref[...]
        pb_ref[...] = tot.astype(BF16)

        @pl.when(q == 2 * x + y)
        def _():
            own_ref[...] = tot

    return pl.pallas_call(
        body, name=name, grid=(nt, 4),
        in_specs=[ANY, pl.BlockSpec((None, tr, cdim), lambda i, q: (q, i, 0))],
        out_specs=[pl.BlockSpec((None, tr, cdim), lambda i, q: (q, i, 0)), pl.BlockSpec((tr, cdim), lambda i, q: (i, 0))],
        out_shape=[_sds((4, r, cdim), BF16), _sds((r, cdim), F32)],
        scratch_shapes=[pltpu.VMEM((2, tr, cdim), F32), pltpu.SemaphoreType.DMA((2,))],
        compiler_params=_params("arbitrary", "arbitrary"),
    )(grad, from_sibling)


def _adamw(w, g, m, v):
    m = ADAM_B1 * m + (1.0 - ADAM_B1) * g
    v = ADAM_B2 * v + (1.0 - ADAM_B2) * (g * g)
    m_hat = m / (1.0 - ADAM_B1 ** ADAM_STEP)
    v_hat = v / (1.0 - ADAM_B2 ** ADAM_STEP)
    delta = -ADAM_LR * (m_hat / (jnp.sqrt(v_hat) + ADAM_EPS) + ADAM_WD * w)
    return delta, m, v


def _finish_weight(name, li, own, from_chips, w, m, v, stacked):
    r, cdim = own.shape
    tr = _row_tile(r, 256)
    if stacked is None:
        stacked = [lax.empty(w.shape, F32) for _ in range(4)]

    def body(own_ref, fc_ref, w_ref, m_ref, v_ref, *rest):
        g_out, d_out, m_out, v_out = rest[4:]
        g = own_ref[...] + fc_ref[0].astype(F32) + fc_ref[1].astype(F32) + fc_ref[2].astype(F32)
        delta, mn, vn = _adamw(w_ref[...], g, m_ref[...], v_ref[...])
        g_out[...] = g
        d_out[...] = delta
        m_out[...] = mn
        v_out[...] = vn

    tile = pl.BlockSpec((tr, cdim), lambda i: (i, 0))
    lay = pl.BlockSpec((None, tr, cdim), lambda i: (li, i, 0))
    return pl.pallas_call(
        body, name=name, grid=(r // tr,),
        in_specs=[tile, pl.BlockSpec((3, tr, cdim), lambda i: (0, i, 0)), lay, lay, lay] + [ANY] * 4,
        out_specs=[lay] * 4, out_shape=[_sds(w.shape, F32)] * 4,
        input_output_aliases={5: 0, 6: 1, 7: 2, 8: 3},
        compiler_params=_params("parallel"),
    )(own, from_chips, w, m, v, *stacked)


def _allgather_small(name, v, reduce, after=None):
    r = v.shape[0]
    deps = [] if after is None else [after]

    def body(x_ref, *rest):
        out_ref, rest = rest[len(deps)], rest[len(deps) + 1:]
        if reduce:
            sum_ref, send_sems, recv_sems, local_sem = rest
        else:
            send_sems, recv_sems, local_sem = rest
        x, y, c = _place()
        me, sib = (x, y, c), (x, y, 1 - c)
        chips = [(1 - x, y), (x, 1 - y), (1 - x, 1 - y)]

        def rows(px, py, pc):
            return out_ref.at[pl.ds(pl.multiple_of((4 * px + 2 * py + pc) * r, 8), r), :]

        def copy(k, block, to, src=None):
            return pltpu.make_async_remote_copy(
                src_ref=rows(*block) if src is None else src, dst_ref=rows(*block),
                send_sem=send_sems.at[k], recv_sem=recv_sems.at[k], device_id=to, device_id_type=MESH)

        mine = pltpu.make_async_copy(x_ref, rows(*me), local_sem)
        mine.start()
        first = [copy(0, me, sib, src=x_ref)]
        first += [copy(1 + j, me, (*chip, c), src=x_ref) for j, chip in enumerate(chips)]
        for cp in first:
            cp.start()
        passed = [copy(4 + j, (*chip, c), sib) for j, chip in enumerate(chips)]
        for j, chip in enumerate(chips):
            copy(1 + j, (*chip, c), me).wait_recv()
            passed[j].start()
        copy(0, sib, me).wait_recv()
        for j, chip in enumerate(chips):
            copy(4 + j, (*chip, 1 - c), me).wait_recv()
        for cp in first + passed:
            cp.wait_send()
        mine.wait()
        if reduce:
            tot = out_ref[0:r, :]
            for d in range(1, N_DEV):
                tot = tot + out_ref[d * r:(d + 1) * r, :]
            sum_ref[...] = tot

    vm = pl.BlockSpec(memory_space=pltpu.VMEM)
    outs = [_sds((N_DEV * r, 128), F32)] + ([_sds((r, 128), F32)] if reduce else [])
    res = pl.pallas_call(
        body, name=name,
        in_specs=[vm] + [ANY] * len(deps), out_specs=[vm] * len(outs), out_shape=outs,
        scratch_shapes=[pltpu.SemaphoreType.DMA((7,)), pltpu.SemaphoreType.DMA((7,)), pltpu.SemaphoreType.DMA],
        compiler_params=pltpu.CompilerParams(vmem_limit_bytes=VMEM_LIMIT_BYTES),
    )(v, *deps)
    return res


def _adamw_small(name, w, g, m, v):
    def body(w_ref, g_ref, m_ref, v_ref, d_out, m_out, v_out):
        delta, mn, vn = _adamw(w_ref[...], g_ref[...], m_ref[...], v_ref[...])
        d_out[...] = delta
        m_out[...] = mn
        v_out[...] = vn

    vm = pl.BlockSpec(memory_space=pltpu.VMEM)
    return pl.pallas_call(
        body, name=name, in_specs=[vm] * 4, out_specs=[vm] * 3, out_shape=[_sds(w.shape, F32)] * 3,
        compiler_params=pltpu.CompilerParams(vmem_limit_bytes=VMEM_LIMIT_BYTES),
    )(w, g, m, v)


def _pack(arrays):
    flat, layout, off = [], [], 0
    for a in arrays:
        flat.append(a.reshape(-1).astype(F32))
        layout.append((off, a.shape))
        off += a.size
    total = -(-off // 1024) * 1024
    if total > off:
        flat.append(jnp.zeros((total - off,), F32))
    return jnp.concatenate(flat).reshape(total // 128, 128), layout


def _unpack(packed, layout):
    flat = packed.reshape(-1)
    return [flat[off:off + math.prod(shape)].reshape(shape) for off, shape in layout]


_BIG = ("win", "wout", "wg", "wu", "wd", "wpg", "wpp")
_BIG_FULL = {"win": "w_in", "wout": "w_out", "wg": "w_gate", "wu": "w_up", "wd": "w_down",
             "wpg": "w_ple_gate", "wpp": "w_ple_proj"}
_SMALL_REPLICATED = ("norm_mix_g", "sgu_ln_g", "sgu_ln_b", "sgu_w", "sgu_b", "cf_conv_b", "cf_ln_g", "cf_ln_b",
                     "pool_w", "pool_scale", "norm_ffn_g", "norm_ple_g", "final_norm_g")
_SMALL_SHARDED = ("sc_conv_w", "cf_conv_w")
_WEIGHTS = ("norm_mix_g", "w_in", "sgu_ln_g", "sgu_ln_b", "sgu_w", "sgu_b", "sc_conv_w", "cf_conv_w", "cf_conv_b",
            "cf_ln_g", "cf_ln_b", "pool_w", "pool_scale", "w_out", "norm_ffn_g", "w_gate", "w_up", "w_down",
            "norm_ple_g", "w_ple_gate", "w_ple_proj", "final_norm_g")


def _pad_rows(a, rows):
    return jnp.concatenate([a, jnp.zeros((rows - a.shape[0],) + a.shape[1:], a.dtype)], axis=0)


def _mixer_params(li, W, sc_full, cf_full):
    return [W["sgu_ln_g"][li][:, None, :], W["sgu_ln_b"][li][:, None, :], W["sgu_w"][li], W["sgu_b"][li][:, :, None],
            _pad_rows(sc_full[li], 8), _pad_rows(cf_full[li], 32),
            W["cf_conv_b"][li][None, :], W["cf_ln_g"][li][None, :], W["cf_ln_b"][li][None, :],
            W["pool_w"][li], W["pool_scale"][li][None, :]]


def _step(W, M, V, x, p, loss_target):
    n_layers = W["w_in"].shape[0]
    h = x[0]
    target = loss_target[0]
    p_all = p[:, 0]
    xi, yi, ci = _place()
    blk = 4 * xi + 2 * yi + ci
    csh = W["sc_conv_w"].shape[2]
    turned = ("w_gate", "w_up")
    W, M, V = ({n: (jnp.swapaxes(a, 1, 2) if n in turned else a) for n, a in t.items()} for t in (W, M, V))

    packed, lay = _pack([W[n] for n in _SMALL_SHARDED])
    (taps,) = _allgather_small("gather_conv_taps", packed, reduce=False)
    per_dev = [_unpack(taps[d * packed.shape[0]:(d + 1) * packed.shape[0]], lay) for d in range(N_DEV)]
    sc_full = jnp.concatenate([pd[0] for pd in per_dev], axis=-1)
    cf_full = jnp.concatenate([pd[1] for pd in per_dev], axis=-1)

    d = h.shape[1]
    placed = [_place_shard("place_" + n, W[_BIG_FULL[n]]) for n in _BIG]
    lands = [[placed[w][li] for w in range(len(_BIG))] for li in range(n_layers)]

    groups = (("win",), ("wout", "wg", "wu"), ("wd", "wpg", "wpp"))
    order = [(li, gi) for li in range(n_layers) for gi in range(len(groups))]
    moving, tok = [], taps
    for li in range(n_layers):
        sems, bufs, tok = _gather_start(li, lands[li], tok)
        moving.append({"sems": sems, "bufs": bufs, "got": {}})
    passed = {}

    def group_of(item):
        li, gi = item
        idx = [_BIG.index(n) for n in groups[gi]]
        sems = [s_ for i in idx for s_ in moving[li]["sems"][8 * i:8 * i + 8]]
        return "%d_%s" % (li, groups[gi][0]), [moving[li]["bufs"][i] for i in idx], sems

    def fetcher(li):
        def fetch(name, after):
            got = moving[li]["got"]
            if name not in got:
                item = (li, next(gi for gi, g_ in enumerate(groups) if name in g_))
                tag, bufs, sems = group_of(item)
                if item not in passed:
                    fwd_sems, bufs, _ = _gather_forward(tag, bufs, sems, after)
                    passed[item] = (fwd_sems, bufs)
                fwd_sems, bufs = passed[item]
                nxt = order.index(item) + 1
                ahead = None
                if nxt < len(order) and order[nxt][0] >= 1:
                    ahead = group_of(order[nxt])[1:]
                arrived, sent_on = _gather_wait(tag, bufs, sems, fwd_sems, after, ahead)
                if ahead is not None:
                    passed[order[nxt]] = sent_on
                for n, a in zip(groups[item[1]], arrived):
                    got[n] = a.reshape(d, d) if n in ("wout", "wpg") else a
            return got[name]
        return fetch

    saved, gathered, smalls = [], [], []
    y1 = None
    for li in range(n_layers):
        sm = {"norm_mix_g": _after(W["norm_mix_g"][li][None, :], tok if li == 0 else None),
              "norm_ffn_g": W["norm_ffn_g"][li][None, :],
              "norm_ple_g": W["norm_ple_g"][li][None, :], "mixer": _mixer_params(li, W, sc_full, cf_full)}
        next_mix_g = W["norm_mix_g"][li + 1][None, :] if li + 1 < n_layers else None
        h, sv, y1 = _layer_fwd(li, h, p_all, fetcher(li), sm, y1, next_mix_g)
        saved.append(sv)
        gathered.append(moving[li]["got"])
        smalls.append(sm)
    loss, dh, d_final_g = _loss_head(h, W["final_norm_g"][None, :], target)

    big_out = {n: None for n in _BIG}
    small_grads = []

    def scatter_begin(li, tag, names, gr, after):
        sems, bufs, tok = _siblings_start(tag, [gr[n] for n in names], after)
        return {"li": li, "tag": tag, "names": names, "sems": sems, "bufs": bufs, "tok": tok}

    def scatter_middle(st, after):
        grads, from_sib = _siblings_wait(st["tag"], st["bufs"], st["sems"], after)
        parts = [_chip_partials("rs_sum_" + n, g_, s_) for n, g_, s_ in zip(st["names"], grads, from_sib)]
        st["own"] = [own for _, own in parts]
        st["sems"], st["bufs"], tok = _chips_start(st["tag"], [pb for pb, _ in parts], None)
        return tok

    def scatter_end(st, after):
        from_chips = _chips_wait(st["tag"], st["bufs"], st["sems"], after)
        for n, own, fc in zip(st["names"], st["own"], from_chips):
            full = _BIG_FULL[n]
            big_out[n] = _finish_weight("adamw_" + n, st["li"], own, fc, W[full], M[full], V[full], big_out[n])
        return big_out[st["names"][0]][0]

    order = _SMALL_REPLICATED + _SMALL_SHARDED

    def reduce_small(after):
        by_layer = small_grads[::-1]
        stacked = lambda fn: jnp.stack([fn(sg) for sg in by_layer])
        mix = lambda i: (lambda sg: sg["mixer"][i])
        grads_small = {
            "norm_mix_g": stacked(lambda sg: sg["norm_mix_g"][0]),
            "sgu_ln_g": stacked(mix(0))[:, :, 0, :], "sgu_ln_b": stacked(mix(1))[:, :, 0, :],
            "sgu_w": stacked(mix(2)), "sgu_b": stacked(mix(3))[:, :, :, 0],
            "sc_conv_w": stacked(mix(4))[:, :SHORT_K], "cf_conv_w": stacked(mix(5))[:, :CONF_K],
            "cf_conv_b": stacked(mix(6))[:, 0], "cf_ln_g": stacked(mix(7))[:, 0], "cf_ln_b": stacked(mix(8))[:, 0],
            "pool_w": stacked(mix(9)), "pool_scale": stacked(mix(10))[:, 0],
            "norm_ffn_g": stacked(lambda sg: sg["norm_ffn_g"][0]),
            "norm_ple_g": stacked(lambda sg: sg["norm_ple_g"][0]),
            "final_norm_g": d_final_g[0],
        }
        packed, lay = _pack([grads_small[n] for n in order] + [loss])
        _, summed = _allgather_small("allreduce_small", packed, reduce=True, after=after)
        return dict(zip(order + ("loss",), _unpack(summed, lay)))

    early, late = ("wpp", "wpg", "wd", "wg", "wu"), ("wout", "win")
    pending, tok = None, None
    for li in reversed(range(n_layers)):
        hooks, first, last = {}, {}, {}
        if pending is not None:
            hooks["after_down_dx"] = functools.partial(scatter_middle, pending)
        if li == 0:
            def begin_early(gr, after, first=first):
                first.update(scatter_begin(0, "0_early", early, gr, after))
                return first["tok"]

            def begin_late(gr, last=last):
                last.update(scatter_begin(0, "0_late", late, gr, None))
                return last["tok"]

            hooks["after_ffn_grads"] = begin_early
            hooks["before_ffn_norm"] = lambda after, first=first: scatter_middle(first, after)
            hooks["after_in_dw"] = begin_late
        dh, gr, sg = _layer_bwd(li, dh, p_all, gathered[li], smalls[li], saved[li], tok, hooks)
        small_grads.append(sg)
        if li > 0:
            done = None if pending is None else scatter_end(pending, dh)
            pending = scatter_begin(li, str(li), _BIG, gr, done)
            tok = pending["tok"]
        else:
            scatter_middle(last, dh)
            done = last["own"][0] if pending is None else scatter_end(pending, last["own"][0])
            done = scatter_end(first, done)
            total = reduce_small(scatter_end(last, done))
    loss_all = total["loss"][0, 0]

    out_g, out_d, out_m, out_v = {}, {}, {}, {}
    pw, lay_r = _pack([W[n] for n in _SMALL_REPLICATED])
    pg, _ = _pack([total[n] for n in _SMALL_REPLICATED])
    pm, _ = _pack([M[n] for n in _SMALL_REPLICATED])
    pv, _ = _pack([V[n] for n in _SMALL_REPLICATED])
    dd, mm, vv = _adamw_small("adamw_small", pw, pg, pm, pv)
    for n, a, b, c_ in zip(_SMALL_REPLICATED, _unpack(dd, lay_r), _unpack(mm, lay_r), _unpack(vv, lay_r)):
        out_g[n], out_d[n], out_m[n], out_v[n] = total[n], a, b, c_
    pick = (jnp.arange(N_DEV) == blk).astype(F32)[None, None, :, None]
    mine = {n: jnp.sum(total[n].reshape(total[n].shape[:2] + (N_DEV, csh)) * pick, axis=2) for n in _SMALL_SHARDED}
    pw, lay_s = _pack([W[n] for n in _SMALL_SHARDED])
    pg, _ = _pack([mine[n] for n in _SMALL_SHARDED])
    pm, _ = _pack([M[n] for n in _SMALL_SHARDED])
    pv, _ = _pack([V[n] for n in _SMALL_SHARDED])
    dd, mm, vv = _adamw_small("adamw_conv_taps", pw, pg, pm, pv)
    for n, a, b, c_ in zip(_SMALL_SHARDED, _unpack(dd, lay_s), _unpack(mm, lay_s), _unpack(vv, lay_s)):
        out_g[n], out_d[n], out_m[n], out_v[n] = mine[n], a, b, c_
    for n in _BIG:
        for k, dst in enumerate((out_g, out_d, out_m, out_v)):
            full = _BIG_FULL[n]
            dst[full] = jnp.swapaxes(big_out[n][k], 1, 2) if full in turned else big_out[n][k]

    return (loss_all, dh[None], *[out_g[n] for n in _WEIGHTS], *[out_d[n] for n in _WEIGHTS],
            *[out_m[n] for n in _WEIGHTS], *[out_v[n] for n in _WEIGHTS])


def kernel(x, p, norm_mix_g, w_in, sgu_ln_g, sgu_ln_b, sgu_w, sgu_b, sc_conv_w, cf_conv_w, cf_conv_b, cf_ln_g, cf_ln_b, pool_w, pool_scale, w_out, norm_ffn_g, w_gate, w_up, w_down, norm_ple_g, w_ple_gate, w_ple_proj, final_norm_g, loss_target, m_norm_mix_g, m_w_in, m_sgu_ln_g, m_sgu_ln_b, m_sgu_w, m_sgu_b, m_sc_conv_w, m_cf_conv_w, m_cf_conv_b, m_cf_ln_g, m_cf_ln_b, m_pool_w, m_pool_scale, m_w_out, m_norm_ffn_g, m_w_gate, m_w_up, m_w_down, m_norm_ple_g, m_w_ple_gate, m_w_ple_proj, m_final_norm_g, v_norm_mix_g, v_w_in, v_sgu_ln_g, v_sgu_ln_b, v_sgu_w, v_sgu_b, v_sc_conv_w, v_cf_conv_w, v_cf_conv_b, v_cf_ln_g, v_cf_ln_b, v_pool_w, v_pool_scale, v_w_out, v_norm_ffn_g, v_w_gate, v_w_up, v_w_down, v_norm_ple_g, v_w_ple_gate, v_w_ple_proj, v_final_norm_g):
    given = dict(locals())
    W = {n: given[n] for n in _WEIGHTS}
    M = {n: given["m_" + n] for n in _WEIGHTS}
    V = {n: given["v_" + n] for n in _WEIGHTS}
    return _step(W, M, V, x, p, loss_target)
```

```python
import functools
import math

import jax
import jax.numpy as jnp
from jax import lax
from jax.experimental import pallas as pl
from jax.experimental.pallas import tpu as pltpu

F32 = jnp.float32
BF16 = jnp.bfloat16
EPS = 1e-6
HEAD = 128
GROUP = 4 * HEAD
HALO = 32
SHORT_K = 3
CONF_K = 31
POOL_WINDOWS = (2, 4, 8, 16)
N_DEV = 8
MESH = pl.DeviceIdType.MESH
VMEM_LIMIT_BYTES = 56 * 1024 * 1024

ADAM_LR = 0.001
ADAM_B1 = 0.9
ADAM_B2 = 0.999
ADAM_EPS = 1e-08
ADAM_WD = 0.01
ADAM_STEP = 10

ANY = pl.BlockSpec(memory_space=pl.ANY)


def _params(*sem):
    return pltpu.CompilerParams(dimension_semantics=sem, vmem_limit_bytes=VMEM_LIMIT_BYTES)


def _rms(x, g):
    return x * lax.rsqrt(jnp.mean(x * x, axis=-1, keepdims=True) + EPS) * g


def _ln(x, g, b):
    mu = jnp.mean(x, axis=-1, keepdims=True)
    xc = x - mu
    var = jnp.mean(xc * xc, axis=-1, keepdims=True)
    return xc * lax.rsqrt(var + EPS) * g + b


def _bdot(a, b, dims=(((1,), (0,)), ((), ()))):
    return lax.dot_general(a.astype(BF16), b.astype(BF16), dims, preferred_element_type=F32)


def _sgu_piece(zu, zv, lg, lb, w, b):
    u = jax.nn.gelu(zu)
    v = _ln(jax.nn.gelu(zv), lg, lb)
    row = lax.broadcasted_iota(jnp.int32, w.shape, 0)
    col = lax.broadcasted_iota(jnp.int32, w.shape, 1)
    wm = jnp.where(row >= col, w, 0.0)
    return u * (_bdot(wm, v) + b)


def _conf_post(c, g, b):
    return jax.nn.silu(_ln(c, g, b))


def _pool_count(first_pos, rows, w):
    pos = first_pos + lax.broadcasted_iota(jnp.int32, (rows, 1), 0) + 1
    return jnp.minimum(pos, w).astype(F32)


_DIMS = {
    "nn": (((1,), (0,)), ((), ())),
    "nt": (((1,), (1,)), ((), ())),
    "tn": (((0,), (0,)), ((), ())),
}


def _tiles(name, grid, ins, in_specs, outs, out_specs, compute, summed=(), deps=()):
    ni = len(ins)
    nd = len(deps)

    def body(*refs):
        vals = compute(*refs[:ni])
        first = functools.reduce(jnp.logical_and, [pl.program_id(a) == 0 for a in range(len(grid))])
        for idx, (r, v) in enumerate(zip(refs[ni + nd:], vals)):
            if idx in summed:
                @pl.when(first)
                def _(r=r):
                    r[...] = jnp.zeros_like(r)

                r[...] += v
            else:
                r[...] = v.astype(r.dtype)

    sem = ("arbitrary" if summed else "parallel",) * len(grid)
    return pl.pallas_call(
        body, name=name, grid=grid, in_specs=list(in_specs) + [ANY] * nd, out_specs=list(out_specs),
        out_shape=list(outs), compiler_params=_params(*sem),
    )(*ins, *deps)


def _resident(shape):
    return pl.BlockSpec(shape, lambda *_: (0,) * len(shape), pipeline_mode=pl.Buffered(1))


def _sds(shape, dtype):
    return jax.ShapeDtypeStruct(tuple(shape), dtype)


def _tile(n, want):
    t = min(n, want)
    assert n % t == 0, (n, want)
    return t


def _rms_fwd(name, h, g):
    s, d = h.shape
    tm = _tile(s, 512)

    def body(h_ref, g_ref, y_ref):
        y_ref[...] = _rms(h_ref[...], g_ref[...]).astype(BF16)

    return pl.pallas_call(
        body, name=name, grid=(s // tm,),
        in_specs=[pl.BlockSpec((tm, d), lambda i: (i, 0)), pl.BlockSpec((1, d), lambda i: (0, 0))],
        out_specs=pl.BlockSpec((tm, d), lambda i: (i, 0)),
        out_shape=_sds((s, d), BF16),
        compiler_params=_params("parallel"),
    )(h, g)


def _rms_back(h, g, dy, dh_in):
    inv = lax.rsqrt(jnp.mean(h * h, axis=-1, keepdims=True) + EPS)
    xhat = h * inv
    dxhat = dy * g
    dh = inv * (dxhat - xhat * jnp.mean(dxhat * xhat, axis=-1, keepdims=True))
    return dh_in + dh, jnp.sum(dy * xhat, axis=0, keepdims=True)


def _loss_head(h, g, target):
    s, d = h.shape
    tm = _tile(s, 256)

    def body(h_ref, g_ref, t_ref, loss_ref, dh_ref, dg_ref):
        y, vjp = jax.vjp(_rms, h_ref[...], g_ref[...])
        err = y - t_ref[...]
        dh, dg = vjp(err * (1.0 / d))
        dh_ref[...] = dh
        per_token = jnp.mean(err * err, axis=-1, keepdims=True)
        part = 0.5 * jnp.sum(per_token, axis=0, keepdims=True)

        @pl.when(pl.program_id(0) == 0)
        def _():
            dg_ref[...] = jnp.zeros_like(dg_ref)
            loss_ref[...] = jnp.zeros_like(loss_ref)

        dg_ref[...] += dg
        loss_ref[...] += part

    tok = pl.BlockSpec((tm, d), lambda i: (i, 0))
    vec = pl.BlockSpec((1, d), lambda i: (0, 0))
    return pl.pallas_call(
        body, name="loss_head", grid=(s // tm,),
        in_specs=[tok, vec, tok],
        out_specs=[pl.BlockSpec((1, 1), lambda i: (0, 0)), tok, vec],
        out_shape=[_sds((1, 1), F32), _sds((s, d), F32), _sds((1, d), F32)],
        compiler_params=_params("arbitrary"),
    )(h, g, target)


_U, _V, _HB, _BG, _CG, _CA, _CGT, _PD = (GROUP * i for i in range(8))


def _cols(c0, w=GROUP):
    return slice(c0, c0 + w)


_LANE_CHUNKS = tuple(_cols(i * HEAD, HEAD) for i in range(GROUP // HEAD))


def _shifted_copies(sh_ref, ext_ref, length):
    for r in range(1, 8):
        sh_ref[r - 1, 0:length, :] = ext_ref[pl.ds(r, length), :]


def _window(ext_ref, sh_ref, off, rows, cc):
    q, r = divmod(off, 8)
    if sh_ref is None or r == 0:
        return ext_ref[pl.ds(off, rows), cc]
    return sh_ref[r - 1, pl.ds(8 * q, rows), cc]


def _taps(ext_ref, w_ref, nk, start, rows, cc, flip=False, sh_ref=None):
    acc = None
    for k in range(nk):
        kw = nk - 1 - k if flip else k
        term = w_ref[kw:kw + 1, cc] * _window(ext_ref, sh_ref, start + k, rows, cc)
        acc = term if acc is None else acc + term
    return acc


def _mixer_param_specs():
    full = lambda *shape: pl.BlockSpec(shape, lambda i: (0,) * len(shape))
    return [
        full(4, 1, HEAD), full(4, 1, HEAD), full(4, HEAD, HEAD), full(4, HEAD, 1),
        full(8, GROUP), full(32, GROUP), full(1, GROUP), full(1, GROUP), full(1, GROUP),
        full(4, HEAD, HEAD), full(1, GROUP),
    ]


def _mixer_fwd(name, z, prm):
    s = z.shape[0]
    t = _tile(s, 256)
    hb = t // HALO

    def body(zp_ref, zm_ref, lg_ref, lb_ref, sw_ref, sb_ref, scw_ref, cfw_ref, cfb_ref, cg_ref, cb_ref,
             pw_ref, ps_ref, o_ref, ext_ref, sh_ref):
        i = pl.program_id(0)
        keep = (i > 0).astype(F32)
        main = pl.ds(HALO, t)

        for n in range(t // HEAD):
            rows = slice(n * HEAD, (n + 1) * HEAD)
            for hh in range(4):
                cu = _cols(_U + hh * HEAD, HEAD)
                cv = _cols(_V + hh * HEAD, HEAD)
                o_ref[rows, _cols(hh * HEAD, HEAD)] = _sgu_piece(
                    zm_ref[rows, cu], zm_ref[rows, cv], lg_ref[hh], lb_ref[hh], sw_ref[hh], sb_ref[hh]
                ).astype(BF16)

        ext_ref[0:HALO, :] = zp_ref[:, _cols(_CG)] * zp_ref[:, _cols(_HB)] * keep
        ext_ref[main, :] = zm_ref[:, _cols(_CG)] * zm_ref[:, _cols(_HB)]
        y = jnp.zeros((t, GROUP), F32)
        for k in range(SHORT_K):
            y = y + scw_ref[k:k + 1, :] * ext_ref[pl.ds(HALO - (SHORT_K - 1) + k, t), :]
        o_ref[:, _cols(GROUP)] = (zm_ref[:, _cols(_BG)] * y).astype(BF16)

        ext_ref[0:HALO, :] = zp_ref[:, _cols(_CA)] * jax.nn.sigmoid(zp_ref[:, _cols(_CGT)]) * keep
        ext_ref[main, :] = zm_ref[:, _cols(_CA)] * jax.nn.sigmoid(zm_ref[:, _cols(_CGT)])
        _shifted_copies(sh_ref, ext_ref, HALO + t - 8)
        c = jnp.concatenate(
            [cfb_ref[:, cc] + _taps(ext_ref, cfw_ref, CONF_K, HALO - (CONF_K - 1), t, cc, sh_ref=sh_ref)
             for cc in _LANE_CHUNKS], axis=1)
        o_ref[:, _cols(2 * GROUP)] = _conf_post(c, cg_ref[...], cb_ref[...]).astype(BF16)

        ext_ref[0:HALO, :] = zp_ref[:, _cols(_PD)] * keep
        ext_ref[main, :] = zm_ref[:, _cols(_PD)]
        for gi, w in enumerate(POOL_WINDOWS):
            cc = _cols(gi * HEAD, HEAD)
            acc = ext_ref[main, cc]
            for j in range(1, w):
                acc = acc + ext_ref[pl.ds(HALO - j, t), cc]
            q = acc / _pool_count(i * t, t, w) - ext_ref[main, cc]
            o_ref[:, _cols(3 * GROUP + gi * HEAD, HEAD)] = (_bdot(q, pw_ref[gi]) * ps_ref[:, cc]).astype(BF16)

    return pl.pallas_call(
        body, name=name, grid=(s // t,),
        in_specs=[pl.BlockSpec((HALO, 8 * GROUP), lambda i: (jnp.maximum(i * hb - 1, 0), 0)),
                  pl.BlockSpec((t, 8 * GROUP), lambda i: (i, 0)),
                  *_mixer_param_specs()],
        out_specs=pl.BlockSpec((t, 4 * GROUP), lambda i: (i, 0)),
        out_shape=_sds((s, 4 * GROUP), BF16),
        scratch_shapes=[pltpu.VMEM((HALO + t, GROUP), F32), pltpu.VMEM((7, HALO + t - 8, GROUP), F32)],
        compiler_params=_params("parallel"),
    )(z, z, *prm)


def _mixer_bwd(name, z, dmo, prm):
    s = z.shape[0]
    t = _tile(s, 256)
    hb = t // HALO
    nt = s // t
    last_halo = s // HALO - 1

    def body(zp_ref, zm_ref, zn_ref, dm_ref, dn_ref,
             lg_ref, lb_ref, sw_ref, sb_ref, scw_ref, cfw_ref, cfb_ref, cg_ref, cb_ref, pw_ref, ps_ref,
             dz_ref, dlg_ref, dlb_ref, dsw_ref, dsb_ref, dscw_ref, dcfw_ref, dcfb_ref, dcg_ref, dcb_ref,
             dpw_ref, dps_ref, extp_ref, extn_ref, extc_ref, sh_ref):
        i = pl.program_id(0)
        keep_prev = (i > 0).astype(F32)
        keep_next = (i < nt - 1).astype(F32)
        main = pl.ds(HALO, t)

        @pl.when(i == 0)
        def _():
            for r in (dlg_ref, dlb_ref, dsw_ref, dsb_ref, dscw_ref, dcfw_ref, dcfb_ref, dcg_ref, dcb_ref,
                      dpw_ref, dps_ref):
                r[...] = jnp.zeros_like(r)

        def rowsum(v):
            return jnp.sum(v, axis=0, keepdims=True)

        for n in range(t // HEAD):
            rows = slice(n * HEAD, (n + 1) * HEAD)
            for hh in range(4):
                cu = _cols(_U + hh * HEAD, HEAD)
                cv = _cols(_V + hh * HEAD, HEAD)
                _, vjp = jax.vjp(_sgu_piece, zm_ref[rows, cu], zm_ref[rows, cv],
                                 lg_ref[hh], lb_ref[hh], sw_ref[hh], sb_ref[hh])
                dzu, dzv, dlg, dlb, dsw, dsb = vjp(dm_ref[rows, _cols(hh * HEAD, HEAD)].astype(F32))
                dz_ref[rows, cu] = dzu.astype(BF16)
                dz_ref[rows, cv] = dzv.astype(BF16)
                dlg_ref[hh] += dlg
                dlb_ref[hh] += dlb
                dsw_ref[hh] += dsw
                dsb_ref[hh] += dsb

        extp_ref[0:HALO, :] = zp_ref[:, _cols(_CG)] * zp_ref[:, _cols(_HB)] * keep_prev
        extp_ref[main, :] = zm_ref[:, _cols(_CG)] * zm_ref[:, _cols(_HB)]
        dob = dm_ref[:, _cols(GROUP)].astype(F32)
        dy = dob * zm_ref[:, _cols(_BG)]
        extn_ref[0:t, :] = dy
        extn_ref[t:t + HALO, :] = dn_ref[:, _cols(GROUP)].astype(F32) * zn_ref[:, _cols(_BG)] * keep_next
        for cc in _LANE_CHUNKS:
            at = lambda c0: _cols(c0 + cc.start, HEAD)
            y = _taps(extp_ref, scw_ref, SHORT_K, HALO - (SHORT_K - 1), t, cc)
            dx = _taps(extn_ref, scw_ref, SHORT_K, 0, t, cc, flip=True)
            dy_c = extn_ref[0:t, cc]
            for k in range(SHORT_K):
                dscw_ref[k:k + 1, cc] += rowsum(dy_c * extp_ref[pl.ds(HALO - (SHORT_K - 1) + k, t), cc])
            dz_ref[:, at(_BG)] = (dm_ref[:, at(GROUP)].astype(F32) * y).astype(BF16)
            dz_ref[:, at(_CG)] = (dx * zm_ref[:, at(_HB)]).astype(BF16)
            dz_ref[:, at(_HB)] = (dx * zm_ref[:, at(_CG)]).astype(BF16)

        extc_ref[0:HALO, :] = zp_ref[:, _cols(_CA)] * jax.nn.sigmoid(zp_ref[:, _cols(_CGT)]) * keep_prev
        extc_ref[main, :] = zm_ref[:, _cols(_CA)] * jax.nn.sigmoid(zm_ref[:, _cols(_CGT)])
        extc_ref[HALO + t:HALO + t + HALO, :] = zn_ref[:, _cols(_CA)] * jax.nn.sigmoid(zn_ref[:, _cols(_CGT)])
        _shifted_copies(sh_ref, extc_ref, HALO + t + HALO - 8)
        conv = lambda start, rows: jnp.concatenate(
            [cfb_ref[:, cc] + _taps(extc_ref, cfw_ref, CONF_K, start, rows, cc, sh_ref=sh_ref) for cc in _LANE_CHUNKS],
            axis=1)
        c_main = conv(HALO - (CONF_K - 1), t)
        c_next = conv(HALO + t - (CONF_K - 1), HALO)
        _, vjp = jax.vjp(_conf_post, c_main, cg_ref[...], cb_ref[...])
        dc, dcg, dcb = vjp(dm_ref[:, _cols(2 * GROUP)].astype(F32))
        dcg_ref[...] += dcg
        dcb_ref[...] += dcb
        dcfb_ref[...] += rowsum(dc)
        _, vjp_next = jax.vjp(lambda cv: _conf_post(cv, cg_ref[...], cb_ref[...]), c_next)
        (dc_next,) = vjp_next(dn_ref[:, _cols(2 * GROUP)].astype(F32) * keep_next)
        extn_ref[0:t, :] = dc
        extn_ref[t:t + HALO, :] = dc_next
        for cc in _LANE_CHUNKS:
            dc_c = extn_ref[0:t, cc]
            for k in range(CONF_K):
                dcfw_ref[k:k + 1, cc] += rowsum(dc_c * _window(extc_ref, sh_ref, HALO - (CONF_K - 1) + k, t, cc))
        _shifted_copies(sh_ref, extn_ref, t + HALO - 8)
        for cc in _LANE_CHUNKS:
            at = lambda c0: _cols(c0 + cc.start, HEAD)
            dhc = _taps(extn_ref, cfw_ref, CONF_K, 0, t, cc, flip=True, sh_ref=sh_ref)
            sg = jax.nn.sigmoid(zm_ref[:, at(_CGT)])
            dz_ref[:, at(_CA)] = (dhc * sg).astype(BF16)
            dz_ref[:, at(_CGT)] = (dhc * zm_ref[:, at(_CA)] * sg * (1.0 - sg)).astype(BF16)

        extp_ref[0:HALO, :] = zp_ref[:, _cols(_PD)] * keep_prev
        extp_ref[main, :] = zm_ref[:, _cols(_PD)]
        for gi, w in enumerate(POOL_WINDOWS):
            cc = _cols(gi * HEAD, HEAD)
            oc = _cols(3 * GROUP + gi * HEAD, HEAD)
            acc = extp_ref[main, cc]
            for j in range(1, w):
                acc = acc + extp_ref[pl.ds(HALO - j, t), cc]
            count = _pool_count(i * t, t, w)
            q = acc / count - extp_ref[main, cc]
            dod = dm_ref[:, oc].astype(F32)
            dps_ref[:, cc] += rowsum(dod * _bdot(q, pw_ref[gi]))
            ds = dod * ps_ref[:, cc]
            dpw_ref[gi] += _bdot(q, ds, _DIMS["tn"])
            dq = _bdot(ds, pw_ref[gi], _DIMS["nt"])
            ds_next = dn_ref[:, oc].astype(F32) * ps_ref[:, cc] * keep_next
            dq_next = _bdot(ds_next, pw_ref[gi], _DIMS["nt"])
            extn_ref[0:t, cc] = dq / count
            extn_ref[t:t + HALO, cc] = dq_next * (1.0 / w)
            back = extn_ref[0:t, cc]
            for j in range(1, w):
                back = back + extn_ref[pl.ds(j, t), cc]
            dz_ref[:, _cols(_PD + gi * HEAD, HEAD)] = (back - dq).astype(BF16)

    full = lambda *shape: pl.BlockSpec(shape, lambda i: (0,) * len(shape))
    grad_specs = [full(4, 1, HEAD), full(4, 1, HEAD), full(4, HEAD, HEAD), full(4, HEAD, 1),
                  full(8, GROUP), full(32, GROUP), full(1, GROUP), full(1, GROUP), full(1, GROUP),
                  full(4, HEAD, HEAD), full(1, GROUP)]
    grad_shapes = [_sds(sp.block_shape, F32) for sp in grad_specs]
    nxt = lambda i: (jnp.minimum((i + 1) * hb, last_halo), 0)
    return pl.pallas_call(
        body, name=name, grid=(nt,),
        in_specs=[pl.BlockSpec((HALO, 8 * GROUP), lambda i: (jnp.maximum(i * hb - 1, 0), 0)),
                  pl.BlockSpec((t, 8 * GROUP), lambda i: (i, 0)),
                  pl.BlockSpec((HALO, 8 * GROUP), nxt),
                  pl.BlockSpec((t, 4 * GROUP), lambda i: (i, 0)),
                  pl.BlockSpec((HALO, 4 * GROUP), nxt),
                  *_mixer_param_specs()],
        out_specs=[pl.BlockSpec((t, 8 * GROUP), lambda i: (i, 0)), *grad_specs],
        out_shape=[_sds((s, 8 * GROUP), BF16), *grad_shapes],
        scratch_shapes=[pltpu.VMEM((HALO + t, GROUP), F32), pltpu.VMEM((t + HALO, GROUP), F32),
                        pltpu.VMEM((HALO + t + HALO, GROUP), F32), pltpu.VMEM((7, HALO + t + HALO - 8, GROUP), F32)],
        compiler_params=_params("arbitrary"),
    )(z, z, z, dmo, dmo, *prm)


def _dot(a, b, kind="nn"):
    return _bdot(a, b, _DIMS[kind])


def _chain(terms, kind):
    acc = None
    for a, b in terms:
        p = _dot(a, b, kind)
        acc = p if acc is None else acc + p
    return acc


def _after(g, token):
    return g if token is None else g + token[0:1, 0:1]


def _layer_fwd(li, h, p_all, fetch, sm, y1=None, next_mix_g=None):
    s, d = h.shape
    nb = N_DEV
    pd = p_all.shape[2]
    sv = {"h": h}
    rows = lambda t, w: pl.BlockSpec((t, w), lambda i: (i, 0))
    vec = pl.BlockSpec((1, d), lambda i: (0, 0))

    if y1 is None:
        y1 = _rms_fwd("rms_mix", h, sm["norm_mix_g"])
    tm = _tile(s, 1024)
    (z,) = _tiles(
        "w_in_fwd", (s // tm, nb),
        [y1, fetch("win", y1)],
        [pl.BlockSpec((tm, d), lambda i, j: (i, 0)), pl.BlockSpec((None, d, GROUP), lambda i, j: (j, 0, 0))],
        [_sds((s, nb * GROUP), F32)], [pl.BlockSpec((tm, GROUP), lambda i, j: (i, j))],
        lambda a, w: (_dot(a[...], w[...]),))
    mo = _mixer_fwd("mixer_fwd", z, sm["mixer"])
    tm = _tile(s, 512)
    def mix_out(a, w, hv, gv):
        h2v = hv[...] + _dot(a[...], w[...])
        return h2v, _rms(h2v, gv[...])

    h2, y2 = _tiles(
        "w_out_fwd", (s // tm,),
        [mo, fetch("wout", z), h, sm["norm_ffn_g"]], [rows(tm, d), _resident((d, d)), rows(tm, d), vec],
        [_sds((s, d), F32), _sds((s, d), BF16)], [rows(tm, d), rows(tm, d)], mix_out)
    sv.update(y1=y1, z=z, mo=mo, h2=h2)

    gate_pre, up_pre, hmid = _ffn_up("ffn_up_fwd", y2, fetch("wg", z), fetch("wu", z))
    wd = fetch("wd", hmid)
    f8 = wd.shape[1]
    tm, tn = _tile(s, 512), _tile(d, 1024)
    (h3,) = _tiles(
        "w_down_fwd", (d // tn, s // tm),
        [hmid, wd, h2],
        [pl.BlockSpec((nb, tm, f8), lambda j, i: (0, i, 0)), pl.BlockSpec((nb, f8, tn), lambda j, i: (0, 0, j)),
         pl.BlockSpec((tm, tn), lambda j, i: (i, j))],
        [_sds((s, d), F32)], [pl.BlockSpec((tm, tn), lambda j, i: (i, j))],
        lambda a, w, hv: (hv[...] + _chain([(a[k], w[k]) for k in range(nb)], "nn"),))
    sv.update(y2=y2, gate_pre=gate_pre, up_pre=up_pre, hmid=hmid, h3=h3)

    y3 = _rms_fwd("rms_ple", h3, sm["norm_ple_g"])
    tm = _tile(s, 1024)
    (pp,) = _tiles(
        "w_ple_proj_fwd", (s // tm, nb),
        [p_all, fetch("wpp", hmid)],
        [pl.BlockSpec((None, tm, pd), lambda i, j: (li, i, 0)), pl.BlockSpec((None, pd, pd), lambda i, j: (j, 0, 0))],
        [_sds((s, d), BF16)], [pl.BlockSpec((tm, pd), lambda i, j: (i, j))],
        lambda a, w: (_dot(a[...], w[...]),))
    tm = _tile(s, 256)

    def ple(a, w, hv, ppv, *next_g):
        pg = _dot(a[...], w[...])
        h4v = hv[...] + jax.nn.sigmoid(pg) * ppv[...].astype(F32)
        return (h4v, pg) + tuple(_rms(h4v, gv[...]) for gv in next_g)

    more = [] if next_mix_g is None else [next_mix_g]
    h4, pg_pre, *y1_next = _tiles(
        "w_ple_gate_fwd", (s // tm,),
        [y3, fetch("wpg", hmid), h3, pp] + more,
        [rows(tm, d), _resident((d, d)), rows(tm, d), rows(tm, d)] + [vec] * len(more),
        [_sds((s, d), F32), _sds((s, d), BF16)] + [_sds((s, d), BF16)] * len(more),
        [rows(tm, d)] * (2 + len(more)), ple)
    sv.update(y3=y3, pp=pp, pg_pre=pg_pre)
    return h4, sv, (y1_next[0] if y1_next else None)


def _ffn_up(name, y2, wg, wu):
    s, d = y2.shape
    nb, f8, _ = wg.shape
    tm = _tile(s, 1024)

    def compute(y_ref, wg_ref, wu_ref):
        yv = y_ref[...]
        g = _dot(yv, wg_ref[...], "nt")
        u = _dot(yv, wu_ref[...], "nt")
        return g, u, jax.nn.silu(g) * u

    wspec = pl.BlockSpec((None, f8, d), lambda i, j: (j, 0, 0))
    ospec = pl.BlockSpec((None, tm, f8), lambda i, j: (j, i, 0))
    return _tiles(name, (s // tm, nb), [y2, wg, wu], [pl.BlockSpec((tm, d), lambda i, j: (i, 0)), wspec, wspec],
                  [_sds((nb, s, f8), BF16)] * 3, [ospec] * 3, compute)


def _swiglu_bwd(dm, g, u):
    sg = jax.nn.sigmoid(g)
    silu = g * sg
    return dm * u * (sg + silu * (1.0 - sg)), dm * silu


def _layer_bwd(li, dh, p_all, gw, sm, sv, start_token=None, hooks=None):
    hooks = hooks or {}
    run = lambda name, *a: hooks[name](*a) if name in hooks else None
    s, d = dh.shape
    nb, f8, _ = gw["wg"].shape
    pd = p_all.shape[2]
    gr, sg = {}, {}
    rows = lambda t, w: pl.BlockSpec((t, w), lambda i: (i, 0))
    vec = pl.BlockSpec((1, d), lambda i: (0, 0))
    tw = _tile(d, 1024)
    whole = lambda w: pl.BlockSpec((s, w), lambda i, j: (0, i))
    whole_j = lambda w: pl.BlockSpec((s, w), lambda i, j: (0, j))
    tn_dot = lambda a, b: (_dot(a[...], b[...], "tn"),)

    tm = _tile(s, 256)

    def ple_dx(dv, pg, ppv, w, hv, gv):
        dhv = dv[...]
        gate = jax.nn.sigmoid(pg[...].astype(F32))
        dpg = (dhv * ppv[...].astype(F32) * gate * (1.0 - gate)).astype(BF16)
        dh3, dg = _rms_back(hv[...], gv[...], _dot(dpg, w[...], "nt"), dhv)
        return dh3, dh3, dg, dpg, dhv * gate

    dh3, dh3_16, sg["norm_ple_g"], dpg, dpp = _tiles(
        "w_ple_gate_dx", (s // tm,),
        [dh, sv["pg_pre"], sv["pp"], gw["wpg"], sv["h3"], _after(sm["norm_ple_g"], start_token)],
        [rows(tm, d), rows(tm, d), rows(tm, d), _resident((d, d)), rows(tm, d), vec],
        [_sds((s, d), F32), _sds((s, d), BF16), _sds((1, d), F32), _sds((s, d), BF16), _sds((s, d), BF16)],
        [rows(tm, d), rows(tm, d), vec, rows(tm, d), rows(tm, d)], ple_dx, summed=(2,))
    (gr["wpp"],) = _tiles(
        "w_ple_proj_dw", (nb,),
        [p_all, dpp], [pl.BlockSpec((None, s, pd), lambda j: (li, 0, 0)), pl.BlockSpec((s, pd), lambda j: (0, j))],
        [_sds((nb, pd, pd), F32)], [pl.BlockSpec((None, pd, pd), lambda j: (j, 0, 0))], tn_dot)
    (gr["wpg"],) = _tiles(
        "w_ple_gate_dw", (d // tw, d // tw), [sv["y3"], dpg], [whole(tw), whole_j(tw)],
        [_sds((d, d), F32)], [pl.BlockSpec((tw, tw), lambda i, j: (i, j))], tn_dot)
    gr["wpg"] = gr["wpg"].reshape(nb, d // nb, d)

    tm = _tile(s, 1024)
    blk_rows = pl.BlockSpec((None, tm, f8), lambda i, j: (j, i, 0))
    def down_dx(a, w, g, u):
        n = tm // 4
        parts = [_swiglu_bwd(_dot(a[q * n:(q + 1) * n, :], w[...], "nt"),
                             g[q * n:(q + 1) * n, :].astype(F32), u[q * n:(q + 1) * n, :].astype(F32)) for q in range(4)]
        return jnp.concatenate([p_[0] for p_ in parts], axis=0), jnp.concatenate([p_[1] for p_ in parts], axis=0)

    dgate, dup = _tiles(
        "w_down_dx", (s // tm, nb),
        [dh3_16, gw["wd"], sv["gate_pre"], sv["up_pre"]],
        [pl.BlockSpec((tm, d), lambda i, j: (i, 0)), pl.BlockSpec((None, f8, d), lambda i, j: (j, 0, 0)), blk_rows, blk_rows],
        [_sds((nb, s, f8), BF16)] * 2, [blk_rows] * 2, down_dx)
    (gr["wd"],) = _tiles(
        "w_down_dw", (nb, d // tw),
        [sv["hmid"], dh3_16], [pl.BlockSpec((None, s, f8), lambda i, j: (i, 0, 0)), whole_j(tw)],
        [_sds((nb, f8, d), F32)], [pl.BlockSpec((None, f8, tw), lambda i, j: (i, 0, j))], tn_dot)
    mid_token = run("after_down_dx", dgate)
    for nm, dact in (("wg", dgate), ("wu", dup)):
        (gr[nm],) = _tiles(
            "w_" + {"wg": "gate", "wu": "up"}[nm] + "_dw", (d // tw, nb),
            [dact, sv["y2"]], [pl.BlockSpec((None, s, f8), lambda i, j: (j, 0, 0)), whole(tw)],
            [_sds((nb, f8, d), F32)], [pl.BlockSpec((None, f8, tw), lambda i, j: (j, 0, i))], tn_dot)
    ffn_token = run("after_ffn_grads", gr, gr["wu"])
    tm, tn = _tile(s, 512), _tile(d, 1024)
    act_spec = pl.BlockSpec((nb, tm, f8), lambda j, i: (0, i, 0))
    wt_spec = pl.BlockSpec((nb, f8, tn), lambda j, i: (0, 0, j))
    out_spec = pl.BlockSpec((tm, tn), lambda j, i: (i, j))
    (dy2,) = _tiles(
        "w_gate_dx", (d // tn, s // tm), [dgate, gw["wg"]], [act_spec, wt_spec],
        [_sds((s, d), F32)], [out_spec],
        lambda a, w: (_chain([(a[k], w[k]) for k in range(nb)], "nn"),),
        deps=[tk for tk in (mid_token, ffn_token) if tk is not None])
    (dy2,) = _tiles(
        "w_up_dx", (d // tn, s // tm), [dup, gw["wu"], dy2], [act_spec, wt_spec, out_spec],
        [_sds((s, d), F32)], [out_spec],
        lambda a, w, prev: (prev[...] + _chain([(a[k], w[k]) for k in range(nb)], "nn"),))
    g_ffn = _after(sm["norm_ffn_g"], run("before_ffn_norm", dy2))

    tm = _tile(s, 256)

    def out_dx(dy, hv, gv, dv, w):
        dh2v, dg = _rms_back(hv[...], gv[...], dy[...], dv[...])
        return dh2v, dh2v, dg, _dot(dh2v, w[...], "nt")

    dh2, dh2_16, sg["norm_ffn_g"], dmo = _tiles(
        "w_out_dx", (s // tm,), [dy2, sv["h2"], g_ffn, dh3, gw["wout"]],
        [rows(tm, d), rows(tm, d), vec, rows(tm, d), _resident((d, d))],
        [_sds((s, d), F32), _sds((s, d), BF16), _sds((1, d), F32), _sds((s, d), BF16)],
        [rows(tm, d), rows(tm, d), vec, rows(tm, d)], out_dx, summed=(2,))
    (gr["wout"],) = _tiles(
        "w_out_dw", (d // tw, d // tw), [sv["mo"], dh2_16], [whole(tw), whole_j(tw)],
        [_sds((d, d), F32)], [pl.BlockSpec((tw, tw), lambda i, j: (i, j))], tn_dot)
    gr["wout"] = gr["wout"].reshape(nb, d // nb, d)
    dz, *mix_grads = _mixer_bwd("mixer_bwd", sv["z"], dmo, sm["mixer"])
    sg["mixer"] = mix_grads
    (gr["win"],) = _tiles(
        "w_in_dw", (d // tw, nb), [sv["y1"], dz], [whole(tw), pl.BlockSpec((s, GROUP), lambda i, j: (0, j))],
        [_sds((nb, d, GROUP), F32)], [pl.BlockSpec((None, tw, GROUP), lambda i, j: (j, i, 0))], tn_dot)
    in_token = run("after_in_dw", gr)
    tm = _tile(s, 256)

    def in_dx(a, w, hv, gv, dv):
        dy1 = _chain([(a[:, k * GROUP:(k + 1) * GROUP], w[k]) for k in range(nb)], "nt")
        return _rms_back(hv[...], gv[...], dy1, dv[...])

    dh_in, sg["norm_mix_g"] = _tiles(
        "w_in_dx", (s // tm,),
        [dz, gw["win"], sv["h"], sm["norm_mix_g"], dh2],
        [rows(tm, nb * GROUP), _resident((nb, d, GROUP)), rows(tm, d), vec, rows(tm, d)],
        [_sds((s, d), F32), _sds((1, d), F32)], [rows(tm, d), vec], in_dx, summed=(1,),
        deps=[] if in_token is None else [in_token])
    return dh_in, gr, sg


def _place():
    return lax.axis_index("x"), lax.axis_index("y"), lax.axis_index("c")


_HBM = pl.BlockSpec(memory_space=pltpu.HBM)
_SEM = pl.BlockSpec(memory_space=pltpu.SEMAPHORE)


def _split_call(name, bufs, old_sems, n_new, after, body, want_token):
    nb, no = len(bufs), len(old_sems)
    extra = [] if after is None else [after]

    def kbody(*refs):
        new = refs[nb + no + len(extra):nb + no + len(extra) + n_new]
        body(refs[:nb], refs[nb:nb + no], new)
        if want_token:
            refs[-1][...] = jnp.zeros_like(refs[-1])

    outs = pl.pallas_call(
        kbody, name=name,
        out_shape=tuple([pltpu.SemaphoreType.DMA(())] * n_new + [pltpu.HBM(b.shape, b.dtype) for b in bufs]
                        + ([_sds((8, 128), F32)] if want_token else [])),
        in_specs=[_HBM] * nb + [_SEM] * no + [ANY] * len(extra),
        out_specs=tuple([_SEM] * n_new + [_HBM] * nb
                        + ([pl.BlockSpec(memory_space=pltpu.VMEM)] if want_token else [])),
        input_output_aliases={i: n_new + i for i in range(nb)},
        compiler_params=pltpu.CompilerParams(has_side_effects=pltpu.SideEffectType.DATAFLOW_SIDE_EFFECTING),
    )(*[pltpu.with_memory_space_constraint(b, pltpu.HBM) for b in bufs], *old_sems, *extra)
    return list(outs[:n_new]), list(outs[n_new:n_new + nb]), (outs[-1] if want_token else None)


def _remote(ref_src, ref_dst, send_sem, recv_sem, to):
    return pltpu.make_async_remote_copy(src_ref=ref_src, dst_ref=ref_dst, send_sem=send_sem, recv_sem=recv_sem,
                                        device_id=to, device_id_type=MESH)


def _place_shard(name, w):
    n_layers, r, cdim = w.shape
    tr = _row_tile(r, 256)
    nt = r // tr

    def body(w_ref, *rest):
        outs, buf, sem = rest[:n_layers], rest[n_layers], rest[n_layers + 1]
        i = pl.program_id(0)
        x, y, c = _place()
        slot = i % 2

        def writes(step, sl):
            rows = pl.ds(pl.multiple_of(step * tr, 16), tr)
            return [pltpu.make_async_copy(buf.at[sl, l], outs[l].at[4 * x + 2 * y + c, rows, :], sem.at[sl, l])
                    for l in range(n_layers)]

        @pl.when(i >= 2)
        def _():
            for cp in writes(i - 2, slot):
                cp.wait()

        buf[slot] = w_ref[...].astype(BF16)
        for cp in writes(i, slot):
            cp.start()

        @pl.when(i == nt - 1)
        def _():
            for cp in writes(i, slot):
                cp.wait()
            if nt >= 2:
                for cp in writes(i - 1, 1 - slot):
                    cp.wait()

    return pl.pallas_call(
        body, name=name, grid=(nt,),
        in_specs=[pl.BlockSpec((n_layers, tr, cdim), lambda i: (0, i, 0))], out_specs=[ANY] * n_layers,
        out_shape=[_sds((N_DEV, r, cdim), BF16)] * n_layers,
        scratch_shapes=[pltpu.VMEM((2, n_layers, tr, cdim), BF16), pltpu.SemaphoreType.DMA((2, n_layers))],
        compiler_params=_params("arbitrary"),
    )(w)


def _gather_start(li, lands, after):
    nw = len(lands)

    def body(bufs, _, new):
        x, y, c = _place()
        chips = [(1 - x, y), (x, 1 - y), (1 - x, 1 - y)]
        for w in range(nw):
            mine = bufs[w].at[4 * x + 2 * y + c]
            _remote(mine, mine, new[8 * w], new[8 * w + 4], (x, y, 1 - c)).start()
            for j, (px, py) in enumerate(chips):
                _remote(mine, mine, new[8 * w + 1 + j], new[8 * w + 5 + j], (px, py, c)).start()

    return _split_call("gather_start_l%s" % li, lands, [], 8 * nw, after, body, True)


def _arrivals(sems, nw):
    return [sems[8 * w + 5 + j] for w in range(nw) for j in range(3)]


def _pass_on(bufs, arrived, new):
    x, y, c = _place()
    for j, (px, py) in enumerate([(1 - x, y), (x, 1 - y), (1 - x, 1 - y)]):
        for w in range(len(bufs)):
            got = bufs[w].at[4 * px + 2 * py + c]
            _remote(got, got, new[6 * w + j], arrived[3 * w + j], (px, py, c)).wait_recv()
            _remote(got, got, new[6 * w + j], new[6 * w + 3 + j], (x, y, 1 - c)).start()


def _wait_rest(bufs, first, fwd):
    x, y, c = _place()
    sib = (x, y, 1 - c)
    for w in range(len(bufs)):
        mine = bufs[w].at[4 * x + 2 * y + c]
        theirs = bufs[w].at[4 * x + 2 * y + 1 - c]
        _remote(theirs, theirs, first[5 * w], first[5 * w + 4], sib).wait_recv()
        for k in range(4):
            _remote(mine, mine, first[5 * w + k], first[5 * w + 4], sib).wait_send()
        for j, (px, py) in enumerate([(1 - x, y), (x, 1 - y), (1 - x, 1 - y)]):
            sent = bufs[w].at[4 * px + 2 * py + c]
            got = bufs[w].at[4 * px + 2 * py + 1 - c]
            _remote(sent, sent, fwd[6 * w + j], fwd[6 * w + 3 + j], sib).wait_send()
            _remote(got, got, fwd[6 * w + j], fwd[6 * w + 3 + j], sib).wait_recv()


def _gather_forward(li, lands, sems, after):
    nw = len(lands)
    return _split_call("gather_forward_l%s" % li, lands, _arrivals(sems, nw), 6 * nw, after,
                       lambda bufs, old, new: _pass_on(bufs, old, new), True)


def _gather_wait(li, lands, sems, fwd_sems, after, ahead=None):
    nw = len(lands)
    first = [sems[8 * w + k] for w in range(nw) for k in range(5)]
    n1, n2 = len(first), len(first) + len(fwd_sems)
    more, arrived = ([], []) if ahead is None else (list(ahead[0]), _arrivals(ahead[1], len(ahead[0])))

    def body(bufs, old, new):
        if ahead is not None:
            _pass_on(bufs[nw:], old[n2:], new)
        _wait_rest(bufs[:nw], old[:n1], old[n1:n2])

    new, bufs, _ = _split_call("gather_wait_l%s" % li, list(lands) + more, first + list(fwd_sems) + arrived,
                               6 * len(more), after, body, False)
    return bufs[:nw], (None if ahead is None else (new, bufs[nw:]))


def _siblings_start(li, grads, after):
    nw = len(grads)
    lands = [lax.empty((4,) + g.shape[1:], g.dtype) for g in grads]

    def body(bufs, _, new):
        x, y, c = _place()
        for w in range(nw):
            for q in range(4):
                _remote(bufs[w].at[2 * q + (1 - c)], bufs[nw + w].at[q], new[8 * w + q], new[8 * w + 4 + q],
                        (x, y, 1 - c)).start()

    return _split_call("rs_siblings_start_l%s" % li, list(grads) + lands, [], 8 * nw, after, body, True)


def _siblings_wait(li, bufs, sems, after):
    nw = len(bufs) // 2

    def body(refs, old, _):
        x, y, c = _place()
        for w in range(nw):
            for q in range(4):
                _remote(refs[w].at[2 * q + (1 - c)], refs[nw + w].at[q], old[8 * w + q], old[8 * w + 4 + q],
                        (x, y, 1 - c)).wait()

    _, bufs, _ = _split_call("rs_siblings_wait_l%s" % li, bufs, sems, 0, after, body, False)
    return bufs[:nw], bufs[nw:]


_FLIPS = ((1, 0), (0, 1), (1, 1))


def _chips_copies(refs, nw, sems):
    x, y, c = _place()
    for w in range(nw):
        for r, (fx, fy) in enumerate(_FLIPS):
            px = 1 - x if fx else x
            py = 1 - y if fy else y
            yield _remote(refs[w].at[2 * px + py], refs[nw + w].at[r], sems[6 * w + r], sems[6 * w + 3 + r], (px, py, c))


def _chips_start(li, partials, after):
    nw = len(partials)
    lands = [lax.empty((3,) + a.shape[1:], a.dtype) for a in partials]

    def body(bufs, _, new):
        for cp in _chips_copies(bufs, nw, new):
            cp.start()

    return _split_call("rs_chips_start_l%s" % li, list(partials) + lands, [], 6 * nw, after, body, True)


def _chips_wait(li, bufs, sems, after):
    nw = len(bufs) // 2

    def body(refs, old, _):
        for cp in _chips_copies(refs, nw, old):
            cp.wait()

    _, bufs, _ = _split_call("rs_chips_wait_l%s" % li, bufs, sems, 0, after, body, False)
    return bufs[nw:]


def _row_tile(n, want):
    best = None
    for t in range(16, min(n, want) + 1, 16):
        if n % t == 0:
            best = t
    assert best is not None, n
    return best


def _chip_partials(name, grad, from_sibling):
    _, r, cdim = grad.shape
    tr = _row_tile(r, 512)
    nt = r // tr
    steps = 4 * nt

    def body(g_hbm, s_ref, pb_ref, own_ref, buf, sem):
        i, q = pl.program_id(0), pl.program_id(1)
        x, y, c = _place()
        n = 4 * i + q
        slot = n % 2

        def fetch(step, into):
            rows = pl.ds(pl.multiple_of((step // 4) * tr, 16), tr)
            return pltpu.make_async_copy(g_hbm.at[2 * (step % 4) + c, rows, :], buf.at[into], sem.at[into])

        @pl.when(n == 0)
        def _():
            fetch(0, 0).start()

        @pl.when(n + 1 < steps)
        def _():
            fetch(n + 1, 1 - slot).start()

        fetch(n, slot).wait()
        tot = buf[slot] + s_ref[...]
        pb_ref[...] = tot.astype(BF16)

        @pl.when(q == 2 * x + y)
        def _():
            own_ref[...] = tot

    return pl.pallas_call(
        body, name=name, grid=(nt, 4),
        in_specs=[ANY, pl.BlockSpec((None, tr, cdim), lambda i, q: (q, i, 0))],
        out_specs=[pl.BlockSpec((None, tr, cdim), lambda i, q: (q, i, 0)), pl.BlockSpec((tr, cdim), lambda i, q: (i, 0))],
        out_shape=[_sds((4, r, cdim), BF16), _sds((r, cdim), F32)],
        scratch_shapes=[pltpu.VMEM((2, tr, cdim), F32), pltpu.SemaphoreType.DMA((2,))],
        compiler_params=_params("arbitrary", "arbitrary"),
    )(grad, from_sibling)


def _adamw(w, g, m, v):
    m = ADAM_B1 * m + (1.0 - ADAM_B1) * g
    v = ADAM_B2 * v + (1.0 - ADAM_B2) * (g * g)
    m_hat = m / (1.0 - ADAM_B1 ** ADAM_STEP)
    v_hat = v / (1.0 - ADAM_B2 ** ADAM_STEP)
    delta = -ADAM_LR * (m_hat / (jnp.sqrt(v_hat) + ADAM_EPS) + ADAM_WD * w)
    return delta, m, v


def _finish_weight(name, li, own, from_chips, w, m, v, stacked):
    r, cdim = own.shape
    tr = _row_tile(r, 256)
    if stacked is None:
        stacked = [lax.empty(w.shape, F32) for _ in range(4)]

    def body(own_ref, fc_ref, w_ref, m_ref, v_ref, *rest):
        g_out, d_out, m_out, v_out = rest[4:]
        g = own_ref[...] + fc_ref[0].astype(F32) + fc_ref[1].astype(F32) + fc_ref[2].astype(F32)
        delta, mn, vn = _adamw(w_ref[...], g, m_ref[...], v_ref[...])
        g_out[...] = g
        d_out[...] = delta
        m_out[...] = mn
        v_out[...] = vn

    tile = pl.BlockSpec((tr, cdim), lambda i: (i, 0))
    lay = pl.BlockSpec((None, tr, cdim), lambda i: (li, i, 0))
    return pl.pallas_call(
        body, name=name, grid=(r // tr,),
        in_specs=[tile, pl.BlockSpec((3, tr, cdim), lambda i: (0, i, 0)), lay, lay, lay] + [ANY] * 4,
        out_specs=[lay] * 4, out_shape=[_sds(w.shape, F32)] * 4,
        input_output_aliases={5: 0, 6: 1, 7: 2, 8: 3},
        compiler_params=_params("parallel"),
    )(own, from_chips, w, m, v, *stacked)


def _allgather_small(name, v, reduce, after=None):
    r = v.shape[0]
    deps = [] if after is None else [after]

    def body(x_ref, *rest):
        out_ref, rest = rest[len(deps)], rest[len(deps) + 1:]
        if reduce:
            sum_ref, send_sems, recv_sems, local_sem = rest
        else:
            send_sems, recv_sems, local_sem = rest
        x, y, c = _place()
        me, sib = (x, y, c), (x, y, 1 - c)
        chips = [(1 - x, y), (x, 1 - y), (1 - x, 1 - y)]

        def rows(px, py, pc):
            return out_ref.at[pl.ds(pl.multiple_of((4 * px + 2 * py + pc) * r, 8), r), :]

        def copy(k, block, to, src=None):
            return pltpu.make_async_remote_copy(
                src_ref=rows(*block) if src is None else src, dst_ref=rows(*block),
                send_sem=send_sems.at[k], recv_sem=recv_sems.at[k], device_id=to, device_id_type=MESH)

        mine = pltpu.make_async_copy(x_ref, rows(*me), local_sem)
        mine.start()
        first = [copy(0, me, sib, src=x_ref)]
        first += [copy(1 + j, me, (*chip, c), src=x_ref) for j, chip in enumerate(chips)]
        for cp in first:
            cp.start()
        passed = [copy(4 + j, (*chip, c), sib) for j, chip in enumerate(chips)]
        for j, chip in enumerate(chips):
            copy(1 + j, (*chip, c), me).wait_recv()
            passed[j].start()
        copy(0, sib, me).wait_recv()
        for j, chip in enumerate(chips):
            copy(4 + j, (*chip, 1 - c), me).wait_recv()
        for cp in first + passed:
            cp.wait_send()
        mine.wait()
        if reduce:
            tot = out_ref[0:r, :]
            for d in range(1, N_DEV):
                tot = tot + out_ref[d * r:(d + 1) * r, :]
            sum_ref[...] = tot

    vm = pl.BlockSpec(memory_space=pltpu.VMEM)
    outs = [_sds((N_DEV * r, 128), F32)] + ([_sds((r, 128), F32)] if reduce else [])
    res = pl.pallas_call(
        body, name=name,
        in_specs=[vm] + [ANY] * len(deps), out_specs=[vm] * len(outs), out_shape=outs,
        scratch_shapes=[pltpu.SemaphoreType.DMA((7,)), pltpu.SemaphoreType.DMA((7,)), pltpu.SemaphoreType.DMA],
        compiler_params=pltpu.CompilerParams(vmem_limit_bytes=VMEM_LIMIT_BYTES),
    )(v, *deps)
    return res


def _adamw_small(name, w, g, m, v):
    def body(w_ref, g_ref, m_ref, v_ref, d_out, m_out, v_out):
        delta, mn, vn = _adamw(w_ref[...], g_ref[...], m_ref[...], v_ref[...])
        d_out[...] = delta
        m_out[...] = mn
        v_out[...] = vn

    vm = pl.BlockSpec(memory_space=pltpu.VMEM)
    return pl.pallas_call(
        body, name=name, in_specs=[vm] * 4, out_specs=[vm] * 3, out_shape=[_sds(w.shape, F32)] * 3,
        compiler_params=pltpu.CompilerParams(vmem_limit_bytes=VMEM_LIMIT_BYTES),
    )(w, g, m, v)


def _pack(arrays):
    flat, layout, off = [], [], 0
    for a in arrays:
        flat.append(a.reshape(-1).astype(F32))
        layout.append((off, a.shape))
        off += a.size
    total = -(-off // 1024) * 1024
    if total > off:
        flat.append(jnp.zeros((total - off,), F32))
    return jnp.concatenate(flat).reshape(total // 128, 128), layout


def _unpack(packed, layout):
    flat = packed.reshape(-1)
    return [flat[off:off + math.prod(shape)].reshape(shape) for off, shape in layout]


_BIG = ("win", "wout", "wg", "wu", "wd", "wpg", "wpp")
_BIG_FULL = {"win": "w_in", "wout": "w_out", "wg": "w_gate", "wu": "w_up", "wd": "w_down",
             "wpg": "w_ple_gate", "wpp": "w_ple_proj"}
_SMALL_REPLICATED = ("norm_mix_g", "sgu_ln_g", "sgu_ln_b", "sgu_w", "sgu_b", "cf_conv_b", "cf_ln_g", "cf_ln_b",
                     "pool_w", "pool_scale", "norm_ffn_g", "norm_ple_g", "final_norm_g")
_SMALL_SHARDED = ("sc_conv_w", "cf_conv_w")
_WEIGHTS = ("norm_mix_g", "w_in", "sgu_ln_g", "sgu_ln_b", "sgu_w", "sgu_b", "sc_conv_w", "cf_conv_w", "cf_conv_b",
            "cf_ln_g", "cf_ln_b", "pool_w", "pool_scale", "w_out", "norm_ffn_g", "w_gate", "w_up", "w_down",
            "norm_ple_g", "w_ple_gate", "w_ple_proj", "final_norm_g")


def _pad_rows(a, rows):
    return jnp.concatenate([a, jnp.zeros((rows - a.shape[0],) + a.shape[1:], a.dtype)], axis=0)


def _mixer_params(li, W, sc_full, cf_full):
    return [W["sgu_ln_g"][li][:, None, :], W["sgu_ln_b"][li][:, None, :], W["sgu_w"][li], W["sgu_b"][li][:, :, None],
            _pad_rows(sc_full[li], 8), _pad_rows(cf_full[li], 32),
            W["cf_conv_b"][li][None, :], W["cf_ln_g"][li][None, :], W["cf_ln_b"][li][None, :],
            W["pool_w"][li], W["pool_scale"][li][None, :]]


def _step(W, M, V, x, p, loss_target):
    n_layers = W["w_in"].shape[0]
    h = x[0]
    target = loss_target[0]
    p_all = p[:, 0]
    xi, yi, ci = _place()
    blk = 4 * xi + 2 * yi + ci
    csh = W["sc_conv_w"].shape[2]
    turned = ("w_gate", "w_up")
    W, M, V = ({n: (jnp.swapaxes(a, 1, 2) if n in turned else a) for n, a in t.items()} for t in (W, M, V))

    packed, lay = _pack([W[n] for n in _SMALL_SHARDED])
    (taps,) = _allgather_small("gather_conv_taps", packed, reduce=False)
    per_dev = [_unpack(taps[d * packed.shape[0]:(d + 1) * packed.shape[0]], lay) for d in range(N_DEV)]
    sc_full = jnp.concatenate([pd[0] for pd in per_dev], axis=-1)
    cf_full = jnp.concatenate([pd[1] for pd in per_dev], axis=-1)

    d = h.shape[1]
    placed = [_place_shard("place_" + n, W[_BIG_FULL[n]]) for n in _BIG]
    lands = [[placed[w][li] for w in range(len(_BIG))] for li in range(n_layers)]

    groups = (("win",), ("wout", "wg", "wu"), ("wd", "wpg", "wpp"))
    order = [(li, gi) for li in range(n_layers) for gi in range(len(groups))]
    moving, tok = [], taps
    for li in range(n_layers):
        sems, bufs, tok = _gather_start(li, lands[li], tok)
        moving.append({"sems": sems, "bufs": bufs, "got": {}})
    passed = {}

    def group_of(item):
        li, gi = item
        idx = [_BIG.index(n) for n in groups[gi]]
        sems = [s_ for i in idx for s_ in moving[li]["sems"][8 * i:8 * i + 8]]
        return "%d_%s" % (li, groups[gi][0]), [moving[li]["bufs"][i] for i in idx], sems

    def fetcher(li):
        def fetch(name, after):
            got = moving[li]["got"]
            if name not in got:
                item = (li, next(gi for gi, g_ in enumerate(groups) if name in g_))
                tag, bufs, sems = group_of(item)
                if item not in passed:
                    fwd_sems, bufs, _ = _gather_forward(tag, bufs, sems, after)
                    passed[item] = (fwd_sems, bufs)
                fwd_sems, bufs = passed[item]
                nxt = order.index(item) + 1
                ahead = None
                if nxt < len(order) and order[nxt][0] >= 1:
                    ahead = group_of(order[nxt])[1:]
                arrived, sent_on = _gather_wait(tag, bufs, sems, fwd_sems, after, ahead)
                if ahead is not None:
                    passed[order[nxt]] = sent_on
                for n, a in zip(groups[item[1]], arrived):
                    got[n] = a.reshape(d, d) if n in ("wout", "wpg") else a
            return got[name]
        return fetch

    saved, gathered, smalls = [], [], []
    y1 = None
    for li in range(n_layers):
        sm = {"norm_mix_g": _after(W["norm_mix_g"][li][None, :], tok if li == 0 else None),
              "norm_ffn_g": W["norm_ffn_g"][li][None, :],
              "norm_ple_g": W["norm_ple_g"][li][None, :], "mixer": _mixer_params(li, W, sc_full, cf_full)}
        next_mix_g = W["norm_mix_g"][li + 1][None, :] if li + 1 < n_layers else None
        h, sv, y1 = _layer_fwd(li, h, p_all, fetcher(li), sm, y1, next_mix_g)
        saved.append(sv)
        gathered.append(moving[li]["got"])
        smalls.append(sm)
    loss, dh, d_final_g = _loss_head(h, W["final_norm_g"][None, :], target)

    big_out = {n: None for n in _BIG}
    small_grads = []

    def scatter_begin(li, tag, names, gr, after):
        sems, bufs, tok = _siblings_start(tag, [gr[n] for n in names], after)
        return {"li": li, "tag": tag, "names": names, "sems": sems, "bufs": bufs, "tok": tok}

    def scatter_middle(st, after):
        grads, from_sib = _siblings_wait(st["tag"], st["bufs"], st["sems"], after)
        parts = [_chip_partials("rs_sum_" + n, g_, s_) for n, g_, s_ in zip(st["names"], grads, from_sib)]
        st["own"] = [own for _, own in parts]
        st["sems"], st["bufs"], tok = _chips_start(st["tag"], [pb for pb, _ in parts], None)
        return tok

    def scatter_end(st, after):
        from_chips = _chips_wait(st["tag"], st["bufs"], st["sems"], after)
        for n, own, fc in zip(st["names"], st["own"], from_chips):
            full = _BIG_FULL[n]
            big_out[n] = _finish_weight("adamw_" + n, st["li"], own, fc, W[full], M[full], V[full], big_out[n])
        return big_out[st["names"][0]][0]

    order = _SMALL_REPLICATED + _SMALL_SHARDED

    def reduce_small(after):
        by_layer = small_grads[::-1]
        stacked = lambda fn: jnp.stack([fn(sg) for sg in by_layer])
        mix = lambda i: (lambda sg: sg["mixer"][i])
        grads_small = {
            "norm_mix_g": stacked(lambda sg: sg["norm_mix_g"][0]),
            "sgu_ln_g": stacked(mix(0))[:, :, 0, :], "sgu_ln_b": stacked(mix(1))[:, :, 0, :],
            "sgu_w": stacked(mix(2)), "sgu_b": stacked(mix(3))[:, :, :, 0],
            "sc_conv_w": stacked(mix(4))[:, :SHORT_K], "cf_conv_w": stacked(mix(5))[:, :CONF_K],
            "cf_conv_b": stacked(mix(6))[:, 0], "cf_ln_g": stacked(mix(7))[:, 0], "cf_ln_b": stacked(mix(8))[:, 0],
            "pool_w": stacked(mix(9)), "pool_scale": stacked(mix(10))[:, 0],
            "norm_ffn_g": stacked(lambda sg: sg["norm_ffn_g"][0]),
            "norm_ple_g": stacked(lambda sg: sg["norm_ple_g"][0]),
            "final_norm_g": d_final_g[0],
        }
        packed, lay = _pack([grads_small[n] for n in order] + [loss])
        _, summed = _allgather_small("allreduce_small", packed, reduce=True, after=after)
        return dict(zip(order + ("loss",), _unpack(summed, lay)))

    early, late = ("wpp", "wpg", "wd", "wg", "wu"), ("wout", "win")
    pending, tok = None, None
    for li in reversed(range(n_layers)):
        hooks, first, last = {}, {}, {}
        if pending is not None:
            hooks["after_down_dx"] = functools.partial(scatter_middle, pending)
        if li == 0:
            def begin_early(gr, after, first=first):
                first.update(scatter_begin(0, "0_early", early, gr, after))
                return first["tok"]

            def begin_late(gr, last=last):
                last.update(scatter_begin(0, "0_late", late, gr, None))
                return last["tok"]

            hooks["after_ffn_grads"] = begin_early
            hooks["before_ffn_norm"] = lambda after, first=first: scatter_middle(first, after)
            hooks["after_in_dw"] = begin_late
        dh, gr, sg = _layer_bwd(li, dh, p_all, gathered[li], smalls[li], saved[li], tok, hooks)
        small_grads.append(sg)
        if li > 0:
            done = None if pending is None else scatter_end(pending, dh)
            pending = scatter_begin(li, str(li), _BIG, gr, done)
            tok = pending["tok"]
        else:
            scatter_middle(last, dh)
            done = last["own"][0] if pending is None else scatter_end(pending, last["own"][0])
            done = scatter_end(first, done)
            total = reduce_small(scatter_end(last, done))
    loss_all = total["loss"][0, 0]

    out_g, out_d, out_m, out_v = {}, {}, {}, {}
    pw, lay_r = _pack([W[n] for n in _SMALL_REPLICATED])
    pg, _ = _pack([total[n] for n in _SMALL_REPLICATED])
    pm, _ = _pack([M[n] for n in _SMALL_REPLICATED])
    pv, _ = _pack([V[n] for n in _SMALL_REPLICATED])
    dd, mm, vv = _adamw_small("adamw_small", pw, pg, pm, pv)
    for n, a, b, c_ in zip(_SMALL_REPLICATED, _unpack(dd, lay_r), _unpack(mm, lay_r), _unpack(vv, lay_r)):
        out_g[n], out_d[n], out_m[n], out_v[n] = total[n], a, b, c_
    pick = (jnp.arange(N_DEV) == blk).astype(F32)[None, None, :, None]
    mine = {n: jnp.sum(total[n].reshape(total[n].shape[:2] + (N_DEV, csh)) * pick, axis=2) for n in _SMALL_SHARDED}
    pw, lay_s = _pack([W[n] for n in _SMALL_SHARDED])
    pg, _ = _pack([mine[n] for n in _SMALL_SHARDED])
    pm, _ = _pack([M[n] for n in _SMALL_SHARDED])
    pv, _ = _pack([V[n] for n in _SMALL_SHARDED])
    dd, mm, vv = _adamw_small("adamw_conv_taps", pw, pg, pm, pv)
    for n, a, b, c_ in zip(_SMALL_SHARDED, _unpack(dd, lay_s), _unpack(mm, lay_s), _unpack(vv, lay_s)):
        out_g[n], out_d[n], out_m[n], out_v[n] = mine[n], a, b, c_
    for n in _BIG:
        for k, dst in enumerate((out_g, out_d, out_m, out_v)):
            full = _BIG_FULL[n]
            dst[full] = jnp.swapaxes(big_out[n][k], 1, 2) if full in turned else big_out[n][k]

    return (loss_all, dh[None], *[out_g[n] for n in _WEIGHTS], *[out_d[n] for n in _WEIGHTS],
            *[out_m[n] for n in _WEIGHTS], *[out_v[n] for n in _WEIGHTS])


def kernel(x, p, norm_mix_g, w_in, sgu_ln_g, sgu_ln_b, sgu_w, sgu_b, sc_conv_w, cf_conv_w, cf_conv_b, cf_ln_g, cf_ln_b, pool_w, pool_scale, w_out, norm_ffn_g, w_gate, w_up, w_down, norm_ple_g, w_ple_gate, w_ple_proj, final_norm_g, loss_target, m_norm_mix_g, m_w_in, m_sgu_ln_g, m_sgu_ln_b, m_sgu_w, m_sgu_b, m_sc_conv_w, m_cf_conv_w, m_cf_conv_b, m_cf_ln_g, m_cf_ln_b, m_pool_w, m_pool_scale, m_w_out, m_norm_ffn_g, m_w_gate, m_w_up, m_w_down, m_norm_ple_g, m_w_ple_gate, m_w_ple_proj, m_final_norm_g, v_norm_mix_g, v_w_in, v_sgu_ln_g, v_sgu_ln_b, v_sgu_w, v_sgu_b, v_sc_conv_w, v_cf_conv_w, v_cf_conv_b, v_cf_ln_g, v_cf_ln_b, v_pool_w, v_pool_scale, v_w_out, v_norm_ffn_g, v_w_gate, v_w_up, v_w_down, v_norm_ple_g, v_w_ple_gate, v_w_ple_proj, v_final_norm_g):
    given = dict(locals())
    W = {n: given[n] for n in _WEIGHTS}
    M = {n: given["m_" + n] for n in _WEIGHTS}
    V = {n: given["v_" + n] for n in _WEIGHTS}
    return _step(W, M, V, x, p, loss_target)
```

```python
import functools
import math

import jax
import jax.numpy as jnp
from jax import lax
from jax.experimental import pallas as pl
from jax.experimental.pallas import tpu as pltpu

F32 = jnp.float32
BF16 = jnp.bfloat16
EPS = 1e-6
HEAD = 128
GROUP = 4 * HEAD
HALO = 32
SHORT_K = 3
CONF_K = 31
POOL_WINDOWS = (2, 4, 8, 16)
N_DEV = 8
MESH = pl.DeviceIdType.MESH
VMEM_LIMIT_BYTES = 56 * 1024 * 1024

ADAM_LR = 0.001
ADAM_B1 = 0.9
ADAM_B2 = 0.999
ADAM_EPS = 1e-08
ADAM_WD = 0.01
ADAM_STEP = 10

ANY = pl.BlockSpec(memory_space=pl.ANY)


def _params(*sem):
    return pltpu.CompilerParams(dimension_semantics=sem, vmem_limit_bytes=VMEM_LIMIT_BYTES)


def _rms(x, g):
    return x * lax.rsqrt(jnp.mean(x * x, axis=-1, keepdims=True) + EPS) * g


def _ln(x, g, b):
    mu = jnp.mean(x, axis=-1, keepdims=True)
    xc = x - mu
    var = jnp.mean(xc * xc, axis=-1, keepdims=True)
    return xc * lax.rsqrt(var + EPS) * g + b


def _bdot(a, b, dims=(((1,), (0,)), ((), ()))):
    return lax.dot_general(a.astype(BF16), b.astype(BF16), dims, preferred_element_type=F32)


def _sgu_piece(zu, zv, lg, lb, w, b):
    u = jax.nn.gelu(zu)
    v = _ln(jax.nn.gelu(zv), lg, lb)
    row = lax.broadcasted_iota(jnp.int32, w.shape, 0)
    col = lax.broadcasted_iota(jnp.int32, w.shape, 1)
    wm = jnp.where(row >= col, w, 0.0)
    return u * (_bdot(wm, v) + b)


def _conf_post(c, g, b):
    return jax.nn.silu(_ln(c, g, b))


def _pool_count(first_pos, rows, w):
    pos = first_pos + lax.broadcasted_iota(jnp.int32, (rows, 1), 0) + 1
    return jnp.minimum(pos, w).astype(F32)


_DIMS = {
    "nn": (((1,), (0,)), ((), ())),
    "nt": (((1,), (1,)), ((), ())),
    "tn": (((0,), (0,)), ((), ())),
}


def _tiles(name, grid, ins, in_specs, outs, out_specs, compute, summed=(), deps=()):
    ni = len(ins)
    nd = len(deps)

    def body(*refs):
        vals = compute(*refs[:ni])
        first = functools.reduce(jnp.logical_and, [pl.program_id(a) == 0 for a in range(len(grid))])
        for idx, (r, v) in enumerate(zip(refs[ni + nd:], vals)):
            if idx in summed:
                @pl.when(first)
                def _(r=r):
                    r[...] = jnp.zeros_like(r)

                r[...] += v
            else:
                r[...] = v.astype(r.dtype)

    sem = ("arbitrary" if summed else "parallel",) * len(grid)
    return pl.pallas_call(
        body, name=name, grid=grid, in_specs=list(in_specs) + [ANY] * nd, out_specs=list(out_specs),
        out_shape=list(outs), compiler_params=_params(*sem),
    )(*ins, *deps)


def _resident(shape):
    return pl.BlockSpec(shape, lambda *_: (0,) * len(shape), pipeline_mode=pl.Buffered(1))


def _sds(shape, dtype):
    return jax.ShapeDtypeStruct(tuple(shape), dtype)


def _tile(n, want):
    t = min(n, want)
    assert n % t == 0, (n, want)
    return t


def _rms_fwd(name, h, g):
    s, d = h.shape
    tm = _tile(s, 512)

    def body(h_ref, g_ref, y_ref):
        y_ref[...] = _rms(h_ref[...], g_ref[...]).astype(BF16)

    return pl.pallas_call(
        body, name=name, grid=(s // tm,),
        in_specs=[pl.BlockSpec((tm, d), lambda i: (i, 0)), pl.BlockSpec((1, d), lambda i: (0, 0))],
        out_specs=pl.BlockSpec((tm, d), lambda i: (i, 0)),
        out_shape=_sds((s, d), BF16),
        compiler_params=_params("parallel"),
    )(h, g)


def _rms_back(h, g, dy, dh_in):
    inv = lax.rsqrt(jnp.mean(h * h, axis=-1, keepdims=True) + EPS)
    xhat = h * inv
    dxhat = dy * g
    dh = inv * (dxhat - xhat * jnp.mean(dxhat * xhat, axis=-1, keepdims=True))
    return dh_in + dh, jnp.sum(dy * xhat, axis=0, keepdims=True)


def _loss_head(h, g, target):
    s, d = h.shape
    tm = _tile(s, 256)

    def body(h_ref, g_ref, t_ref, loss_ref, dh_ref, dg_ref):
        y, vjp = jax.vjp(_rms, h_ref[...], g_ref[...])
        err = y - t_ref[...]
        dh, dg = vjp(err * (1.0 / d))
        dh_ref[...] = dh
        per_token = jnp.mean(err * err, axis=-1, keepdims=True)
        part = 0.5 * jnp.sum(per_token, axis=0, keepdims=True)

        @pl.when(pl.program_id(0) == 0)
        def _():
            dg_ref[...] = jnp.zeros_like(dg_ref)
            loss_ref[...] = jnp.zeros_like(loss_ref)

        dg_ref[...] += dg
        loss_ref[...] += part

    tok = pl.BlockSpec((tm, d), lambda i: (i, 0))
    vec = pl.BlockSpec((1, d), lambda i: (0, 0))
    return pl.pallas_call(
        body, name="loss_head", grid=(s // tm,),
        in_specs=[tok, vec, tok],
        out_specs=[pl.BlockSpec((1, 1), lambda i: (0, 0)), tok, vec],
        out_shape=[_sds((1, 1), F32), _sds((s, d), F32), _sds((1, d), F32)],
        compiler_params=_params("arbitrary"),
    )(h, g, target)


_U, _V, _HB, _BG, _CG, _CA, _CGT, _PD = (GROUP * i for i in range(8))


def _cols(c0, w=GROUP):
    return slice(c0, c0 + w)


class _AsF32:
    def __init__(self, ref):
        self.ref = ref

    def __getitem__(self, idx):
        return self.ref[idx].astype(F32)


_LANE_CHUNKS = tuple(_cols(i * HEAD, HEAD) for i in range(GROUP // HEAD))


def _shifted_copies(sh_ref, ext_ref, length):
    for r in range(1, 8):
        sh_ref[r - 1, 0:length, :] = ext_ref[pl.ds(r, length), :]


def _window(ext_ref, sh_ref, off, rows, cc):
    q, r = divmod(off, 8)
    if sh_ref is None or r == 0:
        return ext_ref[pl.ds(off, rows), cc]
    return sh_ref[r - 1, pl.ds(8 * q, rows), cc]


def _taps(ext_ref, w_ref, nk, start, rows, cc, flip=False, sh_ref=None):
    acc = None
    for k in range(nk):
        kw = nk - 1 - k if flip else k
        term = w_ref[kw:kw + 1, cc] * _window(ext_ref, sh_ref, start + k, rows, cc)
        acc = term if acc is None else acc + term
    return acc


def _mixer_param_specs():
    full = lambda *shape: pl.BlockSpec(shape, lambda i: (0,) * len(shape))
    return [
        full(4, 1, HEAD), full(4, 1, HEAD), full(4, HEAD, HEAD), full(4, HEAD, 1),
        full(8, GROUP), full(32, GROUP), full(1, GROUP), full(1, GROUP), full(1, GROUP),
        full(4, HEAD, HEAD), full(1, GROUP),
    ]


def _mixer_fwd(name, z, prm):
    s = z.shape[0]
    t = _tile(s, 256)
    hb = t // HALO

    def body(zp_ref, zm_ref, lg_ref, lb_ref, sw_ref, sb_ref, scw_ref, cfw_ref, cfb_ref, cg_ref, cb_ref,
             pw_ref, ps_ref, o_ref, ext_ref, sh_ref):
        zp_ref, zm_ref = _AsF32(zp_ref), _AsF32(zm_ref)
        i = pl.program_id(0)
        keep = (i > 0).astype(F32)
        main = pl.ds(HALO, t)

        for n in range(t // HEAD):
            rows = slice(n * HEAD, (n + 1) * HEAD)
            for hh in range(4):
                cu = _cols(_U + hh * HEAD, HEAD)
                cv = _cols(_V + hh * HEAD, HEAD)
                o_ref[rows, _cols(hh * HEAD, HEAD)] = _sgu_piece(
                    zm_ref[rows, cu], zm_ref[rows, cv], lg_ref[hh], lb_ref[hh], sw_ref[hh], sb_ref[hh]
                ).astype(BF16)

        ext_ref[0:HALO, :] = zp_ref[:, _cols(_CG)] * zp_ref[:, _cols(_HB)] * keep
        ext_ref[main, :] = zm_ref[:, _cols(_CG)] * zm_ref[:, _cols(_HB)]
        y = jnp.zeros((t, GROUP), F32)
        for k in range(SHORT_K):
            y = y + scw_ref[k:k + 1, :] * ext_ref[pl.ds(HALO - (SHORT_K - 1) + k, t), :]
        o_ref[:, _cols(GROUP)] = (zm_ref[:, _cols(_BG)] * y).astype(BF16)

        ext_ref[0:HALO, :] = zp_ref[:, _cols(_CA)] * jax.nn.sigmoid(zp_ref[:, _cols(_CGT)]) * keep
        ext_ref[main, :] = zm_ref[:, _cols(_CA)] * jax.nn.sigmoid(zm_ref[:, _cols(_CGT)])
        _shifted_copies(sh_ref, ext_ref, HALO + t - 8)
        c = jnp.concatenate(
            [cfb_ref[:, cc] + _taps(ext_ref, cfw_ref, CONF_K, HALO - (CONF_K - 1), t, cc, sh_ref=sh_ref)
             for cc in _LANE_CHUNKS], axis=1)
        o_ref[:, _cols(2 * GROUP)] = _conf_post(c, cg_ref[...], cb_ref[...]).astype(BF16)

        ext_ref[0:HALO, :] = zp_ref[:, _cols(_PD)] * keep
        ext_ref[main, :] = zm_ref[:, _cols(_PD)]
        for gi, w in enumerate(POOL_WINDOWS):
            cc = _cols(gi * HEAD, HEAD)
            acc = ext_ref[main, cc]
            for j in range(1, w):
                acc = acc + ext_ref[pl.ds(HALO - j, t), cc]
            q = acc / _pool_count(i * t, t, w) - ext_ref[main, cc]
            o_ref[:, _cols(3 * GROUP + gi * HEAD, HEAD)] = (_bdot(q, pw_ref[gi]) * ps_ref[:, cc]).astype(BF16)

    return pl.pallas_call(
        body, name=name, grid=(s // t,),
        in_specs=[pl.BlockSpec((HALO, 8 * GROUP), lambda i: (jnp.maximum(i * hb - 1, 0), 0)),
                  pl.BlockSpec((t, 8 * GROUP), lambda i: (i, 0)),
                  *_mixer_param_specs()],
        out_specs=pl.BlockSpec((t, 4 * GROUP), lambda i: (i, 0)),
        out_shape=_sds((s, 4 * GROUP), BF16),
        scratch_shapes=[pltpu.VMEM((HALO + t, GROUP), F32), pltpu.VMEM((7, HALO + t - 8, GROUP), F32)],
        compiler_params=_params("parallel"),
    )(z, z, *prm)


def _mixer_bwd(name, z, dmo, prm):
    s = z.shape[0]
    t = _tile(s, 256)
    hb = t // HALO
    nt = s // t
    last_halo = s // HALO - 1

    def body(zp_ref, zm_ref, zn_ref, dm_ref, dn_ref,
             lg_ref, lb_ref, sw_ref, sb_ref, scw_ref, cfw_ref, cfb_ref, cg_ref, cb_ref, pw_ref, ps_ref,
             dz_ref, dlg_ref, dlb_ref, dsw_ref, dsb_ref, dscw_ref, dcfw_ref, dcfb_ref, dcg_ref, dcb_ref,
             dpw_ref, dps_ref, extp_ref, extn_ref, extc_ref, sh_ref):
        zp_ref, zm_ref, zn_ref = _AsF32(zp_ref), _AsF32(zm_ref), _AsF32(zn_ref)
        i = pl.program_id(0)
        keep_prev = (i > 0).astype(F32)
        keep_next = (i < nt - 1).astype(F32)
        main = pl.ds(HALO, t)

        @pl.when(i == 0)
        def _():
            for r in (dlg_ref, dlb_ref, dsw_ref, dsb_ref, dscw_ref, dcfw_ref, dcfb_ref, dcg_ref, dcb_ref,
                      dpw_ref, dps_ref):
                r[...] = jnp.zeros_like(r)

        def rowsum(v):
            return jnp.sum(v, axis=0, keepdims=True)

        for n in range(t // HEAD):
            rows = slice(n * HEAD, (n + 1) * HEAD)
            for hh in range(4):
                cu = _cols(_U + hh * HEAD, HEAD)
                cv = _cols(_V + hh * HEAD, HEAD)
                _, vjp = jax.vjp(_sgu_piece, zm_ref[rows, cu], zm_ref[rows, cv],
                                 lg_ref[hh], lb_ref[hh], sw_ref[hh], sb_ref[hh])
                dzu, dzv, dlg, dlb, dsw, dsb = vjp(dm_ref[rows, _cols(hh * HEAD, HEAD)].astype(F32))
                dz_ref[rows, cu] = dzu.astype(BF16)
                dz_ref[rows, cv] = dzv.astype(BF16)
                dlg_ref[hh] += dlg
                dlb_ref[hh] += dlb
                dsw_ref[hh] += dsw
                dsb_ref[hh] += dsb

        extp_ref[0:HALO, :] = zp_ref[:, _cols(_CG)] * zp_ref[:, _cols(_HB)] * keep_prev
        extp_ref[main, :] = zm_ref[:, _cols(_CG)] * zm_ref[:, _cols(_HB)]
        dob = dm_ref[:, _cols(GROUP)].astype(F32)
        dy = dob * zm_ref[:, _cols(_BG)]
        extn_ref[0:t, :] = dy
        extn_ref[t:t + HALO, :] = dn_ref[:, _cols(GROUP)].astype(F32) * zn_ref[:, _cols(_BG)] * keep_next
        for cc in _LANE_CHUNKS:
            at = lambda c0: _cols(c0 + cc.start, HEAD)
            y = _taps(extp_ref, scw_ref, SHORT_K, HALO - (SHORT_K - 1), t, cc)
            dx = _taps(extn_ref, scw_ref, SHORT_K, 0, t, cc, flip=True)
            dy_c = extn_ref[0:t, cc]
            for k in range(SHORT_K):
                dscw_ref[k:k + 1, cc] += rowsum(dy_c * extp_ref[pl.ds(HALO - (SHORT_K - 1) + k, t), cc])
            dz_ref[:, at(_BG)] = (dm_ref[:, at(GROUP)].astype(F32) * y).astype(BF16)
            dz_ref[:, at(_CG)] = (dx * zm_ref[:, at(_HB)]).astype(BF16)
            dz_ref[:, at(_HB)] = (dx * zm_ref[:, at(_CG)]).astype(BF16)

        extc_ref[0:HALO, :] = zp_ref[:, _cols(_CA)] * jax.nn.sigmoid(zp_ref[:, _cols(_CGT)]) * keep_prev
        extc_ref[main, :] = zm_ref[:, _cols(_CA)] * jax.nn.sigmoid(zm_ref[:, _cols(_CGT)])
        extc_ref[HALO + t:HALO + t + HALO, :] = zn_ref[:, _cols(_CA)] * jax.nn.sigmoid(zn_ref[:, _cols(_CGT)])
        _shifted_copies(sh_ref, extc_ref, HALO + t + HALO - 8)
        conv = lambda start, rows: jnp.concatenate(
            [cfb_ref[:, cc] + _taps(extc_ref, cfw_ref, CONF_K, start, rows, cc, sh_ref=sh_ref) for cc in _LANE_CHUNKS],
            axis=1)
        c_main = conv(HALO - (CONF_K - 1), t)
        c_next = conv(HALO + t - (CONF_K - 1), HALO)
        _, vjp = jax.vjp(_conf_post, c_main, cg_ref[...], cb_ref[...])
        dc, dcg, dcb = vjp(dm_ref[:, _cols(2 * GROUP)].astype(F32))
        dcg_ref[...] += dcg
        dcb_ref[...] += dcb
        dcfb_ref[...] += rowsum(dc)
        _, vjp_next = jax.vjp(lambda cv: _conf_post(cv, cg_ref[...], cb_ref[...]), c_next)
        (dc_next,) = vjp_next(dn_ref[:, _cols(2 * GROUP)].astype(F32) * keep_next)
        extn_ref[0:t, :] = dc
        extn_ref[t:t + HALO, :] = dc_next
        for cc in _LANE_CHUNKS:
            dc_c = extn_ref[0:t, cc]
            for k in range(CONF_K):
                dcfw_ref[k:k + 1, cc] += rowsum(dc_c * _window(extc_ref, sh_ref, HALO - (CONF_K - 1) + k, t, cc))
        _shifted_copies(sh_ref, extn_ref, t + HALO - 8)
        for cc in _LANE_CHUNKS:
            at = lambda c0: _cols(c0 + cc.start, HEAD)
            dhc = _taps(extn_ref, cfw_ref, CONF_K, 0, t, cc, flip=True, sh_ref=sh_ref)
            sg = jax.nn.sigmoid(zm_ref[:, at(_CGT)])
            dz_ref[:, at(_CA)] = (dhc * sg).astype(BF16)
            dz_ref[:, at(_CGT)] = (dhc * zm_ref[:, at(_CA)] * sg * (1.0 - sg)).astype(BF16)

        extp_ref[0:HALO, :] = zp_ref[:, _cols(_PD)] * keep_prev
        extp_ref[main, :] = zm_ref[:, _cols(_PD)]
        for gi, w in enumerate(POOL_WINDOWS):
            cc = _cols(gi * HEAD, HEAD)
            oc = _cols(3 * GROUP + gi * HEAD, HEAD)
            acc = extp_ref[main, cc]
            for j in range(1, w):
                acc = acc + extp_ref[pl.ds(HALO - j, t), cc]
            count = _pool_count(i * t, t, w)
            q = acc / count - extp_ref[main, cc]
            dod = dm_ref[:, oc].astype(F32)
            dps_ref[:, cc] += rowsum(dod * _bdot(q, pw_ref[gi]))
            ds = dod * ps_ref[:, cc]
            dpw_ref[gi] += _bdot(q, ds, _DIMS["tn"])
            dq = _bdot(ds, pw_ref[gi], _DIMS["nt"])
            ds_next = dn_ref[:, oc].astype(F32) * ps_ref[:, cc] * keep_next
            dq_next = _bdot(ds_next, pw_ref[gi], _DIMS["nt"])
            extn_ref[0:t, cc] = dq / count
            extn_ref[t:t + HALO, cc] = dq_next * (1.0 / w)
            back = extn_ref[0:t, cc]
            for j in range(1, w):
                back = back + extn_ref[pl.ds(j, t), cc]
            dz_ref[:, _cols(_PD + gi * HEAD, HEAD)] = (back - dq).astype(BF16)

    full = lambda *shape: pl.BlockSpec(shape, lambda i: (0,) * len(shape))
    grad_specs = [full(4, 1, HEAD), full(4, 1, HEAD), full(4, HEAD, HEAD), full(4, HEAD, 1),
                  full(8, GROUP), full(32, GROUP), full(1, GROUP), full(1, GROUP), full(1, GROUP),
                  full(4, HEAD, HEAD), full(1, GROUP)]
    grad_shapes = [_sds(sp.block_shape, F32) for sp in grad_specs]
    nxt = lambda i: (jnp.minimum((i + 1) * hb, last_halo), 0)
    return pl.pallas_call(
        body, name=name, grid=(nt,),
        in_specs=[pl.BlockSpec((HALO, 8 * GROUP), lambda i: (jnp.maximum(i * hb - 1, 0), 0)),
                  pl.BlockSpec((t, 8 * GROUP), lambda i: (i, 0)),
                  pl.BlockSpec((HALO, 8 * GROUP), nxt),
                  pl.BlockSpec((t, 4 * GROUP), lambda i: (i, 0)),
                  pl.BlockSpec((HALO, 4 * GROUP), nxt),
                  *_mixer_param_specs()],
        out_specs=[pl.BlockSpec((t, 8 * GROUP), lambda i: (i, 0)), *grad_specs],
        out_shape=[_sds((s, 8 * GROUP), BF16), *grad_shapes],
        scratch_shapes=[pltpu.VMEM((HALO + t, GROUP), F32), pltpu.VMEM((t + HALO, GROUP), F32),
                        pltpu.VMEM((HALO + t + HALO, GROUP), F32), pltpu.VMEM((7, HALO + t + HALO - 8, GROUP), F32)],
        compiler_params=_params("arbitrary"),
    )(z, z, z, dmo, dmo, *prm)


def _dot(a, b, kind="nn"):
    return _bdot(a, b, _DIMS[kind])


def _chain(terms, kind):
    acc = None
    for a, b in terms:
        p = _dot(a, b, kind)
        acc = p if acc is None else acc + p
    return acc


def _after(g, token):
    return g if token is None else g + token[0:1, 0:1]


def _layer_fwd(li, h, p_all, fetch, sm, y1=None, next_mix_g=None):
    s, d = h.shape
    nb = N_DEV
    pd = p_all.shape[2]
    sv = {"h": h}
    rows = lambda t, w: pl.BlockSpec((t, w), lambda i: (i, 0))
    vec = pl.BlockSpec((1, d), lambda i: (0, 0))

    if y1 is None:
        y1 = _rms_fwd("rms_mix", h, sm["norm_mix_g"])
    tm = _tile(s, 1024)
    (z,) = _tiles(
        "w_in_fwd", (s // tm, nb),
        [y1, fetch("win", y1)],
        [pl.BlockSpec((tm, d), lambda i, j: (i, 0)), pl.BlockSpec((None, d, GROUP), lambda i, j: (j, 0, 0))],
        [_sds((s, nb * GROUP), BF16)], [pl.BlockSpec((tm, GROUP), lambda i, j: (i, j))],
        lambda a, w: (_dot(a[...], w[...]),))
    mo = _mixer_fwd("mixer_fwd", z, sm["mixer"])
    tm = _tile(s, 512)
    def mix_out(a, w, hv, gv):
        h2v = hv[...] + _dot(a[...], w[...])
        return h2v, _rms(h2v, gv[...])

    h2, y2 = _tiles(
        "w_out_fwd", (s // tm,),
        [mo, fetch("wout", z), h, sm["norm_ffn_g"]], [rows(tm, d), _resident((d, d)), rows(tm, d), vec],
        [_sds((s, d), F32), _sds((s, d), BF16)], [rows(tm, d), rows(tm, d)], mix_out)
    sv.update(y1=y1, z=z, mo=mo, h2=h2)

    gate_pre, up_pre, hmid = _ffn_up("ffn_up_fwd", y2, fetch("wg", z), fetch("wu", z))
    wd = fetch("wd", hmid)
    f8 = wd.shape[1]
    tm, tn = _tile(s, 512), _tile(d, 1024)
    (h3,) = _tiles(
        "w_down_fwd", (d // tn, s // tm),
        [hmid, wd, h2],
        [pl.BlockSpec((nb, tm, f8), lambda j, i: (0, i, 0)), pl.BlockSpec((nb, f8, tn), lambda j, i: (0, 0, j)),
         pl.BlockSpec((tm, tn), lambda j, i: (i, j))],
        [_sds((s, d), F32)], [pl.BlockSpec((tm, tn), lambda j, i: (i, j))],
        lambda a, w, hv: (hv[...] + _chain([(a[k], w[k]) for k in range(nb)], "nn"),))
    sv.update(y2=y2, gate_pre=gate_pre, up_pre=up_pre, hmid=hmid, h3=h3)

    y3 = _rms_fwd("rms_ple", h3, sm["norm_ple_g"])
    tm = _tile(s, 1024)
    (pp,) = _tiles(
        "w_ple_proj_fwd", (s // tm, nb),
        [p_all, fetch("wpp", hmid)],
        [pl.BlockSpec((None, tm, pd), lambda i, j: (li, i, 0)), pl.BlockSpec((None, pd, pd), lambda i, j: (j, 0, 0))],
        [_sds((s, d), BF16)], [pl.BlockSpec((tm, pd), lambda i, j: (i, j))],
        lambda a, w: (_dot(a[...], w[...]),))
    tm = _tile(s, 256)

    def ple(a, w, hv, ppv, *next_g):
        pg = _dot(a[...], w[...])
        h4v = hv[...] + jax.nn.sigmoid(pg) * ppv[...].astype(F32)
        return (h4v, pg) + tuple(_rms(h4v, gv[...]) for gv in next_g)

    more = [] if next_mix_g is None else [next_mix_g]
    h4, pg_pre, *y1_next = _tiles(
        "w_ple_gate_fwd", (s // tm,),
        [y3, fetch("wpg", hmid), h3, pp] + more,
        [rows(tm, d), _resident((d, d)), rows(tm, d), rows(tm, d)] + [vec] * len(more),
        [_sds((s, d), F32), _sds((s, d), BF16)] + [_sds((s, d), BF16)] * len(more),
        [rows(tm, d)] * (2 + len(more)), ple)
    sv.update(y3=y3, pp=pp, pg_pre=pg_pre)
    return h4, sv, (y1_next[0] if y1_next else None)


def _ffn_up(name, y2, wg, wu):
    s, d = y2.shape
    nb, f8, _ = wg.shape
    tm = _tile(s, 1024)

    def compute(y_ref, wg_ref, wu_ref):
        yv = y_ref[...]
        g = _dot(yv, wg_ref[...], "nt")
        u = _dot(yv, wu_ref[...], "nt")
        return g, u, jax.nn.silu(g) * u

    wspec = pl.BlockSpec((None, f8, d), lambda i, j: (j, 0, 0))
    ospec = pl.BlockSpec((None, tm, f8), lambda i, j: (j, i, 0))
    return _tiles(name, (s // tm, nb), [y2, wg, wu], [pl.BlockSpec((tm, d), lambda i, j: (i, 0)), wspec, wspec],
                  [_sds((nb, s, f8), BF16)] * 3, [ospec] * 3, compute)


def _swiglu_bwd(dm, g, u):
    sg = jax.nn.sigmoid(g)
    silu = g * sg
    return dm * u * (sg + silu * (1.0 - sg)), dm * silu


def _layer_bwd(li, dh, p_all, gw, sm, sv, start_token=None, hooks=None):
    hooks = hooks or {}
    run = lambda name, *a: hooks[name](*a) if name in hooks else None
    s, d = dh.shape
    nb, f8, _ = gw["wg"].shape
    pd = p_all.shape[2]
    gr, sg = {}, {}
    rows = lambda t, w: pl.BlockSpec((t, w), lambda i: (i, 0))
    vec = pl.BlockSpec((1, d), lambda i: (0, 0))
    tw = _tile(d, 1024)
    whole = lambda w: pl.BlockSpec((s, w), lambda i, j: (0, i))
    whole_j = lambda w: pl.BlockSpec((s, w), lambda i, j: (0, j))
    tn_dot = lambda a, b: (_dot(a[...], b[...], "tn"),)

    tm = _tile(s, 256)

    def ple_dx(dv, pg, ppv, w, hv, gv):
        dhv = dv[...]
        gate = jax.nn.sigmoid(pg[...].astype(F32))
        dpg = (dhv * ppv[...].astype(F32) * gate * (1.0 - gate)).astype(BF16)
        dh3, dg = _rms_back(hv[...], gv[...], _dot(dpg, w[...], "nt"), dhv)
        return dh3, dh3, dg, dpg, dhv * gate

    dh3, dh3_16, sg["norm_ple_g"], dpg, dpp = _tiles(
        "w_ple_gate_dx", (s // tm,),
        [dh, sv["pg_pre"], sv["pp"], gw["wpg"], sv["h3"], _after(sm["norm_ple_g"], start_token)],
        [rows(tm, d), rows(tm, d), rows(tm, d), _resident((d, d)), rows(tm, d), vec],
        [_sds((s, d), F32), _sds((s, d), BF16), _sds((1, d), F32), _sds((s, d), BF16), _sds((s, d), BF16)],
        [rows(tm, d), rows(tm, d), vec, rows(tm, d), rows(tm, d)], ple_dx, summed=(2,))
    (gr["wpp"],) = _tiles(
        "w_ple_proj_dw", (nb,),
        [p_all, dpp], [pl.BlockSpec((None, s, pd), lambda j: (li, 0, 0)), pl.BlockSpec((s, pd), lambda j: (0, j))],
        [_sds((nb, pd, pd), F32)], [pl.BlockSpec((None, pd, pd), lambda j: (j, 0, 0))], tn_dot)
    (gr["wpg"],) = _tiles(
        "w_ple_gate_dw", (d // tw, d // tw), [sv["y3"], dpg], [whole(tw), whole_j(tw)],
        [_sds((d, d), F32)], [pl.BlockSpec((tw, tw), lambda i, j: (i, j))], tn_dot)
    gr["wpg"] = gr["wpg"].reshape(nb, d // nb, d)

    tm = _tile(s, 1024)
    blk_rows = pl.BlockSpec((None, tm, f8), lambda i, j: (j, i, 0))
    def down_dx(a, w, g, u):
        n = tm // 4
        parts = [_swiglu_bwd(_dot(a[q * n:(q + 1) * n, :], w[...], "nt"),
                             g[q * n:(q + 1) * n, :].astype(F32), u[q * n:(q + 1) * n, :].astype(F32)) for q in range(4)]
        return jnp.concatenate([p_[0] for p_ in parts], axis=0), jnp.concatenate([p_[1] for p_ in parts], axis=0)

    dgate, dup = _tiles(
        "w_down_dx", (s // tm, nb),
        [dh3_16, gw["wd"], sv["gate_pre"], sv["up_pre"]],
        [pl.BlockSpec((tm, d), lambda i, j: (i, 0)), pl.BlockSpec((None, f8, d), lambda i, j: (j, 0, 0)), blk_rows, blk_rows],
        [_sds((nb, s, f8), BF16)] * 2, [blk_rows] * 2, down_dx)
    (gr["wd"],) = _tiles(
        "w_down_dw", (nb, d // tw),
        [sv["hmid"], dh3_16], [pl.BlockSpec((None, s, f8), lambda i, j: (i, 0, 0)), whole_j(tw)],
        [_sds((nb, f8, d), F32)], [pl.BlockSpec((None, f8, tw), lambda i, j: (i, 0, j))], tn_dot)
    mid_token = run("after_down_dx", dgate)
    for nm, dact in (("wg", dgate), ("wu", dup)):
        (gr[nm],) = _tiles(
            "w_" + {"wg": "gate", "wu": "up"}[nm] + "_dw", (d // tw, nb),
            [dact, sv["y2"]], [pl.BlockSpec((None, s, f8), lambda i, j: (j, 0, 0)), whole(tw)],
            [_sds((nb, f8, d), F32)], [pl.BlockSpec((None, f8, tw), lambda i, j: (j, 0, i))], tn_dot)
    ffn_token = run("after_ffn_grads", gr, gr["wu"])
    tm, tn = _tile(s, 512), _tile(d, 1024)
    act_spec = pl.BlockSpec((nb, tm, f8), lambda j, i: (0, i, 0))
    wt_spec = pl.BlockSpec((nb, f8, tn), lambda j, i: (0, 0, j))
    out_spec = pl.BlockSpec((tm, tn), lambda j, i: (i, j))
    (dy2,) = _tiles(
        "w_gate_dx", (d // tn, s // tm), [dgate, gw["wg"]], [act_spec, wt_spec],
        [_sds((s, d), F32)], [out_spec],
        lambda a, w: (_chain([(a[k], w[k]) for k in range(nb)], "nn"),),
        deps=[tk for tk in (mid_token, ffn_token) if tk is not None])
    (dy2,) = _tiles(
        "w_up_dx", (d // tn, s // tm), [dup, gw["wu"], dy2], [act_spec, wt_spec, out_spec],
        [_sds((s, d), F32)], [out_spec],
        lambda a, w, prev: (prev[...] + _chain([(a[k], w[k]) for k in range(nb)], "nn"),))
    g_ffn = _after(sm["norm_ffn_g"], run("before_ffn_norm", dy2))

    tm = _tile(s, 256)

    def out_dx(dy, hv, gv, dv, w):
        dh2v, dg = _rms_back(hv[...], gv[...], dy[...], dv[...])
        return dh2v, dh2v, dg, _dot(dh2v, w[...], "nt")

    dh2, dh2_16, sg["norm_ffn_g"], dmo = _tiles(
        "w_out_dx", (s // tm,), [dy2, sv["h2"], g_ffn, dh3, gw["wout"]],
        [rows(tm, d), rows(tm, d), vec, rows(tm, d), _resident((d, d))],
        [_sds((s, d), F32), _sds((s, d), BF16), _sds((1, d), F32), _sds((s, d), BF16)],
        [rows(tm, d), rows(tm, d), vec, rows(tm, d)], out_dx, summed=(2,))
    (gr["wout"],) = _tiles(
        "w_out_dw", (d // tw, d // tw), [sv["mo"], dh2_16], [whole(tw), whole_j(tw)],
        [_sds((d, d), F32)], [pl.BlockSpec((tw, tw), lambda i, j: (i, j))], tn_dot)
    gr["wout"] = gr["wout"].reshape(nb, d // nb, d)
    dz, *mix_grads = _mixer_bwd("mixer_bwd", sv["z"], dmo, sm["mixer"])
    sg["mixer"] = mix_grads
    (gr["win"],) = _tiles(
        "w_in_dw", (d // tw, nb), [sv["y1"], dz], [whole(tw), pl.BlockSpec((s, GROUP), lambda i, j: (0, j))],
        [_sds((nb, d, GROUP), F32)], [pl.BlockSpec((None, tw, GROUP), lambda i, j: (j, i, 0))], tn_dot)
    in_token = run("after_in_dw", gr)
    tm = _tile(s, 256)

    def in_dx(a, w, hv, gv, dv):
        dy1 = _chain([(a[:, k * GROUP:(k + 1) * GROUP], w[k]) for k in range(nb)], "nt")
        return _rms_back(hv[...], gv[...], dy1, dv[...])

    dh_in, sg["norm_mix_g"] = _tiles(
        "w_in_dx", (s // tm,),
        [dz, gw["win"], sv["h"], sm["norm_mix_g"], dh2],
        [rows(tm, nb * GROUP), _resident((nb, d, GROUP)), rows(tm, d), vec, rows(tm, d)],
        [_sds((s, d), F32), _sds((1, d), F32)], [rows(tm, d), vec], in_dx, summed=(1,),
        deps=[] if in_token is None else [in_token])
    return dh_in, gr, sg


def _place():
    return lax.axis_index("x"), lax.axis_index("y"), lax.axis_index("c")


_HBM = pl.BlockSpec(memory_space=pltpu.HBM)
_SEM = pl.BlockSpec(memory_space=pltpu.SEMAPHORE)


def _split_call(name, bufs, old_sems, n_new, after, body, want_token):
    nb, no = len(bufs), len(old_sems)
    extra = [] if after is None else [after]

    def kbody(*refs):
        new = refs[nb + no + len(extra):nb + no + len(extra) + n_new]
        body(refs[:nb], refs[nb:nb + no], new)
        if want_token:
            refs[-1][...] = jnp.zeros_like(refs[-1])

    outs = pl.pallas_call(
        kbody, name=name,
        out_shape=tuple([pltpu.SemaphoreType.DMA(())] * n_new + [pltpu.HBM(b.shape, b.dtype) for b in bufs]
                        + ([_sds((8, 128), F32)] if want_token else [])),
        in_specs=[_HBM] * nb + [_SEM] * no + [ANY] * len(extra),
        out_specs=tuple([_SEM] * n_new + [_HBM] * nb
                        + ([pl.BlockSpec(memory_space=pltpu.VMEM)] if want_token else [])),
        input_output_aliases={i: n_new + i for i in range(nb)},
        compiler_params=pltpu.CompilerParams(has_side_effects=pltpu.SideEffectType.DATAFLOW_SIDE_EFFECTING),
    )(*[pltpu.with_memory_space_constraint(b, pltpu.HBM) for b in bufs], *old_sems, *extra)
    return list(outs[:n_new]), list(outs[n_new:n_new + nb]), (outs[-1] if want_token else None)


def _remote(ref_src, ref_dst, send_sem, recv_sem, to):
    return pltpu.make_async_remote_copy(src_ref=ref_src, dst_ref=ref_dst, send_sem=send_sem, recv_sem=recv_sem,
                                        device_id=to, device_id_type=MESH)


def _place_shard(name, w):
    n_layers, r, cdim = w.shape
    tr = _row_tile(r, 256)
    nt = r // tr

    def body(w_ref, *rest):
        outs, buf, sem = rest[:n_layers], rest[n_layers], rest[n_layers + 1]
        i = pl.program_id(0)
        x, y, c = _place()
        slot = i % 2

        def writes(step, sl):
            rows = pl.ds(pl.multiple_of(step * tr, 16), tr)
            return [pltpu.make_async_copy(buf.at[sl, l], outs[l].at[4 * x + 2 * y + c, rows, :], sem.at[sl, l])
                    for l in range(n_layers)]

        @pl.when(i >= 2)
        def _():
            for cp in writes(i - 2, slot):
                cp.wait()

        buf[slot] = w_ref[...].astype(BF16)
        for cp in writes(i, slot):
            cp.start()

        @pl.when(i == nt - 1)
        def _():
            for cp in writes(i, slot):
                cp.wait()
            if nt >= 2:
                for cp in writes(i - 1, 1 - slot):
                    cp.wait()

    return pl.pallas_call(
        body, name=name, grid=(nt,),
        in_specs=[pl.BlockSpec((n_layers, tr, cdim), lambda i: (0, i, 0))], out_specs=[ANY] * n_layers,
        out_shape=[_sds((N_DEV, r, cdim), BF16)] * n_layers,
        scratch_shapes=[pltpu.VMEM((2, n_layers, tr, cdim), BF16), pltpu.SemaphoreType.DMA((2, n_layers))],
        compiler_params=_params("arbitrary"),
    )(w)


def _gather_start(li, lands, after):
    nw = len(lands)

    def body(bufs, _, new):
        x, y, c = _place()
        chips = [(1 - x, y), (x, 1 - y), (1 - x, 1 - y)]
        for w in range(nw):
            mine = bufs[w].at[4 * x + 2 * y + c]
            _remote(mine, mine, new[8 * w], new[8 * w + 4], (x, y, 1 - c)).start()
            for j, (px, py) in enumerate(chips):
                _remote(mine, mine, new[8 * w + 1 + j], new[8 * w + 5 + j], (px, py, c)).start()

    return _split_call("gather_start_l%s" % li, lands, [], 8 * nw, after, body, True)


def _arrivals(sems, nw):
    return [sems[8 * w + 5 + j] for w in range(nw) for j in range(3)]


def _pass_on(bufs, arrived, new):
    x, y, c = _place()
    for j, (px, py) in enumerate([(1 - x, y), (x, 1 - y), (1 - x, 1 - y)]):
        for w in range(len(bufs)):
            got = bufs[w].at[4 * px + 2 * py + c]
            _remote(got, got, new[6 * w + j], arrived[3 * w + j], (px, py, c)).wait_recv()
            _remote(got, got, new[6 * w + j], new[6 * w + 3 + j], (x, y, 1 - c)).start()


def _wait_rest(bufs, first, fwd):
    x, y, c = _place()
    sib = (x, y, 1 - c)
    for w in range(len(bufs)):
        mine = bufs[w].at[4 * x + 2 * y + c]
        theirs = bufs[w].at[4 * x + 2 * y + 1 - c]
        _remote(theirs, theirs, first[5 * w], first[5 * w + 4], sib).wait_recv()
        for k in range(4):
            _remote(mine, mine, first[5 * w + k], first[5 * w + 4], sib).wait_send()
        for j, (px, py) in enumerate([(1 - x, y), (x, 1 - y), (1 - x, 1 - y)]):
            sent = bufs[w].at[4 * px + 2 * py + c]
            got = bufs[w].at[4 * px + 2 * py + 1 - c]
            _remote(sent, sent, fwd[6 * w + j], fwd[6 * w + 3 + j], sib).wait_send()
            _remote(got, got, fwd[6 * w + j], fwd[6 * w + 3 + j], sib).wait_recv()


def _gather_forward(li, lands, sems, after):
    nw = len(lands)
    return _split_call("gather_forward_l%s" % li, lands, _arrivals(sems, nw), 6 * nw, after,
                       lambda bufs, old, new: _pass_on(bufs, old, new), True)


def _gather_wait(li, lands, sems, fwd_sems, after, ahead=None):
    nw = len(lands)
    first = [sems[8 * w + k] for w in range(nw) for k in range(5)]
    n1, n2 = len(first), len(first) + len(fwd_sems)
    more, arrived = ([], []) if ahead is None else (list(ahead[0]), _arrivals(ahead[1], len(ahead[0])))

    def body(bufs, old, new):
        if ahead is not None:
            _pass_on(bufs[nw:], old[n2:], new)
        _wait_rest(bufs[:nw], old[:n1], old[n1:n2])

    new, bufs, _ = _split_call("gather_wait_l%s" % li, list(lands) + more, first + list(fwd_sems) + arrived,
                               6 * len(more), after, body, False)
    return bufs[:nw], (None if ahead is None else (new, bufs[nw:]))


def _siblings_start(li, grads, after):
    nw = len(grads)
    lands = [lax.empty((4,) + g.shape[1:], g.dtype) for g in grads]

    def body(bufs, _, new):
        x, y, c = _place()
        for w in range(nw):
            for q in range(4):
                _remote(bufs[w].at[2 * q + (1 - c)], bufs[nw + w].at[q], new[8 * w + q], new[8 * w + 4 + q],
                        (x, y, 1 - c)).start()

    return _split_call("rs_siblings_start_l%s" % li, list(grads) + lands, [], 8 * nw, after, body, True)


def _siblings_wait(li, bufs, sems, after):
    nw = len(bufs) // 2

    def body(refs, old, _):
        x, y, c = _place()
        for w in range(nw):
            for q in range(4):
                _remote(refs[w].at[2 * q + (1 - c)], refs[nw + w].at[q], old[8 * w + q], old[8 * w + 4 + q],
                        (x, y, 1 - c)).wait()

    _, bufs, _ = _split_call("rs_siblings_wait_l%s" % li, bufs, sems, 0, after, body, False)
    return bufs[:nw], bufs[nw:]


_FLIPS = ((1, 0), (0, 1), (1, 1))


def _chips_copies(refs, nw, sems):
    x, y, c = _place()
    for w in range(nw):
        for r, (fx, fy) in enumerate(_FLIPS):
            px = 1 - x if fx else x
            py = 1 - y if fy else y
            yield _remote(refs[w].at[2 * px + py], refs[nw + w].at[r], sems[6 * w + r], sems[6 * w + 3 + r], (px, py, c))


def _chips_start(li, partials, after):
    nw = len(partials)
    lands = [lax.empty((3,) + a.shape[1:], a.dtype) for a in partials]

    def body(bufs, _, new):
        for cp in _chips_copies(bufs, nw, new):
            cp.start()

    return _split_call("rs_chips_start_l%s" % li, list(partials) + lands, [], 6 * nw, after, body, True)


def _chips_wait(li, bufs, sems, after):
    nw = len(bufs) // 2

    def body(refs, old, _):
        for cp in _chips_copies(refs, nw, old):
            cp.wait()

    _, bufs, _ = _split_call("rs_chips_wait_l%s" % li, bufs, sems, 0, after, body, False)
    return bufs[nw:]


def _row_tile(n, want):
    best = None
    for t in range(16, min(n, want) + 1, 16):
        if n % t == 0:
            best = t
    assert best is not None, n
    return best


def _chip_partials(name, grad, from_sibling):
    _, r, cdim = grad.shape
    tr = _row_tile(r, 512)
    nt = r // tr
    steps = 4 * nt

    def body(g_hbm, s_ref, pb_ref, own_ref, buf, sem):
        i, q = pl.program_id(0), pl.program_id(1)
        x, y, c = _place()
        n = 4 * i + q
        slot = n % 2

        def fetch(step, into):
            rows = pl.ds(pl.multiple_of((step // 4) * tr, 16), tr)
            return pltpu.make_async_copy(g_hbm.at[2 * (step % 4) + c, rows, :], buf.at[into], sem.at[into])

        @pl.when(n == 0)
        def _():
            fetch(0, 0).start()

        @pl.when(n + 1 < steps)
        def _():
            fetch(n + 1, 1 - slot).start()

        fetch(n, slot).wait()
        tot = buf[slot] + s_ref[...]
        pb_ref[...] = tot.astype(BF16)

        @pl.when(q == 2 * x + y)
        def _():
            own_ref[...] = tot

    return pl.pallas_call(
        body, name=name, grid=(nt, 4),
        in_specs=[ANY, pl.BlockSpec((None, tr, cdim), lambda i, q: (q, i, 0))],
        out_specs=[pl.BlockSpec((None, tr, cdim), lambda i, q: (q, i, 0)), pl.BlockSpec((tr, cdim), lambda i, q: (i, 0))],
        out_shape=[_sds((4, r, cdim), BF16), _sds((r, cdim), F32)],
        scratch_shapes=[pltpu.VMEM((2, tr, cdim), F32), pltpu.SemaphoreType.DMA((2,))],
        compiler_params=_params("arbitrary", "arbitrary"),
    )(grad, from_sibling)


def _adamw(w, g, m, v):
    m = ADAM_B1 * m + (1.0 - ADAM_B1) * g
    v = ADAM_B2 * v + (1.0 - ADAM_B2) * (g * g)
    m_hat = m / (1.0 - ADAM_B1 ** ADAM_STEP)
    v_hat = v / (1.0 - ADAM_B2 ** ADAM_STEP)
    delta = -ADAM_LR * (m_hat / (jnp.sqrt(v_hat) + ADAM_EPS) + ADAM_WD * w)
    return delta, m, v


def _finish_weight(name, li, own, from_chips, w, m, v, stacked):
    r, cdim = own.shape
    tr = _row_tile(r, 256)
    if stacked is None:
        stacked = [lax.empty(w.shape, F32) for _ in range(4)]

    def body(own_ref, fc_ref, w_ref, m_ref, v_ref, *rest):
        g_out, d_out, m_out, v_out = rest[4:]
        g = own_ref[...] + fc_ref[0].astype(F32) + fc_ref[1].astype(F32) + fc_ref[2].astype(F32)
        delta, mn, vn = _adamw(w_ref[...], g, m_ref[...], v_ref[...])
        g_out[...] = g
        d_out[...] = delta
        m_out[...] = mn
        v_out[...] = vn

    tile = pl.BlockSpec((tr, cdim), lambda i: (i, 0))
    lay = pl.BlockSpec((None, tr, cdim), lambda i: (li, i, 0))
    return pl.pallas_call(
        body, name=name, grid=(r // tr,),
        in_specs=[tile, pl.BlockSpec((3, tr, cdim), lambda i: (0, i, 0)), lay, lay, lay] + [ANY] * 4,
        out_specs=[lay] * 4, out_shape=[_sds(w.shape, F32)] * 4,
        input_output_aliases={5: 0, 6: 1, 7: 2, 8: 3},
        compiler_params=_params("parallel"),
    )(own, from_chips, w, m, v, *stacked)


def _allgather_small(name, v, reduce, after=None):
    r = v.shape[0]
    deps = [] if after is None else [after]

    def body(x_ref, *rest):
        out_ref, rest = rest[len(deps)], rest[len(deps) + 1:]
        if reduce:
            sum_ref, send_sems, recv_sems, local_sem = rest
        else:
            send_sems, recv_sems, local_sem = rest
        x, y, c = _place()
        me, sib = (x, y, c), (x, y, 1 - c)
        chips = [(1 - x, y), (x, 1 - y), (1 - x, 1 - y)]

        def rows(px, py, pc):
            return out_ref.at[pl.ds(pl.multiple_of((4 * px + 2 * py + pc) * r, 8), r), :]

        def copy(k, block, to, src=None):
            return pltpu.make_async_remote_copy(
                src_ref=rows(*block) if src is None else src, dst_ref=rows(*block),
                send_sem=send_sems.at[k], recv_sem=recv_sems.at[k], device_id=to, device_id_type=MESH)

        mine = pltpu.make_async_copy(x_ref, rows(*me), local_sem)
        mine.start()
        first = [copy(0, me, sib, src=x_ref)]
        first += [copy(1 + j, me, (*chip, c), src=x_ref) for j, chip in enumerate(chips)]
        for cp in first:
            cp.start()
        passed = [copy(4 + j, (*chip, c), sib) for j, chip in enumerate(chips)]
        for j, chip in enumerate(chips):
            copy(1 + j, (*chip, c), me).wait_recv()
            passed[j].start()
        copy(0, sib, me).wait_recv()
        for j, chip in enumerate(chips):
            copy(4 + j, (*chip, 1 - c), me).wait_recv()
        for cp in first + passed:
            cp.wait_send()
        mine.wait()
        if reduce:
            tot = out_ref[0:r, :]
            for d in range(1, N_DEV):
                tot = tot + out_ref[d * r:(d + 1) * r, :]
            sum_ref[...] = tot

    vm = pl.BlockSpec(memory_space=pltpu.VMEM)
    outs = [_sds((N_DEV * r, 128), F32)] + ([_sds((r, 128), F32)] if reduce else [])
    res = pl.pallas_call(
        body, name=name,
        in_specs=[vm] + [ANY] * len(deps), out_specs=[vm] * len(outs), out_shape=outs,
        scratch_shapes=[pltpu.SemaphoreType.DMA((7,)), pltpu.SemaphoreType.DMA((7,)), pltpu.SemaphoreType.DMA],
        compiler_params=pltpu.CompilerParams(vmem_limit_bytes=VMEM_LIMIT_BYTES),
    )(v, *deps)
    return res


def _adamw_small(name, w, g, m, v):
    def body(w_ref, g_ref, m_ref, v_ref, d_out, m_out, v_out):
        delta, mn, vn = _adamw(w_ref[...], g_ref[...], m_ref[...], v_ref[...])
        d_out[...] = delta
        m_out[...] = mn
        v_out[...] = vn

    vm = pl.BlockSpec(memory_space=pltpu.VMEM)
    return pl.pallas_call(
        body, name=name, in_specs=[vm] * 4, out_specs=[vm] * 3, out_shape=[_sds(w.shape, F32)] * 3,
        compiler_params=pltpu.CompilerParams(vmem_limit_bytes=VMEM_LIMIT_BYTES),
    )(w, g, m, v)


def _pack(arrays):
    flat, layout, off = [], [], 0
    for a in arrays:
        flat.append(a.reshape(-1).astype(F32))
        layout.append((off, a.shape))
        off += a.size
    total = -(-off // 1024) * 1024
    if total > off:
        flat.append(jnp.zeros((total - off,), F32))
    return jnp.concatenate(flat).reshape(total // 128, 128), layout


def _unpack(packed, layout):
    flat = packed.reshape(-1)
    return [flat[off:off + math.prod(shape)].reshape(shape) for off, shape in layout]


_BIG = ("win", "wout", "wg", "wu", "wd", "wpg", "wpp")
_BIG_FULL = {"win": "w_in", "wout": "w_out", "wg": "w_gate", "wu": "w_up", "wd": "w_down",
             "wpg": "w_ple_gate", "wpp": "w_ple_proj"}
_SMALL_REPLICATED = ("norm_mix_g", "sgu_ln_g", "sgu_ln_b", "sgu_w", "sgu_b", "cf_conv_b", "cf_ln_g", "cf_ln_b",
                     "pool_w", "pool_scale", "norm_ffn_g", "norm_ple_g", "final_norm_g")
_SMALL_SHARDED = ("sc_conv_w", "cf_conv_w")
_WEIGHTS = ("norm_mix_g", "w_in", "sgu_ln_g", "sgu_ln_b", "sgu_w", "sgu_b", "sc_conv_w", "cf_conv_w", "cf_conv_b",
            "cf_ln_g", "cf_ln_b", "pool_w", "pool_scale", "w_out", "norm_ffn_g", "w_gate", "w_up", "w_down",
            "norm_ple_g", "w_ple_gate", "w_ple_proj", "final_norm_g")


def _pad_rows(a, rows):
    return jnp.concatenate([a, jnp.zeros((rows - a.shape[0],) + a.shape[1:], a.dtype)], axis=0)


def _mixer_params(li, W, sc_full, cf_full):
    return [W["sgu_ln_g"][li][:, None, :], W["sgu_ln_b"][li][:, None, :], W["sgu_w"][li], W["sgu_b"][li][:, :, None],
            _pad_rows(sc_full[li], 8), _pad_rows(cf_full[li], 32),
            W["cf_conv_b"][li][None, :], W["cf_ln_g"][li][None, :], W["cf_ln_b"][li][None, :],
            W["pool_w"][li], W["pool_scale"][li][None, :]]


def _step(W, M, V, x, p, loss_target):
    n_layers = W["w_in"].shape[0]
    h = x[0]
    target = loss_target[0]
    p_all = p[:, 0]
    xi, yi, ci = _place()
    blk = 4 * xi + 2 * yi + ci
    csh = W["sc_conv_w"].shape[2]
    turned = ("w_gate", "w_up")
    W, M, V = ({n: (jnp.swapaxes(a, 1, 2) if n in turned else a) for n, a in t.items()} for t in (W, M, V))

    packed, lay = _pack([W[n] for n in _SMALL_SHARDED])
    (taps,) = _allgather_small("gather_conv_taps", packed, reduce=False)
    per_dev = [_unpack(taps[d * packed.shape[0]:(d + 1) * packed.shape[0]], lay) for d in range(N_DEV)]
    sc_full = jnp.concatenate([pd[0] for pd in per_dev], axis=-1)
    cf_full = jnp.concatenate([pd[1] for pd in per_dev], axis=-1)

    d = h.shape[1]
    placed = [_place_shard("place_" + n, W[_BIG_FULL[n]]) for n in _BIG]
    lands = [[placed[w][li] for w in range(len(_BIG))] for li in range(n_layers)]

    groups = (("win",), ("wout", "wg", "wu"), ("wd", "wpg", "wpp"))
    order = [(li, gi) for li in range(n_layers) for gi in range(len(groups))]
    moving, tok = [], taps
    for li in range(n_layers):
        sems, bufs, tok = _gather_start(li, lands[li], tok)
        moving.append({"sems": sems, "bufs": bufs, "got": {}})
    passed = {}

    def group_of(item):
        li, gi = item
        idx = [_BIG.index(n) for n in groups[gi]]
        sems = [s_ for i in idx for s_ in moving[li]["sems"][8 * i:8 * i + 8]]
        return "%d_%s" % (li, groups[gi][0]), [moving[li]["bufs"][i] for i in idx], sems

    def fetcher(li):
        def fetch(name, after):
            got = moving[li]["got"]
            if name not in got:
                item = (li, next(gi for gi, g_ in enumerate(groups) if name in g_))
                tag, bufs, sems = group_of(item)
                if item not in passed:
                    fwd_sems, bufs, _ = _gather_forward(tag, bufs, sems, after)
                    passed[item] = (fwd_sems, bufs)
                fwd_sems, bufs = passed[item]
                nxt = order.index(item) + 1
                ahead = None
                if nxt < len(order) and order[nxt][0] >= 1:
                    ahead = group_of(order[nxt])[1:]
                arrived, sent_on = _gather_wait(tag, bufs, sems, fwd_sems, after, ahead)
                if ahead is not None:
                    passed[order[nxt]] = sent_on
                for n, a in zip(groups[item[1]], arrived):
                    got[n] = a.reshape(d, d) if n in ("wout", "wpg") else a
            return got[name]
        return fetch

    saved, gathered, smalls = [], [], []
    y1 = None
    for li in range(n_layers):
        sm = {"norm_mix_g": _after(W["norm_mix_g"][li][None, :], tok if li == 0 else None),
              "norm_ffn_g": W["norm_ffn_g"][li][None, :],
              "norm_ple_g": W["norm_ple_g"][li][None, :], "mixer": _mixer_params(li, W, sc_full, cf_full)}
        next_mix_g = W["norm_mix_g"][li + 1][None, :] if li + 1 < n_layers else None
        h, sv, y1 = _layer_fwd(li, h, p_all, fetcher(li), sm, y1, next_mix_g)
        saved.append(sv)
        gathered.append(moving[li]["got"])
        smalls.append(sm)
    loss, dh, d_final_g = _loss_head(h, W["final_norm_g"][None, :], target)

    big_out = {n: None for n in _BIG}
    small_grads = []

    def scatter_begin(li, tag, names, gr, after):
        sems, bufs, tok = _siblings_start(tag, [gr[n] for n in names], after)
        return {"li": li, "tag": tag, "names": names, "sems": sems, "bufs": bufs, "tok": tok}

    def scatter_middle(st, after):
        grads, from_sib = _siblings_wait(st["tag"], st["bufs"], st["sems"], after)
        parts = [_chip_partials("rs_sum_" + n, g_, s_) for n, g_, s_ in zip(st["names"], grads, from_sib)]
        st["own"] = [own for _, own in parts]
        st["sems"], st["bufs"], tok = _chips_start(st["tag"], [pb for pb, _ in parts], None)
        return tok

    def scatter_end(st, after):
        from_chips = _chips_wait(st["tag"], st["bufs"], st["sems"], after)
        for n, own, fc in zip(st["names"], st["own"], from_chips):
            full = _BIG_FULL[n]
            big_out[n] = _finish_weight("adamw_" + n, st["li"], own, fc, W[full], M[full], V[full], big_out[n])
        return big_out[st["names"][0]][0]

    order = _SMALL_REPLICATED + _SMALL_SHARDED

    def reduce_small(after):
        by_layer = small_grads[::-1]
        stacked = lambda fn: jnp.stack([fn(sg) for sg in by_layer])
        mix = lambda i: (lambda sg: sg["mixer"][i])
        grads_small = {
            "norm_mix_g": stacked(lambda sg: sg["norm_mix_g"][0]),
            "sgu_ln_g": stacked(mix(0))[:, :, 0, :], "sgu_ln_b": stacked(mix(1))[:, :, 0, :],
            "sgu_w": stacked(mix(2)), "sgu_b": stacked(mix(3))[:, :, :, 0],
            "sc_conv_w": stacked(mix(4))[:, :SHORT_K], "cf_conv_w": stacked(mix(5))[:, :CONF_K],
            "cf_conv_b": stacked(mix(6))[:, 0], "cf_ln_g": stacked(mix(7))[:, 0], "cf_ln_b": stacked(mix(8))[:, 0],
            "pool_w": stacked(mix(9)), "pool_scale": stacked(mix(10))[:, 0],
            "norm_ffn_g": stacked(lambda sg: sg["norm_ffn_g"][0]),
            "norm_ple_g": stacked(lambda sg: sg["norm_ple_g"][0]),
            "final_norm_g": d_final_g[0],
        }
        packed, lay = _pack([grads_small[n] for n in order] + [loss])
        _, summed = _allgather_small("allreduce_small", packed, reduce=True, after=after)
        return dict(zip(order + ("loss",), _unpack(summed, lay)))

    early, late = ("wpp", "wpg", "wd", "wg", "wu"), ("wout", "win")
    pending, tok = None, None
    for li in reversed(range(n_layers)):
        hooks, first, last = {}, {}, {}
        if pending is not None:
            hooks["after_down_dx"] = functools.partial(scatter_middle, pending)
        if li == 0:
            def begin_early(gr, after, first=first):
                first.update(scatter_begin(0, "0_early", early, gr, after))
                return first["tok"]

            def begin_late(gr, last=last):
                last.update(scatter_begin(0, "0_late", late, gr, None))
                return last["tok"]

            hooks["after_ffn_grads"] = begin_early
            hooks["before_ffn_norm"] = lambda after, first=first: scatter_middle(first, after)
            hooks["after_in_dw"] = begin_late
        dh, gr, sg = _layer_bwd(li, dh, p_all, gathered[li], smalls[li], saved[li], tok, hooks)
        small_grads.append(sg)
        if li > 0:
            done = None if pending is None else scatter_end(pending, dh)
            pending = scatter_begin(li, str(li), _BIG, gr, done)
            tok = pending["tok"]
        else:
            scatter_middle(last, dh)
            done = last["own"][0] if pending is None else scatter_end(pending, last["own"][0])
            done = scatter_end(first, done)
            total = reduce_small(scatter_end(last, done))
    loss_all = total["loss"][0, 0]

    out_g, out_d, out_m, out_v = {}, {}, {}, {}
    pw, lay_r = _pack([W[n] for n in _SMALL_REPLICATED])
    pg, _ = _pack([total[n] for n in _SMALL_REPLICATED])
    pm, _ = _pack([M[n] for n in _SMALL_REPLICATED])
    pv, _ = _pack([V[n] for n in _SMALL_REPLICATED])
    dd, mm, vv = _adamw_small("adamw_small", pw, pg, pm, pv)
    for n, a, b, c_ in zip(_SMALL_REPLICATED, _unpack(dd, lay_r), _unpack(mm, lay_r), _unpack(vv, lay_r)):
        out_g[n], out_d[n], out_m[n], out_v[n] = total[n], a, b, c_
    pick = (jnp.arange(N_DEV) == blk).astype(F32)[None, None, :, None]
    mine = {n: jnp.sum(total[n].reshape(total[n].shape[:2] + (N_DEV, csh)) * pick, axis=2) for n in _SMALL_SHARDED}
    pw, lay_s = _pack([W[n] for n in _SMALL_SHARDED])
    pg, _ = _pack([mine[n] for n in _SMALL_SHARDED])
    pm, _ = _pack([M[n] for n in _SMALL_SHARDED])
    pv, _ = _pack([V[n] for n in _SMALL_SHARDED])
    dd, mm, vv = _adamw_small("adamw_conv_taps", pw, pg, pm, pv)
    for n, a, b, c_ in zip(_SMALL_SHARDED, _unpack(dd, lay_s), _unpack(mm, lay_s), _unpack(vv, lay_s)):
        out_g[n], out_d[n], out_m[n], out_v[n] = mine[n], a, b, c_
    for n in _BIG:
        for k, dst in enumerate((out_g, out_d, out_m, out_v)):
            full = _BIG_FULL[n]
            dst[full] = jnp.swapaxes(big_out[n][k], 1, 2) if full in turned else big_out[n][k]

    return (loss_all, dh[None], *[out_g[n] for n in _WEIGHTS], *[out_d[n] for n in _WEIGHTS],
            *[out_m[n] for n in _WEIGHTS], *[out_v[n] for n in _WEIGHTS])


def kernel(x, p, norm_mix_g, w_in, sgu_ln_g, sgu_ln_b, sgu_w, sgu_b, sc_conv_w, cf_conv_w, cf_conv_b, cf_ln_g, cf_ln_b, pool_w, pool_scale, w_out, norm_ffn_g, w_gate, w_up, w_down, norm_ple_g, w_ple_gate, w_ple_proj, final_norm_g, loss_target, m_norm_mix_g, m_w_in, m_sgu_ln_g, m_sgu_ln_b, m_sgu_w, m_sgu_b, m_sc_conv_w, m_cf_conv_w, m_cf_conv_b, m_cf_ln_g, m_cf_ln_b, m_pool_w, m_pool_scale, m_w_out, m_norm_ffn_g, m_w_gate, m_w_up, m_w_down, m_norm_ple_g, m_w_ple_gate, m_w_ple_proj, m_final_norm_g, v_norm_mix_g, v_w_in, v_sgu_ln_g, v_sgu_ln_b, v_sgu_w, v_sgu_b, v_sc_conv_w, v_cf_conv_w, v_cf_conv_b, v_cf_ln_g, v_cf_ln_b, v_pool_w, v_pool_scale, v_w_out, v_norm_ffn_g, v_w_gate, v_w_up, v_w_down, v_norm_ple_g, v_w_ple_gate, v_w_ple_proj, v_final_norm_g):
    given = dict(locals())
    W = {n: given[n] for n in _WEIGHTS}
    M = {n: given["m_" + n] for n in _WEIGHTS}
    V = {n: given["v_" + n] for n in _WEIGHTS}
    return _step(W, M, V, x, p, loss_target)
```
